```python
import jax
import jax.numpy as jnp
from jax import lax
import numpy as np

D_MODEL = 2048
BATCH = 8
SEQ = 2048
DEPTH = 2

GRID_W = 64
CTX_LEN = 256
HEAD_DIM = 64
EPS = 1e-6
NEG = -1e30
N_BRANCH = 4
BRANCH_DIM = 512
CONV_DIM = BRANCH_DIM
CONV_K = 31
GMLP_DIM = BRANCH_DIM
GMLP_CHUNK = 128
GMLP_GROUPS = 4
WIN_HEADS = BRANCH_DIM // HEAD_DIM
WIN_KV_HEADS = 2
WINDOW = 128
WIN_BLOCK = 128
ROPE_BASE = 10000.0
NA_HEADS = BRANCH_DIM // HEAD_DIM
NA_ROWS = 8
NA_COLS = 16
NA_QCOLS = 16
NA_KCOLS = NA_QCOLS + NA_COLS
A_OFF = 0
A_W = 2 * CONV_DIM
B_OFF = A_OFF + A_W
B_W = 2 * GMLP_DIM
CQ_OFF = B_OFF + B_W
CQ_W = WIN_HEADS * HEAD_DIM
DQ_OFF = CQ_OFF + CQ_W
DQ_W = NA_HEADS * HEAD_DIM
KV_OFF = DQ_OFF + DQ_W
CKV_W = WIN_KV_HEADS * HEAD_DIM
DKV_W = NA_HEADS * HEAD_DIM
KV_W = 2 * CKV_W + 2 * DKV_W
G_OFF = KV_OFF + KV_W
G_W = N_BRANCH * D_MODEL
IN_COLS = G_OFF + G_W
N_EXPERTS = 64
TOP_K = 8
N_GROUPS = 8
TOPK_GROUPS = 4
EXPERT_DIM = 512
SHARED_DIM = 512
ROUTE_SCALE = 2.5
MOE_BLOCK = 256

kernel_name = "hybrid_dit_conv_gmlp_swa_natten_moe"


def rms_norm(x, g):
    xf = x.astype(jnp.float32)
    y = xf * lax.rsqrt(jnp.mean(xf * xf, axis=-1, keepdims=True) + EPS)
    return (y * g).astype(x.dtype)


def layer_norm(x, g, b):
    xf = x.astype(jnp.float32)
    mu = jnp.mean(xf, axis=-1, keepdims=True)
    var = jnp.mean(jnp.square(xf - mu), axis=-1, keepdims=True)
    return ((xf - mu) * lax.rsqrt(var + EPS) * g + b).astype(x.dtype)


def modulate(h, shift, scale):
    return h * (1.0 + scale) + shift


def heads(t, n):
    return t.reshape(t.shape[:-1] + (n, HEAD_DIM))


def axial_rope(t, prow, pcol):
    half = HEAD_DIM // 2
    quarter = HEAD_DIM // 4
    inv = ROPE_BASE ** (-np.arange(quarter, dtype=np.float32) / quarter)
    tf = t.astype(jnp.float32)
    out = []
    for part, p in ((tf[..., :half], prow), (tf[..., half:], pcol)):
        ang = p[:, None] * inv[None, :]
        cos = jnp.cos(ang)[:, None, :]
        sin = jnp.sin(ang)[:, None, :]
        a, b = part[..., :quarter], part[..., quarter:]
        out += [a * cos - b * sin, a * sin + b * cos]
    return jnp.concatenate(out, axis=-1).astype(t.dtype)


def joint_softmax(parts, sink=None):
    cols = [p.astype(jnp.float32) for p in parts]
    if sink is not None:
        cols.append(jnp.broadcast_to(sink.astype(jnp.float32), cols[0].shape[:-1] + (1,)))
    probs = jax.nn.softmax(jnp.concatenate(cols, axis=-1), axis=-1)
    edges = [int(e) for e in np.cumsum([p.shape[-1] for p in parts])]
    return jnp.split(probs, edges, axis=-1)[:len(parts)]


def conformer_conv(a, b_in, dw, dw_b, g, beta):
    a = a + b_in
    y = a[..., :CONV_DIM] * jax.nn.sigmoid(a[..., CONV_DIM:])
    y = lax.conv_general_dilated(y, dw[:, None, :], window_strides=(1,),
                                 padding=[(CONV_K // 2, CONV_K // 2)],
                                 dimension_numbers=('NWC', 'WIO', 'NWC'),
                                 feature_group_count=CONV_DIM) + dw_b
    return jax.nn.silu(layer_norm(y, g, beta))


def spatial_gating(z, g, beta, ws, bs):
    z = jax.nn.gelu(z)
    u, v = z[..., :GMLP_DIM], z[..., GMLP_DIM:]
    v = layer_norm(v, g, beta)
    bsz, L, _ = v.shape
    v = v.reshape(bsz, L // GMLP_CHUNK, GMLP_CHUNK, GMLP_GROUPS, GMLP_DIM // GMLP_GROUPS)
    s = jnp.einsum('gpq,bnqgc->bnpgc', ws, v) + bs.T[:, :, None]
    return u * s.reshape(bsz, L, GMLP_DIM)


def window_attention(q, k, v, kc, vc, sink):
    bsz, S, hq, dh = q.shape
    hkv = k.shape[2]
    grp = hq // hkv
    nb = S // WIN_BLOCK
    qb = q.reshape(bsz, nb, WIN_BLOCK, hkv, grp, dh) * (dh ** -0.5)

    def band(t):
        tp = jnp.pad(t, ((0, 0), (WIN_BLOCK, WIN_BLOCK), (0, 0), (0, 0)))
        return jnp.concatenate([tp[:, i * WIN_BLOCK:i * WIN_BLOCK + S].reshape(bsz, nb, WIN_BLOCK, hkv, dh)
                                for i in range(3)], axis=2)

    kb, vb = band(k), band(v)
    qpos = np.arange(nb)[:, None, None] * WIN_BLOCK + np.arange(WIN_BLOCK)[None, :, None]
    kpos = np.arange(nb)[:, None, None] * WIN_BLOCK - WIN_BLOCK + np.arange(3 * WIN_BLOCK)[None, None, :]
    mask = (np.abs(qpos - kpos) <= WINDOW) & (kpos >= 0) & (kpos < S)
    s_loc = jnp.einsum('bnqhgd,bnkhd->bnhgqk', qb, kb).astype(jnp.float32)
    s_loc = jnp.where(mask[None, :, None, None], s_loc, NEG)
    s_ctx = jnp.einsum('bnqhgd,bkhd->bnhgqk', qb, kc)
    sink_b = sink.reshape(hkv, grp)[None, None, :, :, None, None]
    p_loc, p_ctx = joint_softmax([s_loc, s_ctx], sink_b)
    o = (jnp.einsum('bnhgqk,bnkhd->bnqhgd', p_loc.astype(v.dtype), vb)
         + jnp.einsum('bnhgqk,bkhd->bnqhgd', p_ctx.astype(v.dtype), vc))
    return o.reshape(bsz, S, hq * dh)


def context_attention(q, k, v, sink=None):
    bsz, C, hq, dh = q.shape
    hkv = k.shape[2]
    grp = hq // hkv
    qg = q.reshape(bsz, C, hkv, grp, dh) * (dh ** -0.5)
    s = jnp.einsum('bqhgd,bkhd->bhgqk', qg, k)
    sk = None if sink is None else sink.reshape(hkv, grp)[None, :, :, None, None]
    (p,) = joint_softmax([s], sk)
    return jnp.einsum('bhgqk,bkhd->bqhgd', p.astype(v.dtype), v).reshape(bsz, C, hq * dh)


def neighbourhood_attention(q, k, v, kc, vc, rpb):
    bsz, S, H, dh = q.shape
    rows = S // GRID_W
    rw = min(NA_ROWS, rows)
    ncb = GRID_W // NA_QCOLS
    r = np.arange(rows)
    rs = np.clip(r - rw // 2, 0, rows - rw)
    j = np.arange(ncb)
    ks = np.clip(j * NA_QCOLS - NA_COLS // 2, 0, GRID_W - NA_KCOLS)
    krow = rs[:, None] + np.arange(rw)[None, :]
    kcol = ks[:, None] + np.arange(NA_KCOLS)[None, :]
    tok = (krow[:, None, :, None] * GRID_W + kcol[None, :, None, :]).reshape(rows, ncb, rw * NA_KCOLS).astype(np.int32)
    qcol = j[:, None] * NA_QCOLS + np.arange(NA_QCOLS)[None, :]
    cs = np.clip(qcol - NA_COLS // 2, 0, GRID_W - NA_COLS)
    cmask = (kcol[:, None, :] >= cs[:, :, None]) & (kcol[:, None, :] < cs[:, :, None] + NA_COLS)
    full = (rows, ncb, NA_QCOLS, rw, NA_KCOLS)
    mask = np.broadcast_to(cmask[:, :, None, :], full[1:]).reshape(ncb, NA_QCOLS, rw * NA_KCOLS)
    ri = krow - r[:, None] + NA_ROWS - 1
    ci = np.clip(kcol[:, None, :] - qcol[:, :, None] + NA_COLS - 1, 0, 2 * NA_COLS - 2)
    ri_f = np.broadcast_to(ri[:, None, None, :, None], full).reshape(rows, ncb, NA_QCOLS, rw * NA_KCOLS)
    ci_f = np.broadcast_to(ci[None, :, :, None, :], full).reshape(rows, ncb, NA_QCOLS, rw * NA_KCOLS)
    bias = jnp.moveaxis(rpb[:, ri_f, ci_f].astype(jnp.float32), 0, 2)
    qb = q.reshape(bsz, rows, ncb, NA_QCOLS, H, dh) * (dh ** -0.5)
    kg = k[:, tok]
    vg = v[:, tok]
    s_loc = jnp.einsum('brjqhd,brjkhd->brjhqk', qb, kg).astype(jnp.float32) + bias[None]
    s_loc = jnp.where(mask[None, None, :, None], s_loc, NEG)
    s_ctx = jnp.einsum('brjqhd,bkhd->brjhqk', qb, kc)
    p_loc, p_ctx = joint_softmax([s_loc, s_ctx])
    o = (jnp.einsum('brjhqk,brjkhd->brjqhd', p_loc.astype(v.dtype), vg)
         + jnp.einsum('brjhqk,bkhd->brjqhd', p_ctx.astype(v.dtype), vc))
    return o.reshape(bsz, S, H * dh)


def split_kv(pkv):
    o1 = CKV_W
    o2 = 2 * CKV_W
    o3 = 2 * CKV_W + DKV_W
    return (heads(pkv[..., :o1], WIN_KV_HEADS), heads(pkv[..., o1:o2], WIN_KV_HEADS),
            heads(pkv[..., o2:o3], NA_HEADS), heads(pkv[..., o3:], NA_HEADS))


def merge_branches(ys, gate_logits, w_branch, w_out):
    acc = jax.nn.sigmoid(gate_logits[..., :D_MODEL]) * (ys[0] @ w_branch[0])
    for i in range(1, N_BRANCH):
        acc = acc + jax.nn.sigmoid(gate_logits[..., i * D_MODEL:(i + 1) * D_MODEL]) * (ys[i] @ w_branch[i])
    return acc @ w_out


def token_mixer(hl, hc, w_in, conv_b_in, conv_dw, conv_dw_b, conv_ln_g, conv_ln_b,
                gmlp_ln_g, gmlp_ln_b, gmlp_ws, gmlp_bs, win_sink, na_rpb, w_branch, w_out, with_ctx):
    S = hl.shape[1]
    pos = np.arange(S)
    prow = (pos // GRID_W).astype(np.float32)
    pcol = (pos % GRID_W).astype(np.float32)
    pl = hl @ w_in
    if with_ctx:
        pc = hc @ w_in
        pkv_c = pc[..., KV_OFF:KV_OFF + KV_W]
    else:
        pkv_c = hc @ w_in[:, KV_OFF:KV_OFF + KV_W]
    ckc, cvc, dkc, dvc = split_kv(pkv_c)
    ckl, cvl, dkl, dvl = split_kv(pl[..., KV_OFF:KV_OFF + KV_W])
    cql = axial_rope(heads(pl[..., CQ_OFF:CQ_OFF + CQ_W], WIN_HEADS), prow, pcol)
    ckl = axial_rope(ckl, prow, pcol)
    dql = heads(pl[..., DQ_OFF:DQ_OFF + DQ_W], NA_HEADS)
    ys_l = [conformer_conv(pl[..., A_OFF:A_OFF + A_W], conv_b_in, conv_dw, conv_dw_b, conv_ln_g, conv_ln_b),
            spatial_gating(pl[..., B_OFF:B_OFF + B_W], gmlp_ln_g, gmlp_ln_b, gmlp_ws, gmlp_bs),
            window_attention(cql, ckl, cvl, ckc, cvc, win_sink),
            neighbourhood_attention(dql, dkl, dvl, dkc, dvc, na_rpb)]
    out_l = merge_branches(ys_l, pl[..., G_OFF:G_OFF + G_W], w_branch, w_out)
    if not with_ctx:
        return out_l, None
    ys_c = [conformer_conv(pc[..., A_OFF:A_OFF + A_W], conv_b_in, conv_dw, conv_dw_b, conv_ln_g, conv_ln_b),
            spatial_gating(pc[..., B_OFF:B_OFF + B_W], gmlp_ln_g, gmlp_ln_b, gmlp_ws, gmlp_bs),
            context_attention(heads(pc[..., CQ_OFF:CQ_OFF + CQ_W], WIN_HEADS), ckc, cvc, win_sink),
            context_attention(heads(pc[..., DQ_OFF:DQ_OFF + DQ_W], NA_HEADS), dkc, dvc)]
    out_c = merge_branches(ys_c, pc[..., G_OFF:G_OFF + G_W], w_branch, w_out)
    return out_l, out_c


def swiglu(h, w1, w3, w2):
    return (jax.nn.silu(h @ w1) * (h @ w3)) @ w2


def grouped_experts(h, eidx, wsel, w1, w3, w2):
    T = h.shape[0]
    A = T * TOP_K
    n_blocks = -(-(A + N_EXPERTS * (MOE_BLOCK - 1)) // MOE_BLOCK)
    flat_e = eidx.reshape(A).astype(jnp.int32)
    order = jnp.argsort(flat_e)
    e_sorted = flat_e[order]
    tok_sorted = (order // TOP_K).astype(jnp.int32)
    w_sorted = wsel.reshape(A)[order]
    counts = jnp.bincount(flat_e, length=N_EXPERTS)
    padded = (counts + MOE_BLOCK - 1) // MOE_BLOCK * MOE_BLOCK
    pad_end = jnp.cumsum(padded)
    pad_start = pad_end - padded
    grp_start = jnp.cumsum(counts) - counts
    dest = pad_start[e_sorted] + jnp.arange(A, dtype=jnp.int32) - grp_start[e_sorted]
    n_slots = n_blocks * MOE_BLOCK
    slot_tok = jnp.zeros((n_slots,), jnp.int32).at[dest].set(tok_sorted)
    slot_w = jnp.zeros((n_slots,), h.dtype).at[dest].set(w_sorted.astype(h.dtype))
    block_e = jnp.minimum(jnp.searchsorted(pad_end, jnp.arange(n_blocks) * MOE_BLOCK, side='right'),
                          N_EXPERTS - 1)

    def step(acc, blk):
        e, tok, wt = blk
        y = swiglu(h[tok], w1[e], w3[e], w2[e])
        return acc.at[tok].add(y * wt[:, None]), None

    out, _ = lax.scan(step, jnp.zeros_like(h),
                      (block_e, slot_tok.reshape(n_blocks, MOE_BLOCK), slot_w.reshape(n_blocks, MOE_BLOCK)))
    return out


def channel_mixer(h, rw, rb, w1, w3, w2, sw1, sw3, sw2):
    T = h.shape[0]
    scores = jax.nn.sigmoid((h @ rw).astype(jnp.float32))
    biased = scores + rb.astype(jnp.float32)
    grp_score = lax.top_k(biased.reshape(T, N_GROUPS, N_EXPERTS // N_GROUPS), 2)[0].sum(-1)
    _, gidx = lax.top_k(grp_score, TOPK_GROUPS)
    gmask = jnp.zeros((T, N_GROUPS), bool).at[jnp.arange(T)[:, None], gidx].set(True)
    emask = jnp.repeat(gmask, N_EXPERTS // N_GROUPS, axis=1)
    _, eidx = lax.top_k(jnp.where(emask, biased, NEG), TOP_K)
    wsel = jnp.take_along_axis(scores, eidx, axis=1)
    wsel = wsel / jnp.sum(wsel, axis=-1, keepdims=True) * ROUTE_SCALE
    return grouped_experts(h, eidx, wsel, w1, w3, w2) + swiglu(h, sw1, sw3, sw2)


def setup_inputs(seed: int = 0) -> dict:
    key = jax.random.key(seed)
    ks = jax.random.split(key, 32)
    D = D_MODEL

    def n(k, shape, s):
        return jax.random.normal(k, shape, jnp.float32) * s

    return {
        'x': n(ks[0], (BATCH, SEQ, D), 1.0),
        'c': n(ks[1], (BATCH, D), 1.0),
        'ctx': n(ks[2], (BATCH, CTX_LEN, D), 1.0),
        'c_ctx': n(ks[3], (D,), 1.0),
        'w_ada': n(ks[4], (DEPTH, D, 6 * D), 0.5 * D ** -0.5),
        'b_ada': n(ks[5], (DEPTH, 6 * D), 0.02),
        'norm_g': 1.0 + n(ks[6], (DEPTH, 4, D), 0.05),
        'w_in': n(ks[7], (DEPTH, D, IN_COLS), D ** -0.5),
        'conv_b_in': n(ks[8], (DEPTH, 2 * CONV_DIM), 0.02),
        'conv_dw': n(ks[9], (DEPTH, CONV_K, CONV_DIM), CONV_K ** -0.5),
        'conv_dw_b': n(ks[10], (DEPTH, CONV_DIM), 0.02),
        'conv_ln_g': 1.0 + n(ks[11], (DEPTH, CONV_DIM), 0.05),
        'conv_ln_b': n(ks[12], (DEPTH, CONV_DIM), 0.02),
        'gmlp_ln_g': 1.0 + n(ks[13], (DEPTH, GMLP_DIM), 0.05),
        'gmlp_ln_b': n(ks[14], (DEPTH, GMLP_DIM), 0.02),
        'gmlp_ws': n(ks[15], (DEPTH, GMLP_GROUPS, GMLP_CHUNK, GMLP_CHUNK), GMLP_CHUNK ** -0.5),
        'gmlp_bs': 1.0 + n(ks[16], (DEPTH, GMLP_GROUPS, GMLP_CHUNK), 0.1),
        'win_sink': n(ks[17], (DEPTH, WIN_HEADS), 0.5),
        'na_rpb': n(ks[18], (DEPTH, NA_HEADS, 2 * NA_ROWS - 1, 2 * NA_COLS - 1), 0.2),
        'w_branch': n(ks[19], (DEPTH, N_BRANCH, BRANCH_DIM, D), BRANCH_DIM ** -0.5),
        'w_out': n(ks[20], (DEPTH, D, D), D ** -0.5),
        'router_w': n(ks[21], (DEPTH, D, N_EXPERTS), D ** -0.5),
        'router_b': n(ks[22], (DEPTH, N_EXPERTS), 0.01),
        'exp_w1': n(ks[23], (DEPTH, N_EXPERTS, D, EXPERT_DIM), D ** -0.5),
        'exp_w3': n(ks[24], (DEPTH, N_EXPERTS, D, EXPERT_DIM), D ** -0.5),
        'exp_w2': n(ks[25], (DEPTH, N_EXPERTS, EXPERT_DIM, D), EXPERT_DIM ** -0.5),
        'sh_w1': n(ks[26], (DEPTH, D, SHARED_DIM), D ** -0.5),
        'sh_w3': n(ks[27], (DEPTH, D, SHARED_DIM), D ** -0.5),
        'sh_w2': n(ks[28], (DEPTH, SHARED_DIM, D), SHARED_DIM ** -0.5),
    }


def reference(x, c, ctx, c_ctx, w_ada, b_ada, norm_g, w_in, conv_b_in, conv_dw, conv_dw_b,
              conv_ln_g, conv_ln_b, gmlp_ln_g, gmlp_ln_b, gmlp_ws, gmlp_bs, win_sink, na_rpb,
              w_branch, w_out, router_w, router_b, exp_w1, exp_w3, exp_w2, sh_w1, sh_w3, sh_w2):
    bsz, S, D = x.shape
    C = ctx.shape[1]
    xl, xc = x, ctx
    for l in range(DEPTH):
        last = l == DEPTH - 1
        mods_l = jnp.split((jax.nn.silu(c) @ w_ada[l] + b_ada[l])[:, None, :], 6, axis=-1)
        mods_c = jnp.split((jax.nn.silu(c_ctx) @ w_ada[l] + b_ada[l])[None, None, :], 6, axis=-1)
        hl = modulate(rms_norm(xl, norm_g[l, 0]), mods_l[0], mods_l[1])
        hc = modulate(rms_norm(xc, norm_g[l, 0]), mods_c[0], mods_c[1])
        ml, mc = token_mixer(hl, hc, w_in[l], conv_b_in[l], conv_dw[l], conv_dw_b[l], conv_ln_g[l],
                             conv_ln_b[l], gmlp_ln_g[l], gmlp_ln_b[l], gmlp_ws[l], gmlp_bs[l],
                             win_sink[l], na_rpb[l], w_branch[l], w_out[l], not last)
        xl = xl + mods_l[2] * rms_norm(ml, norm_g[l, 1])
        hl = modulate(rms_norm(xl, norm_g[l, 2]), mods_l[3], mods_l[4])
        moe_args = (router_w[l], router_b[l], exp_w1[l], exp_w3[l], exp_w2[l], sh_w1[l], sh_w3[l], sh_w2[l])
        if last:
            fl = channel_mixer(hl.reshape(bsz * S, D), *moe_args).reshape(bsz, S, D)
        else:
            xc = xc + mods_c[2] * rms_norm(mc, norm_g[l, 1])
            hc = modulate(rms_norm(xc, norm_g[l, 2]), mods_c[3], mods_c[4])
            f = channel_mixer(jnp.concatenate([hl.reshape(bsz * S, D), hc.reshape(bsz * C, D)], axis=0), *moe_args)
            fl = f[:bsz * S].reshape(bsz, S, D)
            xc = xc + mods_c[5] * rms_norm(f[bsz * S:].reshape(bsz, C, D), norm_g[l, 3])
        xl = xl + mods_l[5] * rms_norm(fl, norm_g[l, 3])
    return xl
```

```python
import functools

import numpy as np
import jax
import jax.numpy as jnp
from jax import lax
from jax.experimental import pallas as pl
from jax.experimental.pallas import tpu as pltpu

F32 = jnp.float32
BF16 = jnp.bfloat16

GRID_W = 64
HEAD_DIM = 64
EPS = 1e-6
NEG = -1e30
BRANCH_DIM = 512
CONV_K = 31
CONV_HALO = 16
GMLP_CHUNK = 128
GMLP_GROUPS = 4
WIN_HEADS = 8
WIN_KV_HEADS = 2
WINDOW = 128
WIN_BLOCK = 128
ROPE_BASE = 10000.0
NA_HEADS = 8
NA_ROWS = 8
NA_COLS = 16
N_EXPERTS = 64
TOP_K = 8
N_GROUPS = 8
GROUP_SIZE = N_EXPERTS // N_GROUPS
TOPK_GROUPS = 4
ROUTE_SCALE = 2.5
MOE_BLOCK = 256
N_MODS = 6
MODS_ROWS = 16
VMEM_LIMIT = 52 * 1024 * 1024

COL_A, COL_B, COL_CQ, COL_DQ, COL_DK, COL_DV, COL_CK, COL_CV = 0, 1024, 2048, 2560, 3072, 3584, 4096, 4224
SMALL_COLS = 4352
KV_COLS = 1280


def _sigmoid(x):
    return 1.0 / (1.0 + jnp.exp(-x))


def _silu(x):
    return x * _sigmoid(x)


def _gelu_tanh(x):
    return 0.5 * x * (1.0 + jnp.tanh(np.sqrt(2.0 / np.pi).astype(np.float32) * (x + 0.044715 * (x * x * x))))


def _rms(x, g):
    return x * lax.rsqrt(jnp.mean(x * x, axis=-1, keepdims=True) + EPS) * g


def _ln(x, g, b):
    mu = jnp.mean(x, axis=-1, keepdims=True)
    xc = x - mu
    var = jnp.mean(xc * xc, axis=-1, keepdims=True)
    return xc * lax.rsqrt(var + EPS) * g + b


def _dot(a, b):
    return jnp.dot(a, b, preferred_element_type=F32)


def _dot_nt(a, b):
    return lax.dot_general(a, b, (((1,), (1,)), ((), ())), preferred_element_type=F32)


def _params(sem):
    return pltpu.CompilerParams(dimension_semantics=sem, vmem_limit_bytes=VMEM_LIMIT)


def _ada_kernel(c_ref, w_ref, b_ref, o_ref):
    s = _silu(c_ref[...])
    o_ref[0] = _dot(s.astype(BF16), w_ref[0].astype(BF16)) + b_ref[0]


def _ada_call(c_all, w_ada, b_ada):
    depth, d, n = w_ada.shape
    tn = 512
    return pl.pallas_call(
        _ada_kernel,
        grid=(depth, n // tn),
        in_specs=[pl.BlockSpec((MODS_ROWS, d), lambda l, j: (0, 0)),
                  pl.BlockSpec((1, d, tn), lambda l, j: (l, 0, j)),
                  pl.BlockSpec((1, 1, tn), lambda l, j: (l, 0, j))],
        out_specs=pl.BlockSpec((1, MODS_ROWS, tn), lambda l, j: (l, 0, j)),
        out_shape=jax.ShapeDtypeStruct((depth, MODS_ROWS, n), F32),
        compiler_params=_params(("arbitrary", "arbitrary")),
        name="ada_mods",
    )(c_all, w_ada, b_ada.reshape(depth, 1, n))


def _proj_kernel(x_ref, g_ref, sh_ref, sc_ref, w_ref, o_ref):
    h = _rms(x_ref[0], g_ref[0:1, :]) * (1.0 + sc_ref[0]) + sh_ref[0]
    o_ref[0] = _dot(h.astype(BF16), w_ref[...])


def _proj_call(x, norm_g, mods, w, shared_row, tn):
    bsz, s, d = x.shape
    n = w.shape[1]
    tm = min(s, 512)
    row = (lambda b: b) if shared_row is None else (lambda b: shared_row)
    return pl.pallas_call(
        _proj_kernel,
        grid=(n // tn, bsz, s // tm),
        in_specs=[pl.BlockSpec((1, tm, d), lambda j, b, i: (b, i, 0)),
                  pl.BlockSpec((4, d), lambda j, b, i: (0, 0)),
                  pl.BlockSpec((1, 1, d), lambda j, b, i: (row(b), 0, 0)),
                  pl.BlockSpec((1, 1, d), lambda j, b, i: (row(b), 0, 1)),
                  pl.BlockSpec((d, tn), lambda j, b, i: (0, j))],
        out_specs=pl.BlockSpec((1, tm, tn), lambda j, b, i: (b, i, j)),
        out_shape=jax.ShapeDtypeStruct((bsz, s, n), F32),
        compiler_params=_params(("arbitrary", "arbitrary", "arbitrary")),
        name="proj_in",
    )(x, norm_g, mods, mods, w)


def _conv_kernel(a_ref, ap_ref, an_ref, bin_ref, dw_ref, dwb_ref, g_ref, b_ref, o_ref, ypad_ref):
    i = pl.program_id(1)
    nblk = pl.num_programs(1)
    ts = a_ref.shape[1]
    c = BRANCH_DIM

    def glu(a):
        a = a + bin_ref[...]
        return a[:, :c] * _sigmoid(a[:, c:])

    ypad_ref[0:CONV_HALO, :] = jnp.where(i > 0, glu(ap_ref[0]), 0.0)
    ypad_ref[CONV_HALO:CONV_HALO + ts, :] = glu(a_ref[0])
    ypad_ref[CONV_HALO + ts:2 * CONV_HALO + ts, :] = jnp.where(i < nblk - 1, glu(an_ref[0]), 0.0)
    rc = 64
    off = CONV_HALO - CONV_K // 2
    for r0 in range(0, ts, rc):
        acc = jnp.zeros((rc, c), F32)
        for j in range(CONV_K):
            acc = acc + ypad_ref[r0 + off + j:r0 + off + j + rc, :] * dw_ref[j:j + 1, :]
        y = _ln(acc + dwb_ref[...], g_ref[...], b_ref[...])
        o_ref[0, r0:r0 + rc, :] = _silu(y).astype(o_ref.dtype)


def _conv_call(pl_all, b_in, dw, dw_b, ln_g, ln_b):
    bsz, s, _ = pl_all.shape
    ts = min(s, 512)
    c = BRANCH_DIM
    hb = ts // CONV_HALO
    last = s // CONV_HALO - 1
    dw_pad = jnp.zeros((32, c), F32).at[:CONV_K].set(dw)
    return pl.pallas_call(
        _conv_kernel,
        grid=(bsz, s // ts),
        in_specs=[pl.BlockSpec((1, ts, 2 * c), lambda b, i: (b, i, 0)),
                  pl.BlockSpec((1, CONV_HALO, 2 * c), lambda b, i: (b, jnp.maximum(i * hb - 1, 0), 0)),
                  pl.BlockSpec((1, CONV_HALO, 2 * c), lambda b, i: (b, jnp.minimum((i + 1) * hb, last), 0)),
                  pl.BlockSpec((1, 2 * c), lambda b, i: (0, 0)),
                  pl.BlockSpec((32, c), lambda b, i: (0, 0)),
                  pl.BlockSpec((1, c), lambda b, i: (0, 0)),
                  pl.BlockSpec((1, c), lambda b, i: (0, 0)),
                  pl.BlockSpec((1, c), lambda b, i: (0, 0))],
        out_specs=pl.BlockSpec((1, ts, c), lambda b, i: (b, i, 0)),
        out_shape=jax.ShapeDtypeStruct((bsz, s, c), BF16),
        scratch_shapes=[pltpu.VMEM((ts + 2 * CONV_HALO, c), F32)],
        compiler_params=_params(("arbitrary", "arbitrary")),
        name="mixer_conv",
    )(pl_all, pl_all, pl_all, b_in.reshape(1, 2 * c), dw_pad, dw_b.reshape(1, c), ln_g.reshape(1, c), ln_b.reshape(1, c))


def _gmlp_kernel(z_ref, g_ref, b_ref, ws_ref, bs_ref, o_ref):
    c = BRANCH_DIM
    gw = c // GMLP_GROUPS
    ts = z_ref.shape[1]
    for n in range(ts // GMLP_CHUNK):
        rows = slice(n * GMLP_CHUNK, (n + 1) * GMLP_CHUNK)
        z = _gelu_tanh(z_ref[0, rows, :])
        u = z[:, :c]
        v = _ln(z[:, c:], g_ref[...], b_ref[...]).astype(BF16)
        for g in range(GMLP_GROUPS):
            cols = slice(g * gw, (g + 1) * gw)
            sg = _dot(ws_ref[g], v[:, cols]) + bs_ref[:, g:g + 1]
            o_ref[0, rows, cols] = (u[:, cols] * sg).astype(o_ref.dtype)


def _gmlp_call(pl_all, ln_g, ln_b, ws, bs):
    bsz, s, _ = pl_all.shape
    ts = min(s, 512)
    c = BRANCH_DIM
    return pl.pallas_call(
        _gmlp_kernel,
        grid=(bsz, s // ts),
        in_specs=[pl.BlockSpec((1, ts, 2 * c), lambda b, i: (b, i, COL_B // (2 * c))),
                  pl.BlockSpec((1, c), lambda b, i: (0, 0)),
                  pl.BlockSpec((1, c), lambda b, i: (0, 0)),
                  pl.BlockSpec((GMLP_GROUPS, GMLP_CHUNK, GMLP_CHUNK), lambda b, i: (0, 0, 0)),
                  pl.BlockSpec((GMLP_CHUNK, GMLP_GROUPS), lambda b, i: (0, 0))],
        out_specs=pl.BlockSpec((1, ts, c), lambda b, i: (b, i, 0)),
        out_shape=jax.ShapeDtypeStruct((bsz, s, c), BF16),
        compiler_params=_params(("arbitrary", "arbitrary")),
        name="mixer_gmlp",
    )(pl_all, ln_g.reshape(1, c), ln_b.reshape(1, c), ws.astype(BF16), bs.T)


def _rope_tables(s):
    half, quarter = HEAD_DIM // 2, HEAD_DIM // 4
    pos = np.arange(s)
    prow = (pos // GRID_W).astype(np.float32)
    pcol = (pos % GRID_W).astype(np.float32)
    inv = (ROPE_BASE ** (-np.arange(quarter, dtype=np.float32) / quarter)).astype(np.float32)
    lane = np.arange(128)
    in_head = lane % HEAD_DIM
    p = np.where((in_head < half)[None, :], prow[:, None], pcol[:, None]).astype(np.float32)
    ang = (p * inv[lane % quarter][None, :]).astype(np.float32).astype(np.float64)
    first = (in_head % half) < quarter
    cos = np.cos(ang)
    sin = np.where(first[None, :], -np.sin(ang), np.sin(ang))
    swap = np.zeros((128, 128), np.float32)
    for j in lane:
        swap[j + quarter if first[j] else j - quarter, j] = 1.0
    return cos.astype(np.float32), sin.astype(np.float32), swap


def _rope_kernel(q_ref, k_ref, cos_ref, sin_ref, swap_ref, qo_ref, ko_ref):
    cos, sin, swap = cos_ref[...], sin_ref[...], swap_ref[...]

    def rot(x):
        parts = []
        for t in range(x.shape[1] // 128):
            xt = x[:, t * 128:(t + 1) * 128]
            parts.append(xt * cos + _dot(xt.astype(BF16), swap) * sin)
        return parts[0] if len(parts) == 1 else jnp.concatenate(parts, axis=1)

    qo_ref[0] = rot(q_ref[0])
    ko_ref[0] = rot(k_ref[0])


def _rope_call(pl_all):
    bsz, s, _ = pl_all.shape
    tm = 512
    cos, sin, swap = _rope_tables(s)
    qw, kw = WIN_HEADS * HEAD_DIM, WIN_KV_HEADS * HEAD_DIM
    return pl.pallas_call(
        _rope_kernel,
        grid=(bsz, s // tm),
        in_specs=[pl.BlockSpec((1, tm, qw), lambda b, i: (b, i, COL_CQ // qw)),
                  pl.BlockSpec((1, tm, kw), lambda b, i: (b, i, COL_CK // kw)),
                  pl.BlockSpec((tm, 128), lambda b, i: (i, 0)),
                  pl.BlockSpec((tm, 128), lambda b, i: (i, 0)),
                  pl.BlockSpec((128, 128), lambda b, i: (0, 0))],
        out_specs=[pl.BlockSpec((1, tm, qw), lambda b, i: (b, i, 0)),
                   pl.BlockSpec((1, tm, kw), lambda b, i: (b, i, 0))],
        out_shape=[jax.ShapeDtypeStruct((bsz, s, qw), F32), jax.ShapeDtypeStruct((bsz, s, kw), F32)],
        compiler_params=_params(("arbitrary", "arbitrary")),
        name="rope",
    )(pl_all, pl_all, jnp.asarray(cos), jnp.asarray(sin), jnp.asarray(swap, BF16))


def _win_kernel(sink_ref, q_ref, kp_ref, kc_ref, kn_ref, vp_ref, vc_ref, vn_ref, kx_ref, vx_ref, o_ref, *, seq):
    g = pl.program_id(1)
    n = pl.program_id(2)
    grp = q_ref.shape[1]
    wb = WIN_BLOCK
    k_loc = jnp.concatenate([kp_ref[0, 0], kc_ref[0, 0], kn_ref[0, 0]], axis=0).astype(BF16)
    v_loc = jnp.concatenate([vp_ref[0, 0], vc_ref[0, 0], vn_ref[0, 0]], axis=0).astype(BF16)
    k_ctx = kx_ref[0, 0].astype(BF16)
    v_ctx = vx_ref[0, 0].astype(BF16)
    qpos = n * wb + lax.broadcasted_iota(jnp.int32, (wb, 3 * wb), 0)
    kpos = (n - 1) * wb + lax.broadcasted_iota(jnp.int32, (wb, 3 * wb), 1)
    mask = (jnp.abs(qpos - kpos) <= WINDOW) & (kpos >= 0) & (kpos < seq)
    for h in range(grp):
        q = (q_ref[0, h] * (HEAD_DIM ** -0.5)).astype(BF16)
        s_loc = jnp.where(mask, _dot_nt(q, k_loc), NEG)
        s_ctx = _dot_nt(q, k_ctx)
        sink = sink_ref[g * grp + h]
        m = jnp.maximum(jnp.maximum(jnp.max(s_loc, axis=-1, keepdims=True),
                                    jnp.max(s_ctx, axis=-1, keepdims=True)), sink)
        p_loc = jnp.exp(s_loc - m)
        p_ctx = jnp.exp(s_ctx - m)
        den = (jnp.sum(p_loc, axis=-1, keepdims=True) + jnp.sum(p_ctx, axis=-1, keepdims=True)
               + jnp.exp(sink - m))
        o = _dot(p_loc.astype(BF16), v_loc) + _dot(p_ctx.astype(BF16), v_ctx)
        o_ref[0, h] = (o / den).astype(o_ref.dtype)


def _win_call(q, k, v, kx, vx, sink):
    bsz, hq, s, dh = q.shape
    hkv = k.shape[1]
    grp = hq // hkv
    cx = kx.shape[2]
    nb = s // WIN_BLOCK
    blk = (1, 1, WIN_BLOCK, dh)
    prev = lambda b, g, n, sk: (b, g, jnp.maximum(n - 1, 0), 0)
    cur = lambda b, g, n, sk: (b, g, n, 0)
    nxt = lambda b, g, n, sk: (b, g, jnp.minimum(n + 1, nb - 1), 0)
    ctx = lambda b, g, n, sk: (b, g, 0, 0)
    return pl.pallas_call(
        functools.partial(_win_kernel, seq=s),
        grid_spec=pltpu.PrefetchScalarGridSpec(
            num_scalar_prefetch=1,
            grid=(bsz, hkv, nb),
            in_specs=[pl.BlockSpec((1, grp, WIN_BLOCK, dh), cur),
                      pl.BlockSpec(blk, prev), pl.BlockSpec(blk, cur), pl.BlockSpec(blk, nxt),
                      pl.BlockSpec(blk, prev), pl.BlockSpec(blk, cur), pl.BlockSpec(blk, nxt),
                      pl.BlockSpec((1, 1, cx, dh), ctx), pl.BlockSpec((1, 1, cx, dh), ctx)],
            out_specs=pl.BlockSpec((1, grp, WIN_BLOCK, dh), cur)),
        out_shape=jax.ShapeDtypeStruct((bsz, hq, s, dh), BF16),
        compiler_params=_params(("arbitrary", "arbitrary", "arbitrary")),
        name="mixer_window_attn",
    )(sink, q, k, k, k, v, v, v, kx, vx)


def _na_bias_table(rpb):
    qc = np.arange(GRID_W)
    kc = np.arange(GRID_W)
    cs = np.clip(qc - NA_COLS // 2, 0, GRID_W - NA_COLS)
    cmask = (kc[None, :] >= cs[:, None]) & (kc[None, :] < cs[:, None] + NA_COLS)
    ci = np.clip(kc[None, :] - qc[:, None] + NA_COLS - 1, 0, 2 * NA_COLS - 2)
    ri = np.arange(NA_ROWS)[:, None] + np.arange(NA_ROWS)[None, :]
    tab = rpb[:, ri[:, None, :, None], ci[None, :, None, :]]
    tab = jnp.where(cmask[None, None, :, None, :], tab, NEG)
    return tab.reshape(rpb.shape[0], NA_ROWS, GRID_W, NA_ROWS * GRID_W)


def _na_kernel(q_ref, k_ref, v_ref, kx_ref, vx_ref, bias_ref, o_ref):
    rows = q_ref.shape[2] // GRID_W
    k_ctx = kx_ref[0, 0].astype(BF16)
    v_ctx = vx_ref[0, 0].astype(BF16)
    span = NA_ROWS * GRID_W

    def body(r, carry):
        rs = jnp.clip(r - NA_ROWS // 2, 0, rows - NA_ROWS)
        q0 = pl.multiple_of(r * GRID_W, GRID_W)
        k0 = pl.multiple_of(rs * GRID_W, GRID_W)
        q = (q_ref[0, 0, pl.ds(q0, GRID_W), :] * (HEAD_DIM ** -0.5)).astype(BF16)
        k_loc = k_ref[0, 0, pl.ds(k0, span), :].astype(BF16)
        v_loc = v_ref[0, 0, pl.ds(k0, span), :].astype(BF16)
        s_loc = _dot_nt(q, k_loc) + bias_ref[0, rs - r + NA_ROWS - 1]
        s_ctx = _dot_nt(q, k_ctx)
        m = jnp.maximum(jnp.max(s_loc, axis=-1, keepdims=True), jnp.max(s_ctx, axis=-1, keepdims=True))
        p_loc = jnp.exp(s_loc - m)
        p_ctx = jnp.exp(s_ctx - m)
        den = jnp.sum(p_loc, axis=-1, keepdims=True) + jnp.sum(p_ctx, axis=-1, keepdims=True)
        o = _dot(p_loc.astype(BF16), v_loc) + _dot(p_ctx.astype(BF16), v_ctx)
        o_ref[0, 0, pl.ds(q0, GRID_W), :] = (o / den).astype(o_ref.dtype)
        return carry

    lax.fori_loop(0, rows, body, 0)


def _na_call(q, k, v, kx, vx, bias_tab):
    bsz, h, s, dh = q.shape
    cx = kx.shape[2]
    full = pl.BlockSpec((1, 1, s, dh), lambda b, hh: (b, hh, 0, 0))
    ctx = pl.BlockSpec((1, 1, cx, dh), lambda b, hh: (b, hh, 0, 0))
    return pl.pallas_call(
        _na_kernel,
        grid=(bsz, h),
        in_specs=[full, full, full, ctx, ctx,
                  pl.BlockSpec((1, NA_ROWS, GRID_W, NA_ROWS * GRID_W), lambda b, hh: (hh, 0, 0, 0))],
        out_specs=full,
        out_shape=jax.ShapeDtypeStruct((bsz, h, s, dh), BF16),
        compiler_params=_params(("arbitrary", "arbitrary")),
        name="mixer_neighbourhood_attn",
    )(q, k, v, kx, vx, bias_tab)


def _ctx_attn_kernel(sink_ref, q_ref, k_ref, v_ref, o_ref, *, use_sink):
    h = pl.program_id(1)
    q = (q_ref[0, 0] * (HEAD_DIM ** -0.5)).astype(BF16)
    s = _dot_nt(q, k_ref[0, 0].astype(BF16))
    m = jnp.max(s, axis=-1, keepdims=True)
    if use_sink:
        sink = sink_ref[h]
        m = jnp.maximum(m, sink)
    p = jnp.exp(s - m)
    den = jnp.sum(p, axis=-1, keepdims=True)
    if use_sink:
        den = den + jnp.exp(sink - m)
    o_ref[0, 0] = (_dot(p.astype(BF16), v_ref[0, 0].astype(BF16)) / den).astype(o_ref.dtype)


def _ctx_attn_call(q, k, v, sink):
    bsz, hq, cx, dh = q.shape
    grp = hq // k.shape[1]
    use_sink = sink is not None
    if sink is None:
        sink = jnp.zeros((hq,), F32)
    qs = pl.BlockSpec((1, 1, cx, dh), lambda b, h, sk: (b, h, 0, 0))
    ks = pl.BlockSpec((1, 1, cx, dh), lambda b, h, sk: (b, h // grp, 0, 0))
    return pl.pallas_call(
        functools.partial(_ctx_attn_kernel, use_sink=use_sink),
        grid_spec=pltpu.PrefetchScalarGridSpec(
            num_scalar_prefetch=1, grid=(bsz, hq), in_specs=[qs, ks, ks], out_specs=qs),
        out_shape=jax.ShapeDtypeStruct((bsz, hq, cx, dh), BF16),
        compiler_params=_params(("arbitrary", "arbitrary")),
        name="context_attn",
    )(sink, q, k, v)


def _merge_kernel(x_ref, ya_ref, yb_ref, yc_ref, yd_ref, wg0_ref, wg1_ref, wg2_ref, wg3_ref, wbr_ref, wout_ref,
                  ng_ref, sh_ref, sc_ref, gt_ref, o_ref, h_ref):
    n = pl.program_id(2)

    @pl.when(n == 0)
    def _():
        h = _rms(x_ref[0], ng_ref[0:1, :]) * (1.0 + sc_ref[0]) + sh_ref[0]
        h_ref[...] = h.astype(BF16)
        o_ref[0] = jnp.zeros(o_ref.shape[1:], F32)

    h = h_ref[...]
    acc = None
    for i, (y_ref, wg_ref) in enumerate(((ya_ref, wg0_ref), (yb_ref, wg1_ref), (yc_ref, wg2_ref), (yd_ref, wg3_ref))):
        t = _sigmoid(_dot(h, wg_ref[...])) * _dot(y_ref[0], wbr_ref[i])
        acc = t if acc is None else acc + t
    o_ref[0] += _dot(acc.astype(BF16), wout_ref[...])

    @pl.when(n == pl.num_programs(2) - 1)
    def _():
        o_ref[0] = x_ref[0] + gt_ref[0] * _rms(o_ref[0], ng_ref[1:2, :])


def _merge_call(x, ys, w_gate, w_branch, w_out, norm_g, mods, shared_row):
    bsz, s, d = x.shape
    c = BRANCH_DIM
    tm = min(s, 512)
    tn = 256
    nt = d // tn
    row = (lambda b: b) if shared_row is None else (lambda b: shared_row)
    ysp = pl.BlockSpec((1, tm, c), lambda b, i, n: (b, i, 0))
    wg = [pl.BlockSpec((d, tn), functools.partial(lambda b, i, n, k: (0, k * nt + n), k=k)) for k in range(4)]
    mod = lambda col: pl.BlockSpec((1, 1, d), lambda b, i, n: (row(b), 0, col))
    return pl.pallas_call(
        _merge_kernel,
        grid=(bsz, s // tm, nt),
        in_specs=[pl.BlockSpec((1, tm, d), lambda b, i, n: (b, i, 0)), ysp, ysp, ysp, ysp, *wg,
                  pl.BlockSpec((4, c, tn), lambda b, i, n: (0, 0, n)),
                  pl.BlockSpec((tn, d), lambda b, i, n: (n, 0)),
                  pl.BlockSpec((4, d), lambda b, i, n: (0, 0)),
                  mod(0), mod(1), mod(2)],
        out_specs=pl.BlockSpec((1, tm, d), lambda b, i, n: (b, i, 0)),
        out_shape=jax.ShapeDtypeStruct((bsz, s, d), F32),
        scratch_shapes=[pltpu.VMEM((tm, d), BF16)],
        compiler_params=_params(("arbitrary", "arbitrary", "arbitrary")),
        name="mixer_merge",
    )(x, *ys, w_gate, w_gate, w_gate, w_gate, w_branch, w_out, norm_g, mods, mods, mods)


def _expert_row_ids():
    r = np.arange(N_EXPERTS)
    return ((r % N_GROUPS) * GROUP_SIZE + r // N_GROUPS).astype(np.int32)


def _router_kernel(rows_ref, x_ref, ng_ref, sh_ref, sc_ref, rw_ref, rb_ref, eid_ref, tri_e_ref, tri_t_ref,
                   sw1_ref, sw3_ref, sw2_ref, h_ref, so_ref, e_ref, r_ref, w_ref, cnt_ref):
    i = pl.program_id(0)
    tm = x_ref.shape[0]

    @pl.when(i == 0)
    def _():
        cnt_ref[...] = jnp.zeros(cnt_ref.shape, F32)

    h = _rms(x_ref[...], ng_ref[2:3, :]) * (1.0 + sc_ref[0]) + sh_ref[0]
    h_ref[...] = h
    hb = h.astype(BF16)
    so_ref[...] = _dot((_silu(_dot(hb, sw1_ref[...])) * _dot(hb, sw3_ref[...])).astype(BF16), sw2_ref[...])

    logits = lax.dot_general(rw_ref[...], h, (((1,), (1,)), ((), ())),
                             precision=lax.Precision.HIGHEST, preferred_element_type=F32)
    scores = _sigmoid(logits)
    biased = scores + rb_ref[...]
    m1 = jnp.full((N_GROUPS, tm), -jnp.inf, F32)
    m2 = m1
    for j in range(GROUP_SIZE):
        v = biased[j * N_GROUPS:(j + 1) * N_GROUPS, :]
        m2 = jnp.maximum(m2, jnp.minimum(m1, v))
        m1 = jnp.maximum(m1, v)
    gs = m1 + m2
    gid = lax.broadcasted_iota(jnp.int32, (N_GROUPS, tm), 0)
    beat = jnp.zeros((N_GROUPS, tm), jnp.int32)
    for g in range(N_GROUPS):
        o = gs[g:g + 1, :]
        beat = beat + jnp.where((o > gs) | ((o == gs) & (g < gid)), 1, 0)
    keep = beat < TOPK_GROUPS
    masked = jnp.concatenate(
        [jnp.where(keep, biased[j * N_GROUPS:(j + 1) * N_GROUPS, :], NEG) for j in range(GROUP_SIZE)], axis=0)
    eid = eid_ref[...]
    row_ids = _expert_row_ids()
    beat = jnp.zeros((N_EXPERTS, tm), jnp.int32)
    for r in range(N_EXPERTS):
        o = masked[r:r + 1, :]
        beat = beat + jnp.where((o > masked) | ((o == masked) & (int(row_ids[r]) < eid)), 1, 0)
    sel = beat < TOP_K
    self32 = jnp.where(sel, 1.0, 0.0)
    selb = self32.astype(BF16)
    wsel = jnp.where(sel, scores, 0.0)
    wsel = wsel / jnp.sum(wsel, axis=0, keepdims=True) * ROUTE_SCALE
    rank = cnt_ref[:, 0:1] + _dot(selb, tri_t_ref[...])
    cnt_ref[...] = cnt_ref[...] + jnp.sum(self32, axis=1, keepdims=True)
    slot = _dot(tri_e_ref[...], selb)
    eidf = eid.astype(F32)
    for k in range(TOP_K):
        mk = sel & (slot == float(k))
        e_ref[k:k + 1, :] = jnp.sum(jnp.where(mk, eidf, 0.0), axis=0, keepdims=True).astype(jnp.int32)
        r_ref[k:k + 1, :] = jnp.sum(jnp.where(mk, rank, 0.0), axis=0, keepdims=True).astype(jnp.int32)
        w_ref[k:k + 1, :] = jnp.sum(jnp.where(mk, wsel, 0.0), axis=0, keepdims=True)


def _router_call(xcat, tile_rows, norm_g, mods, router_w, router_b, sw1, sw3, sw2):
    t, d = xcat.shape
    tm = 256
    f = sw1.shape[1]
    ids = _expert_row_ids()
    rw = router_w.T[ids]
    rb = router_b[ids].reshape(N_EXPERTS, 1)
    tri_e = jnp.asarray(np.tril(np.ones((N_EXPERTS, N_EXPERTS), np.float32), -1), BF16)
    tri_t = jnp.asarray(np.triu(np.ones((tm, tm), np.float32), 1), BF16)
    const = lambda shape: pl.BlockSpec(shape, lambda i, rows: (0,) * len(shape))
    tok = pl.BlockSpec((tm, d), lambda i, rows: (i, 0))
    slots = pl.BlockSpec((TOP_K, tm), lambda i, rows: (0, i))
    return pl.pallas_call(
        _router_kernel,
        grid_spec=pltpu.PrefetchScalarGridSpec(
            num_scalar_prefetch=1,
            grid=(t // tm,),
            in_specs=[tok, const((4, d)),
                      pl.BlockSpec((1, 1, d), lambda i, rows: (rows[i], 0, 3)),
                      pl.BlockSpec((1, 1, d), lambda i, rows: (rows[i], 0, 4)),
                      const((N_EXPERTS, d)), const((N_EXPERTS, 1)), const((N_EXPERTS, 1)),
                      const((N_EXPERTS, N_EXPERTS)), const((tm, tm)),
                      const((d, f)), const((d, f)), const((f, d))],
            out_specs=[tok, tok, slots, slots, slots, const((N_EXPERTS, 128))]),
        out_shape=[jax.ShapeDtypeStruct((t, d), F32), jax.ShapeDtypeStruct((t, d), F32),
                   jax.ShapeDtypeStruct((TOP_K, t), jnp.int32), jax.ShapeDtypeStruct((TOP_K, t), jnp.int32),
                   jax.ShapeDtypeStruct((TOP_K, t), F32), jax.ShapeDtypeStruct((N_EXPERTS, 128), F32)],
        compiler_params=_params(("arbitrary",)),
        name="moe_router",
    )(tile_rows, xcat, norm_g, mods, mods, rw, rb, jnp.asarray(ids).reshape(N_EXPERTS, 1), tri_e, tri_t,
      sw1, sw3, sw2)


def _dispatch_kernel(dest_ref, h_hbm, xs_hbm, sem):
    i = pl.program_id(0)
    td = dest_ref.shape[2] // TOP_K

    def copy(j, k):
        return pltpu.make_async_copy(h_hbm.at[pl.ds(i * td + j, 1)], xs_hbm.at[pl.ds(dest_ref[0, 0, k * td + j], 1)], sem)

    def start(j, c):
        for k in range(TOP_K):
            copy(j, k).start()
        return c

    def wait(j, c):
        for k in range(TOP_K):
            copy(j, k).wait()
        return c

    lax.fori_loop(0, td, start, 0)
    lax.fori_loop(0, td, wait, 0)


def _dispatch_call(h, dest):
    t, d = h.shape
    td = 256
    dest3 = dest.reshape(TOP_K, t // td, td).transpose(1, 0, 2).reshape(t // td, 1, TOP_K * td)
    return pl.pallas_call(
        _dispatch_kernel,
        grid=(t // td,),
        in_specs=[pl.BlockSpec((1, 1, TOP_K * td), lambda i: (i, 0, 0), memory_space=pltpu.SMEM),
                  pl.BlockSpec(memory_space=pl.ANY)],
        out_specs=pl.BlockSpec(memory_space=pl.ANY),
        out_shape=jax.ShapeDtypeStruct((t * TOP_K, d), F32),
        scratch_shapes=[pltpu.SemaphoreType.DMA],
        compiler_params=_params(("arbitrary",)),
        name="moe_dispatch",
    )(dest3, h)


def _gmm_kernel(blk_ref, exp_ref, lo_ref, hi_ref, first_ref, x_ref, w1_ref, w3_ref, w2_ref, o_ref):
    i = pl.program_id(0)
    lo = lo_ref[i]
    hi = hi_ref[i]

    @pl.when(first_ref[i] == 1)
    def _():
        o_ref[...] = jnp.zeros(o_ref.shape, F32)

    @pl.when(hi > lo)
    def _():
        x = x_ref[...].astype(BF16)
        a = (_silu(_dot(x, w1_ref[0])) * _dot(x, w3_ref[0])).astype(BF16)
        y = _dot(a, w2_ref[0])
        rows = lax.broadcasted_iota(jnp.int32, (x.shape[0], 1), 0)
        o_ref[...] += jnp.where((rows >= lo) & (rows < hi), y, 0.0)


def _gmm_call(items, xs, w1, w3, w2):
    a, d = xs.shape
    f = w1.shape[2]
    n_items = items[0].shape[0]
    rows = pl.BlockSpec((MOE_BLOCK, d), lambda i, blk, ex, lo, hi, fi: (blk[i], 0))
    return pl.pallas_call(
        _gmm_kernel,
        grid_spec=pltpu.PrefetchScalarGridSpec(
            num_scalar_prefetch=5,
            grid=(n_items,),
            in_specs=[rows,
                      pl.BlockSpec((1, d, f), lambda i, blk, ex, lo, hi, fi: (ex[i], 0, 0)),
                      pl.BlockSpec((1, d, f), lambda i, blk, ex, lo, hi, fi: (ex[i], 0, 0)),
                      pl.BlockSpec((1, f, d), lambda i, blk, ex, lo, hi, fi: (ex[i], 0, 0))],
            out_specs=rows),
        out_shape=jax.ShapeDtypeStruct((a, d), F32),
        compiler_params=_params(("arbitrary",)),
        name="moe_experts",
    )(*items, xs, w1, w3, w2)


def _moe_items(counts, n_rows):
    nb = n_rows // MOE_BLOCK
    ends = jnp.cumsum(counts)
    pts = jnp.sort(jnp.concatenate([jnp.arange(nb, dtype=jnp.int32) * MOE_BLOCK, ends.astype(jnp.int32)]))
    start = pts
    stop = jnp.concatenate([pts[1:], jnp.full((1,), n_rows, jnp.int32)])
    blk = jnp.minimum(start // MOE_BLOCK, nb - 1)
    ex = jnp.minimum(jnp.searchsorted(ends, start, side='right'), N_EXPERTS - 1).astype(jnp.int32)
    lo = start - blk * MOE_BLOCK
    hi = stop - blk * MOE_BLOCK
    first = jnp.concatenate([jnp.ones((1,), jnp.int32), (blk[1:] != blk[:-1]).astype(jnp.int32)])
    return blk.astype(jnp.int32), ex, lo.astype(jnp.int32), hi.astype(jnp.int32), first


def _combine_kernel(rows_ref, dest_ref, ys_hbm, w_ref, so_ref, x_ref, ng_ref, gt_ref, o_ref, buf_ref, sem):
    tc = x_ref.shape[0]

    def copy(j, k):
        return pltpu.make_async_copy(ys_hbm.at[pl.ds(dest_ref[0, 0, k * tc + j], 1)], buf_ref.at[k, pl.ds(j, 1)], sem)

    def start(j, c):
        for k in range(TOP_K):
            copy(j, k).start()
        return c

    def wait(j, c):
        for k in range(TOP_K):
            copy(j, k).wait()
        return c

    lax.fori_loop(0, tc, start, 0)
    lax.fori_loop(0, tc, wait, 0)
    fl = so_ref[...]
    for k in range(TOP_K):
        fl = fl + buf_ref[k] * w_ref[:, k:k + 1]
    o_ref[...] = x_ref[...] + gt_ref[0] * _rms(fl, ng_ref[3:4, :])


def _combine_call(tile_rows, dest, ys, wsel, shared_out, xcat, norm_g, mods):
    t, d = xcat.shape
    tc = 128
    dest3 = dest.reshape(TOP_K, t // tc, tc).transpose(1, 0, 2).reshape(t // tc, 1, TOP_K * tc)
    tok = pl.BlockSpec((tc, d), lambda i, rows: (i, 0))
    return pl.pallas_call(
        _combine_kernel,
        grid_spec=pltpu.PrefetchScalarGridSpec(
            num_scalar_prefetch=1,
            grid=(t // tc,),
            in_specs=[pl.BlockSpec((1, 1, TOP_K * tc), lambda i, rows: (i, 0, 0), memory_space=pltpu.SMEM),
                      pl.BlockSpec(memory_space=pl.ANY),
                      pl.BlockSpec((tc, TOP_K), lambda i, rows: (i, 0)),
                      tok, tok,
                      pl.BlockSpec((4, d), lambda i, rows: (0, 0)),
                      pl.BlockSpec((1, 1, d), lambda i, rows: (rows[i], 0, 5))],
            out_specs=tok,
            scratch_shapes=[pltpu.VMEM((TOP_K, tc, d), F32), pltpu.SemaphoreType.DMA]),
        out_shape=jax.ShapeDtypeStruct((t, d), F32),
        compiler_params=_params(("arbitrary",)),
        name="moe_combine",
    )(tile_rows, dest3, ys, wsel.T, shared_out, xcat, norm_g, mods)


def _channel_mixer(xcat, mod_rows, norm_g, mods, router_w, router_b, w1, w3, w2, sw1, sw3, sw2):
    t, _ = xcat.shape
    h, shared_out, eidx, rank, wsel, cnt = _router_call(xcat, mod_rows[::2], norm_g, mods, router_w, router_b,
                                                         sw1, sw3, sw2)
    counts = jnp.zeros((N_EXPERTS,), jnp.int32).at[_expert_row_ids()].set(cnt[:, 0].astype(jnp.int32))
    starts = jnp.cumsum(counts) - counts
    dest = starts[eidx] + rank
    xs = _dispatch_call(h, dest)
    ys = _gmm_call(_moe_items(counts, t * TOP_K), xs, w1, w3, w2)
    return _combine_call(mod_rows, dest, ys, wsel, shared_out, xcat, norm_g, mods)


def _heads(t, n):
    b, s, _ = t.shape
    return t.reshape(b, s, n, HEAD_DIM).transpose(0, 2, 1, 3)


def _unheads(t):
    b, n, s, dh = t.shape
    return t.transpose(0, 2, 1, 3).reshape(b, s, n * dh)


def kernel(x, c, ctx, c_ctx, w_ada, b_ada, norm_g, w_in, conv_b_in, conv_dw, conv_dw_b, conv_ln_g, conv_ln_b, gmlp_ln_g, gmlp_ln_b, gmlp_ws, gmlp_bs, win_sink, na_rpb, w_branch, w_out, router_w, router_b, exp_w1, exp_w3, exp_w2, sh_w1, sh_w3, sh_w2):
    bsz, s, d = x.shape
    cx = ctx.shape[1]
    depth = w_ada.shape[0]
    ctx_row = bsz
    c_all = jnp.zeros((MODS_ROWS, d), F32).at[:bsz].set(c).at[ctx_row].set(c_ctx)
    mods_all = _ada_call(c_all, w_ada, b_ada)
    lat_rows = jnp.repeat(jnp.arange(bsz, dtype=jnp.int32), s // 128)
    ctx_rows = jnp.full((bsz * cx // 128,), ctx_row, jnp.int32)
    xl, xc = x, ctx
    for l in range(depth):
        last = l == depth - 1
        mods = mods_all[l].reshape(MODS_ROWS, 1, N_MODS * d)
        g_off = w_in.shape[2] - 4 * d
        kv_off = g_off - KV_COLS
        w_small = jnp.concatenate([w_in[l, :, :kv_off], w_in[l, :, kv_off + 256:g_off],
                                   w_in[l, :, kv_off:kv_off + 256]], axis=1).astype(BF16)
        w_gate = w_in[l, :, g_off:].astype(BF16)
        w_br = w_branch[l].astype(BF16)
        w_o = w_out[l].astype(BF16)
        pl_l = _proj_call(xl, norm_g[l], mods, w_small, None, SMALL_COLS // 2)
        if last:
            pkv_c = _proj_call(xc, norm_g[l], mods, w_small[:, COL_DK:], ctx_row, KV_COLS)
            kv_base = 0
        else:
            pc = _proj_call(xc, norm_g[l], mods, w_small, ctx_row, SMALL_COLS // 2)
            pkv_c, kv_base = pc, COL_DK
        dkc = _heads(pkv_c[..., kv_base:kv_base + 512], NA_HEADS)
        dvc = _heads(pkv_c[..., kv_base + 512:kv_base + 1024], NA_HEADS)
        ckc = _heads(pkv_c[..., kv_base + 1024:kv_base + 1152], WIN_KV_HEADS)
        cvc = _heads(pkv_c[..., kv_base + 1152:kv_base + 1280], WIN_KV_HEADS)
        conv_args = (conv_b_in[l], conv_dw[l], conv_dw_b[l], conv_ln_g[l], conv_ln_b[l])
        gmlp_args = (gmlp_ln_g[l], gmlp_ln_b[l], gmlp_ws[l], gmlp_bs[l])
        ya = _conv_call(pl_l, *conv_args)
        yb = _gmlp_call(pl_l, *gmlp_args)
        q_rope, k_rope = _rope_call(pl_l)
        yc = _unheads(_win_call(_heads(q_rope, WIN_HEADS), _heads(k_rope, WIN_KV_HEADS),
                                _heads(pl_l[..., COL_CV:COL_CV + 128], WIN_KV_HEADS), ckc, cvc, win_sink[l]))
        yd = _unheads(_na_call(_heads(pl_l[..., COL_DQ:COL_DQ + 512], NA_HEADS),
                               _heads(pl_l[..., COL_DK:COL_DK + 512], NA_HEADS),
                               _heads(pl_l[..., COL_DV:COL_DV + 512], NA_HEADS), dkc, dvc,
                               _na_bias_table(na_rpb[l])))
        xl = _merge_call(xl, (ya, yb, yc, yd), w_gate, w_br, w_o, norm_g[l], mods, None)
        moe_w = (router_w[l], router_b[l], exp_w1[l].astype(BF16), exp_w3[l].astype(BF16), exp_w2[l].astype(BF16),
                 sh_w1[l].astype(BF16), sh_w3[l].astype(BF16), sh_w2[l].astype(BF16))
        if last:
            xl = _channel_mixer(xl.reshape(bsz * s, d), lat_rows, norm_g[l], mods, *moe_w).reshape(bsz, s, d)
        else:
            yca = _conv_call(pc, *conv_args)
            ycb = _gmlp_call(pc, *gmlp_args)
            ycc = _unheads(_ctx_attn_call(_heads(pc[..., COL_CQ:COL_CQ + 512], WIN_HEADS), ckc, cvc, win_sink[l]))
            ycd = _unheads(_ctx_attn_call(_heads(pc[..., COL_DQ:COL_DQ + 512], NA_HEADS), dkc, dvc, None))
            xc = _merge_call(xc, (yca, ycb, ycc, ycd), w_gate, w_br, w_o, norm_g[l], mods, ctx_row)
            xcat = jnp.concatenate([xl.reshape(bsz * s, d), xc.reshape(bsz * cx, d)], axis=0)
            out = _channel_mixer(xcat, jnp.concatenate([lat_rows, ctx_rows]), norm_g[l], mods, *moe_w)
            xl = out[:bsz * s].reshape(bsz, s, d)
            xc = out[bsz * s:].reshape(bsz, cx, d)
    return xl
```

```python
import functools

import numpy as np
import jax
import jax.numpy as jnp
from jax import lax
from jax.experimental import pallas as pl
from jax.experimental.pallas import tpu as pltpu

F32 = jnp.float32
BF16 = jnp.bfloat16

GRID_W = 64
HEAD_DIM = 64
EPS = 1e-6
NEG = -1e30
BRANCH_DIM = 512
CONV_K = 31
CONV_HALO = 16
GMLP_CHUNK = 128
GMLP_GROUPS = 4
WIN_HEADS = 8
WIN_KV_HEADS = 2
WINDOW = 128
WIN_BLOCK = 128
ROPE_BASE = 10000.0
NA_HEADS = 8
NA_ROWS = 8
NA_COLS = 16
N_EXPERTS = 64
TOP_K = 8
N_GROUPS = 8
GROUP_SIZE = N_EXPERTS // N_GROUPS
TOPK_GROUPS = 4
ROUTE_SCALE = 2.5
MOE_BLOCK = 256
N_MODS = 6
MODS_ROWS = 16
VMEM_LIMIT = 52 * 1024 * 1024

COL_A, COL_B, COL_CQ, COL_DQ, COL_DK, COL_DV, COL_CK, COL_CV = 0, 1024, 2048, 2560, 3072, 3584, 4096, 4224
SMALL_COLS = 4352
KV_COLS = 1280


def _sigmoid(x):
    return 1.0 / (1.0 + jnp.exp(-x))


def _silu(x):
    return x * _sigmoid(x)


def _gelu_tanh(x):
    return 0.5 * x * (1.0 + jnp.tanh(np.sqrt(2.0 / np.pi).astype(np.float32) * (x + 0.044715 * (x * x * x))))


def _rms(x, g):
    return x * lax.rsqrt(jnp.mean(x * x, axis=-1, keepdims=True) + EPS) * g


def _ln(x, g, b):
    mu = jnp.mean(x, axis=-1, keepdims=True)
    xc = x - mu
    var = jnp.mean(xc * xc, axis=-1, keepdims=True)
    return xc * lax.rsqrt(var + EPS) * g + b


def _dot(a, b):
    return jnp.dot(a, b, preferred_element_type=F32)


def _dot_nt(a, b):
    return lax.dot_general(a, b, (((1,), (1,)), ((), ())), preferred_element_type=F32)


def _params(sem):
    return pltpu.CompilerParams(dimension_semantics=sem, vmem_limit_bytes=VMEM_LIMIT)


def _ada_kernel(c_ref, w_ref, b_ref, o_ref):
    s = _silu(c_ref[...])
    o_ref[0] = _dot(s.astype(BF16), w_ref[0].astype(BF16)) + b_ref[0]


def _ada_call(c_all, w_ada, b_ada):
    depth, d, n = w_ada.shape
    tn = 512
    return pl.pallas_call(
        _ada_kernel,
        grid=(depth, n // tn),
        in_specs=[pl.BlockSpec((MODS_ROWS, d), lambda l, j: (0, 0)),
                  pl.BlockSpec((1, d, tn), lambda l, j: (l, 0, j)),
                  pl.BlockSpec((1, 1, tn), lambda l, j: (l, 0, j))],
        out_specs=pl.BlockSpec((1, MODS_ROWS, tn), lambda l, j: (l, 0, j)),
        out_shape=jax.ShapeDtypeStruct((depth, MODS_ROWS, n), F32),
        compiler_params=_params(("arbitrary", "arbitrary")),
        name="ada_mods",
    )(c_all, w_ada, b_ada.reshape(depth, 1, n))


def _proj_kernel(x_ref, g_ref, sh_ref, sc_ref, w_ref, o_ref):
    h = _rms(x_ref[0], g_ref[0:1, :]) * (1.0 + sc_ref[0]) + sh_ref[0]
    o_ref[0] = _dot(h.astype(BF16), w_ref[...])


def _proj_call(x, norm_g, mods, w, shared_row, tn):
    bsz, s, d = x.shape
    n = w.shape[1]
    tm = min(s, 512)
    row = (lambda b: b) if shared_row is None else (lambda b: shared_row)
    return pl.pallas_call(
        _proj_kernel,
        grid=(n // tn, bsz, s // tm),
        in_specs=[pl.BlockSpec((1, tm, d), lambda j, b, i: (b, i, 0)),
                  pl.BlockSpec((4, d), lambda j, b, i: (0, 0)),
                  pl.BlockSpec((1, 1, d), lambda j, b, i: (row(b), 0, 0)),
                  pl.BlockSpec((1, 1, d), lambda j, b, i: (row(b), 0, 1)),
                  pl.BlockSpec((d, tn), lambda j, b, i: (0, j))],
        out_specs=pl.BlockSpec((1, tm, tn), lambda j, b, i: (b, i, j)),
        out_shape=jax.ShapeDtypeStruct((bsz, s, n), F32),
        compiler_params=_params(("arbitrary", "arbitrary", "arbitrary")),
        name="proj_in",
    )(x, norm_g, mods, mods, w)


def _conv_kernel(a_ref, ap_ref, an_ref, bin_ref, dw_ref, dwb_ref, g_ref, b_ref, o_ref, ypad_ref):
    i = pl.program_id(1)
    nblk = pl.num_programs(1)
    ts = a_ref.shape[1]
    c = BRANCH_DIM

    def glu(a):
        a = a + bin_ref[...]
        return a[:, :c] * _sigmoid(a[:, c:])

    ypad_ref[0:CONV_HALO, :] = jnp.where(i > 0, glu(ap_ref[0]), 0.0)
    ypad_ref[CONV_HALO:CONV_HALO + ts, :] = glu(a_ref[0])
    ypad_ref[CONV_HALO + ts:2 * CONV_HALO + ts, :] = jnp.where(i < nblk - 1, glu(an_ref[0]), 0.0)
    rc = 64
    off = CONV_HALO - CONV_K // 2
    for r0 in range(0, ts, rc):
        acc = jnp.zeros((rc, c), F32)
        for j in range(CONV_K):
            acc = acc + ypad_ref[r0 + off + j:r0 + off + j + rc, :] * dw_ref[j:j + 1, :]
        y = _ln(acc + dwb_ref[...], g_ref[...], b_ref[...])
        o_ref[0, r0:r0 + rc, :] = _silu(y).astype(o_ref.dtype)


def _conv_call(pl_all, b_in, dw, dw_b, ln_g, ln_b):
    bsz, s, _ = pl_all.shape
    ts = min(s, 512)
    c = BRANCH_DIM
    hb = ts // CONV_HALO
    last = s // CONV_HALO - 1
    dw_pad = jnp.zeros((32, c), F32).at[:CONV_K].set(dw)
    return pl.pallas_call(
        _conv_kernel,
        grid=(bsz, s // ts),
        in_specs=[pl.BlockSpec((1, ts, 2 * c), lambda b, i: (b, i, 0)),
                  pl.BlockSpec((1, CONV_HALO, 2 * c), lambda b, i: (b, jnp.maximum(i * hb - 1, 0), 0)),
                  pl.BlockSpec((1, CONV_HALO, 2 * c), lambda b, i: (b, jnp.minimum((i + 1) * hb, last), 0)),
                  pl.BlockSpec((1, 2 * c), lambda b, i: (0, 0)),
                  pl.BlockSpec((32, c), lambda b, i: (0, 0)),
                  pl.BlockSpec((1, c), lambda b, i: (0, 0)),
                  pl.BlockSpec((1, c), lambda b, i: (0, 0)),
                  pl.BlockSpec((1, c), lambda b, i: (0, 0))],
        out_specs=pl.BlockSpec((1, ts, c), lambda b, i: (b, i, 0)),
        out_shape=jax.ShapeDtypeStruct((bsz, s, c), BF16),
        scratch_shapes=[pltpu.VMEM((ts + 2 * CONV_HALO, c), F32)],
        compiler_params=_params(("arbitrary", "arbitrary")),
        name="mixer_conv",
    )(pl_all, pl_all, pl_all, b_in.reshape(1, 2 * c), dw_pad, dw_b.reshape(1, c), ln_g.reshape(1, c), ln_b.reshape(1, c))


def _gmlp_kernel(z_ref, g_ref, b_ref, ws_ref, bs_ref, o_ref):
    c = BRANCH_DIM
    gw = c // GMLP_GROUPS
    ts = z_ref.shape[1]
    for n in range(ts // GMLP_CHUNK):
        rows = slice(n * GMLP_CHUNK, (n + 1) * GMLP_CHUNK)
        z = _gelu_tanh(z_ref[0, rows, :])
        u = z[:, :c]
        v = _ln(z[:, c:], g_ref[...], b_ref[...]).astype(BF16)
        for g in range(GMLP_GROUPS):
            cols = slice(g * gw, (g + 1) * gw)
            sg = _dot(ws_ref[g], v[:, cols]) + bs_ref[:, g:g + 1]
            o_ref[0, rows, cols] = (u[:, cols] * sg).astype(o_ref.dtype)


def _gmlp_call(pl_all, ln_g, ln_b, ws, bs):
    bsz, s, _ = pl_all.shape
    ts = min(s, 512)
    c = BRANCH_DIM
    return pl.pallas_call(
        _gmlp_kernel,
        grid=(bsz, s // ts),
        in_specs=[pl.BlockSpec((1, ts, 2 * c), lambda b, i: (b, i, COL_B // (2 * c))),
                  pl.BlockSpec((1, c), lambda b, i: (0, 0)),
                  pl.BlockSpec((1, c), lambda b, i: (0, 0)),
                  pl.BlockSpec((GMLP_GROUPS, GMLP_CHUNK, GMLP_CHUNK), lambda b, i: (0, 0, 0)),
                  pl.BlockSpec((GMLP_CHUNK, GMLP_GROUPS), lambda b, i: (0, 0))],
        out_specs=pl.BlockSpec((1, ts, c), lambda b, i: (b, i, 0)),
        out_shape=jax.ShapeDtypeStruct((bsz, s, c), BF16),
        compiler_params=_params(("arbitrary", "arbitrary")),
        name="mixer_gmlp",
    )(pl_all, ln_g.reshape(1, c), ln_b.reshape(1, c), ws.astype(BF16), bs.T)


def _rope_tables(s):
    half, quarter = HEAD_DIM // 2, HEAD_DIM // 4
    pos = np.arange(s)
    prow = (pos // GRID_W).astype(np.float32)
    pcol = (pos % GRID_W).astype(np.float32)
    inv = (ROPE_BASE ** (-np.arange(quarter, dtype=np.float32) / quarter)).astype(np.float32)
    lane = np.arange(128)
    in_head = lane % HEAD_DIM
    p = np.where((in_head < half)[None, :], prow[:, None], pcol[:, None]).astype(np.float32)
    ang = (p * inv[lane % quarter][None, :]).astype(np.float32).astype(np.float64)
    first = (in_head % half) < quarter
    cos = np.cos(ang)
    sin = np.where(first[None, :], -np.sin(ang), np.sin(ang))
    swap = np.zeros((128, 128), np.float32)
    for j in lane:
        swap[j + quarter if first[j] else j - quarter, j] = 1.0
    return cos.astype(np.float32), sin.astype(np.float32), swap


def _rope_kernel(q_ref, k_ref, cos_ref, sin_ref, swap_ref, qo_ref, ko_ref):
    cos, sin, swap = cos_ref[...], sin_ref[...], swap_ref[...]

    def rot(x):
        parts = []
        for t in range(x.shape[1] // 128):
            xt = x[:, t * 128:(t + 1) * 128]
            parts.append(xt * cos + _dot(xt.astype(BF16), swap) * sin)
        return parts[0] if len(parts) == 1 else jnp.concatenate(parts, axis=1)

    qo_ref[0] = rot(q_ref[0])
    ko_ref[0] = rot(k_ref[0])


def _rope_call(pl_all):
    bsz, s, _ = pl_all.shape
    tm = 512
    cos, sin, swap = _rope_tables(s)
    qw, kw = WIN_HEADS * HEAD_DIM, WIN_KV_HEADS * HEAD_DIM
    return pl.pallas_call(
        _rope_kernel,
        grid=(bsz, s // tm),
        in_specs=[pl.BlockSpec((1, tm, qw), lambda b, i: (b, i, COL_CQ // qw)),
                  pl.BlockSpec((1, tm, kw), lambda b, i: (b, i, COL_CK // kw)),
                  pl.BlockSpec((tm, 128), lambda b, i: (i, 0)),
                  pl.BlockSpec((tm, 128), lambda b, i: (i, 0)),
                  pl.BlockSpec((128, 128), lambda b, i: (0, 0))],
        out_specs=[pl.BlockSpec((1, tm, qw), lambda b, i: (b, i, 0)),
                   pl.BlockSpec((1, tm, kw), lambda b, i: (b, i, 0))],
        out_shape=[jax.ShapeDtypeStruct((bsz, s, qw), F32), jax.ShapeDtypeStruct((bsz, s, kw), F32)],
        compiler_params=_params(("arbitrary", "arbitrary")),
        name="rope",
    )(pl_all, pl_all, jnp.asarray(cos), jnp.asarray(sin), jnp.asarray(swap, BF16))


def _win_kernel(sink_ref, q_ref, kp_ref, kc_ref, kn_ref, vp_ref, vc_ref, vn_ref, kx_ref, vx_ref, o_ref, *, seq):
    g = pl.program_id(1)
    n = pl.program_id(2)
    grp = q_ref.shape[1]
    wb = WIN_BLOCK
    k_loc = jnp.concatenate([kp_ref[0, 0], kc_ref[0, 0], kn_ref[0, 0]], axis=0).astype(BF16)
    v_loc = jnp.concatenate([vp_ref[0, 0], vc_ref[0, 0], vn_ref[0, 0]], axis=0).astype(BF16)
    k_ctx = kx_ref[0, 0].astype(BF16)
    v_ctx = vx_ref[0, 0].astype(BF16)
    qpos = n * wb + lax.broadcasted_iota(jnp.int32, (wb, 3 * wb), 0)
    kpos = (n - 1) * wb + lax.broadcasted_iota(jnp.int32, (wb, 3 * wb), 1)
    mask = (jnp.abs(qpos - kpos) <= WINDOW) & (kpos >= 0) & (kpos < seq)
    for h in range(grp):
        q = (q_ref[0, h] * (HEAD_DIM ** -0.5)).astype(BF16)
        s_loc = jnp.where(mask, _dot_nt(q, k_loc), NEG)
        s_ctx = _dot_nt(q, k_ctx)
        sink = sink_ref[g * grp + h]
        m = jnp.maximum(jnp.maximum(jnp.max(s_loc, axis=-1, keepdims=True),
                                    jnp.max(s_ctx, axis=-1, keepdims=True)), sink)
        p_loc = jnp.exp(s_loc - m)
        p_ctx = jnp.exp(s_ctx - m)
        den = (jnp.sum(p_loc, axis=-1, keepdims=True) + jnp.sum(p_ctx, axis=-1, keepdims=True)
               + jnp.exp(sink - m))
        o = _dot(p_loc.astype(BF16), v_loc) + _dot(p_ctx.astype(BF16), v_ctx)
        o_ref[0, h] = (o / den).astype(o_ref.dtype)


def _win_call(q, k, v, kx, vx, sink):
    bsz, hq, s, dh = q.shape
    hkv = k.shape[1]
    grp = hq // hkv
    cx = kx.shape[2]
    nb = s // WIN_BLOCK
    blk = (1, 1, WIN_BLOCK, dh)
    prev = lambda b, g, n, sk: (b, g, jnp.maximum(n - 1, 0), 0)
    cur = lambda b, g, n, sk: (b, g, n, 0)
    nxt = lambda b, g, n, sk: (b, g, jnp.minimum(n + 1, nb - 1), 0)
    ctx = lambda b, g, n, sk: (b, g, 0, 0)
    return pl.pallas_call(
        functools.partial(_win_kernel, seq=s),
        grid_spec=pltpu.PrefetchScalarGridSpec(
            num_scalar_prefetch=1,
            grid=(bsz, hkv, nb),
            in_specs=[pl.BlockSpec((1, grp, WIN_BLOCK, dh), cur),
                      pl.BlockSpec(blk, prev), pl.BlockSpec(blk, cur), pl.BlockSpec(blk, nxt),
                      pl.BlockSpec(blk, prev), pl.BlockSpec(blk, cur), pl.BlockSpec(blk, nxt),
                      pl.BlockSpec((1, 1, cx, dh), ctx), pl.BlockSpec((1, 1, cx, dh), ctx)],
            out_specs=pl.BlockSpec((1, grp, WIN_BLOCK, dh), cur)),
        out_shape=jax.ShapeDtypeStruct((bsz, hq, s, dh), BF16),
        compiler_params=_params(("arbitrary", "arbitrary", "arbitrary")),
        name="mixer_window_attn",
    )(sink, q, k, k, k, v, v, v, kx, vx)


def _na_bias_table(rpb):
    qc = np.arange(GRID_W)
    kc = np.arange(GRID_W)
    cs = np.clip(qc - NA_COLS // 2, 0, GRID_W - NA_COLS)
    cmask = (kc[None, :] >= cs[:, None]) & (kc[None, :] < cs[:, None] + NA_COLS)
    ci = np.clip(kc[None, :] - qc[:, None] + NA_COLS - 1, 0, 2 * NA_COLS - 2)
    pick = (ci[None] == np.arange(2 * NA_COLS - 1)[:, None, None]).astype(np.float32)
    t15 = jnp.einsum('hrc,cqk->hrqk', rpb, jnp.asarray(pick), precision=lax.Precision.HIGHEST)
    t15 = jnp.where(cmask[None, None], t15, NEG)
    tab = jnp.stack([t15[:, d0:d0 + NA_ROWS] for d0 in range(NA_ROWS)], axis=1)
    return tab.transpose(0, 1, 3, 2, 4).reshape(rpb.shape[0], NA_ROWS, GRID_W, NA_ROWS * GRID_W)


def _na_kernel(q_ref, k_ref, v_ref, kx_ref, vx_ref, bias_ref, o_ref):
    rows = q_ref.shape[2] // GRID_W
    k_ctx = kx_ref[0, 0].astype(BF16)
    v_ctx = vx_ref[0, 0].astype(BF16)
    k_all = k_ref[0, 0].astype(BF16)
    v_all = v_ref[0, 0].astype(BF16)
    span = NA_ROWS * GRID_W
    chunk = 8
    for r0 in range(0, rows, chunk):
        q = (q_ref[0, 0, r0 * GRID_W:(r0 + chunk) * GRID_W, :] * (HEAD_DIM ** -0.5)).astype(BF16)
        s_parts = []
        for r in range(r0, r0 + chunk):
            rs = min(max(r - NA_ROWS // 2, 0), rows - NA_ROWS)
            k_loc = k_all[rs * GRID_W:rs * GRID_W + span, :]
            qr = q[(r - r0) * GRID_W:(r - r0 + 1) * GRID_W, :]
            s_parts.append(_dot_nt(qr, k_loc) + bias_ref[0, rs - r + NA_ROWS - 1])
        s_loc = jnp.concatenate(s_parts, axis=0)
        s_ctx = _dot_nt(q, k_ctx)
        m = jnp.maximum(jnp.max(s_loc, axis=-1, keepdims=True), jnp.max(s_ctx, axis=-1, keepdims=True))
        p_loc = jnp.exp(s_loc - m)
        p_ctx = jnp.exp(s_ctx - m)
        den = jnp.sum(p_loc, axis=-1, keepdims=True) + jnp.sum(p_ctx, axis=-1, keepdims=True)
        p_loc = p_loc.astype(BF16)
        o_parts = []
        for r in range(r0, r0 + chunk):
            rs = min(max(r - NA_ROWS // 2, 0), rows - NA_ROWS)
            v_loc = v_all[rs * GRID_W:rs * GRID_W + span, :]
            o_parts.append(_dot(p_loc[(r - r0) * GRID_W:(r - r0 + 1) * GRID_W, :], v_loc))
        o = jnp.concatenate(o_parts, axis=0) + _dot(p_ctx.astype(BF16), v_ctx)
        o_ref[0, 0, r0 * GRID_W:(r0 + chunk) * GRID_W, :] = (o / den).astype(o_ref.dtype)


def _na_call(q, k, v, kx, vx, bias_tab):
    bsz, h, s, dh = q.shape
    cx = kx.shape[2]
    full = pl.BlockSpec((1, 1, s, dh), lambda b, hh: (b, hh, 0, 0))
    ctx = pl.BlockSpec((1, 1, cx, dh), lambda b, hh: (b, hh, 0, 0))
    return pl.pallas_call(
        _na_kernel,
        grid=(bsz, h),
        in_specs=[full, full, full, ctx, ctx,
                  pl.BlockSpec((1, NA_ROWS, GRID_W, NA_ROWS * GRID_W), lambda b, hh: (hh, 0, 0, 0))],
        out_specs=full,
        out_shape=jax.ShapeDtypeStruct((bsz, h, s, dh), BF16),
        compiler_params=_params(("arbitrary", "arbitrary")),
        name="mixer_neighbourhood_attn",
    )(q, k, v, kx, vx, bias_tab)


def _ctx_attn_kernel(sink_ref, q_ref, k_ref, v_ref, o_ref, *, use_sink):
    h = pl.program_id(1)
    q = (q_ref[0, 0] * (HEAD_DIM ** -0.5)).astype(BF16)
    s = _dot_nt(q, k_ref[0, 0].astype(BF16))
    m = jnp.max(s, axis=-1, keepdims=True)
    if use_sink:
        sink = sink_ref[h]
        m = jnp.maximum(m, sink)
    p = jnp.exp(s - m)
    den = jnp.sum(p, axis=-1, keepdims=True)
    if use_sink:
        den = den + jnp.exp(sink - m)
    o_ref[0, 0] = (_dot(p.astype(BF16), v_ref[0, 0].astype(BF16)) / den).astype(o_ref.dtype)


def _ctx_attn_call(q, k, v, sink):
    bsz, hq, cx, dh = q.shape
    grp = hq // k.shape[1]
    use_sink = sink is not None
    if sink is None:
        sink = jnp.zeros((hq,), F32)
    qs = pl.BlockSpec((1, 1, cx, dh), lambda b, h, sk: (b, h, 0, 0))
    ks = pl.BlockSpec((1, 1, cx, dh), lambda b, h, sk: (b, h // grp, 0, 0))
    return pl.pallas_call(
        functools.partial(_ctx_attn_kernel, use_sink=use_sink),
        grid_spec=pltpu.PrefetchScalarGridSpec(
            num_scalar_prefetch=1, grid=(bsz, hq), in_specs=[qs, ks, ks], out_specs=qs),
        out_shape=jax.ShapeDtypeStruct((bsz, hq, cx, dh), BF16),
        compiler_params=_params(("arbitrary", "arbitrary")),
        name="context_attn",
    )(sink, q, k, v)


def _merge_kernel(x_ref, ya_ref, yb_ref, yc_ref, yd_ref, wg0_ref, wg1_ref, wg2_ref, wg3_ref, wbr_ref, wout_ref,
                  ng_ref, sh_ref, sc_ref, gt_ref, o_ref, h_ref, acc_ref):
    n = pl.program_id(2)
    nt = acc_ref.shape[0]
    d = o_ref.shape[2]

    @pl.when(n == 0)
    def _():
        h = _rms(x_ref[0], ng_ref[0:1, :]) * (1.0 + sc_ref[0]) + sh_ref[0]
        h_ref[...] = h.astype(BF16)

    h = h_ref[...]
    acc = None
    for i, (y_ref, wg_ref) in enumerate(((ya_ref, wg0_ref), (yb_ref, wg1_ref), (yc_ref, wg2_ref), (yd_ref, wg3_ref))):
        t = _sigmoid(_dot(h, wg_ref[...])) * _dot(y_ref[0], wbr_ref[i])
        acc = t if acc is None else acc + t
    acc_ref[n] = acc.astype(BF16)

    @pl.when(n == nt - 1)
    def _():
        a = jnp.concatenate([acc_ref[j] for j in range(nt)], axis=1)
        oc = d // 4
        for c in range(4):
            o_ref[0, :, c * oc:(c + 1) * oc] = _dot(a, wout_ref[:, c * oc:(c + 1) * oc])
        o_ref[0] = x_ref[0] + gt_ref[0] * _rms(o_ref[0], ng_ref[1:2, :])


def _merge_call(x, ys, w_gate, w_branch, w_out, norm_g, mods, shared_row):
    bsz, s, d = x.shape
    c = BRANCH_DIM
    tm = min(s, 512)
    tn = 256
    nt = d // tn
    row = (lambda b: b) if shared_row is None else (lambda b: shared_row)
    ysp = pl.BlockSpec((1, tm, c), lambda b, i, n: (b, i, 0))
    wg = [pl.BlockSpec((d, tn), functools.partial(lambda b, i, n, k: (0, k * nt + n), k=k)) for k in range(4)]
    mod = lambda col: pl.BlockSpec((1, 1, d), lambda b, i, n: (row(b), 0, col))
    return pl.pallas_call(
        _merge_kernel,
        grid=(bsz, s // tm, nt),
        in_specs=[pl.BlockSpec((1, tm, d), lambda b, i, n: (b, i, 0)), ysp, ysp, ysp, ysp, *wg,
                  pl.BlockSpec((4, c, tn), lambda b, i, n: (0, 0, n)),
                  pl.BlockSpec((d, d), lambda b, i, n: (0, 0), pipeline_mode=pl.Buffered(1)),
                  pl.BlockSpec((4, d), lambda b, i, n: (0, 0)),
                  mod(0), mod(1), mod(2)],
        out_specs=pl.BlockSpec((1, tm, d), lambda b, i, n: (b, i, 0)),
        out_shape=jax.ShapeDtypeStruct((bsz, s, d), F32),
        scratch_shapes=[pltpu.VMEM((tm, d), BF16), pltpu.VMEM((nt, tm, tn), BF16)],
        compiler_params=_params(("arbitrary", "arbitrary", "arbitrary")),
        name="mixer_merge",
    )(x, *ys, w_gate, w_gate, w_gate, w_gate, w_branch, w_out, norm_g, mods, mods, mods)


def _expert_row_ids():
    r = np.arange(N_EXPERTS)
    return ((r % N_GROUPS) * GROUP_SIZE + r // N_GROUPS).astype(np.int32)


def _router_kernel(rows_ref, x_ref, ng_ref, sh_ref, sc_ref, rw_ref, rb_ref, eid_ref, tri_e_ref, tri_t_ref,
                   sw1_ref, sw3_ref, sw2_ref, h_ref, so_ref, e_ref, r_ref, w_ref, cnt_ref):
    i = pl.program_id(0)
    tm = x_ref.shape[0]

    @pl.when(i == 0)
    def _():
        cnt_ref[...] = jnp.zeros(cnt_ref.shape, F32)

    h = _rms(x_ref[...], ng_ref[2:3, :]) * (1.0 + sc_ref[0]) + sh_ref[0]
    h_ref[...] = h
    hb = h.astype(BF16)
    so_ref[...] = _dot((_silu(_dot(hb, sw1_ref[...])) * _dot(hb, sw3_ref[...])).astype(BF16), sw2_ref[...])

    logits = lax.dot_general(rw_ref[...], h, (((1,), (1,)), ((), ())),
                             precision=lax.Precision.HIGHEST, preferred_element_type=F32)
    scores = _sigmoid(logits)
    biased = scores + rb_ref[...]
    m1 = jnp.full((N_GROUPS, tm), -jnp.inf, F32)
    m2 = m1
    for j in range(GROUP_SIZE):
        v = biased[j * N_GROUPS:(j + 1) * N_GROUPS, :]
        m2 = jnp.maximum(m2, jnp.minimum(m1, v))
        m1 = jnp.maximum(m1, v)
    gs = m1 + m2
    gid = lax.broadcasted_iota(jnp.int32, (N_GROUPS, tm), 0)
    beat = jnp.zeros((N_GROUPS, tm), jnp.int32)
    for g in range(N_GROUPS):
        o = gs[g:g + 1, :]
        beat = beat + jnp.where((o > gs) | ((o == gs) & (g < gid)), 1, 0)
    keep = beat < TOPK_GROUPS
    masked = jnp.concatenate(
        [jnp.where(keep, biased[j * N_GROUPS:(j + 1) * N_GROUPS, :], NEG) for j in range(GROUP_SIZE)], axis=0)
    eid = eid_ref[...]
    row_ids = _expert_row_ids()
    beat = jnp.zeros((N_EXPERTS, tm), jnp.int32)
    for r in range(N_EXPERTS):
        o = masked[r:r + 1, :]
        beat = beat + jnp.where((o > masked) | ((o == masked) & (int(row_ids[r]) < eid)), 1, 0)
    sel = beat < TOP_K
    self32 = jnp.where(sel, 1.0, 0.0)
    selb = self32.astype(BF16)
    wsel = jnp.where(sel, scores, 0.0)
    wsel = wsel / jnp.sum(wsel, axis=0, keepdims=True) * ROUTE_SCALE
    rank = cnt_ref[:, 0:1] + _dot(selb, tri_t_ref[...])
    cnt_ref[...] = cnt_ref[...] + jnp.sum(self32, axis=1, keepdims=True)
    slot = _dot(tri_e_ref[...], selb)
    eidf = eid.astype(F32)
    for k in range(TOP_K):
        mk = sel & (slot == float(k))
        e_ref[k:k + 1, :] = jnp.sum(jnp.where(mk, eidf, 0.0), axis=0, keepdims=True).astype(jnp.int32)
        r_ref[k:k + 1, :] = jnp.sum(jnp.where(mk, rank, 0.0), axis=0, keepdims=True).astype(jnp.int32)
        w_ref[k:k + 1, :] = jnp.sum(jnp.where(mk, wsel, 0.0), axis=0, keepdims=True)


def _router_call(xcat, tile_rows, norm_g, mods, router_w, router_b, sw1, sw3, sw2):
    t, d = xcat.shape
    tm = 256
    f = sw1.shape[1]
    ids = _expert_row_ids()
    rw = router_w.T[ids]
    rb = router_b[ids].reshape(N_EXPERTS, 1)
    tri_e = jnp.asarray(np.tril(np.ones((N_EXPERTS, N_EXPERTS), np.float32), -1), BF16)
    tri_t = jnp.asarray(np.triu(np.ones((tm, tm), np.float32), 1), BF16)
    const = lambda shape: pl.BlockSpec(shape, lambda i, rows: (0,) * len(shape))
    tok = pl.BlockSpec((tm, d), lambda i, rows: (i, 0))
    slots = pl.BlockSpec((TOP_K, tm), lambda i, rows: (0, i))
    return pl.pallas_call(
        _router_kernel,
        grid_spec=pltpu.PrefetchScalarGridSpec(
            num_scalar_prefetch=1,
            grid=(t // tm,),
            in_specs=[tok, const((4, d)),
                      pl.BlockSpec((1, 1, d), lambda i, rows: (rows[i], 0, 3)),
                      pl.BlockSpec((1, 1, d), lambda i, rows: (rows[i], 0, 4)),
                      const((N_EXPERTS, d)), const((N_EXPERTS, 1)), const((N_EXPERTS, 1)),
                      const((N_EXPERTS, N_EXPERTS)), const((tm, tm)),
                      const((d, f)), const((d, f)), const((f, d))],
            out_specs=[tok, tok, slots, slots, slots, const((N_EXPERTS, 128))]),
        out_shape=[jax.ShapeDtypeStruct((t, d), F32), jax.ShapeDtypeStruct((t, d), F32),
                   jax.ShapeDtypeStruct((TOP_K, t), jnp.int32), jax.ShapeDtypeStruct((TOP_K, t), jnp.int32),
                   jax.ShapeDtypeStruct((TOP_K, t), F32), jax.ShapeDtypeStruct((N_EXPERTS, 128), F32)],
        compiler_params=_params(("arbitrary",)),
        name="moe_router",
    )(tile_rows, xcat, norm_g, mods, mods, rw, rb, jnp.asarray(ids).reshape(N_EXPERTS, 1), tri_e, tri_t,
      sw1, sw3, sw2)


def _dispatch_kernel(dest_ref, h_ref, xs_hbm, sem):
    td = h_ref.shape[0]

    def copy(j, k):
        return pltpu.make_async_copy(h_ref.at[pl.ds(j, 1)], xs_hbm.at[pl.ds(dest_ref[0, 0, k * td + j], 1)], sem)

    def start(j, c):
        for k in range(TOP_K):
            copy(j, k).start(priority=k % 2)
        return c

    def wait(j, c):
        for k in range(TOP_K):
            copy(j, k).wait()
        return c

    lax.fori_loop(0, td, start, 0)
    lax.fori_loop(0, td, wait, 0)


def _dispatch_call(h, dest):
    t, d = h.shape
    td = 256
    dest3 = dest.reshape(TOP_K, t // td, td).transpose(1, 0, 2).reshape(t // td, 1, TOP_K * td)
    return pl.pallas_call(
        _dispatch_kernel,
        grid=(t // td,),
        in_specs=[pl.BlockSpec((1, 1, TOP_K * td), lambda i: (i, 0, 0), memory_space=pltpu.SMEM),
                  pl.BlockSpec((td, d), lambda i: (i, 0))],
        out_specs=pl.BlockSpec(memory_space=pl.ANY),
        out_shape=jax.ShapeDtypeStruct((t * TOP_K, d), F32),
        scratch_shapes=[pltpu.SemaphoreType.DMA],
        compiler_params=_params(("arbitrary",)),
        name="moe_dispatch",
    )(dest3, h)


def _gmm_kernel(blk_ref, exp_ref, lo_ref, hi_ref, first_ref, newexp_ref, x_ref, w1_ref, w3_ref, w2_ref, o_ref,
                w1b_ref, w3b_ref, w2b_ref):
    i = pl.program_id(0)
    lo = lo_ref[i]
    hi = hi_ref[i]

    nrows = x_ref.shape[0]
    whole = (lo == 0) & (hi == nrows)

    @pl.when((first_ref[i] == 1) & jnp.logical_not(whole))
    def _():
        o_ref[...] = jnp.zeros(o_ref.shape, F32)

    @pl.when(newexp_ref[i] == 1)
    def _():
        w1b_ref[...] = w1_ref[0].astype(BF16)
        w3b_ref[...] = w3_ref[0].astype(BF16)
        w2b_ref[...] = w2_ref[0].astype(BF16)

    def expert():
        x = x_ref[...].astype(BF16)
        a = (_silu(_dot(x, w1b_ref[...])) * _dot(x, w3b_ref[...])).astype(BF16)
        return _dot(a, w2b_ref[...])

    @pl.when(whole)
    def _():
        o_ref[...] = expert()

    @pl.when((hi > lo) & jnp.logical_not(whole))
    def _():
        rows = lax.broadcasted_iota(jnp.int32, (nrows, 1), 0)
        o_ref[...] += jnp.where((rows >= lo) & (rows < hi), expert(), 0.0)


def _gmm_call(items, xs, w1, w3, w2, layer):
    a, d = xs.shape
    f = w1.shape[3]
    n_items = items[0].shape[0]
    rows = pl.BlockSpec((MOE_BLOCK, d), lambda i, blk, ex, *_: (blk[i], 0))
    return pl.pallas_call(
        _gmm_kernel,
        grid_spec=pltpu.PrefetchScalarGridSpec(
            num_scalar_prefetch=len(items),
            grid=(n_items,),
            in_specs=[rows,
                      pl.BlockSpec((None, 1, d, f), lambda i, blk, ex, *_: (layer, ex[i], 0, 0)),
                      pl.BlockSpec((None, 1, d, f), lambda i, blk, ex, *_: (layer, ex[i], 0, 0)),
                      pl.BlockSpec((None, 1, f, d), lambda i, blk, ex, *_: (layer, ex[i], 0, 0))],
            out_specs=rows,
            scratch_shapes=[pltpu.VMEM((d, f), BF16), pltpu.VMEM((d, f), BF16), pltpu.VMEM((f, d), BF16)]),
        out_shape=jax.ShapeDtypeStruct((a, d), F32),
        compiler_params=_params(("arbitrary",)),
        name="moe_experts",
    )(*items, xs, w1, w3, w2)


def _moe_items(counts, n_rows):
    nb = n_rows // MOE_BLOCK
    n_items = nb + N_EXPERTS
    ends = jnp.cumsum(counts).astype(jnp.int32)
    bstart = jnp.arange(nb, dtype=jnp.int32) * MOE_BLOCK
    pos_b = jnp.arange(nb, dtype=jnp.int32) + jnp.sum(ends[None, :] < bstart[:, None], axis=1)
    idx = jnp.arange(n_items, dtype=jnp.int32)
    n_b = jnp.sum(pos_b[None, :] <= idx[:, None], axis=1).astype(jnp.int32)
    n_e = idx + 1 - n_b
    ends0 = jnp.concatenate([jnp.zeros((1,), jnp.int32), ends])
    last_end = jnp.sum(jnp.where(n_e[:, None] == jnp.arange(N_EXPERTS + 1)[None, :], ends0[None, :], 0), axis=1)
    start = jnp.maximum((n_b - 1) * MOE_BLOCK, last_end).astype(jnp.int32)
    stop = jnp.concatenate([start[1:], jnp.full((1,), n_rows, jnp.int32)])
    blk = jnp.minimum(start // MOE_BLOCK, nb - 1)
    ex = jnp.minimum(jnp.sum(ends[None, :] <= start[:, None], axis=1), N_EXPERTS - 1).astype(jnp.int32)
    lo = start - blk * MOE_BLOCK
    hi = stop - blk * MOE_BLOCK
    one = jnp.ones((1,), jnp.int32)
    first = jnp.concatenate([one, (blk[1:] != blk[:-1]).astype(jnp.int32)])
    newexp = jnp.concatenate([one, (ex[1:] != ex[:-1]).astype(jnp.int32)])
    return blk.astype(jnp.int32), ex, lo.astype(jnp.int32), hi.astype(jnp.int32), first, newexp


def _combine_kernel(rows_ref, dest_ref, ys_hbm, w_ref, so_ref, x_ref, ng_ref, gt_ref, o_ref, buf_ref, sem):
    tc = x_ref.shape[0]

    def copy(j, k):
        return pltpu.make_async_copy(ys_hbm.at[pl.ds(dest_ref[0, 0, k * tc + j], 1)], buf_ref.at[k, pl.ds(j, 1)], sem)

    def start(j, c):
        for k in range(TOP_K):
            copy(j, k).start(priority=k % 2)
        return c

    def wait(j, c):
        for k in range(TOP_K):
            copy(j, k).wait()
        return c

    lax.fori_loop(0, tc, start, 0)
    lax.fori_loop(0, tc, wait, 0)
    fl = so_ref[...]
    for k in range(TOP_K):
        fl = fl + buf_ref[k] * w_ref[:, k:k + 1]
    o_ref[...] = x_ref[...] + gt_ref[0] * _rms(fl, ng_ref[3:4, :])


def _combine_call(tile_rows, dest, ys, wsel, shared_out, xcat, norm_g, mods):
    t, d = xcat.shape
    tc = 128
    dest3 = dest.reshape(TOP_K, t // tc, tc).transpose(1, 0, 2).reshape(t // tc, 1, TOP_K * tc)
    tok = pl.BlockSpec((tc, d), lambda i, rows: (i, 0))
    return pl.pallas_call(
        _combine_kernel,
        grid_spec=pltpu.PrefetchScalarGridSpec(
            num_scalar_prefetch=1,
            grid=(t // tc,),
            in_specs=[pl.BlockSpec((1, 1, TOP_K * tc), lambda i, rows: (i, 0, 0), memory_space=pltpu.SMEM),
                      pl.BlockSpec(memory_space=pl.ANY),
                      pl.BlockSpec((tc, TOP_K), lambda i, rows: (i, 0)),
                      tok, tok,
                      pl.BlockSpec((4, d), lambda i, rows: (0, 0)),
                      pl.BlockSpec((1, 1, d), lambda i, rows: (rows[i], 0, 5))],
            out_specs=tok,
            scratch_shapes=[pltpu.VMEM((TOP_K, tc, d), F32), pltpu.SemaphoreType.DMA]),
        out_shape=jax.ShapeDtypeStruct((t, d), F32),
        compiler_params=_params(("arbitrary",)),
        name="moe_combine",
    )(tile_rows, dest3, ys, wsel.T, shared_out, xcat, norm_g, mods)


def _channel_mixer(xcat, mod_rows, norm_g, mods, router_w, router_b, w1, w3, w2, layer, sw1, sw3, sw2):
    t, _ = xcat.shape
    h, shared_out, eidx, rank, wsel, cnt = _router_call(xcat, mod_rows[::2], norm_g, mods, router_w, router_b,
                                                         sw1, sw3, sw2)
    counts = jnp.zeros((N_EXPERTS,), jnp.int32).at[_expert_row_ids()].set(cnt[:, 0].astype(jnp.int32))
    starts = jnp.cumsum(counts) - counts
    sel = eidx[:, :, None] == jnp.arange(N_EXPERTS, dtype=jnp.int32)[None, None, :]
    dest = jnp.sum(jnp.where(sel, starts[None, None, :], 0), axis=-1) + rank
    xs = _dispatch_call(h, dest)
    ys = _gmm_call(_moe_items(counts, t * TOP_K), xs, w1, w3, w2, layer)
    return _combine_call(mod_rows, dest, ys, wsel, shared_out, xcat, norm_g, mods)


def _heads(t, n):
    b, s, _ = t.shape
    return t.reshape(b, s, n, HEAD_DIM).transpose(0, 2, 1, 3)


def _unheads(t):
    b, n, s, dh = t.shape
    return t.transpose(0, 2, 1, 3).reshape(b, s, n * dh)


def kernel(x, c, ctx, c_ctx, w_ada, b_ada, norm_g, w_in, conv_b_in, conv_dw, conv_dw_b, conv_ln_g, conv_ln_b, gmlp_ln_g, gmlp_ln_b, gmlp_ws, gmlp_bs, win_sink, na_rpb, w_branch, w_out, router_w, router_b, exp_w1, exp_w3, exp_w2, sh_w1, sh_w3, sh_w2):
    bsz, s, d = x.shape
    cx = ctx.shape[1]
    depth = w_ada.shape[0]
    ctx_row = bsz
    c_all = jnp.zeros((MODS_ROWS, d), F32).at[:bsz].set(c).at[ctx_row].set(c_ctx)
    mods_all = _ada_call(c_all, w_ada, b_ada)
    lat_rows = jnp.repeat(jnp.arange(bsz, dtype=jnp.int32), s // 128)
    ctx_rows = jnp.full((bsz * cx // 128,), ctx_row, jnp.int32)
    xl, xc = x, ctx
    for l in range(depth):
        last = l == depth - 1
        mods = mods_all[l].reshape(MODS_ROWS, 1, N_MODS * d)
        g_off = w_in.shape[2] - 4 * d
        kv_off = g_off - KV_COLS
        w_small = jnp.concatenate([w_in[l, :, :kv_off], w_in[l, :, kv_off + 256:g_off],
                                   w_in[l, :, kv_off:kv_off + 256]], axis=1).astype(BF16)
        w_gate = w_in[l, :, g_off:].astype(BF16)
        w_br = w_branch[l].astype(BF16)
        w_o = w_out[l].astype(BF16)
        pl_l = _proj_call(xl, norm_g[l], mods, w_small, None, SMALL_COLS // 2)
        if last:
            pkv_c = _proj_call(xc, norm_g[l], mods, w_small[:, COL_DK:], ctx_row, KV_COLS)
            kv_base = 0
        else:
            pc = _proj_call(xc, norm_g[l], mods, w_small, ctx_row, SMALL_COLS // 2)
            pkv_c, kv_base = pc, COL_DK
        dkc = _heads(pkv_c[..., kv_base:kv_base + 512], NA_HEADS)
        dvc = _heads(pkv_c[..., kv_base + 512:kv_base + 1024], NA_HEADS)
        ckc = _heads(pkv_c[..., kv_base + 1024:kv_base + 1152], WIN_KV_HEADS)
        cvc = _heads(pkv_c[..., kv_base + 1152:kv_base + 1280], WIN_KV_HEADS)
        conv_args = (conv_b_in[l], conv_dw[l], conv_dw_b[l], conv_ln_g[l], conv_ln_b[l])
        gmlp_args = (gmlp_ln_g[l], gmlp_ln_b[l], gmlp_ws[l], gmlp_bs[l])
        ya = _conv_call(pl_l, *conv_args)
        yb = _gmlp_call(pl_l, *gmlp_args)
        q_rope, k_rope = _rope_call(pl_l)
        yc = _unheads(_win_call(_heads(q_rope, WIN_HEADS), _heads(k_rope, WIN_KV_HEADS),
                                _heads(pl_l[..., COL_CV:COL_CV + 128], WIN_KV_HEADS), ckc, cvc, win_sink[l]))
        yd = _unheads(_na_call(_heads(pl_l[..., COL_DQ:COL_DQ + 512], NA_HEADS),
                               _heads(pl_l[..., COL_DK:COL_DK + 512], NA_HEADS),
                               _heads(pl_l[..., COL_DV:COL_DV + 512], NA_HEADS), dkc, dvc,
                               _na_bias_table(na_rpb[l])))
        xl = _merge_call(xl, (ya, yb, yc, yd), w_gate, w_br, w_o, norm_g[l], mods, None)
        moe_w = (router_w[l], router_b[l], exp_w1, exp_w3, exp_w2, l,
                 sh_w1[l].astype(BF16), sh_w3[l].astype(BF16), sh_w2[l].astype(BF16))
        if last:
            xl = _channel_mixer(xl.reshape(bsz * s, d), lat_rows, norm_g[l], mods, *moe_w).reshape(bsz, s, d)
        else:
            yca = _conv_call(pc, *conv_args)
            ycb = _gmlp_call(pc, *gmlp_args)
            ycc = _unheads(_ctx_attn_call(_heads(pc[..., COL_CQ:COL_CQ + 512], WIN_HEADS), ckc, cvc, win_sink[l]))
            ycd = _unheads(_ctx_attn_call(_heads(pc[..., COL_DQ:COL_DQ + 512], NA_HEADS), dkc, dvc, None))
            xc = _merge_call(xc, (yca, ycb, ycc, ycd), w_gate, w_br, w_o, norm_g[l], mods, ctx_row)
            xcat = jnp.concatenate([xl.reshape(bsz * s, d), xc.reshape(bsz * cx, d)], axis=0)
            out = _channel_mixer(xcat, jnp.concatenate([lat_rows, ctx_rows]), norm_g[l], mods, *moe_w)
            xl = out[:bsz * s].reshape(bsz, s, d)
            xc = out[bsz * s:].reshape(bsz, cx, d)
    return xl
```

```python
import functools

import numpy as np
import jax
import jax.numpy as jnp
from jax import lax
from jax.experimental import pallas as pl
from jax.experimental.pallas import tpu as pltpu

F32 = jnp.float32
BF16 = jnp.bfloat16

GRID_W = 64
HEAD_DIM = 64
EPS = 1e-6
NEG = -1e30
BRANCH_DIM = 512
CONV_K = 31
CONV_HALO = 16
GMLP_CHUNK = 128
GMLP_GROUPS = 4
WIN_HEADS = 8
WIN_KV_HEADS = 2
WINDOW = 128
WIN_BLOCK = 128
ROPE_BASE = 10000.0
NA_HEADS = 8
NA_ROWS = 8
NA_COLS = 16
N_EXPERTS = 64
TOP_K = 8
N_GROUPS = 8
GROUP_SIZE = N_EXPERTS // N_GROUPS
TOPK_GROUPS = 4
ROUTE_SCALE = 2.5
MOE_BLOCK = 256
N_MODS = 6
MODS_ROWS = 16
VMEM_LIMIT = 52 * 1024 * 1024

COL_A, COL_B, COL_CQ, COL_DQ, COL_DK, COL_DV, COL_CK, COL_CV = 0, 1024, 2048, 2560, 3072, 3584, 4096, 4224
SMALL_COLS = 4352
KV_COLS = 1280


def _sigmoid(x):
    return 1.0 / (1.0 + jnp.exp(-x))


def _silu(x):
    return x * _sigmoid(x)


def _gelu_tanh(x):
    return 0.5 * x * (1.0 + jnp.tanh(np.sqrt(2.0 / np.pi).astype(np.float32) * (x + 0.044715 * (x * x * x))))


def _rms(x, g):
    return x * lax.rsqrt(jnp.mean(x * x, axis=-1, keepdims=True) + EPS) * g


def _ln(x, g, b):
    mu = jnp.mean(x, axis=-1, keepdims=True)
    xc = x - mu
    var = jnp.mean(xc * xc, axis=-1, keepdims=True)
    return xc * lax.rsqrt(var + EPS) * g + b


def _dot(a, b):
    return jnp.dot(a, b, preferred_element_type=F32)


def _dot_nt(a, b):
    return lax.dot_general(a, b, (((1,), (1,)), ((), ())), preferred_element_type=F32)


def _pack_halves(x):
    n = x.shape[1] // 2
    lo = lax.bitcast_convert_type(x[:, :n].astype(BF16).astype(F32), jnp.uint32)
    hi = lax.bitcast_convert_type(x[:, n:].astype(BF16).astype(F32), jnp.uint32)
    return (hi & jnp.uint32(0xFFFF0000)) | (lo >> 16)


def _unpack_halves(w):
    lo = lax.bitcast_convert_type(w << 16, F32)
    hi = lax.bitcast_convert_type(w & jnp.uint32(0xFFFF0000), F32)
    return lo, hi


def _params(sem):
    return pltpu.CompilerParams(dimension_semantics=sem, vmem_limit_bytes=VMEM_LIMIT)


def _ada_kernel(c_ref, w_ref, b_ref, o_ref):
    s = _silu(c_ref[...])
    o_ref[0] = _dot(s.astype(BF16), w_ref[0].astype(BF16)) + b_ref[0]


def _ada_call(c_all, w_ada, b_ada):
    depth, d, n = w_ada.shape
    tn = 512
    return pl.pallas_call(
        _ada_kernel,
        grid=(depth, n // tn),
        in_specs=[pl.BlockSpec((MODS_ROWS, d), lambda l, j: (0, 0)),
                  pl.BlockSpec((1, d, tn), lambda l, j: (l, 0, j)),
                  pl.BlockSpec((1, 1, tn), lambda l, j: (l, 0, j))],
        out_specs=pl.BlockSpec((1, MODS_ROWS, tn), lambda l, j: (l, 0, j)),
        out_shape=jax.ShapeDtypeStruct((depth, MODS_ROWS, n), F32),
        compiler_params=_params(("arbitrary", "arbitrary")),
        name="ada_mods",
    )(c_all, w_ada, b_ada.reshape(depth, 1, n))


def _proj_kernel(x_ref, g_ref, sh_ref, sc_ref, w_ref, o_ref):
    h = _rms(x_ref[0], g_ref[0:1, :]) * (1.0 + sc_ref[0]) + sh_ref[0]
    o_ref[0] = _dot(h.astype(BF16), w_ref[...])


def _proj_call(x, norm_g, mods, w, shared_row, tn):
    bsz, s, d = x.shape
    n = w.shape[1]
    tm = min(s, 512)
    row = (lambda b: b) if shared_row is None else (lambda b: shared_row)
    return pl.pallas_call(
        _proj_kernel,
        grid=(n // tn, bsz, s // tm),
        in_specs=[pl.BlockSpec((1, tm, d), lambda j, b, i: (b, i, 0)),
                  pl.BlockSpec((4, d), lambda j, b, i: (0, 0)),
                  pl.BlockSpec((1, 1, d), lambda j, b, i: (row(b), 0, 0)),
                  pl.BlockSpec((1, 1, d), lambda j, b, i: (row(b), 0, 1)),
                  pl.BlockSpec((d, tn), lambda j, b, i: (0, j))],
        out_specs=pl.BlockSpec((1, tm, tn), lambda j, b, i: (b, i, j)),
        out_shape=jax.ShapeDtypeStruct((bsz, s, n), F32),
        compiler_params=_params(("arbitrary", "arbitrary", "arbitrary")),
        name="proj_in",
    )(x, norm_g, mods, mods, w)


def _conv_kernel(a_ref, ap_ref, an_ref, bin_ref, dw_ref, dwb_ref, g_ref, b_ref, o_ref, ypad_ref):
    i = pl.program_id(1)
    nblk = pl.num_programs(1)
    ts = a_ref.shape[1]
    c = BRANCH_DIM

    def glu(a):
        a = a + bin_ref[...]
        return a[:, :c] * _sigmoid(a[:, c:])

    ypad_ref[0:CONV_HALO, :] = jnp.where(i > 0, glu(ap_ref[0]), 0.0)
    ypad_ref[CONV_HALO:CONV_HALO + ts, :] = glu(a_ref[0])
    ypad_ref[CONV_HALO + ts:2 * CONV_HALO + ts, :] = jnp.where(i < nblk - 1, glu(an_ref[0]), 0.0)
    rc = 64
    off = CONV_HALO - CONV_K // 2
    for r0 in range(0, ts, rc):
        acc = jnp.zeros((rc, c), F32)
        for j in range(CONV_K):
            acc = acc + ypad_ref[r0 + off + j:r0 + off + j + rc, :] * dw_ref[j:j + 1, :]
        y = _ln(acc + dwb_ref[...], g_ref[...], b_ref[...])
        o_ref[0, r0:r0 + rc, :] = _silu(y).astype(o_ref.dtype)


def _conv_call(pl_all, b_in, dw, dw_b, ln_g, ln_b):
    bsz, s, _ = pl_all.shape
    ts = min(s, 512)
    c = BRANCH_DIM
    hb = ts // CONV_HALO
    last = s // CONV_HALO - 1
    dw_pad = jnp.zeros((32, c), F32).at[:CONV_K].set(dw)
    return pl.pallas_call(
        _conv_kernel,
        grid=(bsz, s // ts),
        in_specs=[pl.BlockSpec((1, ts, 2 * c), lambda b, i: (b, i, 0)),
                  pl.BlockSpec((1, CONV_HALO, 2 * c), lambda b, i: (b, jnp.maximum(i * hb - 1, 0), 0)),
                  pl.BlockSpec((1, CONV_HALO, 2 * c), lambda b, i: (b, jnp.minimum((i + 1) * hb, last), 0)),
                  pl.BlockSpec((1, 2 * c), lambda b, i: (0, 0)),
                  pl.BlockSpec((32, c), lambda b, i: (0, 0)),
                  pl.BlockSpec((1, c), lambda b, i: (0, 0)),
                  pl.BlockSpec((1, c), lambda b, i: (0, 0)),
                  pl.BlockSpec((1, c), lambda b, i: (0, 0))],
        out_specs=pl.BlockSpec((1, ts, c), lambda b, i: (b, i, 0)),
        out_shape=jax.ShapeDtypeStruct((bsz, s, c), BF16),
        scratch_shapes=[pltpu.VMEM((ts + 2 * CONV_HALO, c), F32)],
        compiler_params=_params(("arbitrary", "arbitrary")),
        name="mixer_conv",
    )(pl_all, pl_all, pl_all, b_in.reshape(1, 2 * c), dw_pad, dw_b.reshape(1, c), ln_g.reshape(1, c), ln_b.reshape(1, c))


def _gmlp_kernel(z_ref, g_ref, b_ref, ws_ref, bs_ref, o_ref):
    c = BRANCH_DIM
    gw = c // GMLP_GROUPS
    ts = z_ref.shape[1]
    for n in range(ts // GMLP_CHUNK):
        rows = slice(n * GMLP_CHUNK, (n + 1) * GMLP_CHUNK)
        z = _gelu_tanh(z_ref[0, rows, :])
        u = z[:, :c]
        v = _ln(z[:, c:], g_ref[...], b_ref[...]).astype(BF16)
        for g in range(GMLP_GROUPS):
            cols = slice(g * gw, (g + 1) * gw)
            sg = _dot(ws_ref[g], v[:, cols]) + bs_ref[:, g:g + 1]
            o_ref[0, rows, cols] = (u[:, cols] * sg).astype(o_ref.dtype)


def _gmlp_call(pl_all, ln_g, ln_b, ws, bs):
    bsz, s, _ = pl_all.shape
    ts = min(s, 512)
    c = BRANCH_DIM
    return pl.pallas_call(
        _gmlp_kernel,
        grid=(bsz, s // ts),
        in_specs=[pl.BlockSpec((1, ts, 2 * c), lambda b, i: (b, i, COL_B // (2 * c))),
                  pl.BlockSpec((1, c), lambda b, i: (0, 0)),
                  pl.BlockSpec((1, c), lambda b, i: (0, 0)),
                  pl.BlockSpec((GMLP_GROUPS, GMLP_CHUNK, GMLP_CHUNK), lambda b, i: (0, 0, 0)),
                  pl.BlockSpec((GMLP_CHUNK, GMLP_GROUPS), lambda b, i: (0, 0))],
        out_specs=pl.BlockSpec((1, ts, c), lambda b, i: (b, i, 0)),
        out_shape=jax.ShapeDtypeStruct((bsz, s, c), BF16),
        compiler_params=_params(("arbitrary", "arbitrary")),
        name="mixer_gmlp",
    )(pl_all, ln_g.reshape(1, c), ln_b.reshape(1, c), ws.astype(BF16), bs.T)


def _rope_tables(s):
    half, quarter = HEAD_DIM // 2, HEAD_DIM // 4
    pos = np.arange(s)
    prow = (pos // GRID_W).astype(np.float32)
    pcol = (pos % GRID_W).astype(np.float32)
    inv = (ROPE_BASE ** (-np.arange(quarter, dtype=np.float32) / quarter)).astype(np.float32)
    lane = np.arange(128)
    in_head = lane % HEAD_DIM
    p = np.where((in_head < half)[None, :], prow[:, None], pcol[:, None]).astype(np.float32)
    ang = (p * inv[lane % quarter][None, :]).astype(np.float32).astype(np.float64)
    first = (in_head % half) < quarter
    cos = np.cos(ang)
    sin = np.where(first[None, :], -np.sin(ang), np.sin(ang))
    swap = np.zeros((128, 128), np.float32)
    for j in lane:
        swap[j + quarter if first[j] else j - quarter, j] = 1.0
    return cos.astype(np.float32), sin.astype(np.float32), swap


def _rope_kernel(q_ref, k_ref, cos_ref, sin_ref, swap_ref, qo_ref, ko_ref):
    cos, sin, swap = cos_ref[...], sin_ref[...], swap_ref[...]

    def rot(x):
        parts = []
        for t in range(x.shape[1] // 128):
            xt = x[:, t * 128:(t + 1) * 128]
            parts.append(xt * cos + _dot(xt.astype(BF16), swap) * sin)
        return parts[0] if len(parts) == 1 else jnp.concatenate(parts, axis=1)

    qo_ref[0] = rot(q_ref[0])
    ko_ref[0] = rot(k_ref[0])


def _rope_call(pl_all):
    bsz, s, _ = pl_all.shape
    tm = 512
    cos, sin, swap = _rope_tables(s)
    qw, kw = WIN_HEADS * HEAD_DIM, WIN_KV_HEADS * HEAD_DIM
    return pl.pallas_call(
        _rope_kernel,
        grid=(bsz, s // tm),
        in_specs=[pl.BlockSpec((1, tm, qw), lambda b, i: (b, i, COL_CQ // qw)),
                  pl.BlockSpec((1, tm, kw), lambda b, i: (b, i, COL_CK // kw)),
                  pl.BlockSpec((tm, 128), lambda b, i: (i, 0)),
                  pl.BlockSpec((tm, 128), lambda b, i: (i, 0)),
                  pl.BlockSpec((128, 128), lambda b, i: (0, 0))],
        out_specs=[pl.BlockSpec((1, tm, qw), lambda b, i: (b, i, 0)),
                   pl.BlockSpec((1, tm, kw), lambda b, i: (b, i, 0))],
        out_shape=[jax.ShapeDtypeStruct((bsz, s, qw), F32), jax.ShapeDtypeStruct((bsz, s, kw), F32)],
        compiler_params=_params(("arbitrary", "arbitrary")),
        name="rope",
    )(pl_all, pl_all, jnp.asarray(cos), jnp.asarray(sin), jnp.asarray(swap, BF16))


WIN_STEP = 4


def _win_kernel(sink_ref, q_ref, kp_ref, kc_ref, kn_ref, vp_ref, vc_ref, vn_ref, kx_ref, vx_ref, o_ref, *, seq):
    g = pl.program_id(1)
    j = pl.program_id(2)
    grp = q_ref.shape[1]
    wb = WIN_BLOCK
    m_rows = grp * wb
    k_ext = jnp.concatenate([kp_ref[0, 0], kc_ref[0, 0], kn_ref[0, 0]], axis=0).astype(BF16)
    v_ext = jnp.concatenate([vp_ref[0, 0], vc_ref[0, 0], vn_ref[0, 0]], axis=0).astype(BF16)
    k_ctx = kx_ref[0, 0].astype(BF16)
    v_ctx = vx_ref[0, 0].astype(BF16)
    sink = jnp.concatenate([jnp.full((wb, 1), sink_ref[g * grp + h], F32) for h in range(grp)], axis=0)
    row = lax.broadcasted_iota(jnp.int32, (m_rows, 3 * wb), 0) & (wb - 1)
    col = lax.broadcasted_iota(jnp.int32, (m_rows, 3 * wb), 1)
    for t in range(WIN_STEP):
        n = j * WIN_STEP + t
        qpos = n * wb + row
        kpos = (n - 1) * wb + col
        mask = (jnp.abs(qpos - kpos) <= WINDOW) & (kpos >= 0) & (kpos < seq)
        q = jnp.concatenate([q_ref[0, h, t * wb:(t + 1) * wb, :] for h in range(grp)], axis=0)
        q = (q * (HEAD_DIM ** -0.5)).astype(BF16)
        s_loc = jnp.where(mask, _dot_nt(q, k_ext[t * wb:(t + 3) * wb, :]), NEG)
        s_ctx = _dot_nt(q, k_ctx)
        m = jnp.maximum(jnp.maximum(jnp.max(s_loc, axis=-1, keepdims=True),
                                    jnp.max(s_ctx, axis=-1, keepdims=True)), sink)
        p_loc = jnp.exp(s_loc - m)
        p_ctx = jnp.exp(s_ctx - m)
        den = (jnp.sum(p_loc, axis=-1, keepdims=True) + jnp.sum(p_ctx, axis=-1, keepdims=True)
               + jnp.exp(sink - m))
        o = _dot(p_loc.astype(BF16), v_ext[t * wb:(t + 3) * wb, :]) + _dot(p_ctx.astype(BF16), v_ctx)
        o = (o / den).astype(o_ref.dtype)
        for h in range(grp):
            o_ref[0, h, t * wb:(t + 1) * wb, :] = o[h * wb:(h + 1) * wb, :]


def _win_call(q, k, v, kx, vx, sink):
    bsz, hq, s, dh = q.shape
    hkv = k.shape[1]
    grp = hq // hkv
    cx = kx.shape[2]
    nb = s // WIN_BLOCK
    step = WIN_STEP * WIN_BLOCK
    blk = (1, 1, WIN_BLOCK, dh)
    prev = lambda b, g, j, sk: (b, g, jnp.maximum(j * WIN_STEP - 1, 0), 0)
    cur = lambda b, g, j, sk: (b, g, j, 0)
    nxt = lambda b, g, j, sk: (b, g, jnp.minimum((j + 1) * WIN_STEP, nb - 1), 0)
    ctx = lambda b, g, j, sk: (b, g, 0, 0)
    return pl.pallas_call(
        functools.partial(_win_kernel, seq=s),
        grid_spec=pltpu.PrefetchScalarGridSpec(
            num_scalar_prefetch=1,
            grid=(bsz, hkv, nb // WIN_STEP),
            in_specs=[pl.BlockSpec((1, grp, step, dh), cur),
                      pl.BlockSpec(blk, prev), pl.BlockSpec((1, 1, step, dh), cur), pl.BlockSpec(blk, nxt),
                      pl.BlockSpec(blk, prev), pl.BlockSpec((1, 1, step, dh), cur), pl.BlockSpec(blk, nxt),
                      pl.BlockSpec((1, 1, cx, dh), ctx), pl.BlockSpec((1, 1, cx, dh), ctx)],
            out_specs=pl.BlockSpec((1, grp, step, dh), cur)),
        out_shape=jax.ShapeDtypeStruct((bsz, hq, s, dh), BF16),
        compiler_params=_params(("arbitrary", "arbitrary", "arbitrary")),
        name="mixer_window_attn",
    )(sink, q, k, k, k, v, v, v, kx, vx)


def _na_bias_table(rpb):
    qc = np.arange(GRID_W)
    kc = np.arange(GRID_W)
    cs = np.clip(qc - NA_COLS // 2, 0, GRID_W - NA_COLS)
    cmask = (kc[None, :] >= cs[:, None]) & (kc[None, :] < cs[:, None] + NA_COLS)
    ci = np.clip(kc[None, :] - qc[:, None] + NA_COLS - 1, 0, 2 * NA_COLS - 2)
    pick = (ci[None] == np.arange(2 * NA_COLS - 1)[:, None, None]).astype(np.float32)
    t15 = jnp.einsum('hrc,cqk->hrqk', rpb, jnp.asarray(pick), precision=lax.Precision.HIGHEST)
    t15 = jnp.where(cmask[None, None], t15, NEG)
    tab = jnp.stack([t15[:, d0:d0 + NA_ROWS] for d0 in range(NA_ROWS)], axis=1)
    return tab.transpose(0, 1, 3, 2, 4).reshape(rpb.shape[0], NA_ROWS, GRID_W, NA_ROWS * GRID_W)


def _na_kernel(q_ref, k_ref, v_ref, kx_ref, vx_ref, bias_ref, o_ref):
    rows = q_ref.shape[2] // GRID_W
    k_ctx = kx_ref[0, 0].astype(BF16)
    v_ctx = vx_ref[0, 0].astype(BF16)
    k_all = k_ref[0, 0].astype(BF16)
    v_all = v_ref[0, 0].astype(BF16)
    span = NA_ROWS * GRID_W
    chunk = 8
    for r0 in range(0, rows, chunk):
        q = (q_ref[0, 0, r0 * GRID_W:(r0 + chunk) * GRID_W, :] * (HEAD_DIM ** -0.5)).astype(BF16)
        s_parts = []
        for r in range(r0, r0 + chunk):
            rs = min(max(r - NA_ROWS // 2, 0), rows - NA_ROWS)
            k_loc = k_all[rs * GRID_W:rs * GRID_W + span, :]
            qr = q[(r - r0) * GRID_W:(r - r0 + 1) * GRID_W, :]
            s_parts.append(_dot_nt(qr, k_loc) + bias_ref[0, rs - r + NA_ROWS - 1])
        s_loc = jnp.concatenate(s_parts, axis=0)
        s_ctx = _dot_nt(q, k_ctx)
        m = jnp.maximum(jnp.max(s_loc, axis=-1, keepdims=True), jnp.max(s_ctx, axis=-1, keepdims=True))
        p_loc = jnp.exp(s_loc - m)
        p_ctx = jnp.exp(s_ctx - m)
        den = jnp.sum(p_loc, axis=-1, keepdims=True) + jnp.sum(p_ctx, axis=-1, keepdims=True)
        p_loc = p_loc.astype(BF16)
        o_parts = []
        for r in range(r0, r0 + chunk):
            rs = min(max(r - NA_ROWS // 2, 0), rows - NA_ROWS)
            v_loc = v_all[rs * GRID_W:rs * GRID_W + span, :]
            o_parts.append(_dot(p_loc[(r - r0) * GRID_W:(r - r0 + 1) * GRID_W, :], v_loc))
        o = jnp.concatenate(o_parts, axis=0) + _dot(p_ctx.astype(BF16), v_ctx)
        o_ref[0, 0, r0 * GRID_W:(r0 + chunk) * GRID_W, :] = (o / den).astype(o_ref.dtype)


def _na_call(q, k, v, kx, vx, bias_tab):
    bsz, h, s, dh = q.shape
    cx = kx.shape[2]
    full = pl.BlockSpec((1, 1, s, dh), lambda b, hh: (b, hh, 0, 0))
    ctx = pl.BlockSpec((1, 1, cx, dh), lambda b, hh: (b, hh, 0, 0))
    return pl.pallas_call(
        _na_kernel,
        grid=(bsz, h),
        in_specs=[full, full, full, ctx, ctx,
                  pl.BlockSpec((1, NA_ROWS, GRID_W, NA_ROWS * GRID_W), lambda b, hh: (hh, 0, 0, 0))],
        out_specs=full,
        out_shape=jax.ShapeDtypeStruct((bsz, h, s, dh), BF16),
        compiler_params=_params(("arbitrary", "arbitrary")),
        name="mixer_neighbourhood_attn",
    )(q, k, v, kx, vx, bias_tab)


def _ctx_attn_kernel(sink_ref, q_ref, k_ref, v_ref, o_ref, *, use_sink):
    h = pl.program_id(1)
    q = (q_ref[0, 0] * (HEAD_DIM ** -0.5)).astype(BF16)
    s = _dot_nt(q, k_ref[0, 0].astype(BF16))
    m = jnp.max(s, axis=-1, keepdims=True)
    if use_sink:
        sink = sink_ref[h]
        m = jnp.maximum(m, sink)
    p = jnp.exp(s - m)
    den = jnp.sum(p, axis=-1, keepdims=True)
    if use_sink:
        den = den + jnp.exp(sink - m)
    o_ref[0, 0] = (_dot(p.astype(BF16), v_ref[0, 0].astype(BF16)) / den).astype(o_ref.dtype)


def _ctx_attn_call(q, k, v, sink):
    bsz, hq, cx, dh = q.shape
    grp = hq // k.shape[1]
    use_sink = sink is not None
    if sink is None:
        sink = jnp.zeros((hq,), F32)
    qs = pl.BlockSpec((1, 1, cx, dh), lambda b, h, sk: (b, h, 0, 0))
    ks = pl.BlockSpec((1, 1, cx, dh), lambda b, h, sk: (b, h // grp, 0, 0))
    return pl.pallas_call(
        functools.partial(_ctx_attn_kernel, use_sink=use_sink),
        grid_spec=pltpu.PrefetchScalarGridSpec(
            num_scalar_prefetch=1, grid=(bsz, hq), in_specs=[qs, ks, ks], out_specs=qs),
        out_shape=jax.ShapeDtypeStruct((bsz, hq, cx, dh), BF16),
        compiler_params=_params(("arbitrary", "arbitrary")),
        name="context_attn",
    )(sink, q, k, v)


def _merge_kernel(x_ref, ya_ref, yb_ref, yc_ref, yd_ref, wg0_ref, wg1_ref, wg2_ref, wg3_ref, wbr_ref, wout_ref,
                  ng_ref, sh_ref, sc_ref, gt_ref, o_ref, h_ref, acc_ref):
    n = pl.program_id(2)
    nt = acc_ref.shape[0]
    d = o_ref.shape[2]

    @pl.when(n == 0)
    def _():
        h = _rms(x_ref[0], ng_ref[0:1, :]) * (1.0 + sc_ref[0]) + sh_ref[0]
        h_ref[...] = h.astype(BF16)

    h = h_ref[...]
    acc = None
    for i, (y_ref, wg_ref) in enumerate(((ya_ref, wg0_ref), (yb_ref, wg1_ref), (yc_ref, wg2_ref), (yd_ref, wg3_ref))):
        t = _sigmoid(_dot(h, wg_ref[...])) * _dot(y_ref[0], wbr_ref[i])
        acc = t if acc is None else acc + t
    acc_ref[n] = acc.astype(BF16)

    @pl.when(n == nt - 1)
    def _():
        a = jnp.concatenate([acc_ref[j] for j in range(nt)], axis=1)
        oc = d // 4
        for c in range(4):
            o_ref[0, :, c * oc:(c + 1) * oc] = _dot(a, wout_ref[:, c * oc:(c + 1) * oc])
        o_ref[0] = x_ref[0] + gt_ref[0] * _rms(o_ref[0], ng_ref[1:2, :])


def _merge_call(x, ys, w_gate, w_branch, w_out, norm_g, mods, shared_row):
    bsz, s, d = x.shape
    c = BRANCH_DIM
    tm = min(s, 512)
    tn = 256
    nt = d // tn
    row = (lambda b: b) if shared_row is None else (lambda b: shared_row)
    ysp = pl.BlockSpec((1, tm, c), lambda b, i, n: (b, i, 0))
    wg = [pl.BlockSpec((d, tn), functools.partial(lambda b, i, n, k: (0, k * nt + n), k=k)) for k in range(4)]
    mod = lambda col: pl.BlockSpec((1, 1, d), lambda b, i, n: (row(b), 0, col))
    return pl.pallas_call(
        _merge_kernel,
        grid=(bsz, s // tm, nt),
        in_specs=[pl.BlockSpec((1, tm, d), lambda b, i, n: (b, i, 0)), ysp, ysp, ysp, ysp, *wg,
                  pl.BlockSpec((4, c, tn), lambda b, i, n: (0, 0, n)),
                  pl.BlockSpec((d, d), lambda b, i, n: (0, 0), pipeline_mode=pl.Buffered(1)),
                  pl.BlockSpec((4, d), lambda b, i, n: (0, 0)),
                  mod(0), mod(1), mod(2)],
        out_specs=pl.BlockSpec((1, tm, d), lambda b, i, n: (b, i, 0)),
        out_shape=jax.ShapeDtypeStruct((bsz, s, d), F32),
        scratch_shapes=[pltpu.VMEM((tm, d), BF16), pltpu.VMEM((nt, tm, tn), BF16)],
        compiler_params=_params(("arbitrary", "arbitrary", "arbitrary")),
        name="mixer_merge",
    )(x, *ys, w_gate, w_gate, w_gate, w_gate, w_branch, w_out, norm_g, mods, mods, mods)


def _expert_row_ids():
    r = np.arange(N_EXPERTS)
    return ((r % N_GROUPS) * GROUP_SIZE + r // N_GROUPS).astype(np.int32)


def _router_kernel(rows_ref, x_ref, ng_ref, sh_ref, sc_ref, rw_ref, rb_ref, eid_ref, tri_e_ref, tri_t_ref,
                   sw1_ref, sw3_ref, sw2_ref, h_ref, so_ref, e_ref, r_ref, w_ref, cnt_ref):
    i = pl.program_id(0)
    tm = x_ref.shape[0]

    @pl.when(i == 0)
    def _():
        cnt_ref[...] = jnp.zeros(cnt_ref.shape, F32)

    h = _rms(x_ref[...], ng_ref[2:3, :]) * (1.0 + sc_ref[0]) + sh_ref[0]
    h_ref[...] = _pack_halves(h)
    hb = h.astype(BF16)
    so_ref[...] = _dot((_silu(_dot(hb, sw1_ref[...])) * _dot(hb, sw3_ref[...])).astype(BF16), sw2_ref[...])

    h_tail = (h - hb.astype(F32)).astype(BF16)
    head = _dot_nt(rw_ref[...], hb)
    logits = head[:N_EXPERTS] + head[N_EXPERTS:] + _dot_nt(rw_ref[:N_EXPERTS, :], h_tail)
    scores = _sigmoid(logits)
    biased = scores + rb_ref[...]
    m1 = jnp.full((N_GROUPS, tm), -jnp.inf, F32)
    m2 = m1
    for j in range(GROUP_SIZE):
        v = biased[j * N_GROUPS:(j + 1) * N_GROUPS, :]
        m2 = jnp.maximum(m2, jnp.minimum(m1, v))
        m1 = jnp.maximum(m1, v)
    gs = m1 + m2
    gid = lax.broadcasted_iota(jnp.int32, (N_GROUPS, tm), 0)
    beat = jnp.zeros((N_GROUPS, tm), jnp.int32)
    for g in range(N_GROUPS):
        o = gs[g:g + 1, :]
        beat = beat + jnp.where((o > gs) | ((o == gs) & (g < gid)), 1, 0)
    keep = beat < TOPK_GROUPS
    masked = jnp.concatenate(
        [jnp.where(keep, biased[j * N_GROUPS:(j + 1) * N_GROUPS, :], NEG) for j in range(GROUP_SIZE)], axis=0)
    eid = eid_ref[...]
    row_ids = _expert_row_ids()
    beat = jnp.zeros((N_EXPERTS, tm), jnp.int32)
    for r in range(N_EXPERTS):
        o = masked[r:r + 1, :]
        beat = beat + jnp.where((o > masked) | ((o == masked) & (int(row_ids[r]) < eid)), 1, 0)
    sel = beat < TOP_K
    self32 = jnp.where(sel, 1.0, 0.0)
    selb = self32.astype(BF16)
    wsel = jnp.where(sel, scores, 0.0)
    wsel = wsel / jnp.sum(wsel, axis=0, keepdims=True) * ROUTE_SCALE
    rank = cnt_ref[:, 0:1] + _dot(selb, tri_t_ref[...])
    cnt_ref[...] = cnt_ref[...] + jnp.sum(self32, axis=1, keepdims=True)
    slot = _dot(tri_e_ref[...], selb)
    eidf = eid.astype(F32)
    for k in range(TOP_K):
        mk = sel & (slot == float(k))
        e_ref[k:k + 1, :] = jnp.sum(jnp.where(mk, eidf, 0.0), axis=0, keepdims=True).astype(jnp.int32)
        r_ref[k:k + 1, :] = jnp.sum(jnp.where(mk, rank, 0.0), axis=0, keepdims=True).astype(jnp.int32)
        w_ref[k:k + 1, :] = jnp.sum(jnp.where(mk, wsel, 0.0), axis=0, keepdims=True)


def _router_call(xcat, tile_rows, norm_g, mods, router_w, router_b, sw1, sw3, sw2):
    t, d = xcat.shape
    tm = 256
    f = sw1.shape[1]
    ids = _expert_row_ids()
    rw = router_w.T[ids]
    rw_head = rw.astype(BF16)
    rw = jnp.concatenate([rw_head, (rw - rw_head.astype(F32)).astype(BF16)], axis=0)
    rb = router_b[ids].reshape(N_EXPERTS, 1)
    tri_e = jnp.asarray(np.tril(np.ones((N_EXPERTS, N_EXPERTS), np.float32), -1), BF16)
    tri_t = jnp.asarray(np.triu(np.ones((tm, tm), np.float32), 1), BF16)
    const = lambda shape: pl.BlockSpec(shape, lambda i, rows: (0,) * len(shape))
    tok = pl.BlockSpec((tm, d), lambda i, rows: (i, 0))
    slots = pl.BlockSpec((TOP_K, tm), lambda i, rows: (0, i))
    return pl.pallas_call(
        _router_kernel,
        grid_spec=pltpu.PrefetchScalarGridSpec(
            num_scalar_prefetch=1,
            grid=(t // tm,),
            in_specs=[tok, const((4, d)),
                      pl.BlockSpec((1, 1, d), lambda i, rows: (rows[i], 0, 3)),
                      pl.BlockSpec((1, 1, d), lambda i, rows: (rows[i], 0, 4)),
                      const((2 * N_EXPERTS, d)), const((N_EXPERTS, 1)), const((N_EXPERTS, 1)),
                      const((N_EXPERTS, N_EXPERTS)), const((tm, tm)),
                      const((d, f)), const((d, f)), const((f, d))],
            out_specs=[pl.BlockSpec((tm, d // 2), lambda i, rows: (i, 0)), tok, slots, slots, slots,
                       const((N_EXPERTS, 128))]),
        out_shape=[jax.ShapeDtypeStruct((t, d // 2), jnp.uint32), jax.ShapeDtypeStruct((t, d), F32),
                   jax.ShapeDtypeStruct((TOP_K, t), jnp.int32), jax.ShapeDtypeStruct((TOP_K, t), jnp.int32),
                   jax.ShapeDtypeStruct((TOP_K, t), F32), jax.ShapeDtypeStruct((N_EXPERTS, 128), F32)],
        compiler_params=_params(("arbitrary",)),
        name="moe_router",
    )(tile_rows, xcat, norm_g, mods, mods, rw, rb, jnp.asarray(ids).reshape(N_EXPERTS, 1), tri_e, tri_t,
      sw1, sw3, sw2)


def _gmm_kernel(blk_ref, exp_ref, lo_ref, hi_ref, first_ref, newexp_ref, tok_cur_ref, tok_nxt_ref, h_hbm,
                w1_ref, w3_ref, w2_ref, o_ref, xa_ref, xb_ref, w1b_ref, w3b_ref, w2b_ref, sem):
    i = pl.program_id(0)
    last = pl.num_programs(0) - 1
    nrows = xa_ref.shape[0]
    xbufs = (xa_ref, xb_ref)

    def start_gather(tok_ref, s):
        for r in range(nrows):
            pltpu.make_async_copy(h_hbm.at[pl.ds(tok_ref[0, 0, r], 1)], xbufs[s].at[pl.ds(r, 1)],
                                  sem.at[s]).start(priority=r % 2)

    def wait_gather(s):
        for r in range(nrows):
            pltpu.make_async_copy(h_hbm.at[pl.ds(0, 1)], xbufs[s].at[pl.ds(r, 1)], sem.at[s]).wait()

    @pl.when(i == 0)
    def _():
        start_gather(tok_cur_ref, 0)

    @pl.when(first_ref[i] == 1)
    def _():
        o_ref[...] = jnp.zeros(o_ref.shape, o_ref.dtype)

    @pl.when(newexp_ref[i] == 1)
    def _():
        w1b_ref[...] = w1_ref[0].astype(BF16)
        w3b_ref[...] = w3_ref[0].astype(BF16)
        w2b_ref[...] = w2_ref[0].astype(BF16)

    def step(s):
        wait_gather(s)
        start_gather(tok_nxt_ref, 1 - s)
        x_lo, x_hi = _unpack_halves(xbufs[s][...])
        x = jnp.concatenate([x_lo.astype(BF16), x_hi.astype(BF16)], axis=1)
        a = (_silu(_dot(x, w1b_ref[...])) * _dot(x, w3b_ref[...])).astype(BF16)
        y = _pack_halves(_dot(a, w2b_ref[...]))
        rows = lax.broadcasted_iota(jnp.int32, (nrows, 1), 0)
        o_ref[...] = jnp.where((rows >= lo_ref[i]) & (rows < hi_ref[i]), y, o_ref[...])

        @pl.when(i == last)
        def _():
            wait_gather(1 - s)

    for s in range(2):
        pl.when(i % 2 == s)(functools.partial(step, s))


def _gmm_call(items, slot_tok, hp, w1, w3, w2, layer):
    dp = hp.shape[1]
    a = slot_tok.shape[0]
    d, f = w1.shape[2], w1.shape[3]
    n_items = items[0].shape[0]
    nb = a // MOE_BLOCK
    tok3 = slot_tok.reshape(nb, 1, MOE_BLOCK)
    rows = pl.BlockSpec((MOE_BLOCK, dp), lambda i, blk, ex, *_: (blk[i], 0))
    toks = lambda step: pl.BlockSpec((1, 1, MOE_BLOCK), lambda i, blk, ex, *_: (blk[step(i)], 0, 0),
                                     memory_space=pltpu.SMEM)
    return pl.pallas_call(
        _gmm_kernel,
        grid_spec=pltpu.PrefetchScalarGridSpec(
            num_scalar_prefetch=len(items),
            grid=(n_items,),
            in_specs=[toks(lambda i: i), toks(lambda i: jnp.minimum(i + 1, n_items - 1)),
                      pl.BlockSpec(memory_space=pl.ANY),
                      pl.BlockSpec((None, 1, d, f), lambda i, blk, ex, *_: (layer, ex[i], 0, 0)),
                      pl.BlockSpec((None, 1, d, f), lambda i, blk, ex, *_: (layer, ex[i], 0, 0)),
                      pl.BlockSpec((None, 1, f, d), lambda i, blk, ex, *_: (layer, ex[i], 0, 0))],
            out_specs=rows,
            scratch_shapes=[pltpu.VMEM((MOE_BLOCK, dp), jnp.uint32), pltpu.VMEM((MOE_BLOCK, dp), jnp.uint32),
                            pltpu.VMEM((d, f), BF16), pltpu.VMEM((d, f), BF16), pltpu.VMEM((f, d), BF16),
                            pltpu.SemaphoreType.DMA((2,))]),
        out_shape=jax.ShapeDtypeStruct((a, dp), jnp.uint32),
        compiler_params=_params(("arbitrary",)),
        name="moe_experts",
    )(*items, tok3, tok3, hp, w1, w3, w2)


def _moe_items(counts, n_rows):
    nb = n_rows // MOE_BLOCK
    n_items = nb + N_EXPERTS
    ends = jnp.cumsum(counts).astype(jnp.int32)
    bstart = jnp.arange(nb, dtype=jnp.int32) * MOE_BLOCK
    pos_b = jnp.arange(nb, dtype=jnp.int32) + jnp.sum(ends[None, :] < bstart[:, None], axis=1)
    idx = jnp.arange(n_items, dtype=jnp.int32)
    n_b = jnp.sum(pos_b[None, :] <= idx[:, None], axis=1).astype(jnp.int32)
    n_e = idx + 1 - n_b
    ends0 = jnp.concatenate([jnp.zeros((1,), jnp.int32), ends])
    last_end = jnp.sum(jnp.where(n_e[:, None] == jnp.arange(N_EXPERTS + 1)[None, :], ends0[None, :], 0), axis=1)
    start = jnp.maximum((n_b - 1) * MOE_BLOCK, last_end).astype(jnp.int32)
    stop = jnp.concatenate([start[1:], jnp.full((1,), n_rows, jnp.int32)])
    blk = jnp.minimum(start // MOE_BLOCK, nb - 1)
    ex = jnp.minimum(jnp.sum(ends[None, :] <= start[:, None], axis=1), N_EXPERTS - 1).astype(jnp.int32)
    lo = start - blk * MOE_BLOCK
    hi = stop - blk * MOE_BLOCK
    one = jnp.ones((1,), jnp.int32)
    first = jnp.concatenate([one, (blk[1:] != blk[:-1]).astype(jnp.int32)])
    newexp = jnp.concatenate([one, (ex[1:] != ex[:-1]).astype(jnp.int32)])
    return blk.astype(jnp.int32), ex, lo.astype(jnp.int32), hi.astype(jnp.int32), first, newexp


def _combine_kernel(rows_ref, dest_ref, ys_hbm, w_ref, so_ref, x_ref, ng_ref, gt_ref, o_ref, buf_ref, sem):
    tc = x_ref.shape[0]

    def copy(j, k):
        return pltpu.make_async_copy(ys_hbm.at[pl.ds(dest_ref[0, 0, k * tc + j], 1)], buf_ref.at[k, pl.ds(j, 1)], sem)

    def start(j, c):
        for k in range(TOP_K):
            copy(j, k).start(priority=k % 2)
        return c

    def wait(j, c):
        for k in range(TOP_K):
            copy(j, k).wait()
        return c

    lax.fori_loop(0, tc, start, 0)
    lax.fori_loop(0, tc, wait, 0)
    half = x_ref.shape[1] // 2
    fl_lo = so_ref[:, :half]
    fl_hi = so_ref[:, half:]
    for k in range(TOP_K):
        y_lo, y_hi = _unpack_halves(buf_ref[k])
        wk = w_ref[:, k:k + 1]
        fl_lo = fl_lo + y_lo * wk
        fl_hi = fl_hi + y_hi * wk
    fl = jnp.concatenate([fl_lo, fl_hi], axis=1)
    o_ref[...] = x_ref[...] + gt_ref[0] * _rms(fl, ng_ref[3:4, :])


def _combine_call(tile_rows, dest, ys, wsel, shared_out, xcat, norm_g, mods):
    t, d = xcat.shape
    tc = 128
    dest3 = dest.reshape(TOP_K, t // tc, tc).transpose(1, 0, 2).reshape(t // tc, 1, TOP_K * tc)
    tok = pl.BlockSpec((tc, d), lambda i, rows: (i, 0))
    return pl.pallas_call(
        _combine_kernel,
        grid_spec=pltpu.PrefetchScalarGridSpec(
            num_scalar_prefetch=1,
            grid=(t // tc,),
            in_specs=[pl.BlockSpec((1, 1, TOP_K * tc), lambda i, rows: (i, 0, 0), memory_space=pltpu.SMEM),
                      pl.BlockSpec(memory_space=pl.ANY),
                      pl.BlockSpec((tc, TOP_K), lambda i, rows: (i, 0)),
                      tok, tok,
                      pl.BlockSpec((4, d), lambda i, rows: (0, 0)),
                      pl.BlockSpec((1, 1, d), lambda i, rows: (rows[i], 0, 5))],
            out_specs=tok,
            scratch_shapes=[pltpu.VMEM((TOP_K, tc, d // 2), jnp.uint32), pltpu.SemaphoreType.DMA]),
        out_shape=jax.ShapeDtypeStruct((t, d), F32),
        compiler_params=_params(("arbitrary",)),
        name="moe_combine",
    )(tile_rows, dest3, ys, wsel.T, shared_out, xcat, norm_g, mods)


def _channel_mixer(xcat, mod_rows, norm_g, mods, router_w, router_b, w1, w3, w2, layer, sw1, sw3, sw2):
    t, _ = xcat.shape
    h, shared_out, eidx, rank, wsel, cnt = _router_call(xcat, mod_rows[::2], norm_g, mods, router_w, router_b,
                                                         sw1, sw3, sw2)
    counts = jnp.zeros((N_EXPERTS,), jnp.int32).at[_expert_row_ids()].set(cnt[:, 0].astype(jnp.int32))
    starts = jnp.cumsum(counts) - counts
    sel = eidx[:, :, None] == jnp.arange(N_EXPERTS, dtype=jnp.int32)[None, None, :]
    dest = jnp.sum(jnp.where(sel, starts[None, None, :], 0), axis=-1) + rank
    slot_tok = (jnp.argsort(dest.reshape(-1)) % t).astype(jnp.int32)
    ys = _gmm_call(_moe_items(counts, t * TOP_K), slot_tok, h, w1, w3, w2, layer)
    return _combine_call(mod_rows, dest, ys, wsel, shared_out, xcat, norm_g, mods)


def _heads(t, n):
    b, s, _ = t.shape
    return t.reshape(b, s, n, HEAD_DIM).transpose(0, 2, 1, 3)


def _unheads(t):
    b, n, s, dh = t.shape
    return t.transpose(0, 2, 1, 3).reshape(b, s, n * dh)


def kernel(x, c, ctx, c_ctx, w_ada, b_ada, norm_g, w_in, conv_b_in, conv_dw, conv_dw_b, conv_ln_g, conv_ln_b, gmlp_ln_g, gmlp_ln_b, gmlp_ws, gmlp_bs, win_sink, na_rpb, w_branch, w_out, router_w, router_b, exp_w1, exp_w3, exp_w2, sh_w1, sh_w3, sh_w2):
    bsz, s, d = x.shape
    cx = ctx.shape[1]
    depth = w_ada.shape[0]
    ctx_row = bsz
    c_all = jnp.zeros((MODS_ROWS, d), F32).at[:bsz].set(c).at[ctx_row].set(c_ctx)
    mods_all = _ada_call(c_all, w_ada, b_ada)
    lat_rows = jnp.repeat(jnp.arange(bsz, dtype=jnp.int32), s // 128)
    ctx_rows = jnp.full((bsz * cx // 128,), ctx_row, jnp.int32)
    xl, xc = x, ctx
    for l in range(depth):
        last = l == depth - 1
        mods = mods_all[l].reshape(MODS_ROWS, 1, N_MODS * d)
        g_off = w_in.shape[2] - 4 * d
        kv_off = g_off - KV_COLS
        w_small = jnp.concatenate([w_in[l, :, :kv_off], w_in[l, :, kv_off + 256:g_off],
                                   w_in[l, :, kv_off:kv_off + 256]], axis=1).astype(BF16)
        w_gate = w_in[l, :, g_off:].astype(BF16)
        w_br = w_branch[l].astype(BF16)
        w_o = w_out[l].astype(BF16)
        pl_l = _proj_call(xl, norm_g[l], mods, w_small, None, SMALL_COLS // 2)
        if last:
            pkv_c = _proj_call(xc, norm_g[l], mods, w_small[:, COL_DK:], ctx_row, KV_COLS)
            kv_base = 0
        else:
            pc = _proj_call(xc, norm_g[l], mods, w_small, ctx_row, SMALL_COLS // 2)
            pkv_c, kv_base = pc, COL_DK
        dkc = _heads(pkv_c[..., kv_base:kv_base + 512], NA_HEADS)
        dvc = _heads(pkv_c[..., kv_base + 512:kv_base + 1024], NA_HEADS)
        ckc = _heads(pkv_c[..., kv_base + 1024:kv_base + 1152], WIN_KV_HEADS)
        cvc = _heads(pkv_c[..., kv_base + 1152:kv_base + 1280], WIN_KV_HEADS)
        conv_args = (conv_b_in[l], conv_dw[l], conv_dw_b[l], conv_ln_g[l], conv_ln_b[l])
        gmlp_args = (gmlp_ln_g[l], gmlp_ln_b[l], gmlp_ws[l], gmlp_bs[l])
        ya = _conv_call(pl_l, *conv_args)
        yb = _gmlp_call(pl_l, *gmlp_args)
        q_rope, k_rope = _rope_call(pl_l)
        yc = _unheads(_win_call(_heads(q_rope, WIN_HEADS), _heads(k_rope, WIN_KV_HEADS),
                                _heads(pl_l[..., COL_CV:COL_CV + 128], WIN_KV_HEADS), ckc, cvc, win_sink[l]))
        yd = _unheads(_na_call(_heads(pl_l[..., COL_DQ:COL_DQ + 512], NA_HEADS),
                               _heads(pl_l[..., COL_DK:COL_DK + 512], NA_HEADS),
                               _heads(pl_l[..., COL_DV:COL_DV + 512], NA_HEADS), dkc, dvc,
                               _na_bias_table(na_rpb[l])))
        xl = _merge_call(xl, (ya, yb, yc, yd), w_gate, w_br, w_o, norm_g[l], mods, None)
        moe_w = (router_w[l], router_b[l], exp_w1, exp_w3, exp_w2, l,
                 sh_w1[l].astype(BF16), sh_w3[l].astype(BF16), sh_w2[l].astype(BF16))
        if last:
            xl = _channel_mixer(xl.reshape(bsz * s, d), lat_rows, norm_g[l], mods, *moe_w).reshape(bsz, s, d)
        else:
            yca = _conv_call(pc, *conv_args)
            ycb = _gmlp_call(pc, *gmlp_args)
            ycc = _unheads(_ctx_attn_call(_heads(pc[..., COL_CQ:COL_CQ + 512], WIN_HEADS), ckc, cvc, win_sink[l]))
            ycd = _unheads(_ctx_attn_call(_heads(pc[..., COL_DQ:COL_DQ + 512], NA_HEADS), dkc, dvc, None))
            xc = _merge_call(xc, (yca, ycb, ycc, ycd), w_gate, w_br, w_o, norm_g[l], mods, ctx_row)
            xcat = jnp.concatenate([xl.reshape(bsz * s, d), xc.reshape(bsz * cx, d)], axis=0)
            out = _channel_mixer(xcat, jnp.concatenate([lat_rows, ctx_rows]), norm_g[l], mods, *moe_w)
            xl = out[:bsz * s].reshape(bsz, s, d)
            xc = out[bsz * s:].reshape(bsz, cx, d)
    return xl
```

```python
import functools

import numpy as np
import jax
import jax.numpy as jnp
from jax import lax
from jax.experimental import pallas as pl
from jax.experimental.pallas import tpu as pltpu

F32 = jnp.float32
BF16 = jnp.bfloat16

GRID_W = 64
HEAD_DIM = 64
EPS = 1e-6
NEG = -1e30
BRANCH_DIM = 512
CONV_K = 31
CONV_HALO = 16
GMLP_CHUNK = 128
GMLP_GROUPS = 4
WIN_HEADS = 8
WIN_KV_HEADS = 2
WINDOW = 128
WIN_BLOCK = 128
ROPE_BASE = 10000.0
NA_HEADS = 8
NA_ROWS = 8
NA_COLS = 16
N_EXPERTS = 64
TOP_K = 8
N_GROUPS = 8
GROUP_SIZE = N_EXPERTS // N_GROUPS
TOPK_GROUPS = 4
ROUTE_SCALE = 2.5
MOE_BLOCK = 256
N_MODS = 6
MODS_ROWS = 16
VMEM_LIMIT = 52 * 1024 * 1024

COL_A, COL_B, COL_CQ, COL_DQ, COL_DK, COL_DV, COL_CK, COL_CV = 0, 1024, 2048, 2560, 3072, 3584, 4096, 4224
SMALL_COLS = 4352
KV_COLS = 1280


def _sigmoid(x):
    return 1.0 / (1.0 + jnp.exp(-x))


def _silu(x):
    return x * _sigmoid(x)


def _gelu_tanh(x):
    return 0.5 * x * (1.0 + jnp.tanh(np.sqrt(2.0 / np.pi).astype(np.float32) * (x + 0.044715 * (x * x * x))))


def _rms(x, g):
    return x * lax.rsqrt(jnp.mean(x * x, axis=-1, keepdims=True) + EPS) * g


def _ln(x, g, b):
    mu = jnp.mean(x, axis=-1, keepdims=True)
    xc = x - mu
    var = jnp.mean(xc * xc, axis=-1, keepdims=True)
    return xc * lax.rsqrt(var + EPS) * g + b


def _dot(a, b):
    return jnp.dot(a, b, preferred_element_type=F32)


def _dot_nt(a, b):
    return lax.dot_general(a, b, (((1,), (1,)), ((), ())), preferred_element_type=F32)


def _pack_halves(x):
    n = x.shape[1] // 2
    lo = lax.bitcast_convert_type(x[:, :n].astype(BF16).astype(F32), jnp.uint32)
    hi = lax.bitcast_convert_type(x[:, n:].astype(BF16).astype(F32), jnp.uint32)
    return (hi & jnp.uint32(0xFFFF0000)) | (lo >> 16)


def _unpack_halves(w):
    lo = lax.bitcast_convert_type(w << 16, F32)
    hi = lax.bitcast_convert_type(w & jnp.uint32(0xFFFF0000), F32)
    return lo, hi


def _params(sem):
    return pltpu.CompilerParams(dimension_semantics=sem, vmem_limit_bytes=VMEM_LIMIT)


def _ada_kernel(c_ref, w_ref, b_ref, o_ref):
    s = _silu(c_ref[...])
    o_ref[0] = _dot(s.astype(BF16), w_ref[0].astype(BF16)) + b_ref[0]


def _ada_call(c_all, w_ada, b_ada):
    depth, d, n = w_ada.shape
    tn = 512
    return pl.pallas_call(
        _ada_kernel,
        grid=(depth, n // tn),
        in_specs=[pl.BlockSpec((MODS_ROWS, d), lambda l, j: (0, 0)),
                  pl.BlockSpec((1, d, tn), lambda l, j: (l, 0, j)),
                  pl.BlockSpec((1, 1, tn), lambda l, j: (l, 0, j))],
        out_specs=pl.BlockSpec((1, MODS_ROWS, tn), lambda l, j: (l, 0, j)),
        out_shape=jax.ShapeDtypeStruct((depth, MODS_ROWS, n), F32),
        compiler_params=_params(("arbitrary", "arbitrary")),
        name="ada_mods",
    )(c_all, w_ada, b_ada.reshape(depth, 1, n))


def _proj_kernel(x_ref, g_ref, sh_ref, sc_ref, w_ref, o_ref):
    h = _rms(x_ref[0], g_ref[0:1, :]) * (1.0 + sc_ref[0]) + sh_ref[0]
    o_ref[0] = _dot(h.astype(BF16), w_ref[...])


def _proj_call(x, norm_g, mods, w, shared_row, tn):
    bsz, s, d = x.shape
    n = w.shape[1]
    tm = min(s, 512)
    row = (lambda b: b) if shared_row is None else (lambda b: shared_row)
    return pl.pallas_call(
        _proj_kernel,
        grid=(n // tn, bsz, s // tm),
        in_specs=[pl.BlockSpec((1, tm, d), lambda j, b, i: (b, i, 0)),
                  pl.BlockSpec((4, d), lambda j, b, i: (0, 0)),
                  pl.BlockSpec((1, 1, d), lambda j, b, i: (row(b), 0, 0)),
                  pl.BlockSpec((1, 1, d), lambda j, b, i: (row(b), 0, 1)),
                  pl.BlockSpec((d, tn), lambda j, b, i: (0, j))],
        out_specs=pl.BlockSpec((1, tm, tn), lambda j, b, i: (b, i, j)),
        out_shape=jax.ShapeDtypeStruct((bsz, s, n), F32),
        compiler_params=_params(("arbitrary", "arbitrary", "arbitrary")),
        name="proj_in",
    )(x, norm_g, mods, mods, w)


def _conv_kernel(a_ref, ap_ref, an_ref, bin_ref, dw_ref, dwb_ref, g_ref, b_ref, o_ref, ypad_ref):
    i = pl.program_id(1)
    nblk = pl.num_programs(1)
    ts = a_ref.shape[1]
    c = BRANCH_DIM

    def glu(a):
        a = a + bin_ref[...]
        return a[:, :c] * _sigmoid(a[:, c:])

    ypad_ref[0:CONV_HALO, :] = jnp.where(i > 0, glu(ap_ref[0]), 0.0)
    ypad_ref[CONV_HALO:CONV_HALO + ts, :] = glu(a_ref[0])
    ypad_ref[CONV_HALO + ts:2 * CONV_HALO + ts, :] = jnp.where(i < nblk - 1, glu(an_ref[0]), 0.0)
    rc = 64
    off = CONV_HALO - CONV_K // 2
    for r0 in range(0, ts, rc):
        acc = jnp.zeros((rc, c), F32)
        for j in range(CONV_K):
            acc = acc + ypad_ref[r0 + off + j:r0 + off + j + rc, :] * dw_ref[j:j + 1, :]
        y = _ln(acc + dwb_ref[...], g_ref[...], b_ref[...])
        o_ref[0, r0:r0 + rc, :] = _silu(y).astype(o_ref.dtype)


def _conv_call(pl_all, b_in, dw, dw_b, ln_g, ln_b):
    bsz, s, _ = pl_all.shape
    ts = min(s, 512)
    c = BRANCH_DIM
    hb = ts // CONV_HALO
    last = s // CONV_HALO - 1
    dw_pad = jnp.zeros((32, c), F32).at[:CONV_K].set(dw)
    return pl.pallas_call(
        _conv_kernel,
        grid=(bsz, s // ts),
        in_specs=[pl.BlockSpec((1, ts, 2 * c), lambda b, i: (b, i, 0)),
                  pl.BlockSpec((1, CONV_HALO, 2 * c), lambda b, i: (b, jnp.maximum(i * hb - 1, 0), 0)),
                  pl.BlockSpec((1, CONV_HALO, 2 * c), lambda b, i: (b, jnp.minimum((i + 1) * hb, last), 0)),
                  pl.BlockSpec((1, 2 * c), lambda b, i: (0, 0)),
                  pl.BlockSpec((32, c), lambda b, i: (0, 0)),
                  pl.BlockSpec((1, c), lambda b, i: (0, 0)),
                  pl.BlockSpec((1, c), lambda b, i: (0, 0)),
                  pl.BlockSpec((1, c), lambda b, i: (0, 0))],
        out_specs=pl.BlockSpec((1, ts, c), lambda b, i: (b, i, 0)),
        out_shape=jax.ShapeDtypeStruct((bsz, s, c), BF16),
        scratch_shapes=[pltpu.VMEM((ts + 2 * CONV_HALO, c), F32)],
        compiler_params=_params(("arbitrary", "arbitrary")),
        name="mixer_conv",
    )(pl_all, pl_all, pl_all, b_in.reshape(1, 2 * c), dw_pad, dw_b.reshape(1, c), ln_g.reshape(1, c), ln_b.reshape(1, c))


def _gmlp_kernel(z_ref, g_ref, b_ref, ws_ref, bs_ref, o_ref):
    c = BRANCH_DIM
    gw = c // GMLP_GROUPS
    ts = z_ref.shape[1]
    for n in range(ts // GMLP_CHUNK):
        rows = slice(n * GMLP_CHUNK, (n + 1) * GMLP_CHUNK)
        z = _gelu_tanh(z_ref[0, rows, :])
        u = z[:, :c]
        v = _ln(z[:, c:], g_ref[...], b_ref[...]).astype(BF16)
        for g in range(GMLP_GROUPS):
            cols = slice(g * gw, (g + 1) * gw)
            sg = _dot(ws_ref[g], v[:, cols]) + bs_ref[:, g:g + 1]
            o_ref[0, rows, cols] = (u[:, cols] * sg).astype(o_ref.dtype)


def _gmlp_call(pl_all, ln_g, ln_b, ws, bs):
    bsz, s, _ = pl_all.shape
    ts = min(s, 512)
    c = BRANCH_DIM
    return pl.pallas_call(
        _gmlp_kernel,
        grid=(bsz, s // ts),
        in_specs=[pl.BlockSpec((1, ts, 2 * c), lambda b, i: (b, i, COL_B // (2 * c))),
                  pl.BlockSpec((1, c), lambda b, i: (0, 0)),
                  pl.BlockSpec((1, c), lambda b, i: (0, 0)),
                  pl.BlockSpec((GMLP_GROUPS, GMLP_CHUNK, GMLP_CHUNK), lambda b, i: (0, 0, 0)),
                  pl.BlockSpec((GMLP_CHUNK, GMLP_GROUPS), lambda b, i: (0, 0))],
        out_specs=pl.BlockSpec((1, ts, c), lambda b, i: (b, i, 0)),
        out_shape=jax.ShapeDtypeStruct((bsz, s, c), BF16),
        compiler_params=_params(("arbitrary", "arbitrary")),
        name="mixer_gmlp",
    )(pl_all, ln_g.reshape(1, c), ln_b.reshape(1, c), ws.astype(BF16), bs.T)


def _rope_tables(s):
    half, quarter = HEAD_DIM // 2, HEAD_DIM // 4
    pos = np.arange(s)
    prow = (pos // GRID_W).astype(np.float32)
    pcol = (pos % GRID_W).astype(np.float32)
    inv = (ROPE_BASE ** (-np.arange(quarter, dtype=np.float32) / quarter)).astype(np.float32)
    lane = np.arange(128)
    in_head = lane % HEAD_DIM
    p = np.where((in_head < half)[None, :], prow[:, None], pcol[:, None]).astype(np.float32)
    ang = (p * inv[lane % quarter][None, :]).astype(np.float32).astype(np.float64)
    first = (in_head % half) < quarter
    cos = np.cos(ang)
    sin = np.where(first[None, :], -np.sin(ang), np.sin(ang))
    swap = np.zeros((128, 128), np.float32)
    for j in lane:
        swap[j + quarter if first[j] else j - quarter, j] = 1.0
    return cos.astype(np.float32), sin.astype(np.float32), swap


def _rope_kernel(q_ref, k_ref, cos_ref, sin_ref, swap_ref, qo_ref, ko_ref):
    cos, sin, swap = cos_ref[...], sin_ref[...], swap_ref[...]

    def rot(x):
        parts = []
        for t in range(x.shape[1] // 128):
            xt = x[:, t * 128:(t + 1) * 128]
            parts.append(xt * cos + _dot(xt.astype(BF16), swap) * sin)
        return parts[0] if len(parts) == 1 else jnp.concatenate(parts, axis=1)

    qo_ref[0] = rot(q_ref[0])
    ko_ref[0] = rot(k_ref[0])


def _rope_call(pl_all):
    bsz, s, _ = pl_all.shape
    tm = 512
    cos, sin, swap = _rope_tables(s)
    qw, kw = WIN_HEADS * HEAD_DIM, WIN_KV_HEADS * HEAD_DIM
    return pl.pallas_call(
        _rope_kernel,
        grid=(bsz, s // tm),
        in_specs=[pl.BlockSpec((1, tm, qw), lambda b, i: (b, i, COL_CQ // qw)),
                  pl.BlockSpec((1, tm, kw), lambda b, i: (b, i, COL_CK // kw)),
                  pl.BlockSpec((tm, 128), lambda b, i: (i, 0)),
                  pl.BlockSpec((tm, 128), lambda b, i: (i, 0)),
                  pl.BlockSpec((128, 128), lambda b, i: (0, 0))],
        out_specs=[pl.BlockSpec((1, tm, qw), lambda b, i: (b, i, 0)),
                   pl.BlockSpec((1, tm, kw), lambda b, i: (b, i, 0))],
        out_shape=[jax.ShapeDtypeStruct((bsz, s, qw), F32), jax.ShapeDtypeStruct((bsz, s, kw), F32)],
        compiler_params=_params(("arbitrary", "arbitrary")),
        name="rope",
    )(pl_all, pl_all, jnp.asarray(cos), jnp.asarray(sin), jnp.asarray(swap, BF16))


WIN_STEP = 4


def _win_kernel(sink_ref, q_ref, kp_ref, kc_ref, kn_ref, vp_ref, vc_ref, vn_ref, kx_ref, vx_ref, o_ref, *, seq):
    g = pl.program_id(1)
    j = pl.program_id(2)
    grp = q_ref.shape[1]
    wb = WIN_BLOCK
    m_rows = grp * wb
    k_ext = jnp.concatenate([kp_ref[0, 0], kc_ref[0, 0], kn_ref[0, 0]], axis=0).astype(BF16)
    v_ext = jnp.concatenate([vp_ref[0, 0], vc_ref[0, 0], vn_ref[0, 0]], axis=0).astype(BF16)
    k_ctx = kx_ref[0, 0].astype(BF16)
    v_ctx = vx_ref[0, 0].astype(BF16)
    sink = jnp.concatenate([jnp.full((wb, 1), sink_ref[g * grp + h], F32) for h in range(grp)], axis=0)
    row = lax.broadcasted_iota(jnp.int32, (m_rows, 3 * wb), 0) & (wb - 1)
    col = lax.broadcasted_iota(jnp.int32, (m_rows, 3 * wb), 1)
    in_band = jnp.where(jnp.abs(col - wb - row) <= WINDOW, 1.0, 0.0)
    col1 = lax.broadcasted_iota(jnp.int32, (1, 3 * wb), 1)
    for t in range(WIN_STEP):
        n = j * WIN_STEP + t
        kpos = (n - 1) * wb + col1
        in_seq = jnp.where((kpos >= 0) & (kpos < seq), 1.0, 0.0)
        q = jnp.concatenate([q_ref[0, h, t * wb:(t + 1) * wb, :] for h in range(grp)], axis=0)
        q = (q * (HEAD_DIM ** -0.5)).astype(BF16)
        s_loc = jnp.where(in_band * in_seq > 0.5, _dot_nt(q, k_ext[t * wb:(t + 3) * wb, :]), NEG)
        s_ctx = _dot_nt(q, k_ctx)
        m = jnp.maximum(jnp.maximum(jnp.max(s_loc, axis=-1, keepdims=True),
                                    jnp.max(s_ctx, axis=-1, keepdims=True)), sink)
        p_loc = jnp.exp(s_loc - m)
        p_ctx = jnp.exp(s_ctx - m)
        den = (jnp.sum(p_loc, axis=-1, keepdims=True) + jnp.sum(p_ctx, axis=-1, keepdims=True)
               + jnp.exp(sink - m))
        o = _dot(p_loc.astype(BF16), v_ext[t * wb:(t + 3) * wb, :]) + _dot(p_ctx.astype(BF16), v_ctx)
        o = (o / den).astype(o_ref.dtype)
        for h in range(grp):
            o_ref[0, h, t * wb:(t + 1) * wb, :] = o[h * wb:(h + 1) * wb, :]


def _win_call(q, k, v, kx, vx, sink):
    bsz, hq, s, dh = q.shape
    hkv = k.shape[1]
    grp = hq // hkv
    cx = kx.shape[2]
    nb = s // WIN_BLOCK
    step = WIN_STEP * WIN_BLOCK
    blk = (1, 1, WIN_BLOCK, dh)
    prev = lambda b, g, j, sk: (b, g, jnp.maximum(j * WIN_STEP - 1, 0), 0)
    cur = lambda b, g, j, sk: (b, g, j, 0)
    nxt = lambda b, g, j, sk: (b, g, jnp.minimum((j + 1) * WIN_STEP, nb - 1), 0)
    ctx = lambda b, g, j, sk: (b, g, 0, 0)
    return pl.pallas_call(
        functools.partial(_win_kernel, seq=s),
        grid_spec=pltpu.PrefetchScalarGridSpec(
            num_scalar_prefetch=1,
            grid=(bsz, hkv, nb // WIN_STEP),
            in_specs=[pl.BlockSpec((1, grp, step, dh), cur),
                      pl.BlockSpec(blk, prev), pl.BlockSpec((1, 1, step, dh), cur), pl.BlockSpec(blk, nxt),
                      pl.BlockSpec(blk, prev), pl.BlockSpec((1, 1, step, dh), cur), pl.BlockSpec(blk, nxt),
                      pl.BlockSpec((1, 1, cx, dh), ctx), pl.BlockSpec((1, 1, cx, dh), ctx)],
            out_specs=pl.BlockSpec((1, grp, step, dh), cur)),
        out_shape=jax.ShapeDtypeStruct((bsz, hq, s, dh), BF16),
        compiler_params=_params(("arbitrary", "arbitrary", "arbitrary")),
        name="mixer_window_attn",
    )(sink, q, k, k, k, v, v, v, kx, vx)


def _na_bias_table(rpb):
    qc = np.arange(GRID_W)
    kc = np.arange(GRID_W)
    cs = np.clip(qc - NA_COLS // 2, 0, GRID_W - NA_COLS)
    cmask = (kc[None, :] >= cs[:, None]) & (kc[None, :] < cs[:, None] + NA_COLS)
    ci = np.clip(kc[None, :] - qc[:, None] + NA_COLS - 1, 0, 2 * NA_COLS - 2)
    pick = (ci[None] == np.arange(2 * NA_COLS - 1)[:, None, None]).astype(np.float32)
    t15 = jnp.einsum('hrc,cqk->hrqk', rpb, jnp.asarray(pick), precision=lax.Precision.HIGHEST)
    t15 = jnp.where(cmask[None, None], t15, NEG)
    tab = jnp.stack([t15[:, d0:d0 + NA_ROWS] for d0 in range(NA_ROWS)], axis=1)
    return tab.transpose(0, 1, 3, 2, 4).reshape(rpb.shape[0], NA_ROWS, GRID_W, NA_ROWS * GRID_W)


def _na_kernel(q_ref, k_ref, v_ref, kx_ref, vx_ref, bias_ref, o_ref):
    rows = q_ref.shape[2] // GRID_W
    k_ctx = kx_ref[0, 0].astype(BF16)
    v_ctx = vx_ref[0, 0].astype(BF16)
    k_all = k_ref[0, 0].astype(BF16)
    v_all = v_ref[0, 0].astype(BF16)
    span = NA_ROWS * GRID_W
    chunk = 8
    for r0 in range(0, rows, chunk):
        q = (q_ref[0, 0, r0 * GRID_W:(r0 + chunk) * GRID_W, :] * (HEAD_DIM ** -0.5)).astype(BF16)
        s_parts = []
        for r in range(r0, r0 + chunk):
            rs = min(max(r - NA_ROWS // 2, 0), rows - NA_ROWS)
            k_loc = k_all[rs * GRID_W:rs * GRID_W + span, :]
            qr = q[(r - r0) * GRID_W:(r - r0 + 1) * GRID_W, :]
            s_parts.append(_dot_nt(qr, k_loc) + bias_ref[0, rs - r + NA_ROWS - 1])
        s_loc = jnp.concatenate(s_parts, axis=0)
        s_ctx = _dot_nt(q, k_ctx)
        m = jnp.maximum(jnp.max(s_loc, axis=-1, keepdims=True), jnp.max(s_ctx, axis=-1, keepdims=True))
        p_loc = jnp.exp(s_loc - m)
        p_ctx = jnp.exp(s_ctx - m)
        den = jnp.sum(p_loc, axis=-1, keepdims=True) + jnp.sum(p_ctx, axis=-1, keepdims=True)
        p_loc = p_loc.astype(BF16)
        o_parts = []
        for r in range(r0, r0 + chunk):
            rs = min(max(r - NA_ROWS // 2, 0), rows - NA_ROWS)
            v_loc = v_all[rs * GRID_W:rs * GRID_W + span, :]
            o_parts.append(_dot(p_loc[(r - r0) * GRID_W:(r - r0 + 1) * GRID_W, :], v_loc))
        o = jnp.concatenate(o_parts, axis=0) + _dot(p_ctx.astype(BF16), v_ctx)
        o_ref[0, 0, r0 * GRID_W:(r0 + chunk) * GRID_W, :] = (o / den).astype(o_ref.dtype)


def _na_call(q, k, v, kx, vx, bias_tab):
    bsz, h, s, dh = q.shape
    cx = kx.shape[2]
    full = pl.BlockSpec((1, 1, s, dh), lambda b, hh: (b, hh, 0, 0))
    ctx = pl.BlockSpec((1, 1, cx, dh), lambda b, hh: (b, hh, 0, 0))
    return pl.pallas_call(
        _na_kernel,
        grid=(bsz, h),
        in_specs=[full, full, full, ctx, ctx,
                  pl.BlockSpec((1, NA_ROWS, GRID_W, NA_ROWS * GRID_W), lambda b, hh: (hh, 0, 0, 0))],
        out_specs=full,
        out_shape=jax.ShapeDtypeStruct((bsz, h, s, dh), BF16),
        compiler_params=_params(("arbitrary", "arbitrary")),
        name="mixer_neighbourhood_attn",
    )(q, k, v, kx, vx, bias_tab)


def _ctx_attn_kernel(sink_ref, q_ref, k_ref, v_ref, o_ref, *, use_sink):
    h = pl.program_id(1)
    q = (q_ref[0, 0] * (HEAD_DIM ** -0.5)).astype(BF16)
    s = _dot_nt(q, k_ref[0, 0].astype(BF16))
    m = jnp.max(s, axis=-1, keepdims=True)
    if use_sink:
        sink = sink_ref[h]
        m = jnp.maximum(m, sink)
    p = jnp.exp(s - m)
    den = jnp.sum(p, axis=-1, keepdims=True)
    if use_sink:
        den = den + jnp.exp(sink - m)
    o_ref[0, 0] = (_dot(p.astype(BF16), v_ref[0, 0].astype(BF16)) / den).astype(o_ref.dtype)


def _ctx_attn_call(q, k, v, sink):
    bsz, hq, cx, dh = q.shape
    grp = hq // k.shape[1]
    use_sink = sink is not None
    if sink is None:
        sink = jnp.zeros((hq,), F32)
    qs = pl.BlockSpec((1, 1, cx, dh), lambda b, h, sk: (b, h, 0, 0))
    ks = pl.BlockSpec((1, 1, cx, dh), lambda b, h, sk: (b, h // grp, 0, 0))
    return pl.pallas_call(
        functools.partial(_ctx_attn_kernel, use_sink=use_sink),
        grid_spec=pltpu.PrefetchScalarGridSpec(
            num_scalar_prefetch=1, grid=(bsz, hq), in_specs=[qs, ks, ks], out_specs=qs),
        out_shape=jax.ShapeDtypeStruct((bsz, hq, cx, dh), BF16),
        compiler_params=_params(("arbitrary", "arbitrary")),
        name="context_attn",
    )(sink, q, k, v)


def _merge_kernel(x_ref, ya_ref, yb_ref, yc_ref, yd_ref, wg0_ref, wg1_ref, wg2_ref, wg3_ref, wbr_ref, wout_ref,
                  ng_ref, sh_ref, sc_ref, gt_ref, o_ref, h_ref, acc_ref):
    n = pl.program_id(2)
    nt = acc_ref.shape[0]
    d = o_ref.shape[2]

    @pl.when(n == 0)
    def _():
        h = _rms(x_ref[0], ng_ref[0:1, :]) * (1.0 + sc_ref[0]) + sh_ref[0]
        h_ref[...] = h.astype(BF16)

    h = h_ref[...]
    acc = None
    for i, (y_ref, wg_ref) in enumerate(((ya_ref, wg0_ref), (yb_ref, wg1_ref), (yc_ref, wg2_ref), (yd_ref, wg3_ref))):
        t = _sigmoid(_dot(h, wg_ref[...])) * _dot(y_ref[0], wbr_ref[i])
        acc = t if acc is None else acc + t
    acc_ref[n] = acc.astype(BF16)

    @pl.when(n == nt - 1)
    def _():
        a = jnp.concatenate([acc_ref[j] for j in range(nt)], axis=1)
        oc = d // 4
        for c in range(4):
            o_ref[0, :, c * oc:(c + 1) * oc] = _dot(a, wout_ref[:, c * oc:(c + 1) * oc])
        o_ref[0] = x_ref[0] + gt_ref[0] * _rms(o_ref[0], ng_ref[1:2, :])


def _merge_call(x, ys, w_gate, w_branch, w_out, norm_g, mods, shared_row):
    bsz, s, d = x.shape
    c = BRANCH_DIM
    tm = min(s, 512)
    tn = 256
    nt = d // tn
    row = (lambda b: b) if shared_row is None else (lambda b: shared_row)
    ysp = pl.BlockSpec((1, tm, c), lambda b, i, n: (b, i, 0))
    wg = [pl.BlockSpec((d, tn), functools.partial(lambda b, i, n, k: (0, k * nt + n), k=k)) for k in range(4)]
    mod = lambda col: pl.BlockSpec((1, 1, d), lambda b, i, n: (row(b), 0, col))
    return pl.pallas_call(
        _merge_kernel,
        grid=(bsz, s // tm, nt),
        in_specs=[pl.BlockSpec((1, tm, d), lambda b, i, n: (b, i, 0)), ysp, ysp, ysp, ysp, *wg,
                  pl.BlockSpec((4, c, tn), lambda b, i, n: (0, 0, n)),
                  pl.BlockSpec((d, d), lambda b, i, n: (0, 0), pipeline_mode=pl.Buffered(1)),
                  pl.BlockSpec((4, d), lambda b, i, n: (0, 0)),
                  mod(0), mod(1), mod(2)],
        out_specs=pl.BlockSpec((1, tm, d), lambda b, i, n: (b, i, 0)),
        out_shape=jax.ShapeDtypeStruct((bsz, s, d), F32),
        scratch_shapes=[pltpu.VMEM((tm, d), BF16), pltpu.VMEM((nt, tm, tn), BF16)],
        compiler_params=_params(("arbitrary", "arbitrary", "arbitrary")),
        name="mixer_merge",
    )(x, *ys, w_gate, w_gate, w_gate, w_gate, w_branch, w_out, norm_g, mods, mods, mods)


def _expert_row_ids():
    r = np.arange(N_EXPERTS)
    return ((r % N_GROUPS) * GROUP_SIZE + r // N_GROUPS).astype(np.int32)


def _router_kernel(rows_ref, x_ref, ng_ref, sh_ref, sc_ref, rw_ref, rb_ref, eid_ref, tri_e_ref, tri_t_ref,
                   sw1_ref, sw3_ref, sw2_ref, h_ref, so_ref, e_ref, r_ref, w_ref, cnt_ref):
    i = pl.program_id(0)
    tm = x_ref.shape[0]

    @pl.when(i == 0)
    def _():
        cnt_ref[...] = jnp.zeros(cnt_ref.shape, F32)

    h = _rms(x_ref[...], ng_ref[2:3, :]) * (1.0 + sc_ref[0]) + sh_ref[0]
    h_ref[...] = _pack_halves(h)
    hb = h.astype(BF16)
    so_ref[...] = _dot((_silu(_dot(hb, sw1_ref[...])) * _dot(hb, sw3_ref[...])).astype(BF16), sw2_ref[...])

    h_tail = (h - hb.astype(F32)).astype(BF16)
    head = _dot_nt(rw_ref[...], hb)
    logits = head[:N_EXPERTS] + head[N_EXPERTS:] + _dot_nt(rw_ref[:N_EXPERTS, :], h_tail)
    scores = _sigmoid(logits)
    biased = scores + rb_ref[...]
    m1 = jnp.full((N_GROUPS, tm), -jnp.inf, F32)
    m2 = m1
    for j in range(GROUP_SIZE):
        v = biased[j * N_GROUPS:(j + 1) * N_GROUPS, :]
        m2 = jnp.maximum(m2, jnp.minimum(m1, v))
        m1 = jnp.maximum(m1, v)
    gs = m1 + m2
    gid = lax.broadcasted_iota(jnp.int32, (N_GROUPS, tm), 0)
    beat = jnp.zeros((N_GROUPS, tm), jnp.int32)
    for g in range(N_GROUPS):
        o = gs[g:g + 1, :]
        beat = beat + jnp.where((o > gs) | ((o == gs) & (g < gid)), 1, 0)
    keep = beat < TOPK_GROUPS
    masked = jnp.concatenate(
        [jnp.where(keep, biased[j * N_GROUPS:(j + 1) * N_GROUPS, :], NEG) for j in range(GROUP_SIZE)], axis=0)
    eid = eid_ref[...]
    row_ids = _expert_row_ids()
    beat = jnp.zeros((N_EXPERTS, tm), jnp.int32)
    for r in range(N_EXPERTS):
        o = masked[r:r + 1, :]
        beat = beat + jnp.where((o > masked) | ((o == masked) & (int(row_ids[r]) < eid)), 1, 0)
    sel = beat < TOP_K
    self32 = jnp.where(sel, 1.0, 0.0)
    selb = self32.astype(BF16)
    wsel = jnp.where(sel, scores, 0.0)
    wsel = wsel / jnp.sum(wsel, axis=0, keepdims=True) * ROUTE_SCALE
    rank = cnt_ref[:, 0:1] + _dot(selb, tri_t_ref[...])
    cnt_ref[...] = cnt_ref[...] + jnp.sum(self32, axis=1, keepdims=True)
    slot = _dot(tri_e_ref[...], selb)
    eidf = eid.astype(F32)
    for k in range(TOP_K):
        mk = sel & (slot == float(k))
        e_ref[k:k + 1, :] = jnp.sum(jnp.where(mk, eidf, 0.0), axis=0, keepdims=True).astype(jnp.int32)
        r_ref[k:k + 1, :] = jnp.sum(jnp.where(mk, rank, 0.0), axis=0, keepdims=True).astype(jnp.int32)
        w_ref[k:k + 1, :] = jnp.sum(jnp.where(mk, wsel, 0.0), axis=0, keepdims=True)


def _router_call(xcat, tile_rows, norm_g, mods, router_w, router_b, sw1, sw3, sw2):
    t, d = xcat.shape
    tm = 256
    f = sw1.shape[1]
    ids = _expert_row_ids()
    rw = router_w.T[ids]
    rw_head = rw.astype(BF16)
    rw = jnp.concatenate([rw_head, (rw - rw_head.astype(F32)).astype(BF16)], axis=0)
    rb = router_b[ids].reshape(N_EXPERTS, 1)
    tri_e = jnp.asarray(np.tril(np.ones((N_EXPERTS, N_EXPERTS), np.float32), -1), BF16)
    tri_t = jnp.asarray(np.triu(np.ones((tm, tm), np.float32), 1), BF16)
    const = lambda shape: pl.BlockSpec(shape, lambda i, rows: (0,) * len(shape))
    tok = pl.BlockSpec((tm, d), lambda i, rows: (i, 0))
    slots = pl.BlockSpec((TOP_K, tm), lambda i, rows: (0, i))
    return pl.pallas_call(
        _router_kernel,
        grid_spec=pltpu.PrefetchScalarGridSpec(
            num_scalar_prefetch=1,
            grid=(t // tm,),
            in_specs=[tok, const((4, d)),
                      pl.BlockSpec((1, 1, d), lambda i, rows: (rows[i], 0, 3)),
                      pl.BlockSpec((1, 1, d), lambda i, rows: (rows[i], 0, 4)),
                      const((2 * N_EXPERTS, d)), const((N_EXPERTS, 1)), const((N_EXPERTS, 1)),
                      const((N_EXPERTS, N_EXPERTS)), const((tm, tm)),
                      const((d, f)), const((d, f)), const((f, d))],
            out_specs=[pl.BlockSpec((tm, d // 2), lambda i, rows: (i, 0)), tok, slots, slots, slots,
                       const((N_EXPERTS, 128))]),
        out_shape=[jax.ShapeDtypeStruct((t, d // 2), jnp.uint32), jax.ShapeDtypeStruct((t, d), F32),
                   jax.ShapeDtypeStruct((TOP_K, t), jnp.int32), jax.ShapeDtypeStruct((TOP_K, t), jnp.int32),
                   jax.ShapeDtypeStruct((TOP_K, t), F32), jax.ShapeDtypeStruct((N_EXPERTS, 128), F32)],
        compiler_params=_params(("arbitrary",)),
        name="moe_router",
    )(tile_rows, xcat, norm_g, mods, mods, rw, rb, jnp.asarray(ids).reshape(N_EXPERTS, 1), tri_e, tri_t,
      sw1, sw3, sw2)


def _gmm_kernel(blk_ref, exp_ref, lo_ref, hi_ref, first_ref, newexp_ref, tok_cur_ref, tok_nxt_ref, h_hbm,
                w1_ref, w3_ref, w2_ref, o_ref, xa_ref, xb_ref, w1b_ref, w3b_ref, w2b_ref, sem):
    i = pl.program_id(0)
    last = pl.num_programs(0) - 1
    nrows = xa_ref.shape[0]
    xbufs = (xa_ref, xb_ref)

    def start_gather(tok_ref, s):
        for r in range(nrows):
            pltpu.make_async_copy(h_hbm.at[pl.ds(tok_ref[0, 0, r], 1)], xbufs[s].at[pl.ds(r, 1)],
                                  sem.at[s]).start(priority=r % 2)

    def wait_gather(s):
        for r in range(nrows):
            pltpu.make_async_copy(h_hbm.at[pl.ds(0, 1)], xbufs[s].at[pl.ds(r, 1)], sem.at[s]).wait()

    @pl.when(i == 0)
    def _():
        start_gather(tok_cur_ref, 0)

    @pl.when(first_ref[i] == 1)
    def _():
        o_ref[...] = jnp.zeros(o_ref.shape, o_ref.dtype)

    @pl.when(newexp_ref[i] == 1)
    def _():
        w1b_ref[...] = w1_ref[0].astype(BF16)
        w3b_ref[...] = w3_ref[0].astype(BF16)
        w2b_ref[...] = w2_ref[0].astype(BF16)

    def step(s):
        wait_gather(s)

        @pl.when(lo_ref[i] >= 0)
        def _():
            start_gather(tok_nxt_ref, 1 - s)

        x_lo, x_hi = _unpack_halves(xbufs[s][...])
        x = jnp.concatenate([x_lo.astype(BF16), x_hi.astype(BF16)], axis=1)
        a = (_silu(_dot(x, w1b_ref[...])) * _dot(x, w3b_ref[...])).astype(BF16)
        y = _pack_halves(_dot(a, w2b_ref[...]))
        rows = lax.broadcasted_iota(jnp.int32, (nrows, 1), 0)
        o_ref[...] = jnp.where((rows >= lo_ref[i]) & (rows < hi_ref[i]), y, o_ref[...])

        @pl.when(i == last)
        def _():
            wait_gather(1 - s)

    for s in range(2):
        pl.when(i % 2 == s)(functools.partial(step, s))


def _gmm_call(items, slot_tok, hp, w1, w3, w2, layer):
    dp = hp.shape[1]
    a = slot_tok.shape[0]
    d, f = w1.shape[2], w1.shape[3]
    n_items = items[0].shape[0]
    nb = a // MOE_BLOCK
    tok3 = slot_tok.reshape(nb, 1, MOE_BLOCK)
    rows = pl.BlockSpec((MOE_BLOCK, dp), lambda i, blk, ex, *_: (blk[i], 0))
    toks = lambda step: pl.BlockSpec((1, 1, MOE_BLOCK), lambda i, blk, ex, *_: (blk[step(i)], 0, 0),
                                     memory_space=pltpu.SMEM)
    return pl.pallas_call(
        _gmm_kernel,
        grid_spec=pltpu.PrefetchScalarGridSpec(
            num_scalar_prefetch=len(items),
            grid=(n_items,),
            in_specs=[toks(lambda i: i), toks(lambda i: jnp.minimum(i + 1, n_items - 1)),
                      pl.BlockSpec(memory_space=pl.ANY),
                      pl.BlockSpec((None, 1, d, f), lambda i, blk, ex, *_: (layer, ex[i], 0, 0)),
                      pl.BlockSpec((None, 1, d, f), lambda i, blk, ex, *_: (layer, ex[i], 0, 0)),
                      pl.BlockSpec((None, 1, f, d), lambda i, blk, ex, *_: (layer, ex[i], 0, 0))],
            out_specs=rows,
            scratch_shapes=[pltpu.VMEM((MOE_BLOCK, dp), jnp.uint32), pltpu.VMEM((MOE_BLOCK, dp), jnp.uint32),
                            pltpu.VMEM((d, f), BF16), pltpu.VMEM((d, f), BF16), pltpu.VMEM((f, d), BF16),
                            pltpu.SemaphoreType.DMA((2,))]),
        out_shape=jax.ShapeDtypeStruct((a, dp), jnp.uint32),
        compiler_params=_params(("arbitrary",)),
        name="moe_experts",
    )(*items, tok3, tok3, hp, w1, w3, w2)


def _moe_items(counts, n_rows):
    nb = n_rows // MOE_BLOCK
    n_items = nb + N_EXPERTS
    ends = jnp.cumsum(counts).astype(jnp.int32)
    bstart = jnp.arange(nb, dtype=jnp.int32) * MOE_BLOCK
    pos_b = jnp.arange(nb, dtype=jnp.int32) + jnp.sum(ends[None, :] < bstart[:, None], axis=1)
    idx = jnp.arange(n_items, dtype=jnp.int32)
    n_b = jnp.sum(pos_b[None, :] <= idx[:, None], axis=1).astype(jnp.int32)
    n_e = idx + 1 - n_b
    ends0 = jnp.concatenate([jnp.zeros((1,), jnp.int32), ends])
    last_end = jnp.sum(jnp.where(n_e[:, None] == jnp.arange(N_EXPERTS + 1)[None, :], ends0[None, :], 0), axis=1)
    start = jnp.maximum((n_b - 1) * MOE_BLOCK, last_end).astype(jnp.int32)
    stop = jnp.concatenate([start[1:], jnp.full((1,), n_rows, jnp.int32)])
    blk = jnp.minimum(start // MOE_BLOCK, nb - 1)
    ex = jnp.minimum(jnp.sum(ends[None, :] <= start[:, None], axis=1), N_EXPERTS - 1).astype(jnp.int32)
    lo = start - blk * MOE_BLOCK
    hi = stop - blk * MOE_BLOCK
    one = jnp.ones((1,), jnp.int32)
    first = jnp.concatenate([one, (blk[1:] != blk[:-1]).astype(jnp.int32)])
    newexp = jnp.concatenate([one, (ex[1:] != ex[:-1]).astype(jnp.int32)])
    return blk.astype(jnp.int32), ex, lo.astype(jnp.int32), hi.astype(jnp.int32), first, newexp


def _combine_kernel(rows_ref, dest_ref, dest_nxt_ref, ys_hbm, w_ref, so_ref, x_ref, ng_ref, gt_ref, o_ref,
                    buf_ref, sem):
    i = pl.program_id(0)
    tc = x_ref.shape[0]
    slot = i % 2

    def gather(idx_ref, s):
        def body(j, c):
            for k in range(TOP_K):
                pltpu.make_async_copy(ys_hbm.at[pl.ds(idx_ref[0, 0, k * tc + j], 1)], buf_ref.at[s, k, pl.ds(j, 1)],
                                      sem.at[s]).start(priority=k % 2)
            return c
        lax.fori_loop(0, tc, body, 0)

    @pl.when(i == 0)
    def _():
        gather(dest_ref, 0)

    @pl.when(i < pl.num_programs(0) - 1)
    def _():
        gather(dest_nxt_ref, 1 - slot)

    def wait(j, c):
        for k in range(TOP_K):
            pltpu.make_async_copy(ys_hbm.at[pl.ds(0, 1)], buf_ref.at[slot, k, pl.ds(j, 1)], sem.at[slot]).wait()
        return c

    lax.fori_loop(0, tc, wait, 0)
    half = x_ref.shape[1] // 2
    fl_lo = so_ref[:, :half]
    fl_hi = so_ref[:, half:]
    for k in range(TOP_K):
        y_lo, y_hi = _unpack_halves(buf_ref[slot, k])
        wk = w_ref[:, k:k + 1]
        fl_lo = fl_lo + y_lo * wk
        fl_hi = fl_hi + y_hi * wk
    fl = jnp.concatenate([fl_lo, fl_hi], axis=1)
    o_ref[...] = x_ref[...] + gt_ref[0] * _rms(fl, ng_ref[3:4, :])


def _combine_call(tile_rows, dest, ys, wsel, shared_out, xcat, norm_g, mods):
    t, d = xcat.shape
    tc = 128
    dest3 = dest.reshape(TOP_K, t // tc, tc).transpose(1, 0, 2).reshape(t // tc, 1, TOP_K * tc)
    tok = pl.BlockSpec((tc, d), lambda i, rows: (i, 0))
    nt = t // tc
    return pl.pallas_call(
        _combine_kernel,
        grid_spec=pltpu.PrefetchScalarGridSpec(
            num_scalar_prefetch=1,
            grid=(nt,),
            in_specs=[pl.BlockSpec((1, 1, TOP_K * tc), lambda i, rows: (i, 0, 0), memory_space=pltpu.SMEM),
                      pl.BlockSpec((1, 1, TOP_K * tc), lambda i, rows: (jnp.minimum(i + 1, nt - 1), 0, 0),
                                   memory_space=pltpu.SMEM),
                      pl.BlockSpec(memory_space=pl.ANY),
                      pl.BlockSpec((tc, TOP_K), lambda i, rows: (i, 0)),
                      tok, tok,
                      pl.BlockSpec((4, d), lambda i, rows: (0, 0)),
                      pl.BlockSpec((1, 1, d), lambda i, rows: (rows[i], 0, 5))],
            out_specs=tok,
            scratch_shapes=[pltpu.VMEM((2, TOP_K, tc, d // 2), jnp.uint32), pltpu.SemaphoreType.DMA((2,))]),
        out_shape=jax.ShapeDtypeStruct((t, d), F32),
        compiler_params=_params(("arbitrary",)),
        name="moe_combine",
    )(tile_rows, dest3, dest3, ys, wsel.T, shared_out, xcat, norm_g, mods)


def _channel_mixer(xcat, mod_rows, norm_g, mods, router_w, router_b, w1, w3, w2, layer, sw1, sw3, sw2):
    t, _ = xcat.shape
    h, shared_out, eidx, rank, wsel, cnt = _router_call(xcat, mod_rows[::2], norm_g, mods, router_w, router_b,
                                                         sw1, sw3, sw2)
    counts = jnp.zeros((N_EXPERTS,), jnp.int32).at[_expert_row_ids()].set(cnt[:, 0].astype(jnp.int32))
    starts = jnp.cumsum(counts) - counts
    sel = eidx[:, :, None] == jnp.arange(N_EXPERTS, dtype=jnp.int32)[None, None, :]
    dest = jnp.sum(jnp.where(sel, starts[None, None, :], 0), axis=-1) + rank
    slot_tok = (jnp.argsort(dest.reshape(-1)) % t).astype(jnp.int32)
    ys = _gmm_call(_moe_items(counts, t * TOP_K), slot_tok, h, w1, w3, w2, layer)
    return _combine_call(mod_rows, dest, ys, wsel, shared_out, xcat, norm_g, mods)


def _heads(t, n):
    b, s, _ = t.shape
    return t.reshape(b, s, n, HEAD_DIM).transpose(0, 2, 1, 3)


def _unheads(t):
    b, n, s, dh = t.shape
    return t.transpose(0, 2, 1, 3).reshape(b, s, n * dh)


def kernel(x, c, ctx, c_ctx, w_ada, b_ada, norm_g, w_in, conv_b_in, conv_dw, conv_dw_b, conv_ln_g, conv_ln_b, gmlp_ln_g, gmlp_ln_b, gmlp_ws, gmlp_bs, win_sink, na_rpb, w_branch, w_out, router_w, router_b, exp_w1, exp_w3, exp_w2, sh_w1, sh_w3, sh_w2):
    bsz, s, d = x.shape
    cx = ctx.shape[1]
    depth = w_ada.shape[0]
    ctx_row = bsz
    c_all = jnp.zeros((MODS_ROWS, d), F32).at[:bsz].set(c).at[ctx_row].set(c_ctx)
    mods_all = _ada_call(c_all, w_ada, b_ada)
    lat_rows = jnp.repeat(jnp.arange(bsz, dtype=jnp.int32), s // 128)
    ctx_rows = jnp.full((bsz * cx // 128,), ctx_row, jnp.int32)
    xl, xc = x, ctx
    for l in range(depth):
        last = l == depth - 1
        mods = mods_all[l].reshape(MODS_ROWS, 1, N_MODS * d)
        g_off = w_in.shape[2] - 4 * d
        kv_off = g_off - KV_COLS
        w_small = jnp.concatenate([w_in[l, :, :kv_off], w_in[l, :, kv_off + 256:g_off],
                                   w_in[l, :, kv_off:kv_off + 256]], axis=1).astype(BF16)
        w_gate = w_in[l, :, g_off:].astype(BF16)
        w_br = w_branch[l].astype(BF16)
        w_o = w_out[l].astype(BF16)
        pl_l = _proj_call(xl, norm_g[l], mods, w_small, None, SMALL_COLS // 2)
        if last:
            pkv_c = _proj_call(xc, norm_g[l], mods, w_small[:, COL_DK:], ctx_row, KV_COLS)
            kv_base = 0
        else:
            pc = _proj_call(xc, norm_g[l], mods, w_small, ctx_row, SMALL_COLS // 2)
            pkv_c, kv_base = pc, COL_DK
        dkc = _heads(pkv_c[..., kv_base:kv_base + 512], NA_HEADS)
        dvc = _heads(pkv_c[..., kv_base + 512:kv_base + 1024], NA_HEADS)
        ckc = _heads(pkv_c[..., kv_base + 1024:kv_base + 1152], WIN_KV_HEADS)
        cvc = _heads(pkv_c[..., kv_base + 1152:kv_base + 1280], WIN_KV_HEADS)
        conv_args = (conv_b_in[l], conv_dw[l], conv_dw_b[l], conv_ln_g[l], conv_ln_b[l])
        gmlp_args = (gmlp_ln_g[l], gmlp_ln_b[l], gmlp_ws[l], gmlp_bs[l])
        ya = _conv_call(pl_l, *conv_args)
        yb = _gmlp_call(pl_l, *gmlp_args)
        q_rope, k_rope = _rope_call(pl_l)
        yc = _unheads(_win_call(_heads(q_rope, WIN_HEADS), _heads(k_rope, WIN_KV_HEADS),
                                _heads(pl_l[..., COL_CV:COL_CV + 128], WIN_KV_HEADS), ckc, cvc, win_sink[l]))
        yd = _unheads(_na_call(_heads(pl_l[..., COL_DQ:COL_DQ + 512], NA_HEADS),
                               _heads(pl_l[..., COL_DK:COL_DK + 512], NA_HEADS),
                               _heads(pl_l[..., COL_DV:COL_DV + 512], NA_HEADS), dkc, dvc,
                               _na_bias_table(na_rpb[l])))
        xl = _merge_call(xl, (ya, yb, yc, yd), w_gate, w_br, w_o, norm_g[l], mods, None)
        moe_w = (router_w[l], router_b[l], exp_w1, exp_w3, exp_w2, l,
                 sh_w1[l].astype(BF16), sh_w3[l].astype(BF16), sh_w2[l].astype(BF16))
        if last:
            xl = _channel_mixer(xl.reshape(bsz * s, d), lat_rows, norm_g[l], mods, *moe_w).reshape(bsz, s, d)
        else:
            yca = _conv_call(pc, *conv_args)
            ycb = _gmlp_call(pc, *gmlp_args)
            ycc = _unheads(_ctx_attn_call(_heads(pc[..., COL_CQ:COL_CQ + 512], WIN_HEADS), ckc, cvc, win_sink[l]))
            ycd = _unheads(_ctx_attn_call(_heads(pc[..., COL_DQ:COL_DQ + 512], NA_HEADS), dkc, dvc, None))
            xc = _merge_call(xc, (yca, ycb, ycc, ycd), w_gate, w_br, w_o, norm_g[l], mods, ctx_row)
            xcat = jnp.concatenate([xl.reshape(bsz * s, d), xc.reshape(bsz * cx, d)], axis=0)
            out = _channel_mixer(xcat, jnp.concatenate([lat_rows, ctx_rows]), norm_g[l], mods, *moe_w)
            xl = out[:bsz * s].reshape(bsz, s, d)
            xc = out[bsz * s:].reshape(bsz, cx, d)
    return xl
```

```python
import functools

import numpy as np
import jax
import jax.numpy as jnp
from jax import lax
from jax.experimental import pallas as pl
from jax.experimental.pallas import tpu as pltpu

F32 = jnp.float32
BF16 = jnp.bfloat16

GRID_W = 64
HEAD_DIM = 64
EPS = 1e-6
NEG = -1e30
BRANCH_DIM = 512
CONV_K = 31
CONV_HALO = 16
GMLP_CHUNK = 128
GMLP_GROUPS = 4
WIN_HEADS = 8
WIN_KV_HEADS = 2
WINDOW = 128
WIN_BLOCK = 128
ROPE_BASE = 10000.0
NA_HEADS = 8
NA_ROWS = 8
NA_COLS = 16
N_EXPERTS = 64
TOP_K = 8
N_GROUPS = 8
GROUP_SIZE = N_EXPERTS // N_GROUPS
TOPK_GROUPS = 4
ROUTE_SCALE = 2.5
MOE_BLOCK = 256
N_MODS = 6
MODS_ROWS = 16
VMEM_LIMIT = 52 * 1024 * 1024

COL_A, COL_B, COL_CQ, COL_DQ, COL_DK, COL_DV, COL_CK, COL_CV = 0, 1024, 2048, 2560, 3072, 3584, 4096, 4224
SMALL_COLS = 4352
KV_COLS = 1280


def _sigmoid(x):
    return 1.0 / (1.0 + jnp.exp(-x))


def _silu(x):
    return x * _sigmoid(x)


def _gelu_tanh(x):
    return 0.5 * x * (1.0 + jnp.tanh(np.sqrt(2.0 / np.pi).astype(np.float32) * (x + 0.044715 * (x * x * x))))


def _rms(x, g):
    return x * lax.rsqrt(jnp.mean(x * x, axis=-1, keepdims=True) + EPS) * g


def _ln(x, g, b):
    mu = jnp.mean(x, axis=-1, keepdims=True)
    xc = x - mu
    var = jnp.mean(xc * xc, axis=-1, keepdims=True)
    return xc * lax.rsqrt(var + EPS) * g + b


def _dot(a, b):
    return jnp.dot(a, b, preferred_element_type=F32)


def _dot_nt(a, b):
    return lax.dot_general(a, b, (((1,), (1,)), ((), ())), preferred_element_type=F32)


def _pack_halves(x):
    n = x.shape[1] // 2
    lo = lax.bitcast_convert_type(x[:, :n].astype(BF16).astype(F32), jnp.uint32)
    hi = lax.bitcast_convert_type(x[:, n:].astype(BF16).astype(F32), jnp.uint32)
    return (hi & jnp.uint32(0xFFFF0000)) | (lo >> 16)


def _unpack_halves(w):
    lo = lax.bitcast_convert_type(w << 16, F32)
    hi = lax.bitcast_convert_type(w & jnp.uint32(0xFFFF0000), F32)
    return lo, hi


def _params(sem):
    return pltpu.CompilerParams(dimension_semantics=sem, vmem_limit_bytes=VMEM_LIMIT)


def _ada_kernel(c_ref, w_ref, b_ref, o_ref):
    s = _silu(c_ref[...])
    o_ref[0] = _dot(s.astype(BF16), w_ref[0].astype(BF16)) + b_ref[0]


def _ada_call(c_all, w_ada, b_ada):
    depth, d, n = w_ada.shape
    tn = 512
    return pl.pallas_call(
        _ada_kernel,
        grid=(depth, n // tn),
        in_specs=[pl.BlockSpec((MODS_ROWS, d), lambda l, j: (0, 0)),
                  pl.BlockSpec((1, d, tn), lambda l, j: (l, 0, j)),
                  pl.BlockSpec((1, 1, tn), lambda l, j: (l, 0, j))],
        out_specs=pl.BlockSpec((1, MODS_ROWS, tn), lambda l, j: (l, 0, j)),
        out_shape=jax.ShapeDtypeStruct((depth, MODS_ROWS, n), F32),
        compiler_params=_params(("arbitrary", "arbitrary")),
        name="ada_mods",
    )(c_all, w_ada, b_ada.reshape(depth, 1, n))


def _proj_kernel(x_ref, g_ref, sh_ref, sc_ref, w_ref, o_ref):
    h = _rms(x_ref[0], g_ref[0:1, :]) * (1.0 + sc_ref[0]) + sh_ref[0]
    o_ref[0] = _dot(h.astype(BF16), w_ref[...])


def _proj_call(x, norm_g, mods, w, shared_row, tn):
    bsz, s, d = x.shape
    n = w.shape[1]
    tm = min(s, 512)
    row = (lambda b: b) if shared_row is None else (lambda b: shared_row)
    return pl.pallas_call(
        _proj_kernel,
        grid=(n // tn, bsz, s // tm),
        in_specs=[pl.BlockSpec((1, tm, d), lambda j, b, i: (b, i, 0)),
                  pl.BlockSpec((4, d), lambda j, b, i: (0, 0)),
                  pl.BlockSpec((1, 1, d), lambda j, b, i: (row(b), 0, 0)),
                  pl.BlockSpec((1, 1, d), lambda j, b, i: (row(b), 0, 1)),
                  pl.BlockSpec((d, tn), lambda j, b, i: (0, j))],
        out_specs=pl.BlockSpec((1, tm, tn), lambda j, b, i: (b, i, j)),
        out_shape=jax.ShapeDtypeStruct((bsz, s, n), F32),
        compiler_params=_params(("arbitrary", "arbitrary", "arbitrary")),
        name="proj_in",
    )(x, norm_g, mods, mods, w)


def _conv_kernel(a_ref, ap_ref, an_ref, bin_ref, dw_ref, dwb_ref, g_ref, b_ref, o_ref, ypad_ref):
    i = pl.program_id(1)
    nblk = pl.num_programs(1)
    ts = a_ref.shape[1]
    c = BRANCH_DIM

    def glu(a):
        a = a + bin_ref[...]
        return a[:, :c] * _sigmoid(a[:, c:])

    ypad_ref[0:CONV_HALO, :] = jnp.where(i > 0, glu(ap_ref[0]), 0.0)
    ypad_ref[CONV_HALO:CONV_HALO + ts, :] = glu(a_ref[0])
    ypad_ref[CONV_HALO + ts:2 * CONV_HALO + ts, :] = jnp.where(i < nblk - 1, glu(an_ref[0]), 0.0)
    rc = 64
    off = CONV_HALO - CONV_K // 2
    for r0 in range(0, ts, rc):
        acc = jnp.zeros((rc, c), F32)
        for j in range(CONV_K):
            acc = acc + ypad_ref[r0 + off + j:r0 + off + j + rc, :] * dw_ref[j:j + 1, :]
        y = _ln(acc + dwb_ref[...], g_ref[...], b_ref[...])
        o_ref[0, r0:r0 + rc, :] = _silu(y).astype(o_ref.dtype)


def _conv_call(pl_all, b_in, dw, dw_b, ln_g, ln_b):
    bsz, s, _ = pl_all.shape
    ts = min(s, 512)
    c = BRANCH_DIM
    hb = ts // CONV_HALO
    last = s // CONV_HALO - 1
    dw_pad = jnp.zeros((32, c), F32).at[:CONV_K].set(dw)
    return pl.pallas_call(
        _conv_kernel,
        grid=(bsz, s // ts),
        in_specs=[pl.BlockSpec((1, ts, 2 * c), lambda b, i: (b, i, 0)),
                  pl.BlockSpec((1, CONV_HALO, 2 * c), lambda b, i: (b, jnp.maximum(i * hb - 1, 0), 0)),
                  pl.BlockSpec((1, CONV_HALO, 2 * c), lambda b, i: (b, jnp.minimum((i + 1) * hb, last), 0)),
                  pl.BlockSpec((1, 2 * c), lambda b, i: (0, 0)),
                  pl.BlockSpec((32, c), lambda b, i: (0, 0)),
                  pl.BlockSpec((1, c), lambda b, i: (0, 0)),
                  pl.BlockSpec((1, c), lambda b, i: (0, 0)),
                  pl.BlockSpec((1, c), lambda b, i: (0, 0))],
        out_specs=pl.BlockSpec((1, ts, c), lambda b, i: (b, i, 0)),
        out_shape=jax.ShapeDtypeStruct((bsz, s, c), BF16),
        scratch_shapes=[pltpu.VMEM((ts + 2 * CONV_HALO, c), F32)],
        compiler_params=_params(("arbitrary", "arbitrary")),
        name="mixer_conv",
    )(pl_all, pl_all, pl_all, b_in.reshape(1, 2 * c), dw_pad, dw_b.reshape(1, c), ln_g.reshape(1, c), ln_b.reshape(1, c))


def _gmlp_kernel(z_ref, g_ref, b_ref, ws_ref, bs_ref, o_ref):
    c = BRANCH_DIM
    gw = c // GMLP_GROUPS
    ts = z_ref.shape[1]
    for n in range(ts // GMLP_CHUNK):
        rows = slice(n * GMLP_CHUNK, (n + 1) * GMLP_CHUNK)
        z = _gelu_tanh(z_ref[0, rows, :])
        u = z[:, :c]
        v = _ln(z[:, c:], g_ref[...], b_ref[...]).astype(BF16)
        for g in range(GMLP_GROUPS):
            cols = slice(g * gw, (g + 1) * gw)
            sg = _dot(ws_ref[g], v[:, cols]) + bs_ref[:, g:g + 1]
            o_ref[0, rows, cols] = (u[:, cols] * sg).astype(o_ref.dtype)


def _gmlp_call(pl_all, ln_g, ln_b, ws, bs):
    bsz, s, _ = pl_all.shape
    ts = min(s, 512)
    c = BRANCH_DIM
    return pl.pallas_call(
        _gmlp_kernel,
        grid=(bsz, s // ts),
        in_specs=[pl.BlockSpec((1, ts, 2 * c), lambda b, i: (b, i, COL_B // (2 * c))),
                  pl.BlockSpec((1, c), lambda b, i: (0, 0)),
                  pl.BlockSpec((1, c), lambda b, i: (0, 0)),
                  pl.BlockSpec((GMLP_GROUPS, GMLP_CHUNK, GMLP_CHUNK), lambda b, i: (0, 0, 0)),
                  pl.BlockSpec((GMLP_CHUNK, GMLP_GROUPS), lambda b, i: (0, 0))],
        out_specs=pl.BlockSpec((1, ts, c), lambda b, i: (b, i, 0)),
        out_shape=jax.ShapeDtypeStruct((bsz, s, c), BF16),
        compiler_params=_params(("arbitrary", "arbitrary")),
        name="mixer_gmlp",
    )(pl_all, ln_g.reshape(1, c), ln_b.reshape(1, c), ws.astype(BF16), bs.T)


def _rope_tables(s):
    half, quarter = HEAD_DIM // 2, HEAD_DIM // 4
    pos = np.arange(s)
    prow = (pos // GRID_W).astype(np.float32)
    pcol = (pos % GRID_W).astype(np.float32)
    inv = (ROPE_BASE ** (-np.arange(quarter, dtype=np.float32) / quarter)).astype(np.float32)
    lane = np.arange(128)
    in_head = lane % HEAD_DIM
    p = np.where((in_head < half)[None, :], prow[:, None], pcol[:, None]).astype(np.float32)
    ang = (p * inv[lane % quarter][None, :]).astype(np.float32).astype(np.float64)
    first = (in_head % half) < quarter
    cos = np.cos(ang)
    sin = np.where(first[None, :], -np.sin(ang), np.sin(ang))
    swap = np.zeros((128, 128), np.float32)
    for j in lane:
        swap[j + quarter if first[j] else j - quarter, j] = 1.0
    return cos.astype(np.float32), sin.astype(np.float32), swap


def _rope_kernel(q_ref, k_ref, cos_ref, sin_ref, swap_ref, qo_ref, ko_ref):
    cos, sin, swap = cos_ref[...], sin_ref[...], swap_ref[...]

    def rot(x):
        parts = []
        for t in range(x.shape[1] // 128):
            xt = x[:, t * 128:(t + 1) * 128]
            parts.append(xt * cos + _dot(xt.astype(BF16), swap) * sin)
        return parts[0] if len(parts) == 1 else jnp.concatenate(parts, axis=1)

    qo_ref[0] = rot(q_ref[0])
    ko_ref[0] = rot(k_ref[0])


def _rope_call(pl_all):
    bsz, s, _ = pl_all.shape
    tm = 512
    cos, sin, swap = _rope_tables(s)
    qw, kw = WIN_HEADS * HEAD_DIM, WIN_KV_HEADS * HEAD_DIM
    return pl.pallas_call(
        _rope_kernel,
        grid=(bsz, s // tm),
        in_specs=[pl.BlockSpec((1, tm, qw), lambda b, i: (b, i, COL_CQ // qw)),
                  pl.BlockSpec((1, tm, kw), lambda b, i: (b, i, COL_CK // kw)),
                  pl.BlockSpec((tm, 128), lambda b, i: (i, 0)),
                  pl.BlockSpec((tm, 128), lambda b, i: (i, 0)),
                  pl.BlockSpec((128, 128), lambda b, i: (0, 0))],
        out_specs=[pl.BlockSpec((1, tm, qw), lambda b, i: (b, i, 0)),
                   pl.BlockSpec((1, tm, kw), lambda b, i: (b, i, 0))],
        out_shape=[jax.ShapeDtypeStruct((bsz, s, qw), F32), jax.ShapeDtypeStruct((bsz, s, kw), F32)],
        compiler_params=_params(("arbitrary", "arbitrary")),
        name="rope",
    )(pl_all, pl_all, jnp.asarray(cos), jnp.asarray(sin), jnp.asarray(swap, BF16))


WIN_STEP = 4


def _win_kernel(sink_ref, q_ref, kp_ref, kc_ref, kn_ref, vp_ref, vc_ref, vn_ref, kx_ref, vx_ref, o_ref, *, seq):
    g = pl.program_id(1)
    j = pl.program_id(2)
    grp = q_ref.shape[1]
    wb = WIN_BLOCK
    m_rows = grp * wb
    k_ext = jnp.concatenate([kp_ref[0, 0], kc_ref[0, 0], kn_ref[0, 0]], axis=0).astype(BF16)
    v_ext = jnp.concatenate([vp_ref[0, 0], vc_ref[0, 0], vn_ref[0, 0]], axis=0).astype(BF16)
    k_ctx = kx_ref[0, 0].astype(BF16)
    v_ctx = vx_ref[0, 0].astype(BF16)
    sink = jnp.concatenate([jnp.full((wb, 1), sink_ref[g * grp + h], F32) for h in range(grp)], axis=0)
    row = lax.broadcasted_iota(jnp.int32, (m_rows, 3 * wb), 0) & (wb - 1)
    col = lax.broadcasted_iota(jnp.int32, (m_rows, 3 * wb), 1)
    in_band = jnp.where(jnp.abs(col - wb - row) <= WINDOW, 1.0, 0.0)
    col1 = lax.broadcasted_iota(jnp.int32, (1, 3 * wb), 1)
    for t in range(WIN_STEP):
        n = j * WIN_STEP + t
        kpos = (n - 1) * wb + col1
        in_seq = jnp.where((kpos >= 0) & (kpos < seq), 1.0, 0.0)
        q = jnp.concatenate([q_ref[0, h, t * wb:(t + 1) * wb, :] for h in range(grp)], axis=0)
        q = (q * (HEAD_DIM ** -0.5)).astype(BF16)
        s_loc = jnp.where(in_band * in_seq > 0.5, _dot_nt(q, k_ext[t * wb:(t + 3) * wb, :]), NEG)
        s_ctx = _dot_nt(q, k_ctx)
        m = jnp.maximum(jnp.maximum(jnp.max(s_loc, axis=-1, keepdims=True),
                                    jnp.max(s_ctx, axis=-1, keepdims=True)), sink)
        p_loc = jnp.exp(s_loc - m)
        p_ctx = jnp.exp(s_ctx - m)
        den = (jnp.sum(p_loc, axis=-1, keepdims=True) + jnp.sum(p_ctx, axis=-1, keepdims=True)
               + jnp.exp(sink - m))
        o = _dot(p_loc.astype(BF16), v_ext[t * wb:(t + 3) * wb, :]) + _dot(p_ctx.astype(BF16), v_ctx)
        o = (o / den).astype(o_ref.dtype)
        for h in range(grp):
            o_ref[0, h, t * wb:(t + 1) * wb, :] = o[h * wb:(h + 1) * wb, :]


def _win_call(q, k, v, kx, vx, sink):
    bsz, hq, s, dh = q.shape
    hkv = k.shape[1]
    grp = hq // hkv
    cx = kx.shape[2]
    nb = s // WIN_BLOCK
    step = WIN_STEP * WIN_BLOCK
    blk = (1, 1, WIN_BLOCK, dh)
    prev = lambda b, g, j, sk: (b, g, jnp.maximum(j * WIN_STEP - 1, 0), 0)
    cur = lambda b, g, j, sk: (b, g, j, 0)
    nxt = lambda b, g, j, sk: (b, g, jnp.minimum((j + 1) * WIN_STEP, nb - 1), 0)
    ctx = lambda b, g, j, sk: (b, g, 0, 0)
    return pl.pallas_call(
        functools.partial(_win_kernel, seq=s),
        grid_spec=pltpu.PrefetchScalarGridSpec(
            num_scalar_prefetch=1,
            grid=(bsz, hkv, nb // WIN_STEP),
            in_specs=[pl.BlockSpec((1, grp, step, dh), cur),
                      pl.BlockSpec(blk, prev), pl.BlockSpec((1, 1, step, dh), cur), pl.BlockSpec(blk, nxt),
                      pl.BlockSpec(blk, prev), pl.BlockSpec((1, 1, step, dh), cur), pl.BlockSpec(blk, nxt),
                      pl.BlockSpec((1, 1, cx, dh), ctx), pl.BlockSpec((1, 1, cx, dh), ctx)],
            out_specs=pl.BlockSpec((1, grp, step, dh), cur)),
        out_shape=jax.ShapeDtypeStruct((bsz, hq, s, dh), BF16),
        compiler_params=_params(("arbitrary", "arbitrary", "arbitrary")),
        name="mixer_window_attn",
    )(sink, q, k, k, k, v, v, v, kx, vx)


def _na_bias_table(rpb):
    qc = np.arange(GRID_W)
    kc = np.arange(GRID_W)
    cs = np.clip(qc - NA_COLS // 2, 0, GRID_W - NA_COLS)
    cmask = (kc[None, :] >= cs[:, None]) & (kc[None, :] < cs[:, None] + NA_COLS)
    ci = np.clip(kc[None, :] - qc[:, None] + NA_COLS - 1, 0, 2 * NA_COLS - 2)
    pick = (ci[None] == np.arange(2 * NA_COLS - 1)[:, None, None]).astype(np.float32)
    t15 = jnp.einsum('hrc,cqk->hrqk', rpb, jnp.asarray(pick), precision=lax.Precision.HIGHEST)
    t15 = jnp.where(cmask[None, None], t15, NEG)
    tab = jnp.stack([t15[:, d0:d0 + NA_ROWS] for d0 in range(NA_ROWS)], axis=1)
    return tab.transpose(0, 1, 3, 2, 4).reshape(rpb.shape[0], NA_ROWS, GRID_W, NA_ROWS * GRID_W)


def _na_kernel(q_ref, k_ref, v_ref, kx_ref, vx_ref, bias_ref, o_ref):
    rows = q_ref.shape[2] // GRID_W
    k_ctx = kx_ref[0, 0].astype(BF16)
    v_ctx = vx_ref[0, 0].astype(BF16)
    k_all = k_ref[0, 0].astype(BF16)
    v_all = v_ref[0, 0].astype(BF16)
    span = NA_ROWS * GRID_W
    chunk = 8
    for r0 in range(0, rows, chunk):
        q = (q_ref[0, 0, r0 * GRID_W:(r0 + chunk) * GRID_W, :] * (HEAD_DIM ** -0.5)).astype(BF16)
        s_parts = []
        for r in range(r0, r0 + chunk):
            rs = min(max(r - NA_ROWS // 2, 0), rows - NA_ROWS)
            k_loc = k_all[rs * GRID_W:rs * GRID_W + span, :]
            qr = q[(r - r0) * GRID_W:(r - r0 + 1) * GRID_W, :]
            s_parts.append(_dot_nt(qr, k_loc) + bias_ref[0, rs - r + NA_ROWS - 1])
        s_loc = jnp.concatenate(s_parts, axis=0)
        s_ctx = _dot_nt(q, k_ctx)
        m = jnp.maximum(jnp.max(s_loc, axis=-1, keepdims=True), jnp.max(s_ctx, axis=-1, keepdims=True))
        p_loc = jnp.exp(s_loc - m)
        p_ctx = jnp.exp(s_ctx - m)
        den = jnp.sum(p_loc, axis=-1, keepdims=True) + jnp.sum(p_ctx, axis=-1, keepdims=True)
        p_loc = p_loc.astype(BF16)
        o_parts = []
        for r in range(r0, r0 + chunk):
            rs = min(max(r - NA_ROWS // 2, 0), rows - NA_ROWS)
            v_loc = v_all[rs * GRID_W:rs * GRID_W + span, :]
            o_parts.append(_dot(p_loc[(r - r0) * GRID_W:(r - r0 + 1) * GRID_W, :], v_loc))
        o = jnp.concatenate(o_parts, axis=0) + _dot(p_ctx.astype(BF16), v_ctx)
        o_ref[0, 0, r0 * GRID_W:(r0 + chunk) * GRID_W, :] = (o / den).astype(o_ref.dtype)


def _na_call(q, k, v, kx, vx, bias_tab):
    bsz, h, s, dh = q.shape
    cx = kx.shape[2]
    full = pl.BlockSpec((1, 1, s, dh), lambda b, hh: (b, hh, 0, 0))
    ctx = pl.BlockSpec((1, 1, cx, dh), lambda b, hh: (b, hh, 0, 0))
    return pl.pallas_call(
        _na_kernel,
        grid=(bsz, h),
        in_specs=[full, full, full, ctx, ctx,
                  pl.BlockSpec((1, NA_ROWS, GRID_W, NA_ROWS * GRID_W), lambda b, hh: (hh, 0, 0, 0))],
        out_specs=full,
        out_shape=jax.ShapeDtypeStruct((bsz, h, s, dh), BF16),
        compiler_params=_params(("arbitrary", "arbitrary")),
        name="mixer_neighbourhood_attn",
    )(q, k, v, kx, vx, bias_tab)


def _ctx_attn_kernel(sink_ref, q_ref, k_ref, v_ref, o_ref, *, use_sink):
    h = pl.program_id(1)
    q = (q_ref[0, 0] * (HEAD_DIM ** -0.5)).astype(BF16)
    s = _dot_nt(q, k_ref[0, 0].astype(BF16))
    m = jnp.max(s, axis=-1, keepdims=True)
    if use_sink:
        sink = sink_ref[h]
        m = jnp.maximum(m, sink)
    p = jnp.exp(s - m)
    den = jnp.sum(p, axis=-1, keepdims=True)
    if use_sink:
        den = den + jnp.exp(sink - m)
    o_ref[0, 0] = (_dot(p.astype(BF16), v_ref[0, 0].astype(BF16)) / den).astype(o_ref.dtype)


def _ctx_attn_call(q, k, v, sink):
    bsz, hq, cx, dh = q.shape
    grp = hq // k.shape[1]
    use_sink = sink is not None
    if sink is None:
        sink = jnp.zeros((hq,), F32)
    qs = pl.BlockSpec((1, 1, cx, dh), lambda b, h, sk: (b, h, 0, 0))
    ks = pl.BlockSpec((1, 1, cx, dh), lambda b, h, sk: (b, h // grp, 0, 0))
    return pl.pallas_call(
        functools.partial(_ctx_attn_kernel, use_sink=use_sink),
        grid_spec=pltpu.PrefetchScalarGridSpec(
            num_scalar_prefetch=1, grid=(bsz, hq), in_specs=[qs, ks, ks], out_specs=qs),
        out_shape=jax.ShapeDtypeStruct((bsz, hq, cx, dh), BF16),
        compiler_params=_params(("arbitrary", "arbitrary")),
        name="context_attn",
    )(sink, q, k, v)


def _merge_kernel(x_ref, ya_ref, yb_ref, yc_ref, yd_ref, wg0_ref, wg1_ref, wg2_ref, wg3_ref, wbr_ref, wout_ref,
                  ng_ref, sh_ref, sc_ref, gt_ref, o_ref, h_ref, acc_ref):
    n = pl.program_id(2)
    nt = acc_ref.shape[0]
    d = o_ref.shape[2]

    @pl.when(n == 0)
    def _():
        h = _rms(x_ref[0], ng_ref[0:1, :]) * (1.0 + sc_ref[0]) + sh_ref[0]
        h_ref[...] = h.astype(BF16)

    h = h_ref[...]
    acc = None
    for i, (y_ref, wg_ref) in enumerate(((ya_ref, wg0_ref), (yb_ref, wg1_ref), (yc_ref, wg2_ref), (yd_ref, wg3_ref))):
        t = _sigmoid(_dot(h, wg_ref[...])) * _dot(y_ref[0], wbr_ref[i])
        acc = t if acc is None else acc + t
    acc_ref[n] = acc.astype(BF16)

    @pl.when(n == nt - 1)
    def _():
        a = jnp.concatenate([acc_ref[j] for j in range(nt)], axis=1)
        oc = d // 4
        for c in range(4):
            o_ref[0, :, c * oc:(c + 1) * oc] = _dot(a, wout_ref[:, c * oc:(c + 1) * oc])
        o_ref[0] = x_ref[0] + gt_ref[0] * _rms(o_ref[0], ng_ref[1:2, :])


def _merge_call(x, ys, w_gate, w_branch, w_out, norm_g, mods, shared_row):
    bsz, s, d = x.shape
    c = BRANCH_DIM
    tm = min(s, 512)
    tn = 256
    nt = d // tn
    row = (lambda b: b) if shared_row is None else (lambda b: shared_row)
    ysp = pl.BlockSpec((1, tm, c), lambda b, i, n: (b, i, 0))
    wg = [pl.BlockSpec((d, tn), functools.partial(lambda b, i, n, k: (0, k * nt + n), k=k)) for k in range(4)]
    mod = lambda col: pl.BlockSpec((1, 1, d), lambda b, i, n: (row(b), 0, col))
    return pl.pallas_call(
        _merge_kernel,
        grid=(bsz, s // tm, nt),
        in_specs=[pl.BlockSpec((1, tm, d), lambda b, i, n: (b, i, 0)), ysp, ysp, ysp, ysp, *wg,
                  pl.BlockSpec((4, c, tn), lambda b, i, n: (0, 0, n)),
                  pl.BlockSpec((d, d), lambda b, i, n: (0, 0), pipeline_mode=pl.Buffered(1)),
                  pl.BlockSpec((4, d), lambda b, i, n: (0, 0)),
                  mod(0), mod(1), mod(2)],
        out_specs=pl.BlockSpec((1, tm, d), lambda b, i, n: (b, i, 0)),
        out_shape=jax.ShapeDtypeStruct((bsz, s, d), F32),
        scratch_shapes=[pltpu.VMEM((tm, d), BF16), pltpu.VMEM((nt, tm, tn), BF16)],
        compiler_params=_params(("arbitrary", "arbitrary", "arbitrary")),
        name="mixer_merge",
    )(x, *ys, w_gate, w_gate, w_gate, w_gate, w_branch, w_out, norm_g, mods, mods, mods)


def _expert_row_ids():
    r = np.arange(N_EXPERTS)
    return ((r % N_GROUPS) * GROUP_SIZE + r // N_GROUPS).astype(np.int32)


def _router_kernel(rows_ref, x_ref, ng_ref, sh_ref, sc_ref, rw_ref, rb_ref, eid_ref, tri_e_ref, tri_t_ref,
                   sw1_ref, sw3_ref, sw2_ref, h_ref, so_ref, e_ref, r_ref, w_ref, cnt_ref):
    i = pl.program_id(0)
    tm = x_ref.shape[0]

    @pl.when(i == 0)
    def _():
        cnt_ref[...] = jnp.zeros(cnt_ref.shape, F32)

    h = _rms(x_ref[...], ng_ref[2:3, :]) * (1.0 + sc_ref[0]) + sh_ref[0]
    h_ref[:, 0, :] = _pack_halves(h)
    hb = h.astype(BF16)
    so_ref[...] = _dot((_silu(_dot(hb, sw1_ref[...])) * _dot(hb, sw3_ref[...])).astype(BF16), sw2_ref[...])

    h_tail = (h - hb.astype(F32)).astype(BF16)
    head = _dot_nt(rw_ref[...], hb)
    logits = head[:N_EXPERTS] + head[N_EXPERTS:] + _dot_nt(rw_ref[:N_EXPERTS, :], h_tail)
    scores = _sigmoid(logits)
    biased = scores + rb_ref[...]
    m1 = jnp.full((N_GROUPS, tm), -jnp.inf, F32)
    m2 = m1
    for j in range(GROUP_SIZE):
        v = biased[j * N_GROUPS:(j + 1) * N_GROUPS, :]
        m2 = jnp.maximum(m2, jnp.minimum(m1, v))
        m1 = jnp.maximum(m1, v)
    gs = m1 + m2
    gid = lax.broadcasted_iota(jnp.int32, (N_GROUPS, tm), 0)
    beat = jnp.zeros((N_GROUPS, tm), jnp.int32)
    for g in range(N_GROUPS):
        o = gs[g:g + 1, :]
        beat = beat + jnp.where((o > gs) | ((o == gs) & (g < gid)), 1, 0)
    keep = beat < TOPK_GROUPS
    masked = jnp.concatenate(
        [jnp.where(keep, biased[j * N_GROUPS:(j + 1) * N_GROUPS, :], NEG) for j in range(GROUP_SIZE)], axis=0)
    eid = eid_ref[...]
    row_ids = _expert_row_ids()
    beat = jnp.zeros((N_EXPERTS, tm), jnp.int32)
    for r in range(N_EXPERTS):
        o = masked[r:r + 1, :]
        beat = beat + jnp.where((o > masked) | ((o == masked) & (int(row_ids[r]) < eid)), 1, 0)
    sel = beat < TOP_K
    self32 = jnp.where(sel, 1.0, 0.0)
    selb = self32.astype(BF16)
    wsel = jnp.where(sel, scores, 0.0)
    wsel = wsel / jnp.sum(wsel, axis=0, keepdims=True) * ROUTE_SCALE
    rank = cnt_ref[:, 0:1] + _dot(selb, tri_t_ref[...])
    cnt_ref[...] = cnt_ref[...] + jnp.sum(self32, axis=1, keepdims=True)
    slot = _dot(tri_e_ref[...], selb)
    eidf = eid.astype(F32)
    for k in range(TOP_K):
        mk = sel & (slot == float(k))
        e_ref[k:k + 1, :] = jnp.sum(jnp.where(mk, eidf, 0.0), axis=0, keepdims=True).astype(jnp.int32)
        r_ref[k:k + 1, :] = jnp.sum(jnp.where(mk, rank, 0.0), axis=0, keepdims=True).astype(jnp.int32)
        w_ref[k:k + 1, :] = jnp.sum(jnp.where(mk, wsel, 0.0), axis=0, keepdims=True)


def _router_call(xcat, tile_rows, norm_g, mods, router_w, router_b, sw1, sw3, sw2):
    t, d = xcat.shape
    tm = 256
    f = sw1.shape[1]
    ids = _expert_row_ids()
    rw = router_w.T[ids]
    rw_head = rw.astype(BF16)
    rw = jnp.concatenate([rw_head, (rw - rw_head.astype(F32)).astype(BF16)], axis=0)
    rb = router_b[ids].reshape(N_EXPERTS, 1)
    tri_e = jnp.asarray(np.tril(np.ones((N_EXPERTS, N_EXPERTS), np.float32), -1), BF16)
    tri_t = jnp.asarray(np.triu(np.ones((tm, tm), np.float32), 1), BF16)
    const = lambda shape: pl.BlockSpec(shape, lambda i, rows: (0,) * len(shape))
    tok = pl.BlockSpec((tm, d), lambda i, rows: (i, 0))
    slots = pl.BlockSpec((TOP_K, tm), lambda i, rows: (0, i))
    return pl.pallas_call(
        _router_kernel,
        grid_spec=pltpu.PrefetchScalarGridSpec(
            num_scalar_prefetch=1,
            grid=(t // tm,),
            in_specs=[tok, const((4, d)),
                      pl.BlockSpec((1, 1, d), lambda i, rows: (rows[i], 0, 3)),
                      pl.BlockSpec((1, 1, d), lambda i, rows: (rows[i], 0, 4)),
                      const((2 * N_EXPERTS, d)), const((N_EXPERTS, 1)), const((N_EXPERTS, 1)),
                      const((N_EXPERTS, N_EXPERTS)), const((tm, tm)),
                      const((d, f)), const((d, f)), const((f, d))],
            out_specs=[pl.BlockSpec((tm, 1, d // 2), lambda i, rows: (i, 0, 0)), tok, slots, slots, slots,
                       const((N_EXPERTS, 128))]),
        out_shape=[jax.ShapeDtypeStruct((t, 1, d // 2), jnp.uint32), jax.ShapeDtypeStruct((t, d), F32),
                   jax.ShapeDtypeStruct((TOP_K, t), jnp.int32), jax.ShapeDtypeStruct((TOP_K, t), jnp.int32),
                   jax.ShapeDtypeStruct((TOP_K, t), F32), jax.ShapeDtypeStruct((N_EXPERTS, 128), F32)],
        compiler_params=_params(("arbitrary",)),
        name="moe_router",
    )(tile_rows, xcat, norm_g, mods, mods, rw, rb, jnp.asarray(ids).reshape(N_EXPERTS, 1), tri_e, tri_t,
      sw1, sw3, sw2)


def _gmm_kernel(blk_ref, exp_ref, lo_ref, hi_ref, first_ref, newexp_ref, tok_cur_ref, tok_nxt_ref, h_hbm, ws_ref,
                w1_ref, w3_ref, w2_ref, o_ref, xa_ref, xb_ref, w1b_ref, w3b_ref, w2b_ref, sem):
    i = pl.program_id(0)
    last = pl.num_programs(0) - 1
    nrows = xa_ref.shape[0]
    xbufs = (xa_ref, xb_ref)

    def start_gather(tok_ref, s):
        for r in range(nrows):
            pltpu.make_async_copy(h_hbm.at[tok_ref[0, 0, r]], xbufs[s].at[pl.ds(r, 1)],
                                  sem.at[s]).start(priority=r % 2)

    def wait_gather(s):
        for r in range(nrows):
            pltpu.make_async_copy(h_hbm.at[0], xbufs[s].at[pl.ds(r, 1)], sem.at[s]).wait()

    @pl.when(i == 0)
    def _():
        start_gather(tok_cur_ref, 0)

    @pl.when(first_ref[i] == 1)
    def _():
        o_ref[...] = jnp.zeros(o_ref.shape, o_ref.dtype)

    @pl.when(newexp_ref[i] == 1)
    def _():
        w1b_ref[...] = w1_ref[0].astype(BF16)
        w3b_ref[...] = w3_ref[0].astype(BF16)
        w2b_ref[...] = w2_ref[0].astype(BF16)

    def step(s):
        wait_gather(s)

        @pl.when(lo_ref[i] >= 0)
        def _():
            start_gather(tok_nxt_ref, 1 - s)

        x_lo, x_hi = _unpack_halves(xbufs[s][...])
        x = jnp.concatenate([x_lo.astype(BF16), x_hi.astype(BF16)], axis=1)
        wcol = jnp.concatenate([jnp.transpose(ws_ref[0, t])[:, 0:1] for t in range(ws_ref.shape[1])], axis=0)
        a = (_silu(_dot(x, w1b_ref[...])) * _dot(x, w3b_ref[...]) * wcol).astype(BF16)
        y = _pack_halves(_dot(a, w2b_ref[...]))
        rows = lax.broadcasted_iota(jnp.int32, (nrows, 1), 0)
        keep = (rows >= lo_ref[i]) & (rows < hi_ref[i])
        o_ref[:, 0, :] = jnp.where(keep, y, o_ref[:, 0, :])

        @pl.when(i == last)
        def _():
            wait_gather(1 - s)

    for s in range(2):
        pl.when(i % 2 == s)(functools.partial(step, s))


def _gmm_call(items, slot_tok, slot_w, hp, w1, w3, w2, layer):
    a = slot_tok.shape[0]
    d, f = w1.shape[2], w1.shape[3]
    n_items = items[0].shape[0]
    nb = a // MOE_BLOCK
    tok3 = slot_tok.reshape(nb, 1, MOE_BLOCK)
    ws = jnp.pad(slot_w.reshape(nb, MOE_BLOCK // 128, 1, 128), ((0, 0), (0, 0), (0, 7), (0, 0)))
    rows = pl.BlockSpec((MOE_BLOCK,) + hp.shape[1:], lambda i, blk, ex, *_: (blk[i], 0, 0))
    toks = lambda step: pl.BlockSpec((1, 1, MOE_BLOCK), lambda i, blk, ex, *_: (blk[step(i)], 0, 0),
                                     memory_space=pltpu.SMEM)
    return pl.pallas_call(
        _gmm_kernel,
        grid_spec=pltpu.PrefetchScalarGridSpec(
            num_scalar_prefetch=len(items),
            grid=(n_items,),
            in_specs=[toks(lambda i: i), toks(lambda i: jnp.minimum(i + 1, n_items - 1)),
                      pl.BlockSpec(memory_space=pl.ANY),
                      pl.BlockSpec((1, MOE_BLOCK // 128, 8, 128), lambda i, blk, ex, *_: (blk[i], 0, 0, 0)),
                      pl.BlockSpec((None, 1, d, f), lambda i, blk, ex, *_: (layer, ex[i], 0, 0)),
                      pl.BlockSpec((None, 1, d, f), lambda i, blk, ex, *_: (layer, ex[i], 0, 0)),
                      pl.BlockSpec((None, 1, f, d), lambda i, blk, ex, *_: (layer, ex[i], 0, 0))],
            out_specs=rows,
            scratch_shapes=[pltpu.VMEM((MOE_BLOCK, hp.shape[2]), jnp.uint32),
                            pltpu.VMEM((MOE_BLOCK, hp.shape[2]), jnp.uint32),
                            pltpu.VMEM((d, f), BF16), pltpu.VMEM((d, f), BF16), pltpu.VMEM((f, d), BF16),
                            pltpu.SemaphoreType.DMA((2,))]),
        out_shape=jax.ShapeDtypeStruct((a,) + hp.shape[1:], jnp.uint32),
        compiler_params=_params(("arbitrary",)),
        name="moe_experts",
    )(*items, tok3, tok3, hp, ws, w1, w3, w2)


def _moe_items(counts, n_rows):
    nb = n_rows // MOE_BLOCK
    n_items = nb + N_EXPERTS
    ends = jnp.cumsum(counts).astype(jnp.int32)
    bstart = jnp.arange(nb, dtype=jnp.int32) * MOE_BLOCK
    pos_b = jnp.arange(nb, dtype=jnp.int32) + jnp.sum(ends[None, :] < bstart[:, None], axis=1)
    idx = jnp.arange(n_items, dtype=jnp.int32)
    n_b = jnp.sum(pos_b[None, :] <= idx[:, None], axis=1).astype(jnp.int32)
    n_e = idx + 1 - n_b
    ends0 = jnp.concatenate([jnp.zeros((1,), jnp.int32), ends])
    last_end = jnp.sum(jnp.where(n_e[:, None] == jnp.arange(N_EXPERTS + 1)[None, :], ends0[None, :], 0), axis=1)
    start = jnp.maximum((n_b - 1) * MOE_BLOCK, last_end).astype(jnp.int32)
    stop = jnp.concatenate([start[1:], jnp.full((1,), n_rows, jnp.int32)])
    blk = jnp.minimum(start // MOE_BLOCK, nb - 1)
    ex = jnp.minimum(jnp.sum(ends[None, :] <= start[:, None], axis=1), N_EXPERTS - 1).astype(jnp.int32)
    lo = start - blk * MOE_BLOCK
    hi = stop - blk * MOE_BLOCK
    one = jnp.ones((1,), jnp.int32)
    first = jnp.concatenate([one, (blk[1:] != blk[:-1]).astype(jnp.int32)])
    newexp = jnp.concatenate([one, (ex[1:] != ex[:-1]).astype(jnp.int32)])
    return blk.astype(jnp.int32), ex, lo.astype(jnp.int32), hi.astype(jnp.int32), first, newexp


def _combine_kernel(rows_ref, dest_ref, dest_nxt_ref, ys_hbm, so_ref, x_ref, ng_ref, gt_ref, o_ref,
                    buf_ref, sem):
    i = pl.program_id(0)
    tc = x_ref.shape[0]
    slot = i % 2

    def gather(idx_ref, s):
        def body(j, c):
            for k in range(TOP_K):
                pltpu.make_async_copy(ys_hbm.at[idx_ref[0, 0, k * tc + j]], buf_ref.at[s, k, pl.ds(j, 1)],
                                      sem.at[s]).start(priority=k % 2)
            return c
        lax.fori_loop(0, tc, body, 0, unroll=4)

    @pl.when(i == 0)
    def _():
        gather(dest_ref, 0)

    @pl.when(i < pl.num_programs(0) - 1)
    def _():
        gather(dest_nxt_ref, 1 - slot)

    pltpu.make_async_copy(buf_ref.at[slot], buf_ref.at[slot], sem.at[slot]).wait()
    acc_lo, acc_hi = _unpack_halves(buf_ref[slot, 0])
    for k in range(1, TOP_K):
        y_lo, y_hi = _unpack_halves(buf_ref[slot, k])
        acc_lo = acc_lo + y_lo
        acc_hi = acc_hi + y_hi
    fl = jnp.concatenate([acc_lo, acc_hi], axis=1) + so_ref[...]
    o_ref[...] = x_ref[...] + gt_ref[0] * _rms(fl, ng_ref[3:4, :])


def _combine_call(tile_rows, dest, ys, shared_out, xcat, norm_g, mods):
    t, d = xcat.shape
    tc = 128
    dest3 = dest.reshape(TOP_K, t // tc, tc).transpose(1, 0, 2).reshape(t // tc, 1, TOP_K * tc)
    tok = pl.BlockSpec((tc, d), lambda i, rows: (i, 0))
    nt = t // tc
    return pl.pallas_call(
        _combine_kernel,
        grid_spec=pltpu.PrefetchScalarGridSpec(
            num_scalar_prefetch=1,
            grid=(nt,),
            in_specs=[pl.BlockSpec((1, 1, TOP_K * tc), lambda i, rows: (i, 0, 0), memory_space=pltpu.SMEM),
                      pl.BlockSpec((1, 1, TOP_K * tc), lambda i, rows: (jnp.minimum(i + 1, nt - 1), 0, 0),
                                   memory_space=pltpu.SMEM),
                      pl.BlockSpec(memory_space=pl.ANY),
                      tok, tok,
                      pl.BlockSpec((4, d), lambda i, rows: (0, 0)),
                      pl.BlockSpec((1, 1, d), lambda i, rows: (rows[i], 0, 5))],
            out_specs=tok,
            scratch_shapes=[pltpu.VMEM((2, TOP_K, tc, d // 2), jnp.uint32), pltpu.SemaphoreType.DMA((2,))]),
        out_shape=jax.ShapeDtypeStruct((t, d), F32),
        compiler_params=_params(("arbitrary",)),
        name="moe_combine",
    )(tile_rows, dest3, dest3, ys, shared_out, xcat, norm_g, mods)


def _channel_mixer(xcat, mod_rows, norm_g, mods, router_w, router_b, w1, w3, w2, layer, sw1, sw3, sw2):
    t, _ = xcat.shape
    h, shared_out, eidx, rank, wsel, cnt = _router_call(xcat, mod_rows[::2], norm_g, mods, router_w, router_b,
                                                         sw1, sw3, sw2)
    counts = jnp.zeros((N_EXPERTS,), jnp.int32).at[_expert_row_ids()].set(cnt[:, 0].astype(jnp.int32))
    starts = jnp.cumsum(counts) - counts
    sel = eidx[:, :, None] == jnp.arange(N_EXPERTS, dtype=jnp.int32)[None, None, :]
    dest = jnp.sum(jnp.where(sel, starts[None, None, :], 0), axis=-1) + rank
    flat_tok = jnp.tile(jnp.arange(t, dtype=jnp.int32), TOP_K)
    _, slot_tok, slot_w = lax.sort((dest.reshape(-1), flat_tok, wsel.reshape(-1)), num_keys=1)
    ys = _gmm_call(_moe_items(counts, t * TOP_K), slot_tok, slot_w, h, w1, w3, w2, layer)
    return _combine_call(mod_rows, dest, ys, shared_out, xcat, norm_g, mods)


def _heads(t, n):
    b, s, _ = t.shape
    return t.reshape(b, s, n, HEAD_DIM).transpose(0, 2, 1, 3)


def _unheads(t):
    b, n, s, dh = t.shape
    return t.transpose(0, 2, 1, 3).reshape(b, s, n * dh)


def kernel(x, c, ctx, c_ctx, w_ada, b_ada, norm_g, w_in, conv_b_in, conv_dw, conv_dw_b, conv_ln_g, conv_ln_b, gmlp_ln_g, gmlp_ln_b, gmlp_ws, gmlp_bs, win_sink, na_rpb, w_branch, w_out, router_w, router_b, exp_w1, exp_w3, exp_w2, sh_w1, sh_w3, sh_w2):
    bsz, s, d = x.shape
    cx = ctx.shape[1]
    depth = w_ada.shape[0]
    ctx_row = bsz
    c_all = jnp.zeros((MODS_ROWS, d), F32).at[:bsz].set(c).at[ctx_row].set(c_ctx)
    mods_all = _ada_call(c_all, w_ada, b_ada)
    lat_rows = jnp.repeat(jnp.arange(bsz, dtype=jnp.int32), s // 128)
    ctx_rows = jnp.full((bsz * cx // 128,), ctx_row, jnp.int32)
    xl, xc = x, ctx
    for l in range(depth):
        last = l == depth - 1
        mods = mods_all[l].reshape(MODS_ROWS, 1, N_MODS * d)
        g_off = w_in.shape[2] - 4 * d
        kv_off = g_off - KV_COLS
        w_small = jnp.concatenate([w_in[l, :, :kv_off], w_in[l, :, kv_off + 256:g_off],
                                   w_in[l, :, kv_off:kv_off + 256]], axis=1).astype(BF16)
        w_gate = w_in[l, :, g_off:].astype(BF16)
        w_br = w_branch[l].astype(BF16)
        w_o = w_out[l].astype(BF16)
        pl_l = _proj_call(xl, norm_g[l], mods, w_small, None, SMALL_COLS // 2)
        if last:
            pkv_c = _proj_call(xc, norm_g[l], mods, w_small[:, COL_DK:], ctx_row, KV_COLS)
            kv_base = 0
        else:
            pc = _proj_call(xc, norm_g[l], mods, w_small, ctx_row, SMALL_COLS // 2)
            pkv_c, kv_base = pc, COL_DK
        dkc = _heads(pkv_c[..., kv_base:kv_base + 512], NA_HEADS)
        dvc = _heads(pkv_c[..., kv_base + 512:kv_base + 1024], NA_HEADS)
        ckc = _heads(pkv_c[..., kv_base + 1024:kv_base + 1152], WIN_KV_HEADS)
        cvc = _heads(pkv_c[..., kv_base + 1152:kv_base + 1280], WIN_KV_HEADS)
        conv_args = (conv_b_in[l], conv_dw[l], conv_dw_b[l], conv_ln_g[l], conv_ln_b[l])
        gmlp_args = (gmlp_ln_g[l], gmlp_ln_b[l], gmlp_ws[l], gmlp_bs[l])
        ya = _conv_call(pl_l, *conv_args)
        yb = _gmlp_call(pl_l, *gmlp_args)
        q_rope, k_rope = _rope_call(pl_l)
        yc = _unheads(_win_call(_heads(q_rope, WIN_HEADS), _heads(k_rope, WIN_KV_HEADS),
                                _heads(pl_l[..., COL_CV:COL_CV + 128], WIN_KV_HEADS), ckc, cvc, win_sink[l]))
        yd = _unheads(_na_call(_heads(pl_l[..., COL_DQ:COL_DQ + 512], NA_HEADS),
                               _heads(pl_l[..., COL_DK:COL_DK + 512], NA_HEADS),
                               _heads(pl_l[..., COL_DV:COL_DV + 512], NA_HEADS), dkc, dvc,
                               _na_bias_table(na_rpb[l])))
        xl = _merge_call(xl, (ya, yb, yc, yd), w_gate, w_br, w_o, norm_g[l], mods, None)
        moe_w = (router_w[l], router_b[l], exp_w1, exp_w3, exp_w2, l,
                 sh_w1[l].astype(BF16), sh_w3[l].astype(BF16), sh_w2[l].astype(BF16))
        if last:
            xl = _channel_mixer(xl.reshape(bsz * s, d), lat_rows, norm_g[l], mods, *moe_w).reshape(bsz, s, d)
        else:
            yca = _conv_call(pc, *conv_args)
            ycb = _gmlp_call(pc, *gmlp_args)
            ycc = _unheads(_ctx_attn_call(_heads(pc[..., COL_CQ:COL_CQ + 512], WIN_HEADS), ckc, cvc, win_sink[l]))
            ycd = _unheads(_ctx_attn_call(_heads(pc[..., COL_DQ:COL_DQ + 512], NA_HEADS), dkc, dvc, None))
            xc = _merge_call(xc, (yca, ycb, ycc, ycd), w_gate, w_br, w_o, norm_g[l], mods, ctx_row)
            xcat = jnp.concatenate([xl.reshape(bsz * s, d), xc.reshape(bsz * cx, d)], axis=0)
            out = _channel_mixer(xcat, jnp.concatenate([lat_rows, ctx_rows]), norm_g[l], mods, *moe_w)
            xl = out[:bsz * s].reshape(bsz, s, d)
            xc = out[bsz * s:].reshape(bsz, cx, d)
    return xl
```

```python
import functools

import numpy as np
import jax
import jax.numpy as jnp
from jax import lax
from jax.experimental import pallas as pl
from jax.experimental.pallas import tpu as pltpu

F32 = jnp.float32
BF16 = jnp.bfloat16

GRID_W = 64
HEAD_DIM = 64
EPS = 1e-6
NEG = -1e30
BRANCH_DIM = 512
CONV_K = 31
CONV_HALO = 16
GMLP_CHUNK = 128
GMLP_GROUPS = 4
WIN_HEADS = 8
WIN_KV_HEADS = 2
WINDOW = 128
WIN_BLOCK = 128
ROPE_BASE = 10000.0
NA_HEADS = 8
NA_ROWS = 8
NA_COLS = 16
N_EXPERTS = 64
TOP_K = 8
N_GROUPS = 8
GROUP_SIZE = N_EXPERTS // N_GROUPS
TOPK_GROUPS = 4
ROUTE_SCALE = 2.5
MOE_BLOCK = 256
N_MODS = 6
MODS_ROWS = 16
VMEM_LIMIT = 52 * 1024 * 1024

COL_A, COL_B, COL_CQ, COL_DQ, COL_DK, COL_DV, COL_CK, COL_CV = 0, 1024, 2048, 2560, 3072, 3584, 4096, 4224
SMALL_COLS = 4352
KV_COLS = 1280


def _sigmoid(x):
    return 1.0 / (1.0 + jnp.exp(-x))


def _silu(x):
    return x * _sigmoid(x)


def _gelu_tanh(x):
    return 0.5 * x * (1.0 + jnp.tanh(np.sqrt(2.0 / np.pi).astype(np.float32) * (x + 0.044715 * (x * x * x))))


def _rms(x, g):
    return x * lax.rsqrt(jnp.mean(x * x, axis=-1, keepdims=True) + EPS) * g


def _ln(x, g, b):
    mu = jnp.mean(x, axis=-1, keepdims=True)
    xc = x - mu
    var = jnp.mean(xc * xc, axis=-1, keepdims=True)
    return xc * lax.rsqrt(var + EPS) * g + b


def _dot(a, b):
    return jnp.dot(a, b, preferred_element_type=F32)


def _dot_nt(a, b):
    return lax.dot_general(a, b, (((1,), (1,)), ((), ())), preferred_element_type=F32)


def _pack_halves(x):
    n = x.shape[1] // 2
    lo = lax.bitcast_convert_type(x[:, :n].astype(BF16).astype(F32), jnp.uint32)
    hi = lax.bitcast_convert_type(x[:, n:].astype(BF16).astype(F32), jnp.uint32)
    return (hi & jnp.uint32(0xFFFF0000)) | (lo >> 16)


def _unpack_halves(w):
    lo = lax.bitcast_convert_type(w << 16, F32)
    hi = lax.bitcast_convert_type(w & jnp.uint32(0xFFFF0000), F32)
    return lo, hi


def _with_ones(v):
    return jnp.concatenate([v, jnp.ones_like(v)], axis=1)


def _params(sem):
    return pltpu.CompilerParams(dimension_semantics=sem, vmem_limit_bytes=VMEM_LIMIT)


def _ada_kernel(c_ref, w_ref, b_ref, o_ref):
    s = _silu(c_ref[...])
    o_ref[0] = _dot(s.astype(BF16), w_ref[0].astype(BF16)) + b_ref[0]


def _ada_call(c_all, w_ada, b_ada):
    depth, d, n = w_ada.shape
    tn = 512
    return pl.pallas_call(
        _ada_kernel,
        grid=(depth, n // tn),
        in_specs=[pl.BlockSpec((MODS_ROWS, d), lambda l, j: (0, 0)),
                  pl.BlockSpec((1, d, tn), lambda l, j: (l, 0, j)),
                  pl.BlockSpec((1, 1, tn), lambda l, j: (l, 0, j))],
        out_specs=pl.BlockSpec((1, MODS_ROWS, tn), lambda l, j: (l, 0, j)),
        out_shape=jax.ShapeDtypeStruct((depth, MODS_ROWS, n), F32),
        compiler_params=_params(("arbitrary", "arbitrary")),
        name="ada_mods",
    )(c_all, w_ada, b_ada.reshape(depth, 1, n))


def _proj_kernel(x_ref, g_ref, sh_ref, sc_ref, w_ref, o_ref):
    h = _rms(x_ref[0], g_ref[0:1, :]) * (1.0 + sc_ref[0]) + sh_ref[0]
    o_ref[0] = _dot(h.astype(BF16), w_ref[...])


def _proj_call(x, norm_g, mods, w, shared_row, tn):
    bsz, s, d = x.shape
    n = w.shape[1]
    tm = min(s, 512)
    row = (lambda b: b) if shared_row is None else (lambda b: shared_row)
    return pl.pallas_call(
        _proj_kernel,
        grid=(n // tn, bsz, s // tm),
        in_specs=[pl.BlockSpec((1, tm, d), lambda j, b, i: (b, i, 0)),
                  pl.BlockSpec((4, d), lambda j, b, i: (0, 0)),
                  pl.BlockSpec((1, 1, d), lambda j, b, i: (row(b), 0, 0)),
                  pl.BlockSpec((1, 1, d), lambda j, b, i: (row(b), 0, 1)),
                  pl.BlockSpec((d, tn), lambda j, b, i: (0, j))],
        out_specs=pl.BlockSpec((1, tm, tn), lambda j, b, i: (b, i, j)),
        out_shape=jax.ShapeDtypeStruct((bsz, s, n), F32),
        compiler_params=_params(("arbitrary", "arbitrary", "arbitrary")),
        name="proj_in",
    )(x, norm_g, mods, mods, w)


def _conv_kernel(a_ref, ap_ref, an_ref, bin_ref, dw_ref, dwb_ref, g_ref, b_ref, o_ref, ypad_ref):
    i = pl.program_id(1)
    nblk = pl.num_programs(1)
    ts = a_ref.shape[1]
    c = BRANCH_DIM

    def glu(a):
        a = a + bin_ref[...]
        return a[:, :c] * _sigmoid(a[:, c:])

    ypad_ref[0:CONV_HALO, :] = jnp.where(i > 0, glu(ap_ref[0]), 0.0)
    ypad_ref[CONV_HALO:CONV_HALO + ts, :] = glu(a_ref[0])
    ypad_ref[CONV_HALO + ts:2 * CONV_HALO + ts, :] = jnp.where(i < nblk - 1, glu(an_ref[0]), 0.0)
    rc = 64
    off = CONV_HALO - CONV_K // 2
    for r0 in range(0, ts, rc):
        acc = jnp.zeros((rc, c), F32)
        for j in range(CONV_K):
            acc = acc + ypad_ref[r0 + off + j:r0 + off + j + rc, :] * dw_ref[j:j + 1, :]
        y = _ln(acc + dwb_ref[...], g_ref[...], b_ref[...])
        o_ref[0, r0:r0 + rc, :] = _silu(y).astype(o_ref.dtype)


def _conv_call(pl_all, b_in, dw, dw_b, ln_g, ln_b):
    bsz, s, _ = pl_all.shape
    ts = min(s, 512)
    c = BRANCH_DIM
    hb = ts // CONV_HALO
    last = s // CONV_HALO - 1
    dw_pad = jnp.zeros((32, c), F32).at[:CONV_K].set(dw)
    return pl.pallas_call(
        _conv_kernel,
        grid=(bsz, s // ts),
        in_specs=[pl.BlockSpec((1, ts, 2 * c), lambda b, i: (b, i, 0)),
                  pl.BlockSpec((1, CONV_HALO, 2 * c), lambda b, i: (b, jnp.maximum(i * hb - 1, 0), 0)),
                  pl.BlockSpec((1, CONV_HALO, 2 * c), lambda b, i: (b, jnp.minimum((i + 1) * hb, last), 0)),
                  pl.BlockSpec((1, 2 * c), lambda b, i: (0, 0)),
                  pl.BlockSpec((32, c), lambda b, i: (0, 0)),
                  pl.BlockSpec((1, c), lambda b, i: (0, 0)),
                  pl.BlockSpec((1, c), lambda b, i: (0, 0)),
                  pl.BlockSpec((1, c), lambda b, i: (0, 0))],
        out_specs=pl.BlockSpec((1, ts, c), lambda b, i: (b, i, 0)),
        out_shape=jax.ShapeDtypeStruct((bsz, s, c), BF16),
        scratch_shapes=[pltpu.VMEM((ts + 2 * CONV_HALO, c), F32)],
        compiler_params=_params(("arbitrary", "arbitrary")),
        name="mixer_conv",
    )(pl_all, pl_all, pl_all, b_in.reshape(1, 2 * c), dw_pad, dw_b.reshape(1, c), ln_g.reshape(1, c), ln_b.reshape(1, c))


def _gmlp_kernel(z_ref, g_ref, b_ref, ws_ref, bs_ref, o_ref):
    c = BRANCH_DIM
    gw = c // GMLP_GROUPS
    ts = z_ref.shape[1]
    for n in range(ts // GMLP_CHUNK):
        rows = slice(n * GMLP_CHUNK, (n + 1) * GMLP_CHUNK)
        z = _gelu_tanh(z_ref[0, rows, :])
        u = z[:, :c]
        v = _ln(z[:, c:], g_ref[...], b_ref[...]).astype(BF16)
        for g in range(GMLP_GROUPS):
            cols = slice(g * gw, (g + 1) * gw)
            sg = _dot(ws_ref[g], v[:, cols]) + bs_ref[:, g:g + 1]
            o_ref[0, rows, cols] = (u[:, cols] * sg).astype(o_ref.dtype)


def _gmlp_call(pl_all, ln_g, ln_b, ws, bs):
    bsz, s, _ = pl_all.shape
    ts = min(s, 512)
    c = BRANCH_DIM
    return pl.pallas_call(
        _gmlp_kernel,
        grid=(bsz, s // ts),
        in_specs=[pl.BlockSpec((1, ts, 2 * c), lambda b, i: (b, i, COL_B // (2 * c))),
                  pl.BlockSpec((1, c), lambda b, i: (0, 0)),
                  pl.BlockSpec((1, c), lambda b, i: (0, 0)),
                  pl.BlockSpec((GMLP_GROUPS, GMLP_CHUNK, GMLP_CHUNK), lambda b, i: (0, 0, 0)),
                  pl.BlockSpec((GMLP_CHUNK, GMLP_GROUPS), lambda b, i: (0, 0))],
        out_specs=pl.BlockSpec((1, ts, c), lambda b, i: (b, i, 0)),
        out_shape=jax.ShapeDtypeStruct((bsz, s, c), BF16),
        compiler_params=_params(("arbitrary", "arbitrary")),
        name="mixer_gmlp",
    )(pl_all, ln_g.reshape(1, c), ln_b.reshape(1, c), ws.astype(BF16), bs.T)


def _rope_tables(s):
    half, quarter = HEAD_DIM // 2, HEAD_DIM // 4
    pos = np.arange(s)
    prow = (pos // GRID_W).astype(np.float32)
    pcol = (pos % GRID_W).astype(np.float32)
    inv = (ROPE_BASE ** (-np.arange(quarter, dtype=np.float32) / quarter)).astype(np.float32)
    lane = np.arange(128)
    in_head = lane % HEAD_DIM
    p = np.where((in_head < half)[None, :], prow[:, None], pcol[:, None]).astype(np.float32)
    ang = (p * inv[lane % quarter][None, :]).astype(np.float32).astype(np.float64)
    first = (in_head % half) < quarter
    cos = np.cos(ang)
    sin = np.where(first[None, :], -np.sin(ang), np.sin(ang))
    swap = np.zeros((128, 128), np.float32)
    for j in lane:
        swap[j + quarter if first[j] else j - quarter, j] = 1.0
    return cos.astype(np.float32), sin.astype(np.float32), swap


def _rope_kernel(q_ref, k_ref, cos_ref, sin_ref, swap_ref, qo_ref, ko_ref):
    cos, sin, swap = cos_ref[...], sin_ref[...], swap_ref[...]

    def rot(x):
        parts = []
        for t in range(x.shape[1] // 128):
            xt = x[:, t * 128:(t + 1) * 128]
            parts.append(xt * cos + _dot(xt.astype(BF16), swap) * sin)
        return parts[0] if len(parts) == 1 else jnp.concatenate(parts, axis=1)

    qo_ref[0] = rot(q_ref[0])
    ko_ref[0] = rot(k_ref[0])


def _rope_call(pl_all):
    bsz, s, _ = pl_all.shape
    tm = 512
    cos, sin, swap = _rope_tables(s)
    qw, kw = WIN_HEADS * HEAD_DIM, WIN_KV_HEADS * HEAD_DIM
    return pl.pallas_call(
        _rope_kernel,
        grid=(bsz, s // tm),
        in_specs=[pl.BlockSpec((1, tm, qw), lambda b, i: (b, i, COL_CQ // qw)),
                  pl.BlockSpec((1, tm, kw), lambda b, i: (b, i, COL_CK // kw)),
                  pl.BlockSpec((tm, 128), lambda b, i: (i, 0)),
                  pl.BlockSpec((tm, 128), lambda b, i: (i, 0)),
                  pl.BlockSpec((128, 128), lambda b, i: (0, 0))],
        out_specs=[pl.BlockSpec((1, tm, qw), lambda b, i: (b, i, 0)),
                   pl.BlockSpec((1, tm, kw), lambda b, i: (b, i, 0))],
        out_shape=[jax.ShapeDtypeStruct((bsz, s, qw), F32), jax.ShapeDtypeStruct((bsz, s, kw), F32)],
        compiler_params=_params(("arbitrary", "arbitrary")),
        name="rope",
    )(pl_all, pl_all, jnp.asarray(cos), jnp.asarray(sin), jnp.asarray(swap, BF16))


WIN_STEP = 4


def _win_kernel(sink_ref, q_ref, kp_ref, kc_ref, kn_ref, vp_ref, vc_ref, vn_ref, kx_ref, vx_ref, o_ref, *, seq):
    g = pl.program_id(1)
    j = pl.program_id(2)
    grp = q_ref.shape[1]
    wb = WIN_BLOCK
    m_rows = grp * wb
    k_ext = jnp.concatenate([kp_ref[0, 0], kc_ref[0, 0], kn_ref[0, 0]], axis=0).astype(BF16)
    v_ext = _with_ones(jnp.concatenate([vp_ref[0, 0], vc_ref[0, 0], vn_ref[0, 0]], axis=0).astype(BF16))
    k_ctx = kx_ref[0, 0].astype(BF16)
    v_ctx = _with_ones(vx_ref[0, 0].astype(BF16))
    sink = jnp.concatenate([jnp.full((wb, 1), sink_ref[g * grp + h], F32) for h in range(grp)], axis=0)
    row = lax.broadcasted_iota(jnp.int32, (m_rows, 3 * wb), 0) & (wb - 1)
    col = lax.broadcasted_iota(jnp.int32, (m_rows, 3 * wb), 1)
    in_band = jnp.where(jnp.abs(col - wb - row) <= WINDOW, 1.0, 0.0)
    col1 = lax.broadcasted_iota(jnp.int32, (1, 3 * wb), 1)
    for t in range(WIN_STEP):
        n = j * WIN_STEP + t
        kpos = (n - 1) * wb + col1
        in_seq = jnp.where((kpos >= 0) & (kpos < seq), 1.0, 0.0)
        q = jnp.concatenate([q_ref[0, h, t * wb:(t + 1) * wb, :] for h in range(grp)], axis=0)
        q = (q * (HEAD_DIM ** -0.5)).astype(BF16)
        s_loc = jnp.where(in_band * in_seq > 0.5, _dot_nt(q, k_ext[t * wb:(t + 3) * wb, :]), NEG)
        s_ctx = _dot_nt(q, k_ctx)
        m = jnp.maximum(jnp.max(jnp.concatenate([s_loc, s_ctx], axis=1), axis=-1, keepdims=True), sink)
        p_loc = jnp.exp(s_loc - m).astype(BF16)
        p_ctx = jnp.exp(s_ctx - m).astype(BF16)
        ov = _dot(p_loc, v_ext[t * wb:(t + 3) * wb, :]) + _dot(p_ctx, v_ctx)
        den = ov[:, HEAD_DIM:HEAD_DIM + 1] + jnp.exp(sink - m)
        o = (ov[:, :HEAD_DIM] / den).astype(o_ref.dtype)
        for h in range(grp):
            o_ref[0, h, t * wb:(t + 1) * wb, :] = o[h * wb:(h + 1) * wb, :]


def _win_call(q, k, v, kx, vx, sink):
    bsz, hq, s, dh = q.shape
    hkv = k.shape[1]
    grp = hq // hkv
    cx = kx.shape[2]
    nb = s // WIN_BLOCK
    step = WIN_STEP * WIN_BLOCK
    blk = (1, 1, WIN_BLOCK, dh)
    prev = lambda b, g, j, sk: (b, g, jnp.maximum(j * WIN_STEP - 1, 0), 0)
    cur = lambda b, g, j, sk: (b, g, j, 0)
    nxt = lambda b, g, j, sk: (b, g, jnp.minimum((j + 1) * WIN_STEP, nb - 1), 0)
    ctx = lambda b, g, j, sk: (b, g, 0, 0)
    return pl.pallas_call(
        functools.partial(_win_kernel, seq=s),
        grid_spec=pltpu.PrefetchScalarGridSpec(
            num_scalar_prefetch=1,
            grid=(bsz, hkv, nb // WIN_STEP),
            in_specs=[pl.BlockSpec((1, grp, step, dh), cur),
                      pl.BlockSpec(blk, prev), pl.BlockSpec((1, 1, step, dh), cur), pl.BlockSpec(blk, nxt),
                      pl.BlockSpec(blk, prev), pl.BlockSpec((1, 1, step, dh), cur), pl.BlockSpec(blk, nxt),
                      pl.BlockSpec((1, 1, cx, dh), ctx), pl.BlockSpec((1, 1, cx, dh), ctx)],
            out_specs=pl.BlockSpec((1, grp, step, dh), cur)),
        out_shape=jax.ShapeDtypeStruct((bsz, hq, s, dh), BF16),
        compiler_params=_params(("arbitrary", "arbitrary", "arbitrary")),
        name="mixer_window_attn",
    )(sink, q, k, k, k, v, v, v, kx, vx)


def _na_bias_table(rpb):
    qc = np.arange(GRID_W)
    kc = np.arange(GRID_W)
    cs = np.clip(qc - NA_COLS // 2, 0, GRID_W - NA_COLS)
    cmask = (kc[None, :] >= cs[:, None]) & (kc[None, :] < cs[:, None] + NA_COLS)
    ci = np.clip(kc[None, :] - qc[:, None] + NA_COLS - 1, 0, 2 * NA_COLS - 2)
    pick = (ci[None] == np.arange(2 * NA_COLS - 1)[:, None, None]).astype(np.float32)
    t15 = jnp.einsum('hrc,cqk->hrqk', rpb, jnp.asarray(pick), precision=lax.Precision.HIGHEST)
    t15 = jnp.where(cmask[None, None], t15, NEG)
    tab = jnp.stack([t15[:, d0:d0 + NA_ROWS] for d0 in range(NA_ROWS)], axis=1)
    return tab.transpose(0, 1, 3, 2, 4).reshape(rpb.shape[0], NA_ROWS, GRID_W, NA_ROWS * GRID_W)


def _na_kernel(q_ref, k_ref, v_ref, kx_ref, vx_ref, bias_ref, o_ref):
    rows = q_ref.shape[2] // GRID_W
    k_ctx = kx_ref[0, 0].astype(BF16)
    v_ctx = _with_ones(vx_ref[0, 0].astype(BF16))
    k_all = k_ref[0, 0].astype(BF16)
    v_all = _with_ones(v_ref[0, 0].astype(BF16))
    span = NA_ROWS * GRID_W
    chunk = 8
    for r0 in range(0, rows, chunk):
        q = (q_ref[0, 0, r0 * GRID_W:(r0 + chunk) * GRID_W, :] * (HEAD_DIM ** -0.5)).astype(BF16)
        s_parts = []
        for r in range(r0, r0 + chunk):
            rs = min(max(r - NA_ROWS // 2, 0), rows - NA_ROWS)
            k_loc = k_all[rs * GRID_W:rs * GRID_W + span, :]
            qr = q[(r - r0) * GRID_W:(r - r0 + 1) * GRID_W, :]
            s_parts.append(_dot_nt(qr, k_loc) + bias_ref[0, rs - r + NA_ROWS - 1])
        s_loc = jnp.concatenate(s_parts, axis=0)
        s_ctx = _dot_nt(q, k_ctx)
        m = jnp.max(jnp.concatenate([s_loc, s_ctx], axis=1), axis=-1, keepdims=True)
        p_loc = jnp.exp(s_loc - m).astype(BF16)
        p_ctx = jnp.exp(s_ctx - m).astype(BF16)
        o_parts = []
        for r in range(r0, r0 + chunk):
            rs = min(max(r - NA_ROWS // 2, 0), rows - NA_ROWS)
            v_loc = v_all[rs * GRID_W:rs * GRID_W + span, :]
            o_parts.append(_dot(p_loc[(r - r0) * GRID_W:(r - r0 + 1) * GRID_W, :], v_loc))
        ov = jnp.concatenate(o_parts, axis=0) + _dot(p_ctx, v_ctx)
        o = ov[:, :HEAD_DIM] / ov[:, HEAD_DIM:HEAD_DIM + 1]
        o_ref[0, 0, r0 * GRID_W:(r0 + chunk) * GRID_W, :] = o.astype(o_ref.dtype)


def _na_call(q, k, v, kx, vx, bias_tab):
    bsz, h, s, dh = q.shape
    cx = kx.shape[2]
    full = pl.BlockSpec((1, 1, s, dh), lambda b, hh: (b, hh, 0, 0))
    ctx = pl.BlockSpec((1, 1, cx, dh), lambda b, hh: (b, hh, 0, 0))
    return pl.pallas_call(
        _na_kernel,
        grid=(bsz, h),
        in_specs=[full, full, full, ctx, ctx,
                  pl.BlockSpec((1, NA_ROWS, GRID_W, NA_ROWS * GRID_W), lambda b, hh: (hh, 0, 0, 0))],
        out_specs=full,
        out_shape=jax.ShapeDtypeStruct((bsz, h, s, dh), BF16),
        compiler_params=_params(("arbitrary", "arbitrary")),
        name="mixer_neighbourhood_attn",
    )(q, k, v, kx, vx, bias_tab)


def _ctx_attn_kernel(sink_ref, q_ref, k_ref, v_ref, o_ref, *, use_sink):
    h = pl.program_id(1)
    q = (q_ref[0, 0] * (HEAD_DIM ** -0.5)).astype(BF16)
    s = _dot_nt(q, k_ref[0, 0].astype(BF16))
    m = jnp.max(s, axis=-1, keepdims=True)
    if use_sink:
        sink = sink_ref[h]
        m = jnp.maximum(m, sink)
    p = jnp.exp(s - m)
    den = jnp.sum(p, axis=-1, keepdims=True)
    if use_sink:
        den = den + jnp.exp(sink - m)
    o_ref[0, 0] = (_dot(p.astype(BF16), v_ref[0, 0].astype(BF16)) / den).astype(o_ref.dtype)


def _ctx_attn_call(q, k, v, sink):
    bsz, hq, cx, dh = q.shape
    grp = hq // k.shape[1]
    use_sink = sink is not None
    if sink is None:
        sink = jnp.zeros((hq,), F32)
    qs = pl.BlockSpec((1, 1, cx, dh), lambda b, h, sk: (b, h, 0, 0))
    ks = pl.BlockSpec((1, 1, cx, dh), lambda b, h, sk: (b, h // grp, 0, 0))
    return pl.pallas_call(
        functools.partial(_ctx_attn_kernel, use_sink=use_sink),
        grid_spec=pltpu.PrefetchScalarGridSpec(
            num_scalar_prefetch=1, grid=(bsz, hq), in_specs=[qs, ks, ks], out_specs=qs),
        out_shape=jax.ShapeDtypeStruct((bsz, hq, cx, dh), BF16),
        compiler_params=_params(("arbitrary", "arbitrary")),
        name="context_attn",
    )(sink, q, k, v)


def _merge_kernel(x_ref, ya_ref, yb_ref, yc_ref, yd_ref, wg0_ref, wg1_ref, wg2_ref, wg3_ref, wbr_ref, wout_ref,
                  ng_ref, sh_ref, sc_ref, gt_ref, o_ref, h_ref, acc_ref):
    n = pl.program_id(2)
    nt = acc_ref.shape[0]
    d = o_ref.shape[2]

    @pl.when(n == 0)
    def _():
        h = _rms(x_ref[0], ng_ref[0:1, :]) * (1.0 + sc_ref[0]) + sh_ref[0]
        h_ref[...] = h.astype(BF16)

    h = h_ref[...]
    acc = None
    for i, (y_ref, wg_ref) in enumerate(((ya_ref, wg0_ref), (yb_ref, wg1_ref), (yc_ref, wg2_ref), (yd_ref, wg3_ref))):
        t = _sigmoid(_dot(h, wg_ref[...])) * _dot(y_ref[0], wbr_ref[i])
        acc = t if acc is None else acc + t
    acc_ref[n] = acc.astype(BF16)

    @pl.when(n == nt - 1)
    def _():
        a = jnp.concatenate([acc_ref[j] for j in range(nt)], axis=1)
        oc = d // 4
        for c in range(4):
            o_ref[0, :, c * oc:(c + 1) * oc] = _dot(a, wout_ref[:, c * oc:(c + 1) * oc])
        o_ref[0] = x_ref[0] + gt_ref[0] * _rms(o_ref[0], ng_ref[1:2, :])


def _merge_call(x, ys, w_gate, w_branch, w_out, norm_g, mods, shared_row):
    bsz, s, d = x.shape
    c = BRANCH_DIM
    tm = min(s, 512)
    tn = 256
    nt = d // tn
    row = (lambda b: b) if shared_row is None else (lambda b: shared_row)
    ysp = pl.BlockSpec((1, tm, c), lambda b, i, n: (b, i, 0))
    wg = [pl.BlockSpec((d, tn), functools.partial(lambda b, i, n, k: (0, k * nt + n), k=k)) for k in range(4)]
    mod = lambda col: pl.BlockSpec((1, 1, d), lambda b, i, n: (row(b), 0, col))
    return pl.pallas_call(
        _merge_kernel,
        grid=(bsz, s // tm, nt),
        in_specs=[pl.BlockSpec((1, tm, d), lambda b, i, n: (b, i, 0)), ysp, ysp, ysp, ysp, *wg,
                  pl.BlockSpec((4, c, tn), lambda b, i, n: (0, 0, n)),
                  pl.BlockSpec((d, d), lambda b, i, n: (0, 0), pipeline_mode=pl.Buffered(1)),
                  pl.BlockSpec((4, d), lambda b, i, n: (0, 0)),
                  mod(0), mod(1), mod(2)],
        out_specs=pl.BlockSpec((1, tm, d), lambda b, i, n: (b, i, 0)),
        out_shape=jax.ShapeDtypeStruct((bsz, s, d), F32),
        scratch_shapes=[pltpu.VMEM((tm, d), BF16), pltpu.VMEM((nt, tm, tn), BF16)],
        compiler_params=_params(("arbitrary", "arbitrary", "arbitrary")),
        name="mixer_merge",
    )(x, *ys, w_gate, w_gate, w_gate, w_gate, w_branch, w_out, norm_g, mods, mods, mods)


def _expert_row_ids():
    r = np.arange(N_EXPERTS)
    return ((r % N_GROUPS) * GROUP_SIZE + r // N_GROUPS).astype(np.int32)


def _router_kernel(rows_ref, x_ref, ng_ref, sh_ref, sc_ref, rw_ref, rb_ref, eid_ref, tri_e_ref, tri_t_ref,
                   sw1_ref, sw3_ref, sw2_ref, h_ref, so_ref, e_ref, r_ref, w_ref, cnt_ref):
    i = pl.program_id(0)
    tm = x_ref.shape[0]

    @pl.when(i == 0)
    def _():
        cnt_ref[...] = jnp.zeros(cnt_ref.shape, F32)

    h = _rms(x_ref[...], ng_ref[2:3, :]) * (1.0 + sc_ref[0]) + sh_ref[0]
    h_ref[:, 0, :] = _pack_halves(h)
    hb = h.astype(BF16)
    so_ref[...] = _dot((_silu(_dot(hb, sw1_ref[...])) * _dot(hb, sw3_ref[...])).astype(BF16), sw2_ref[...])

    h_tail = (h - hb.astype(F32)).astype(BF16)
    head = _dot_nt(rw_ref[...], hb)
    logits = head[:N_EXPERTS] + head[N_EXPERTS:] + _dot_nt(rw_ref[:N_EXPERTS, :], h_tail)
    scores = _sigmoid(logits)
    biased = scores + rb_ref[...]
    m1 = jnp.full((N_GROUPS, tm), -jnp.inf, F32)
    m2 = m1
    for j in range(GROUP_SIZE):
        v = biased[j * N_GROUPS:(j + 1) * N_GROUPS, :]
        m2 = jnp.maximum(m2, jnp.minimum(m1, v))
        m1 = jnp.maximum(m1, v)
    gs = m1 + m2
    gid = lax.broadcasted_iota(jnp.int32, (N_GROUPS, tm), 0)
    beat = jnp.zeros((N_GROUPS, tm), jnp.int32)
    for g in range(N_GROUPS):
        o = gs[g:g + 1, :]
        beat = beat + jnp.where((o > gs) | ((o == gs) & (g < gid)), 1, 0)
    keep = beat < TOPK_GROUPS
    masked = jnp.concatenate(
        [jnp.where(keep, biased[j * N_GROUPS:(j + 1) * N_GROUPS, :], NEG) for j in range(GROUP_SIZE)], axis=0)
    eid = eid_ref[...]
    row_ids = _expert_row_ids()
    beat = jnp.zeros((N_EXPERTS, tm), jnp.int32)
    for r in range(N_EXPERTS):
        o = masked[r:r + 1, :]
        beat = beat + jnp.where((o > masked) | ((o == masked) & (int(row_ids[r]) < eid)), 1, 0)
    sel = beat < TOP_K
    self32 = jnp.where(sel, 1.0, 0.0)
    selb = self32.astype(BF16)
    wsel = jnp.where(sel, scores, 0.0)
    wsel = wsel / jnp.sum(wsel, axis=0, keepdims=True) * ROUTE_SCALE
    rank = cnt_ref[:, 0:1] + _dot(selb, tri_t_ref[...])
    cnt_ref[...] = cnt_ref[...] + jnp.sum(self32, axis=1, keepdims=True)
    slot = _dot(tri_e_ref[...], selb)
    eidf = eid.astype(F32)
    for k in range(TOP_K):
        mk = sel & (slot == float(k))
        e_ref[k:k + 1, :] = jnp.sum(jnp.where(mk, eidf, 0.0), axis=0, keepdims=True).astype(jnp.int32)
        r_ref[k:k + 1, :] = jnp.sum(jnp.where(mk, rank, 0.0), axis=0, keepdims=True).astype(jnp.int32)
        w_ref[k:k + 1, :] = jnp.sum(jnp.where(mk, wsel, 0.0), axis=0, keepdims=True)


def _router_call(xcat, tile_rows, norm_g, mods, router_w, router_b, sw1, sw3, sw2):
    t, d = xcat.shape
    tm = 256
    f = sw1.shape[1]
    ids = _expert_row_ids()
    rw = router_w.T[ids]
    rw_head = rw.astype(BF16)
    rw = jnp.concatenate([rw_head, (rw - rw_head.astype(F32)).astype(BF16)], axis=0)
    rb = router_b[ids].reshape(N_EXPERTS, 1)
    tri_e = jnp.asarray(np.tril(np.ones((N_EXPERTS, N_EXPERTS), np.float32), -1), BF16)
    tri_t = jnp.asarray(np.triu(np.ones((tm, tm), np.float32), 1), BF16)
    const = lambda shape: pl.BlockSpec(shape, lambda i, rows: (0,) * len(shape))
    tok = pl.BlockSpec((tm, d), lambda i, rows: (i, 0))
    slots = pl.BlockSpec((TOP_K, tm), lambda i, rows: (0, i))
    return pl.pallas_call(
        _router_kernel,
        grid_spec=pltpu.PrefetchScalarGridSpec(
            num_scalar_prefetch=1,
            grid=(t // tm,),
            in_specs=[tok, const((4, d)),
                      pl.BlockSpec((1, 1, d), lambda i, rows: (rows[i], 0, 3)),
                      pl.BlockSpec((1, 1, d), lambda i, rows: (rows[i], 0, 4)),
                      const((2 * N_EXPERTS, d)), const((N_EXPERTS, 1)), const((N_EXPERTS, 1)),
                      const((N_EXPERTS, N_EXPERTS)), const((tm, tm)),
                      const((d, f)), const((d, f)), const((f, d))],
            out_specs=[pl.BlockSpec((tm, 1, d // 2), lambda i, rows: (i, 0, 0)), tok, slots, slots, slots,
                       const((N_EXPERTS, 128))]),
        out_shape=[jax.ShapeDtypeStruct((t, 1, d // 2), jnp.uint32), jax.ShapeDtypeStruct((t, d), F32),
                   jax.ShapeDtypeStruct((TOP_K, t), jnp.int32), jax.ShapeDtypeStruct((TOP_K, t), jnp.int32),
                   jax.ShapeDtypeStruct((TOP_K, t), F32), jax.ShapeDtypeStruct((N_EXPERTS, 128), F32)],
        compiler_params=_params(("arbitrary",)),
        name="moe_router",
    )(tile_rows, xcat, norm_g, mods, mods, rw, rb, jnp.asarray(ids).reshape(N_EXPERTS, 1), tri_e, tri_t,
      sw1, sw3, sw2)


def _gmm_kernel(blk_ref, exp_ref, lo_ref, hi_ref, first_ref, newexp_ref, tok_cur_ref, tok_nxt_ref, h_hbm, ws_ref,
                w1_ref, w3_ref, w2_ref, o_ref, xa_ref, xb_ref, w1b_ref, w3b_ref, w2b_ref, sem):
    i = pl.program_id(0)
    last = pl.num_programs(0) - 1
    nrows = xa_ref.shape[0]
    xbufs = (xa_ref, xb_ref)

    def start_gather(tok_ref, s):
        for r in range(nrows):
            pltpu.make_async_copy(h_hbm.at[tok_ref[0, 0, r]], xbufs[s].at[pl.ds(r, 1)],
                                  sem.at[s]).start(priority=r % 2)

    def wait_gather(s):
        for r in range(nrows):
            pltpu.make_async_copy(h_hbm.at[0], xbufs[s].at[pl.ds(r, 1)], sem.at[s]).wait()

    @pl.when(i == 0)
    def _():
        start_gather(tok_cur_ref, 0)

    @pl.when(first_ref[i] == 1)
    def _():
        o_ref[...] = jnp.zeros(o_ref.shape, o_ref.dtype)

    @pl.when(newexp_ref[i] == 1)
    def _():
        w1b_ref[...] = w1_ref[0].astype(BF16)
        w3b_ref[...] = w3_ref[0].astype(BF16)
        w2b_ref[...] = w2_ref[0].astype(BF16)

    def step(s):
        wait_gather(s)

        @pl.when(lo_ref[i] >= 0)
        def _():
            start_gather(tok_nxt_ref, 1 - s)

        x_lo, x_hi = _unpack_halves(xbufs[s][...])
        x = jnp.concatenate([x_lo.astype(BF16), x_hi.astype(BF16)], axis=1)
        wcol = jnp.concatenate([jnp.transpose(ws_ref[0, t])[:, 0:1] for t in range(ws_ref.shape[1])], axis=0)
        a = (_silu(_dot(x, w1b_ref[...])) * _dot(x, w3b_ref[...]) * wcol).astype(BF16)
        y = _pack_halves(_dot(a, w2b_ref[...]))
        rows = lax.broadcasted_iota(jnp.int32, (nrows, 1), 0)
        keep = (rows >= lo_ref[i]) & (rows < hi_ref[i])
        o_ref[:, 0, :] = jnp.where(keep, y, o_ref[:, 0, :])

        @pl.when(i == last)
        def _():
            wait_gather(1 - s)

    for s in range(2):
        pl.when(i % 2 == s)(functools.partial(step, s))


def _gmm_call(items, slot_tok, slot_w, hp, w1, w3, w2, layer):
    a = slot_tok.shape[0]
    d, f = w1.shape[2], w1.shape[3]
    n_items = items[0].shape[0]
    nb = a // MOE_BLOCK
    tok3 = slot_tok.reshape(nb, 1, MOE_BLOCK)
    ws = jnp.pad(slot_w.reshape(nb, MOE_BLOCK // 128, 1, 128), ((0, 0), (0, 0), (0, 7), (0, 0)))
    rows = pl.BlockSpec((MOE_BLOCK,) + hp.shape[1:], lambda i, blk, ex, *_: (blk[i], 0, 0))
    toks = lambda step: pl.BlockSpec((1, 1, MOE_BLOCK), lambda i, blk, ex, *_: (blk[step(i)], 0, 0),
                                     memory_space=pltpu.SMEM)
    return pl.pallas_call(
        _gmm_kernel,
        grid_spec=pltpu.PrefetchScalarGridSpec(
            num_scalar_prefetch=len(items),
            grid=(n_items,),
            in_specs=[toks(lambda i: i), toks(lambda i: jnp.minimum(i + 1, n_items - 1)),
                      pl.BlockSpec(memory_space=pl.ANY),
                      pl.BlockSpec((1, MOE_BLOCK // 128, 8, 128), lambda i, blk, ex, *_: (blk[i], 0, 0, 0)),
                      pl.BlockSpec((None, 1, d, f), lambda i, blk, ex, *_: (layer, ex[i], 0, 0)),
                      pl.BlockSpec((None, 1, d, f), lambda i, blk, ex, *_: (layer, ex[i], 0, 0)),
                      pl.BlockSpec((None, 1, f, d), lambda i, blk, ex, *_: (layer, ex[i], 0, 0))],
            out_specs=rows,
            scratch_shapes=[pltpu.VMEM((MOE_BLOCK, hp.shape[2]), jnp.uint32),
                            pltpu.VMEM((MOE_BLOCK, hp.shape[2]), jnp.uint32),
                            pltpu.VMEM((d, f), BF16), pltpu.VMEM((d, f), BF16), pltpu.VMEM((f, d), BF16),
                            pltpu.SemaphoreType.DMA((2,))]),
        out_shape=jax.ShapeDtypeStruct((a,) + hp.shape[1:], jnp.uint32),
        compiler_params=_params(("arbitrary",)),
        name="moe_experts",
    )(*items, tok3, tok3, hp, ws, w1, w3, w2)


def _moe_items(counts, n_rows):
    nb = n_rows // MOE_BLOCK
    n_items = nb + N_EXPERTS
    ends = jnp.cumsum(counts).astype(jnp.int32)
    bstart = jnp.arange(nb, dtype=jnp.int32) * MOE_BLOCK
    pos_b = jnp.arange(nb, dtype=jnp.int32) + jnp.sum(ends[None, :] < bstart[:, None], axis=1)
    idx = jnp.arange(n_items, dtype=jnp.int32)
    n_b = jnp.sum(pos_b[None, :] <= idx[:, None], axis=1).astype(jnp.int32)
    n_e = idx + 1 - n_b
    ends0 = jnp.concatenate([jnp.zeros((1,), jnp.int32), ends])
    last_end = jnp.sum(jnp.where(n_e[:, None] == jnp.arange(N_EXPERTS + 1)[None, :], ends0[None, :], 0), axis=1)
    start = jnp.maximum((n_b - 1) * MOE_BLOCK, last_end).astype(jnp.int32)
    stop = jnp.concatenate([start[1:], jnp.full((1,), n_rows, jnp.int32)])
    blk = jnp.minimum(start // MOE_BLOCK, nb - 1)
    ex = jnp.minimum(jnp.sum(ends[None, :] <= start[:, None], axis=1), N_EXPERTS - 1).astype(jnp.int32)
    lo = start - blk * MOE_BLOCK
    hi = stop - blk * MOE_BLOCK
    one = jnp.ones((1,), jnp.int32)
    first = jnp.concatenate([one, (blk[1:] != blk[:-1]).astype(jnp.int32)])
    newexp = jnp.concatenate([one, (ex[1:] != ex[:-1]).astype(jnp.int32)])
    return blk.astype(jnp.int32), ex, lo.astype(jnp.int32), hi.astype(jnp.int32), first, newexp


def _combine_kernel(rows_ref, dest_ref, dest_nxt_ref, ys_hbm, so_ref, x_ref, ng_ref, gt_ref, o_ref,
                    buf_ref, sem):
    i = pl.program_id(0)
    tc = x_ref.shape[0]
    slot = i % 2

    def gather(idx_ref, s):
        def body(j, c):
            for k in range(TOP_K):
                pltpu.make_async_copy(ys_hbm.at[idx_ref[0, 0, k * tc + j]], buf_ref.at[s, k, pl.ds(j, 1)],
                                      sem.at[s]).start(priority=k % 2)
            return c
        lax.fori_loop(0, tc, body, 0, unroll=4)

    @pl.when(i == 0)
    def _():
        gather(dest_ref, 0)

    @pl.when(i < pl.num_programs(0) - 1)
    def _():
        gather(dest_nxt_ref, 1 - slot)

    pltpu.make_async_copy(buf_ref.at[slot], buf_ref.at[slot], sem.at[slot]).wait()
    acc_lo, acc_hi = _unpack_halves(buf_ref[slot, 0])
    for k in range(1, TOP_K):
        y_lo, y_hi = _unpack_halves(buf_ref[slot, k])
        acc_lo = acc_lo + y_lo
        acc_hi = acc_hi + y_hi
    fl = jnp.concatenate([acc_lo, acc_hi], axis=1) + so_ref[...]
    o_ref[...] = x_ref[...] + gt_ref[0] * _rms(fl, ng_ref[3:4, :])


def _combine_call(tile_rows, dest, ys, shared_out, xcat, norm_g, mods):
    t, d = xcat.shape
    tc = 128
    dest3 = dest.reshape(TOP_K, t // tc, tc).transpose(1, 0, 2).reshape(t // tc, 1, TOP_K * tc)
    tok = pl.BlockSpec((tc, d), lambda i, rows: (i, 0))
    nt = t // tc
    return pl.pallas_call(
        _combine_kernel,
        grid_spec=pltpu.PrefetchScalarGridSpec(
            num_scalar_prefetch=1,
            grid=(nt,),
            in_specs=[pl.BlockSpec((1, 1, TOP_K * tc), lambda i, rows: (i, 0, 0), memory_space=pltpu.SMEM),
                      pl.BlockSpec((1, 1, TOP_K * tc), lambda i, rows: (jnp.minimum(i + 1, nt - 1), 0, 0),
                                   memory_space=pltpu.SMEM),
                      pl.BlockSpec(memory_space=pl.ANY),
                      tok, tok,
                      pl.BlockSpec((4, d), lambda i, rows: (0, 0)),
                      pl.BlockSpec((1, 1, d), lambda i, rows: (rows[i], 0, 5))],
            out_specs=tok,
            scratch_shapes=[pltpu.VMEM((2, TOP_K, tc, d // 2), jnp.uint32), pltpu.SemaphoreType.DMA((2,))]),
        out_shape=jax.ShapeDtypeStruct((t, d), F32),
        compiler_params=_params(("arbitrary",)),
        name="moe_combine",
    )(tile_rows, dest3, dest3, ys, shared_out, xcat, norm_g, mods)


def _channel_mixer(xcat, mod_rows, norm_g, mods, router_w, router_b, w1, w3, w2, layer, sw1, sw3, sw2):
    t, _ = xcat.shape
    h, shared_out, eidx, rank, wsel, cnt = _router_call(xcat, mod_rows[::2], norm_g, mods, router_w, router_b,
                                                         sw1, sw3, sw2)
    counts = jnp.zeros((N_EXPERTS,), jnp.int32).at[_expert_row_ids()].set(cnt[:, 0].astype(jnp.int32))
    starts = jnp.cumsum(counts) - counts
    sel = eidx[:, :, None] == jnp.arange(N_EXPERTS, dtype=jnp.int32)[None, None, :]
    dest = jnp.sum(jnp.where(sel, starts[None, None, :], 0), axis=-1) + rank
    flat_tok = jnp.tile(jnp.arange(t, dtype=jnp.int32), TOP_K)
    _, slot_tok, slot_w = lax.sort((dest.reshape(-1), flat_tok, wsel.reshape(-1)), num_keys=1)
    ys = _gmm_call(_moe_items(counts, t * TOP_K), slot_tok, slot_w, h, w1, w3, w2, layer)
    return _combine_call(mod_rows, dest, ys, shared_out, xcat, norm_g, mods)


def _heads(t, n):
    b, s, _ = t.shape
    return t.reshape(b, s, n, HEAD_DIM).transpose(0, 2, 1, 3)


def _unheads(t):
    b, n, s, dh = t.shape
    return t.transpose(0, 2, 1, 3).reshape(b, s, n * dh)


def kernel(x, c, ctx, c_ctx, w_ada, b_ada, norm_g, w_in, conv_b_in, conv_dw, conv_dw_b, conv_ln_g, conv_ln_b, gmlp_ln_g, gmlp_ln_b, gmlp_ws, gmlp_bs, win_sink, na_rpb, w_branch, w_out, router_w, router_b, exp_w1, exp_w3, exp_w2, sh_w1, sh_w3, sh_w2):
    bsz, s, d = x.shape
    cx = ctx.shape[1]
    depth = w_ada.shape[0]
    ctx_row = bsz
    c_all = jnp.zeros((MODS_ROWS, d), F32).at[:bsz].set(c).at[ctx_row].set(c_ctx)
    mods_all = _ada_call(c_all, w_ada, b_ada)
    lat_rows = jnp.repeat(jnp.arange(bsz, dtype=jnp.int32), s // 128)
    ctx_rows = jnp.full((bsz * cx // 128,), ctx_row, jnp.int32)
    xl, xc = x, ctx
    for l in range(depth):
        last = l == depth - 1
        mods = mods_all[l].reshape(MODS_ROWS, 1, N_MODS * d)
        g_off = w_in.shape[2] - 4 * d
        kv_off = g_off - KV_COLS
        w_small = jnp.concatenate([w_in[l, :, :kv_off], w_in[l, :, kv_off + 256:g_off],
                                   w_in[l, :, kv_off:kv_off + 256]], axis=1).astype(BF16)
        w_gate = w_in[l, :, g_off:].astype(BF16)
        w_br = w_branch[l].astype(BF16)
        w_o = w_out[l].astype(BF16)
        pl_l = _proj_call(xl, norm_g[l], mods, w_small, None, SMALL_COLS // 2)
        if last:
            pkv_c = _proj_call(xc, norm_g[l], mods, w_small[:, COL_DK:], ctx_row, KV_COLS)
            kv_base = 0
        else:
            pc = _proj_call(xc, norm_g[l], mods, w_small, ctx_row, SMALL_COLS // 2)
            pkv_c, kv_base = pc, COL_DK
        dkc = _heads(pkv_c[..., kv_base:kv_base + 512], NA_HEADS)
        dvc = _heads(pkv_c[..., kv_base + 512:kv_base + 1024], NA_HEADS)
        ckc = _heads(pkv_c[..., kv_base + 1024:kv_base + 1152], WIN_KV_HEADS)
        cvc = _heads(pkv_c[..., kv_base + 1152:kv_base + 1280], WIN_KV_HEADS)
        conv_args = (conv_b_in[l], conv_dw[l], conv_dw_b[l], conv_ln_g[l], conv_ln_b[l])
        gmlp_args = (gmlp_ln_g[l], gmlp_ln_b[l], gmlp_ws[l], gmlp_bs[l])
        ya = _conv_call(pl_l, *conv_args)
        yb = _gmlp_call(pl_l, *gmlp_args)
        q_rope, k_rope = _rope_call(pl_l)
        yc = _unheads(_win_call(_heads(q_rope, WIN_HEADS), _heads(k_rope, WIN_KV_HEADS),
                                _heads(pl_l[..., COL_CV:COL_CV + 128], WIN_KV_HEADS), ckc, cvc, win_sink[l]))
        yd = _unheads(_na_call(_heads(pl_l[..., COL_DQ:COL_DQ + 512], NA_HEADS),
                               _heads(pl_l[..., COL_DK:COL_DK + 512], NA_HEADS),
                               _heads(pl_l[..., COL_DV:COL_DV + 512], NA_HEADS), dkc, dvc,
                               _na_bias_table(na_rpb[l])))
        xl = _merge_call(xl, (ya, yb, yc, yd), w_gate, w_br, w_o, norm_g[l], mods, None)
        moe_w = (router_w[l], router_b[l], exp_w1, exp_w3, exp_w2, l,
                 sh_w1[l].astype(BF16), sh_w3[l].astype(BF16), sh_w2[l].astype(BF16))
        if last:
            xl = _channel_mixer(xl.reshape(bsz * s, d), lat_rows, norm_g[l], mods, *moe_w).reshape(bsz, s, d)
        else:
            yca = _conv_call(pc, *conv_args)
            ycb = _gmlp_call(pc, *gmlp_args)
            ycc = _unheads(_ctx_attn_call(_heads(pc[..., COL_CQ:COL_CQ + 512], WIN_HEADS), ckc, cvc, win_sink[l]))
            ycd = _unheads(_ctx_attn_call(_heads(pc[..., COL_DQ:COL_DQ + 512], NA_HEADS), dkc, dvc, None))
            xc = _merge_call(xc, (yca, ycb, ycc, ycd), w_gate, w_br, w_o, norm_g[l], mods, ctx_row)
            xcat = jnp.concatenate([xl.reshape(bsz * s, d), xc.reshape(bsz * cx, d)], axis=0)
            out = _channel_mixer(xcat, jnp.concatenate([lat_rows, ctx_rows]), norm_g[l], mods, *moe_w)
            xl = out[:bsz * s].reshape(bsz, s, d)
            xc = out[bsz * s:].reshape(bsz, cx, d)
    return xl
```

```python
import functools

import numpy as np
import jax
import jax.numpy as jnp
from jax import lax
from jax.experimental import pallas as pl
from jax.experimental.pallas import tpu as pltpu

F32 = jnp.float32
BF16 = jnp.bfloat16

GRID_W = 64
HEAD_DIM = 64
EPS = 1e-6
NEG = -1e30
BRANCH_DIM = 512
CONV_K = 31
CONV_HALO = 16
GMLP_CHUNK = 128
GMLP_GROUPS = 4
WIN_HEADS = 8
WIN_KV_HEADS = 2
WINDOW = 128
WIN_BLOCK = 128
ROPE_BASE = 10000.0
NA_HEADS = 8
NA_ROWS = 8
NA_COLS = 16
N_EXPERTS = 64
TOP_K = 8
N_GROUPS = 8
GROUP_SIZE = N_EXPERTS // N_GROUPS
TOPK_GROUPS = 4
ROUTE_SCALE = 2.5
MOE_BLOCK = 256
N_MODS = 6
MODS_ROWS = 16
VMEM_LIMIT = 52 * 1024 * 1024

COL_A, COL_B, COL_CQ, COL_DQ, COL_DK, COL_DV, COL_CK, COL_CV = 0, 1024, 2048, 2560, 3072, 3584, 4096, 4224
SMALL_COLS = 4352
KV_COLS = 1280


def _sigmoid(x):
    return 1.0 / (1.0 + jnp.exp(-x))


def _silu(x):
    return x * _sigmoid(x)


def _gelu_tanh(x):
    return 0.5 * x * (1.0 + jnp.tanh(np.sqrt(2.0 / np.pi).astype(np.float32) * (x + 0.044715 * (x * x * x))))


def _rms(x, g):
    return x * lax.rsqrt(jnp.mean(x * x, axis=-1, keepdims=True) + EPS) * g


def _ln(x, g, b):
    mu = jnp.mean(x, axis=-1, keepdims=True)
    xc = x - mu
    var = jnp.mean(xc * xc, axis=-1, keepdims=True)
    return xc * lax.rsqrt(var + EPS) * g + b


def _dot(a, b):
    return jnp.dot(a, b, preferred_element_type=F32)


def _dot_nt(a, b):
    return lax.dot_general(a, b, (((1,), (1,)), ((), ())), preferred_element_type=F32)


def _pack_halves(x):
    n = x.shape[1] // 2
    lo = lax.bitcast_convert_type(x[:, :n].astype(BF16).astype(F32), jnp.uint32)
    hi = lax.bitcast_convert_type(x[:, n:].astype(BF16).astype(F32), jnp.uint32)
    return (hi & jnp.uint32(0xFFFF0000)) | (lo >> 16)


def _unpack_halves(w):
    lo = lax.bitcast_convert_type(w << 16, F32)
    hi = lax.bitcast_convert_type(w & jnp.uint32(0xFFFF0000), F32)
    return lo, hi


def _with_ones(v):
    return jnp.concatenate([v, jnp.ones_like(v)], axis=1)


def _params(sem):
    return pltpu.CompilerParams(dimension_semantics=sem, vmem_limit_bytes=VMEM_LIMIT)


def _ada_kernel(c_ref, w_ref, b_ref, o_ref):
    s = _silu(c_ref[...])
    o_ref[0] = _dot(s.astype(BF16), w_ref[0].astype(BF16)) + b_ref[0]


def _ada_call(c_all, w_ada, b_ada):
    depth, d, n = w_ada.shape
    tn = 512
    return pl.pallas_call(
        _ada_kernel,
        grid=(depth, n // tn),
        in_specs=[pl.BlockSpec((MODS_ROWS, d), lambda l, j: (0, 0)),
                  pl.BlockSpec((1, d, tn), lambda l, j: (l, 0, j)),
                  pl.BlockSpec((1, 1, tn), lambda l, j: (l, 0, j))],
        out_specs=pl.BlockSpec((1, MODS_ROWS, tn), lambda l, j: (l, 0, j)),
        out_shape=jax.ShapeDtypeStruct((depth, MODS_ROWS, n), F32),
        compiler_params=_params(("arbitrary", "arbitrary")),
        name="ada_mods",
    )(c_all, w_ada, b_ada.reshape(depth, 1, n))


def _proj_kernel(x_ref, g_ref, sh_ref, sc_ref, w_ref, o_ref):
    h = _rms(x_ref[0], g_ref[0:1, :]) * (1.0 + sc_ref[0]) + sh_ref[0]
    o_ref[0] = _dot(h.astype(BF16), w_ref[...])


def _proj_call(x, norm_g, mods, w, shared_row, tn):
    bsz, s, d = x.shape
    n = w.shape[1]
    tm = min(s, 512)
    row = (lambda b: b) if shared_row is None else (lambda b: shared_row)
    return pl.pallas_call(
        _proj_kernel,
        grid=(n // tn, bsz, s // tm),
        in_specs=[pl.BlockSpec((1, tm, d), lambda j, b, i: (b, i, 0)),
                  pl.BlockSpec((4, d), lambda j, b, i: (0, 0)),
                  pl.BlockSpec((1, 1, d), lambda j, b, i: (row(b), 0, 0)),
                  pl.BlockSpec((1, 1, d), lambda j, b, i: (row(b), 0, 1)),
                  pl.BlockSpec((d, tn), lambda j, b, i: (0, j))],
        out_specs=pl.BlockSpec((1, tm, tn), lambda j, b, i: (b, i, j)),
        out_shape=jax.ShapeDtypeStruct((bsz, s, n), F32),
        compiler_params=_params(("arbitrary", "arbitrary", "arbitrary")),
        name="proj_in",
    )(x, norm_g, mods, mods, w)


def _conv_kernel(a_ref, ap_ref, an_ref, bin_ref, dw_ref, dwb_ref, g_ref, b_ref, o_ref, ypad_ref):
    i = pl.program_id(1)
    nblk = pl.num_programs(1)
    ts = a_ref.shape[1]
    c = BRANCH_DIM

    def glu(a):
        a = a + bin_ref[...]
        return a[:, :c] * _sigmoid(a[:, c:])

    ypad_ref[0:CONV_HALO, :] = jnp.where(i > 0, glu(ap_ref[0]), 0.0)
    ypad_ref[CONV_HALO:CONV_HALO + ts, :] = glu(a_ref[0])
    ypad_ref[CONV_HALO + ts:2 * CONV_HALO + ts, :] = jnp.where(i < nblk - 1, glu(an_ref[0]), 0.0)
    rc = 64
    off = CONV_HALO - CONV_K // 2
    sub = 8
    for r0 in range(0, ts, rc):
        acc = None
        for r in range(sub):
            z = None
            for base in range(0, off + CONV_K, sub):
                j = base + r - off
                if 0 <= j < CONV_K:
                    term = ypad_ref[r0 + base:r0 + base + rc + sub, :] * dw_ref[j:j + 1, :]
                    z = term if z is None else z + term
            if z is not None:
                z = z[r:r + rc, :]
                acc = z if acc is None else acc + z
        y = _ln(acc + dwb_ref[...], g_ref[...], b_ref[...])
        o_ref[0, r0:r0 + rc, :] = _silu(y).astype(o_ref.dtype)


def _conv_call(pl_all, b_in, dw, dw_b, ln_g, ln_b):
    bsz, s, _ = pl_all.shape
    ts = min(s, 512)
    c = BRANCH_DIM
    hb = ts // CONV_HALO
    last = s // CONV_HALO - 1
    dw_pad = jnp.zeros((32, c), F32).at[:CONV_K].set(dw)
    return pl.pallas_call(
        _conv_kernel,
        grid=(bsz, s // ts),
        in_specs=[pl.BlockSpec((1, ts, 2 * c), lambda b, i: (b, i, 0)),
                  pl.BlockSpec((1, CONV_HALO, 2 * c), lambda b, i: (b, jnp.maximum(i * hb - 1, 0), 0)),
                  pl.BlockSpec((1, CONV_HALO, 2 * c), lambda b, i: (b, jnp.minimum((i + 1) * hb, last), 0)),
                  pl.BlockSpec((1, 2 * c), lambda b, i: (0, 0)),
                  pl.BlockSpec((32, c), lambda b, i: (0, 0)),
                  pl.BlockSpec((1, c), lambda b, i: (0, 0)),
                  pl.BlockSpec((1, c), lambda b, i: (0, 0)),
                  pl.BlockSpec((1, c), lambda b, i: (0, 0))],
        out_specs=pl.BlockSpec((1, ts, c), lambda b, i: (b, i, 0)),
        out_shape=jax.ShapeDtypeStruct((bsz, s, c), BF16),
        scratch_shapes=[pltpu.VMEM((ts + 2 * CONV_HALO, c), F32)],
        compiler_params=_params(("arbitrary", "arbitrary")),
        name="mixer_conv",
    )(pl_all, pl_all, pl_all, b_in.reshape(1, 2 * c), dw_pad, dw_b.reshape(1, c), ln_g.reshape(1, c), ln_b.reshape(1, c))


def _gmlp_kernel(z_ref, g_ref, b_ref, ws_ref, bs_ref, o_ref):
    c = BRANCH_DIM
    gw = c // GMLP_GROUPS
    ts = z_ref.shape[1]
    for n in range(ts // GMLP_CHUNK):
        rows = slice(n * GMLP_CHUNK, (n + 1) * GMLP_CHUNK)
        z = _gelu_tanh(z_ref[0, rows, :])
        u = z[:, :c]
        v = _ln(z[:, c:], g_ref[...], b_ref[...]).astype(BF16)
        for g in range(GMLP_GROUPS):
            cols = slice(g * gw, (g + 1) * gw)
            sg = _dot(ws_ref[g], v[:, cols]) + bs_ref[:, g:g + 1]
            o_ref[0, rows, cols] = (u[:, cols] * sg).astype(o_ref.dtype)


def _gmlp_call(pl_all, ln_g, ln_b, ws, bs):
    bsz, s, _ = pl_all.shape
    ts = min(s, 512)
    c = BRANCH_DIM
    return pl.pallas_call(
        _gmlp_kernel,
        grid=(bsz, s // ts),
        in_specs=[pl.BlockSpec((1, ts, 2 * c), lambda b, i: (b, i, COL_B // (2 * c))),
                  pl.BlockSpec((1, c), lambda b, i: (0, 0)),
                  pl.BlockSpec((1, c), lambda b, i: (0, 0)),
                  pl.BlockSpec((GMLP_GROUPS, GMLP_CHUNK, GMLP_CHUNK), lambda b, i: (0, 0, 0)),
                  pl.BlockSpec((GMLP_CHUNK, GMLP_GROUPS), lambda b, i: (0, 0))],
        out_specs=pl.BlockSpec((1, ts, c), lambda b, i: (b, i, 0)),
        out_shape=jax.ShapeDtypeStruct((bsz, s, c), BF16),
        compiler_params=_params(("arbitrary", "arbitrary")),
        name="mixer_gmlp",
    )(pl_all, ln_g.reshape(1, c), ln_b.reshape(1, c), ws.astype(BF16), bs.T)


def _rope_tables(s):
    half, quarter = HEAD_DIM // 2, HEAD_DIM // 4
    pos = np.arange(s)
    prow = (pos // GRID_W).astype(np.float32)
    pcol = (pos % GRID_W).astype(np.float32)
    inv = (ROPE_BASE ** (-np.arange(quarter, dtype=np.float32) / quarter)).astype(np.float32)
    lane = np.arange(128)
    in_head = lane % HEAD_DIM
    p = np.where((in_head < half)[None, :], prow[:, None], pcol[:, None]).astype(np.float32)
    ang = (p * inv[lane % quarter][None, :]).astype(np.float32).astype(np.float64)
    first = (in_head % half) < quarter
    cos = np.cos(ang)
    sin = np.where(first[None, :], -np.sin(ang), np.sin(ang))
    swap = np.zeros((128, 128), np.float32)
    for j in lane:
        swap[j + quarter if first[j] else j - quarter, j] = 1.0
    return cos.astype(np.float32), sin.astype(np.float32), swap


def _rope_kernel(q_ref, k_ref, cos_ref, sin_ref, swap_ref, qo_ref, ko_ref):
    cos, sin, swap = cos_ref[...], sin_ref[...], swap_ref[...]

    def rot(x):
        parts = []
        for t in range(x.shape[1] // 128):
            xt = x[:, t * 128:(t + 1) * 128]
            parts.append(xt * cos + _dot(xt.astype(BF16), swap) * sin)
        return parts[0] if len(parts) == 1 else jnp.concatenate(parts, axis=1)

    qo_ref[0] = rot(q_ref[0])
    ko_ref[0] = rot(k_ref[0])


def _rope_call(pl_all):
    bsz, s, _ = pl_all.shape
    tm = 512
    cos, sin, swap = _rope_tables(s)
    qw, kw = WIN_HEADS * HEAD_DIM, WIN_KV_HEADS * HEAD_DIM
    return pl.pallas_call(
        _rope_kernel,
        grid=(bsz, s // tm),
        in_specs=[pl.BlockSpec((1, tm, qw), lambda b, i: (b, i, COL_CQ // qw)),
                  pl.BlockSpec((1, tm, kw), lambda b, i: (b, i, COL_CK // kw)),
                  pl.BlockSpec((tm, 128), lambda b, i: (i, 0)),
                  pl.BlockSpec((tm, 128), lambda b, i: (i, 0)),
                  pl.BlockSpec((128, 128), lambda b, i: (0, 0))],
        out_specs=[pl.BlockSpec((1, tm, qw), lambda b, i: (b, i, 0)),
                   pl.BlockSpec((1, tm, kw), lambda b, i: (b, i, 0))],
        out_shape=[jax.ShapeDtypeStruct((bsz, s, qw), F32), jax.ShapeDtypeStruct((bsz, s, kw), F32)],
        compiler_params=_params(("arbitrary", "arbitrary")),
        name="rope",
    )(pl_all, pl_all, jnp.asarray(cos), jnp.asarray(sin), jnp.asarray(swap, BF16))


WIN_STEP = 4


def _win_kernel(sink_ref, q_ref, kp_ref, kc_ref, kn_ref, vp_ref, vc_ref, vn_ref, kx_ref, vx_ref, o_ref, *, seq):
    g = pl.program_id(1)
    j = pl.program_id(2)
    grp = q_ref.shape[1]
    wb = WIN_BLOCK
    m_rows = grp * wb
    k_ext = jnp.concatenate([kp_ref[0, 0], kc_ref[0, 0], kn_ref[0, 0]], axis=0).astype(BF16)
    v_ext = _with_ones(jnp.concatenate([vp_ref[0, 0], vc_ref[0, 0], vn_ref[0, 0]], axis=0).astype(BF16))
    k_ctx = kx_ref[0, 0].astype(BF16)
    v_ctx = _with_ones(vx_ref[0, 0].astype(BF16))
    sink = jnp.concatenate([jnp.full((wb, 1), sink_ref[g * grp + h], F32) for h in range(grp)], axis=0)
    row = lax.broadcasted_iota(jnp.int32, (m_rows, 3 * wb), 0) & (wb - 1)
    col = lax.broadcasted_iota(jnp.int32, (m_rows, 3 * wb), 1)
    in_band = jnp.where(jnp.abs(col - wb - row) <= WINDOW, 1.0, 0.0)
    col1 = lax.broadcasted_iota(jnp.int32, (1, 3 * wb), 1)
    for t in range(WIN_STEP):
        n = j * WIN_STEP + t
        kpos = (n - 1) * wb + col1
        in_seq = jnp.where((kpos >= 0) & (kpos < seq), 1.0, 0.0)
        q = jnp.concatenate([q_ref[0, h, t * wb:(t + 1) * wb, :] for h in range(grp)], axis=0)
        q = (q * (HEAD_DIM ** -0.5)).astype(BF16)
        s_loc = jnp.where(in_band * in_seq > 0.5, _dot_nt(q, k_ext[t * wb:(t + 3) * wb, :]), NEG)
        s_ctx = _dot_nt(q, k_ctx)
        m = jnp.maximum(jnp.max(jnp.concatenate([s_loc, s_ctx], axis=1), axis=-1, keepdims=True), sink)
        p_loc = jnp.exp(s_loc - m).astype(BF16)
        p_ctx = jnp.exp(s_ctx - m).astype(BF16)
        ov = _dot(p_loc, v_ext[t * wb:(t + 3) * wb, :]) + _dot(p_ctx, v_ctx)
        den = ov[:, HEAD_DIM:HEAD_DIM + 1] + jnp.exp(sink - m)
        o = (ov[:, :HEAD_DIM] / den).astype(o_ref.dtype)
        for h in range(grp):
            o_ref[0, h, t * wb:(t + 1) * wb, :] = o[h * wb:(h + 1) * wb, :]


def _win_call(q, k, v, kx, vx, sink):
    bsz, hq, s, dh = q.shape
    hkv = k.shape[1]
    grp = hq // hkv
    cx = kx.shape[2]
    nb = s // WIN_BLOCK
    step = WIN_STEP * WIN_BLOCK
    blk = (1, 1, WIN_BLOCK, dh)
    prev = lambda b, g, j, sk: (b, g, jnp.maximum(j * WIN_STEP - 1, 0), 0)
    cur = lambda b, g, j, sk: (b, g, j, 0)
    nxt = lambda b, g, j, sk: (b, g, jnp.minimum((j + 1) * WIN_STEP, nb - 1), 0)
    ctx = lambda b, g, j, sk: (b, g, 0, 0)
    return pl.pallas_call(
        functools.partial(_win_kernel, seq=s),
        grid_spec=pltpu.PrefetchScalarGridSpec(
            num_scalar_prefetch=1,
            grid=(bsz, hkv, nb // WIN_STEP),
            in_specs=[pl.BlockSpec((1, grp, step, dh), cur),
                      pl.BlockSpec(blk, prev), pl.BlockSpec((1, 1, step, dh), cur), pl.BlockSpec(blk, nxt),
                      pl.BlockSpec(blk, prev), pl.BlockSpec((1, 1, step, dh), cur), pl.BlockSpec(blk, nxt),
                      pl.BlockSpec((1, 1, cx, dh), ctx), pl.BlockSpec((1, 1, cx, dh), ctx)],
            out_specs=pl.BlockSpec((1, grp, step, dh), cur)),
        out_shape=jax.ShapeDtypeStruct((bsz, hq, s, dh), BF16),
        compiler_params=_params(("arbitrary", "arbitrary", "arbitrary")),
        name="mixer_window_attn",
    )(sink, q, k, k, k, v, v, v, kx, vx)


def _na_bias_table(rpb):
    qc = np.arange(GRID_W)
    kc = np.arange(GRID_W)
    cs = np.clip(qc - NA_COLS // 2, 0, GRID_W - NA_COLS)
    cmask = (kc[None, :] >= cs[:, None]) & (kc[None, :] < cs[:, None] + NA_COLS)
    ci = np.clip(kc[None, :] - qc[:, None] + NA_COLS - 1, 0, 2 * NA_COLS - 2)
    pick = (ci[None] == np.arange(2 * NA_COLS - 1)[:, None, None]).astype(np.float32)
    t15 = jnp.einsum('hrc,cqk->hrqk', rpb, jnp.asarray(pick), precision=lax.Precision.HIGHEST)
    t15 = jnp.where(cmask[None, None], t15, NEG)
    tab = jnp.stack([t15[:, d0:d0 + NA_ROWS] for d0 in range(NA_ROWS)], axis=1)
    return tab.transpose(0, 1, 3, 2, 4).reshape(rpb.shape[0], NA_ROWS, GRID_W, NA_ROWS * GRID_W)


def _na_kernel(q_ref, k_ref, v_ref, kx_ref, vx_ref, bias_ref, o_ref):
    rows = q_ref.shape[2] // GRID_W
    k_ctx = kx_ref[0, 0].astype(BF16)
    v_ctx = _with_ones(vx_ref[0, 0].astype(BF16))
    k_all = k_ref[0, 0].astype(BF16)
    v_all = _with_ones(v_ref[0, 0].astype(BF16))
    span = NA_ROWS * GRID_W
    chunk = 8
    for r0 in range(0, rows, chunk):
        q = (q_ref[0, 0, r0 * GRID_W:(r0 + chunk) * GRID_W, :] * (HEAD_DIM ** -0.5)).astype(BF16)
        s_parts = []
        for r in range(r0, r0 + chunk):
            rs = min(max(r - NA_ROWS // 2, 0), rows - NA_ROWS)
            k_loc = k_all[rs * GRID_W:rs * GRID_W + span, :]
            qr = q[(r - r0) * GRID_W:(r - r0 + 1) * GRID_W, :]
            s_parts.append(_dot_nt(qr, k_loc) + bias_ref[0, rs - r + NA_ROWS - 1])
        s_loc = jnp.concatenate(s_parts, axis=0)
        s_ctx = _dot_nt(q, k_ctx)
        m = jnp.max(jnp.concatenate([s_loc, s_ctx], axis=1), axis=-1, keepdims=True)
        p_loc = jnp.exp(s_loc - m).astype(BF16)
        p_ctx = jnp.exp(s_ctx - m).astype(BF16)
        o_parts = []
        for r in range(r0, r0 + chunk):
            rs = min(max(r - NA_ROWS // 2, 0), rows - NA_ROWS)
            v_loc = v_all[rs * GRID_W:rs * GRID_W + span, :]
            o_parts.append(_dot(p_loc[(r - r0) * GRID_W:(r - r0 + 1) * GRID_W, :], v_loc))
        ov = jnp.concatenate(o_parts, axis=0) + _dot(p_ctx, v_ctx)
        o = ov[:, :HEAD_DIM] / ov[:, HEAD_DIM:HEAD_DIM + 1]
        o_ref[0, 0, r0 * GRID_W:(r0 + chunk) * GRID_W, :] = o.astype(o_ref.dtype)


def _na_call(q, k, v, kx, vx, bias_tab):
    bsz, h, s, dh = q.shape
    cx = kx.shape[2]
    full = pl.BlockSpec((1, 1, s, dh), lambda b, hh: (b, hh, 0, 0))
    ctx = pl.BlockSpec((1, 1, cx, dh), lambda b, hh: (b, hh, 0, 0))
    return pl.pallas_call(
        _na_kernel,
        grid=(bsz, h),
        in_specs=[full, full, full, ctx, ctx,
                  pl.BlockSpec((1, NA_ROWS, GRID_W, NA_ROWS * GRID_W), lambda b, hh: (hh, 0, 0, 0))],
        out_specs=full,
        out_shape=jax.ShapeDtypeStruct((bsz, h, s, dh), BF16),
        compiler_params=_params(("arbitrary", "arbitrary")),
        name="mixer_neighbourhood_attn",
    )(q, k, v, kx, vx, bias_tab)


def _ctx_attn_kernel(sink_ref, q_ref, k_ref, v_ref, o_ref, *, use_sink):
    h = pl.program_id(1)
    q = (q_ref[0, 0] * (HEAD_DIM ** -0.5)).astype(BF16)
    s = _dot_nt(q, k_ref[0, 0].astype(BF16))
    m = jnp.max(s, axis=-1, keepdims=True)
    if use_sink:
        sink = sink_ref[h]
        m = jnp.maximum(m, sink)
    p = jnp.exp(s - m)
    den = jnp.sum(p, axis=-1, keepdims=True)
    if use_sink:
        den = den + jnp.exp(sink - m)
    o_ref[0, 0] = (_dot(p.astype(BF16), v_ref[0, 0].astype(BF16)) / den).astype(o_ref.dtype)


def _ctx_attn_call(q, k, v, sink):
    bsz, hq, cx, dh = q.shape
    grp = hq // k.shape[1]
    use_sink = sink is not None
    if sink is None:
        sink = jnp.zeros((hq,), F32)
    qs = pl.BlockSpec((1, 1, cx, dh), lambda b, h, sk: (b, h, 0, 0))
    ks = pl.BlockSpec((1, 1, cx, dh), lambda b, h, sk: (b, h // grp, 0, 0))
    return pl.pallas_call(
        functools.partial(_ctx_attn_kernel, use_sink=use_sink),
        grid_spec=pltpu.PrefetchScalarGridSpec(
            num_scalar_prefetch=1, grid=(bsz, hq), in_specs=[qs, ks, ks], out_specs=qs),
        out_shape=jax.ShapeDtypeStruct((bsz, hq, cx, dh), BF16),
        compiler_params=_params(("arbitrary", "arbitrary")),
        name="context_attn",
    )(sink, q, k, v)


def _merge_kernel(x_ref, ya_ref, yb_ref, yc_ref, yd_ref, wg0_ref, wg1_ref, wg2_ref, wg3_ref, wbr_ref, wout_ref,
                  ng_ref, sh_ref, sc_ref, gt_ref, o_ref, h_ref, acc_ref):
    n = pl.program_id(2)
    nt = acc_ref.shape[0]
    d = o_ref.shape[2]

    @pl.when(n == 0)
    def _():
        h = _rms(x_ref[0], ng_ref[0:1, :]) * (1.0 + sc_ref[0]) + sh_ref[0]
        h_ref[...] = h.astype(BF16)

    h = h_ref[...]
    acc = None
    for i, (y_ref, wg_ref) in enumerate(((ya_ref, wg0_ref), (yb_ref, wg1_ref), (yc_ref, wg2_ref), (yd_ref, wg3_ref))):
        t = _sigmoid(_dot(h, wg_ref[...])) * _dot(y_ref[0], wbr_ref[i])
        acc = t if acc is None else acc + t
    acc_ref[n] = acc.astype(BF16)

    @pl.when(n == nt - 1)
    def _():
        a = jnp.concatenate([acc_ref[j] for j in range(nt)], axis=1)
        oc = d // 4
        for c in range(4):
            o_ref[0, :, c * oc:(c + 1) * oc] = _dot(a, wout_ref[:, c * oc:(c + 1) * oc])
        o_ref[0] = x_ref[0] + gt_ref[0] * _rms(o_ref[0], ng_ref[1:2, :])


def _merge_call(x, ys, w_gate, w_branch, w_out, norm_g, mods, shared_row):
    bsz, s, d = x.shape
    c = BRANCH_DIM
    tm = min(s, 512)
    tn = 256
    nt = d // tn
    row = (lambda b: b) if shared_row is None else (lambda b: shared_row)
    ysp = pl.BlockSpec((1, tm, c), lambda b, i, n: (b, i, 0))
    wg = [pl.BlockSpec((d, tn), functools.partial(lambda b, i, n, k: (0, k * nt + n), k=k)) for k in range(4)]
    mod = lambda col: pl.BlockSpec((1, 1, d), lambda b, i, n: (row(b), 0, col))
    return pl.pallas_call(
        _merge_kernel,
        grid=(bsz, s // tm, nt),
        in_specs=[pl.BlockSpec((1, tm, d), lambda b, i, n: (b, i, 0)), ysp, ysp, ysp, ysp, *wg,
                  pl.BlockSpec((4, c, tn), lambda b, i, n: (0, 0, n)),
                  pl.BlockSpec((d, d), lambda b, i, n: (0, 0), pipeline_mode=pl.Buffered(1)),
                  pl.BlockSpec((4, d), lambda b, i, n: (0, 0)),
                  mod(0), mod(1), mod(2)],
        out_specs=pl.BlockSpec((1, tm, d), lambda b, i, n: (b, i, 0)),
        out_shape=jax.ShapeDtypeStruct((bsz, s, d), F32),
        scratch_shapes=[pltpu.VMEM((tm, d), BF16), pltpu.VMEM((nt, tm, tn), BF16)],
        compiler_params=_params(("arbitrary", "arbitrary", "arbitrary")),
        name="mixer_merge",
    )(x, *ys, w_gate, w_gate, w_gate, w_gate, w_branch, w_out, norm_g, mods, mods, mods)


def _expert_row_ids():
    r = np.arange(N_EXPERTS)
    return ((r % N_GROUPS) * GROUP_SIZE + r // N_GROUPS).astype(np.int32)


def _router_kernel(rows_ref, x_ref, ng_ref, sh_ref, sc_ref, rw_ref, rb_ref, eid_ref, tri_e_ref, tri_t_ref,
                   sw1_ref, sw3_ref, sw2_ref, h_ref, so_ref, e_ref, r_ref, w_ref, cnt_ref):
    i = pl.program_id(0)
    tm = x_ref.shape[0]

    @pl.when(i == 0)
    def _():
        cnt_ref[...] = jnp.zeros(cnt_ref.shape, F32)

    h = _rms(x_ref[...], ng_ref[2:3, :]) * (1.0 + sc_ref[0]) + sh_ref[0]
    h_ref[:, 0, :] = _pack_halves(h)
    hb = h.astype(BF16)
    so_ref[...] = _dot((_silu(_dot(hb, sw1_ref[...])) * _dot(hb, sw3_ref[...])).astype(BF16), sw2_ref[...])

    h_tail = (h - hb.astype(F32)).astype(BF16)
    head = _dot_nt(rw_ref[...], hb)
    logits = head[:N_EXPERTS] + head[N_EXPERTS:] + _dot_nt(rw_ref[:N_EXPERTS, :], h_tail)
    scores = _sigmoid(logits)
    biased = scores + rb_ref[...]
    m1 = jnp.full((N_GROUPS, tm), -jnp.inf, F32)
    m2 = m1
    for j in range(GROUP_SIZE):
        v = biased[j * N_GROUPS:(j + 1) * N_GROUPS, :]
        m2 = jnp.maximum(m2, jnp.minimum(m1, v))
        m1 = jnp.maximum(m1, v)
    gs = m1 + m2
    gid = lax.broadcasted_iota(jnp.int32, (N_GROUPS, tm), 0)
    beat = jnp.zeros((N_GROUPS, tm), jnp.int32)
    for g in range(N_GROUPS):
        o = gs[g:g + 1, :]
        beat = beat + jnp.where((o > gs) | ((o == gs) & (g < gid)), 1, 0)
    keep = beat < TOPK_GROUPS
    masked = jnp.concatenate(
        [jnp.where(keep, biased[j * N_GROUPS:(j + 1) * N_GROUPS, :], NEG) for j in range(GROUP_SIZE)], axis=0)
    eid = eid_ref[...]
    row_ids = _expert_row_ids()
    beat = jnp.zeros((N_EXPERTS, tm), jnp.int32)
    for r in range(N_EXPERTS):
        o = masked[r:r + 1, :]
        beat = beat + jnp.where((o > masked) | ((o == masked) & (int(row_ids[r]) < eid)), 1, 0)
    sel = beat < TOP_K
    self32 = jnp.where(sel, 1.0, 0.0)
    selb = self32.astype(BF16)
    wsel = jnp.where(sel, scores, 0.0)
    wsel = wsel / jnp.sum(wsel, axis=0, keepdims=True) * ROUTE_SCALE
    rank = cnt_ref[:, 0:1] + _dot(selb, tri_t_ref[...])
    cnt_ref[...] = cnt_ref[...] + jnp.sum(self32, axis=1, keepdims=True)
    slot = _dot(tri_e_ref[...], selb)
    eidf = eid.astype(F32)
    for k in range(TOP_K):
        mk = sel & (slot == float(k))
        e_ref[k:k + 1, :] = jnp.sum(jnp.where(mk, eidf, 0.0), axis=0, keepdims=True).astype(jnp.int32)
        r_ref[k:k + 1, :] = jnp.sum(jnp.where(mk, rank, 0.0), axis=0, keepdims=True).astype(jnp.int32)
        w_ref[k:k + 1, :] = jnp.sum(jnp.where(mk, wsel, 0.0), axis=0, keepdims=True)


def _router_call(xcat, tile_rows, norm_g, mods, router_w, router_b, sw1, sw3, sw2):
    t, d = xcat.shape
    tm = 256
    f = sw1.shape[1]
    ids = _expert_row_ids()
    rw = router_w.T[ids]
    rw_head = rw.astype(BF16)
    rw = jnp.concatenate([rw_head, (rw - rw_head.astype(F32)).astype(BF16)], axis=0)
    rb = router_b[ids].reshape(N_EXPERTS, 1)
    tri_e = jnp.asarray(np.tril(np.ones((N_EXPERTS, N_EXPERTS), np.float32), -1), BF16)
    tri_t = jnp.asarray(np.triu(np.ones((tm, tm), np.float32), 1), BF16)
    const = lambda shape: pl.BlockSpec(shape, lambda i, rows: (0,) * len(shape))
    tok = pl.BlockSpec((tm, d), lambda i, rows: (i, 0))
    slots = pl.BlockSpec((TOP_K, tm), lambda i, rows: (0, i))
    return pl.pallas_call(
        _router_kernel,
        grid_spec=pltpu.PrefetchScalarGridSpec(
            num_scalar_prefetch=1,
            grid=(t // tm,),
            in_specs=[tok, const((4, d)),
                      pl.BlockSpec((1, 1, d), lambda i, rows: (rows[i], 0, 3)),
                      pl.BlockSpec((1, 1, d), lambda i, rows: (rows[i], 0, 4)),
                      const((2 * N_EXPERTS, d)), const((N_EXPERTS, 1)), const((N_EXPERTS, 1)),
                      const((N_EXPERTS, N_EXPERTS)), const((tm, tm)),
                      const((d, f)), const((d, f)), const((f, d))],
            out_specs=[pl.BlockSpec((tm, 1, d // 2), lambda i, rows: (i, 0, 0)), tok, slots, slots, slots,
                       const((N_EXPERTS, 128))]),
        out_shape=[jax.ShapeDtypeStruct((t, 1, d // 2), jnp.uint32), jax.ShapeDtypeStruct((t, d), F32),
                   jax.ShapeDtypeStruct((TOP_K, t), jnp.int32), jax.ShapeDtypeStruct((TOP_K, t), jnp.int32),
                   jax.ShapeDtypeStruct((TOP_K, t), F32), jax.ShapeDtypeStruct((N_EXPERTS, 128), F32)],
        compiler_params=_params(("arbitrary",)),
        name="moe_router",
    )(tile_rows, xcat, norm_g, mods, mods, rw, rb, jnp.asarray(ids).reshape(N_EXPERTS, 1), tri_e, tri_t,
      sw1, sw3, sw2)


def _gmm_kernel(blk_ref, exp_ref, lo_ref, hi_ref, first_ref, newexp_ref, tok_cur_ref, tok_nxt_ref, h_hbm,
                w1_ref, w3_ref, w2_ref, o_ref, xa_ref, xb_ref, w1b_ref, w3b_ref, w2b_ref, sem):
    i = pl.program_id(0)
    last = pl.num_programs(0) - 1
    nrows = xa_ref.shape[0]
    xbufs = (xa_ref, xb_ref)

    def start_gather(tok_ref, s):
        for r in range(nrows):
            pltpu.make_async_copy(h_hbm.at[tok_ref[0, 0, r]], xbufs[s].at[pl.ds(r, 1)],
                                  sem.at[s]).start(priority=r % 2)

    def wait_gather(s):
        for r in range(nrows):
            pltpu.make_async_copy(h_hbm.at[0], xbufs[s].at[pl.ds(r, 1)], sem.at[s]).wait()

    @pl.when(i == 0)
    def _():
        start_gather(tok_cur_ref, 0)

    @pl.when(first_ref[i] == 1)
    def _():
        o_ref[...] = jnp.zeros(o_ref.shape, o_ref.dtype)

    @pl.when(newexp_ref[i] == 1)
    def _():
        w1b_ref[...] = w1_ref[0].astype(BF16)
        w3b_ref[...] = w3_ref[0].astype(BF16)
        w2b_ref[...] = w2_ref[0].astype(BF16)

    def step(s):
        wait_gather(s)

        @pl.when(lo_ref[i] >= 0)
        def _():
            start_gather(tok_nxt_ref, 1 - s)

        x_lo, x_hi = _unpack_halves(xbufs[s][...])
        x = jnp.concatenate([x_lo.astype(BF16), x_hi.astype(BF16)], axis=1)
        a = (_silu(_dot(x, w1b_ref[...])) * _dot(x, w3b_ref[...])).astype(BF16)
        y = _pack_halves(_dot(a, w2b_ref[...]))
        rows = lax.broadcasted_iota(jnp.int32, (nrows, 1), 0)
        keep = (rows >= lo_ref[i]) & (rows < hi_ref[i])
        o_ref[:, 0, :] = jnp.where(keep, y, o_ref[:, 0, :])

        @pl.when(i == last)
        def _():
            wait_gather(1 - s)

    for s in range(2):
        pl.when(i % 2 == s)(functools.partial(step, s))


def _gmm_call(items, slot_tok, hp, w1, w3, w2, layer):
    a = slot_tok.shape[0]
    d, f = w1.shape[2], w1.shape[3]
    n_items = items[0].shape[0]
    nb = a // MOE_BLOCK
    tok3 = slot_tok.reshape(nb, 1, MOE_BLOCK)
    rows = pl.BlockSpec((MOE_BLOCK,) + hp.shape[1:], lambda i, blk, ex, *_: (blk[i], 0, 0))
    toks = lambda step: pl.BlockSpec((1, 1, MOE_BLOCK), lambda i, blk, ex, *_: (blk[step(i)], 0, 0),
                                     memory_space=pltpu.SMEM)
    return pl.pallas_call(
        _gmm_kernel,
        grid_spec=pltpu.PrefetchScalarGridSpec(
            num_scalar_prefetch=len(items),
            grid=(n_items,),
            in_specs=[toks(lambda i: i), toks(lambda i: jnp.minimum(i + 1, n_items - 1)),
                      pl.BlockSpec(memory_space=pl.ANY),
                      pl.BlockSpec((None, 1, d, f), lambda i, blk, ex, *_: (layer, ex[i], 0, 0)),
                      pl.BlockSpec((None, 1, d, f), lambda i, blk, ex, *_: (layer, ex[i], 0, 0)),
                      pl.BlockSpec((None, 1, f, d), lambda i, blk, ex, *_: (layer, ex[i], 0, 0))],
            out_specs=rows,
            scratch_shapes=[pltpu.VMEM((MOE_BLOCK, hp.shape[2]), jnp.uint32),
                            pltpu.VMEM((MOE_BLOCK, hp.shape[2]), jnp.uint32),
                            pltpu.VMEM((d, f), BF16), pltpu.VMEM((d, f), BF16), pltpu.VMEM((f, d), BF16),
                            pltpu.SemaphoreType.DMA((2,))]),
        out_shape=jax.ShapeDtypeStruct((a,) + hp.shape[1:], jnp.uint32),
        compiler_params=_params(("arbitrary",)),
        name="moe_experts",
    )(*items, tok3, tok3, hp, w1, w3, w2)


def _moe_items(counts, n_rows):
    nb = n_rows // MOE_BLOCK
    n_items = nb + N_EXPERTS
    ends = jnp.cumsum(counts).astype(jnp.int32)
    bstart = jnp.arange(nb, dtype=jnp.int32) * MOE_BLOCK
    pos_b = jnp.arange(nb, dtype=jnp.int32) + jnp.sum(ends[None, :] < bstart[:, None], axis=1)
    idx = jnp.arange(n_items, dtype=jnp.int32)
    n_b = jnp.sum(pos_b[None, :] <= idx[:, None], axis=1).astype(jnp.int32)
    n_e = idx + 1 - n_b
    ends0 = jnp.concatenate([jnp.zeros((1,), jnp.int32), ends])
    last_end = jnp.sum(jnp.where(n_e[:, None] == jnp.arange(N_EXPERTS + 1)[None, :], ends0[None, :], 0), axis=1)
    start = jnp.maximum((n_b - 1) * MOE_BLOCK, last_end).astype(jnp.int32)
    stop = jnp.concatenate([start[1:], jnp.full((1,), n_rows, jnp.int32)])
    blk = jnp.minimum(start // MOE_BLOCK, nb - 1)
    ex = jnp.minimum(jnp.sum(ends[None, :] <= start[:, None], axis=1), N_EXPERTS - 1).astype(jnp.int32)
    lo = start - blk * MOE_BLOCK
    hi = stop - blk * MOE_BLOCK
    one = jnp.ones((1,), jnp.int32)
    first = jnp.concatenate([one, (blk[1:] != blk[:-1]).astype(jnp.int32)])
    newexp = jnp.concatenate([one, (ex[1:] != ex[:-1]).astype(jnp.int32)])
    return blk.astype(jnp.int32), ex, lo.astype(jnp.int32), hi.astype(jnp.int32), first, newexp


def _combine_kernel(rows_ref, dest_ref, dest_nxt_ref, ys_hbm, w_ref, so_ref, x_ref, ng_ref, gt_ref, o_ref,
                    buf_ref, sem):
    i = pl.program_id(0)
    tc = x_ref.shape[0]
    slot = i % 2

    def gather(idx_ref, s):
        def body(j, c):
            for k in range(TOP_K):
                pltpu.make_async_copy(ys_hbm.at[idx_ref[0, 0, k * tc + j]], buf_ref.at[s, k, pl.ds(j, 1)],
                                      sem.at[s]).start(priority=k % 2)
            return c
        lax.fori_loop(0, tc, body, 0, unroll=4)

    @pl.when(i == 0)
    def _():
        gather(dest_ref, 0)

    @pl.when(i < pl.num_programs(0) - 1)
    def _():
        gather(dest_nxt_ref, 1 - slot)

    pltpu.make_async_copy(buf_ref.at[slot], buf_ref.at[slot], sem.at[slot]).wait()
    half = x_ref.shape[1] // 2
    acc_lo = so_ref[:, :half]
    acc_hi = so_ref[:, half:]
    for k in range(TOP_K):
        y_lo, y_hi = _unpack_halves(buf_ref[slot, k])
        wk = w_ref[:, k:k + 1]
        acc_lo = acc_lo + y_lo * wk
        acc_hi = acc_hi + y_hi * wk
    fl = jnp.concatenate([acc_lo, acc_hi], axis=1)
    o_ref[...] = x_ref[...] + gt_ref[0] * _rms(fl, ng_ref[3:4, :])


def _combine_call(tile_rows, dest, ys, wsel, shared_out, xcat, norm_g, mods):
    t, d = xcat.shape
    tc = 128
    dest3 = dest.reshape(TOP_K, t // tc, tc).transpose(1, 0, 2).reshape(t // tc, 1, TOP_K * tc)
    tok = pl.BlockSpec((tc, d), lambda i, rows: (i, 0))
    nt = t // tc
    return pl.pallas_call(
        _combine_kernel,
        grid_spec=pltpu.PrefetchScalarGridSpec(
            num_scalar_prefetch=1,
            grid=(nt,),
            in_specs=[pl.BlockSpec((1, 1, TOP_K * tc), lambda i, rows: (i, 0, 0), memory_space=pltpu.SMEM),
                      pl.BlockSpec((1, 1, TOP_K * tc), lambda i, rows: (jnp.minimum(i + 1, nt - 1), 0, 0),
                                   memory_space=pltpu.SMEM),
                      pl.BlockSpec(memory_space=pl.ANY),
                      pl.BlockSpec((tc, TOP_K), lambda i, rows: (i, 0)),
                      tok, tok,
                      pl.BlockSpec((4, d), lambda i, rows: (0, 0)),
                      pl.BlockSpec((1, 1, d), lambda i, rows: (rows[i], 0, 5))],
            out_specs=tok,
            scratch_shapes=[pltpu.VMEM((2, TOP_K, tc, d // 2), jnp.uint32), pltpu.SemaphoreType.DMA((2,))]),
        out_shape=jax.ShapeDtypeStruct((t, d), F32),
        compiler_params=_params(("arbitrary",)),
        name="moe_combine",
    )(tile_rows, dest3, dest3, ys, wsel.T, shared_out, xcat, norm_g, mods)


def _channel_mixer(xcat, mod_rows, norm_g, mods, router_w, router_b, w1, w3, w2, layer, sw1, sw3, sw2):
    t, _ = xcat.shape
    h, shared_out, eidx, rank, wsel, cnt = _router_call(xcat, mod_rows[::2], norm_g, mods, router_w, router_b,
                                                         sw1, sw3, sw2)
    counts = jnp.zeros((N_EXPERTS,), jnp.int32).at[_expert_row_ids()].set(cnt[:, 0].astype(jnp.int32))
    starts = jnp.cumsum(counts) - counts
    sel = eidx[:, :, None] == jnp.arange(N_EXPERTS, dtype=jnp.int32)[None, None, :]
    dest = jnp.sum(jnp.where(sel, starts[None, None, :], 0), axis=-1) + rank
    slot_tok = (jnp.argsort(dest.reshape(-1)) % t).astype(jnp.int32)
    ys = _gmm_call(_moe_items(counts, t * TOP_K), slot_tok, h, w1, w3, w2, layer)
    return _combine_call(mod_rows, dest, ys, wsel, shared_out, xcat, norm_g, mods)


def _heads(t, n):
    b, s, _ = t.shape
    return t.reshape(b, s, n, HEAD_DIM).transpose(0, 2, 1, 3)


def _unheads(t):
    b, n, s, dh = t.shape
    return t.transpose(0, 2, 1, 3).reshape(b, s, n * dh)


def kernel(x, c, ctx, c_ctx, w_ada, b_ada, norm_g, w_in, conv_b_in, conv_dw, conv_dw_b, conv_ln_g, conv_ln_b, gmlp_ln_g, gmlp_ln_b, gmlp_ws, gmlp_bs, win_sink, na_rpb, w_branch, w_out, router_w, router_b, exp_w1, exp_w3, exp_w2, sh_w1, sh_w3, sh_w2):
    bsz, s, d = x.shape
    cx = ctx.shape[1]
    depth = w_ada.shape[0]
    ctx_row = bsz
    c_all = jnp.zeros((MODS_ROWS, d), F32).at[:bsz].set(c).at[ctx_row].set(c_ctx)
    mods_all = _ada_call(c_all, w_ada, b_ada)
    lat_rows = jnp.repeat(jnp.arange(bsz, dtype=jnp.int32), s // 128)
    ctx_rows = jnp.full((bsz * cx // 128,), ctx_row, jnp.int32)
    xl, xc = x, ctx
    for l in range(depth):
        last = l == depth - 1
        mods = mods_all[l].reshape(MODS_ROWS, 1, N_MODS * d)
        g_off = w_in.shape[2] - 4 * d
        kv_off = g_off - KV_COLS
        w_small = jnp.concatenate([w_in[l, :, :kv_off], w_in[l, :, kv_off + 256:g_off],
                                   w_in[l, :, kv_off:kv_off + 256]], axis=1).astype(BF16)
        w_gate = w_in[l, :, g_off:].astype(BF16)
        w_br = w_branch[l].astype(BF16)
        w_o = w_out[l].astype(BF16)
        pl_l = _proj_call(xl, norm_g[l], mods, w_small, None, SMALL_COLS // 2)
        if last:
            pkv_c = _proj_call(xc, norm_g[l], mods, w_small[:, COL_DK:], ctx_row, KV_COLS)
            kv_base = 0
        else:
            pc = _proj_call(xc, norm_g[l], mods, w_small, ctx_row, SMALL_COLS // 2)
            pkv_c, kv_base = pc, COL_DK
        dkc = _heads(pkv_c[..., kv_base:kv_base + 512], NA_HEADS)
        dvc = _heads(pkv_c[..., kv_base + 512:kv_base + 1024], NA_HEADS)
        ckc = _heads(pkv_c[..., kv_base + 1024:kv_base + 1152], WIN_KV_HEADS)
        cvc = _heads(pkv_c[..., kv_base + 1152:kv_base + 1280], WIN_KV_HEADS)
        conv_args = (conv_b_in[l], conv_dw[l], conv_dw_b[l], conv_ln_g[l], conv_ln_b[l])
        gmlp_args = (gmlp_ln_g[l], gmlp_ln_b[l], gmlp_ws[l], gmlp_bs[l])
        ya = _conv_call(pl_l, *conv_args)
        yb = _gmlp_call(pl_l, *gmlp_args)
        q_rope, k_rope = _rope_call(pl_l)
        yc = _unheads(_win_call(_heads(q_rope, WIN_HEADS), _heads(k_rope, WIN_KV_HEADS),
                                _heads(pl_l[..., COL_CV:COL_CV + 128], WIN_KV_HEADS), ckc, cvc, win_sink[l]))
        yd = _unheads(_na_call(_heads(pl_l[..., COL_DQ:COL_DQ + 512], NA_HEADS),
                               _heads(pl_l[..., COL_DK:COL_DK + 512], NA_HEADS),
                               _heads(pl_l[..., COL_DV:COL_DV + 512], NA_HEADS), dkc, dvc,
                               _na_bias_table(na_rpb[l])))
        xl = _merge_call(xl, (ya, yb, yc, yd), w_gate, w_br, w_o, norm_g[l], mods, None)
        moe_w = (router_w[l], router_b[l], exp_w1, exp_w3, exp_w2, l,
                 sh_w1[l].astype(BF16), sh_w3[l].astype(BF16), sh_w2[l].astype(BF16))
        if last:
            xl = _channel_mixer(xl.reshape(bsz * s, d), lat_rows, norm_g[l], mods, *moe_w).reshape(bsz, s, d)
        else:
            yca = _conv_call(pc, *conv_args)
            ycb = _gmlp_call(pc, *gmlp_args)
            ycc = _unheads(_ctx_attn_call(_heads(pc[..., COL_CQ:COL_CQ + 512], WIN_HEADS), ckc, cvc, win_sink[l]))
            ycd = _unheads(_ctx_attn_call(_heads(pc[..., COL_DQ:COL_DQ + 512], NA_HEADS), dkc, dvc, None))
            xc = _merge_call(xc, (yca, ycb, ycc, ycd), w_gate, w_br, w_o, norm_g[l], mods, ctx_row)
            xcat = jnp.concatenate([xl.reshape(bsz * s, d), xc.reshape(bsz * cx, d)], axis=0)
            out = _channel_mixer(xcat, jnp.concatenate([lat_rows, ctx_rows]), norm_g[l], mods, *moe_w)
            xl = out[:bsz * s].reshape(bsz, s, d)
            xc = out[bsz * s:].reshape(bsz, cx, d)
    return xl
```

```python
import functools

import numpy as np
import jax
import jax.numpy as jnp
from jax import lax
from jax.experimental import pallas as pl
from jax.experimental.pallas import tpu as pltpu

F32 = jnp.float32
BF16 = jnp.bfloat16

GRID_W = 64
HEAD_DIM = 64
EPS = 1e-6
NEG = -1e30
BRANCH_DIM = 512
CONV_K = 31
CONV_HALO = 16
GMLP_CHUNK = 128
GMLP_GROUPS = 4
WIN_HEADS = 8
WIN_KV_HEADS = 2
WINDOW = 128
WIN_BLOCK = 128
ROPE_BASE = 10000.0
NA_HEADS = 8
NA_ROWS = 8
NA_COLS = 16
N_EXPERTS = 64
TOP_K = 8
N_GROUPS = 8
GROUP_SIZE = N_EXPERTS // N_GROUPS
TOPK_GROUPS = 4
ROUTE_SCALE = 2.5
MOE_BLOCK = 256
N_MODS = 6
MODS_ROWS = 16
VMEM_LIMIT = 52 * 1024 * 1024

COL_A, COL_B, COL_CQ, COL_DQ, COL_DK, COL_DV, COL_CK, COL_CV = 0, 1024, 2048, 2560, 3072, 3584, 4096, 4224
SMALL_COLS = 4352
KV_COLS = 1280


def _sigmoid(x):
    return 1.0 / (1.0 + jnp.exp(-x))


def _silu(x):
    return x * _sigmoid(x)


def _gelu_tanh(x):
    return 0.5 * x * (1.0 + jnp.tanh(np.sqrt(2.0 / np.pi).astype(np.float32) * (x + 0.044715 * (x * x * x))))


def _rms(x, g):
    return x * lax.rsqrt(jnp.mean(x * x, axis=-1, keepdims=True) + EPS) * g


def _ln(x, g, b):
    mu = jnp.mean(x, axis=-1, keepdims=True)
    xc = x - mu
    var = jnp.mean(xc * xc, axis=-1, keepdims=True)
    return xc * lax.rsqrt(var + EPS) * g + b


def _dot(a, b):
    return jnp.dot(a, b, preferred_element_type=F32)


def _dot_nt(a, b):
    return lax.dot_general(a, b, (((1,), (1,)), ((), ())), preferred_element_type=F32)


def _pack_halves(x):
    n = x.shape[1] // 2
    lo = lax.bitcast_convert_type(x[:, :n].astype(BF16).astype(F32), jnp.uint32)
    hi = lax.bitcast_convert_type(x[:, n:].astype(BF16).astype(F32), jnp.uint32)
    return (hi & jnp.uint32(0xFFFF0000)) | (lo >> 16)


def _unpack_halves(w):
    lo = lax.bitcast_convert_type(w << 16, F32)
    hi = lax.bitcast_convert_type(w & jnp.uint32(0xFFFF0000), F32)
    return lo, hi


def _with_ones(v):
    return jnp.concatenate([v, jnp.ones_like(v)], axis=1)


def _params(sem):
    return pltpu.CompilerParams(dimension_semantics=sem, vmem_limit_bytes=VMEM_LIMIT)


def _ada_kernel(c_ref, w_ref, b_ref, o_ref):
    s = _silu(c_ref[...])
    o_ref[0] = _dot(s.astype(BF16), w_ref[0].astype(BF16)) + b_ref[0]


def _ada_call(c_all, w_ada, b_ada):
    depth, d, n = w_ada.shape
    tn = 512
    return pl.pallas_call(
        _ada_kernel,
        grid=(depth, n // tn),
        in_specs=[pl.BlockSpec((MODS_ROWS, d), lambda l, j: (0, 0)),
                  pl.BlockSpec((1, d, tn), lambda l, j: (l, 0, j)),
                  pl.BlockSpec((1, 1, tn), lambda l, j: (l, 0, j))],
        out_specs=pl.BlockSpec((1, MODS_ROWS, tn), lambda l, j: (l, 0, j)),
        out_shape=jax.ShapeDtypeStruct((depth, MODS_ROWS, n), F32),
        compiler_params=_params(("arbitrary", "arbitrary")),
        name="ada_mods",
    )(c_all, w_ada, b_ada.reshape(depth, 1, n))


def _proj_kernel(x_ref, g_ref, sh_ref, sc_ref, w_ref, o_ref):
    h = _rms(x_ref[0], g_ref[0:1, :]) * (1.0 + sc_ref[0]) + sh_ref[0]
    o_ref[0] = _dot(h.astype(BF16), w_ref[...])


def _proj_call(x2, bsz, s, row_off, norm_g, mods, w, shared_row, tn):
    d = x2.shape[1]
    n = w.shape[1]
    tm = min(s, 512)
    x = x2.reshape(-1, tm, d)
    base, per = row_off // tm, s // tm
    row = (lambda b: b) if shared_row is None else (lambda b: shared_row)
    return pl.pallas_call(
        _proj_kernel,
        grid=(n // tn, bsz, s // tm),
        in_specs=[pl.BlockSpec((1, tm, d), lambda j, b, i: (base + b * per + i, 0, 0)),
                  pl.BlockSpec((4, d), lambda j, b, i: (0, 0)),
                  pl.BlockSpec((1, 1, d), lambda j, b, i: (row(b), 0, 0)),
                  pl.BlockSpec((1, 1, d), lambda j, b, i: (row(b), 0, 1)),
                  pl.BlockSpec((d, tn), lambda j, b, i: (0, j))],
        out_specs=pl.BlockSpec((1, tm, tn), lambda j, b, i: (b, i, j)),
        out_shape=jax.ShapeDtypeStruct((bsz, s, n), F32),
        compiler_params=_params(("arbitrary", "arbitrary", "arbitrary")),
        name="proj_in",
    )(x, norm_g, mods, mods, w)


def _conv_kernel(a_ref, ap_ref, an_ref, bin_ref, dw_ref, dwb_ref, g_ref, b_ref, o_ref, ypad_ref):
    i = pl.program_id(1)
    nblk = pl.num_programs(1)
    ts = a_ref.shape[1]
    c = BRANCH_DIM

    def glu(a):
        a = a + bin_ref[...]
        return a[:, :c] * _sigmoid(a[:, c:])

    ypad_ref[0:CONV_HALO, :] = jnp.where(i > 0, glu(ap_ref[0]), 0.0)
    ypad_ref[CONV_HALO:CONV_HALO + ts, :] = glu(a_ref[0])
    ypad_ref[CONV_HALO + ts:2 * CONV_HALO + ts, :] = jnp.where(i < nblk - 1, glu(an_ref[0]), 0.0)
    rc = 64
    off = CONV_HALO - CONV_K // 2
    sub = 8
    for r0 in range(0, ts, rc):
        acc = None
        for r in range(sub):
            z = None
            for base in range(0, off + CONV_K, sub):
                j = base + r - off
                if 0 <= j < CONV_K:
                    term = ypad_ref[r0 + base:r0 + base + rc + sub, :] * dw_ref[j:j + 1, :]
                    z = term if z is None else z + term
            if z is not None:
                z = z[r:r + rc, :]
                acc = z if acc is None else acc + z
        y = _ln(acc + dwb_ref[...], g_ref[...], b_ref[...])
        o_ref[0, r0:r0 + rc, :] = _silu(y).astype(o_ref.dtype)


def _conv_call(pl_all, b_in, dw, dw_b, ln_g, ln_b):
    bsz, s, _ = pl_all.shape
    ts = min(s, 512)
    c = BRANCH_DIM
    hb = ts // CONV_HALO
    last = s // CONV_HALO - 1
    dw_pad = jnp.zeros((32, c), F32).at[:CONV_K].set(dw)
    return pl.pallas_call(
        _conv_kernel,
        grid=(bsz, s // ts),
        in_specs=[pl.BlockSpec((1, ts, 2 * c), lambda b, i: (b, i, 0)),
                  pl.BlockSpec((1, CONV_HALO, 2 * c), lambda b, i: (b, jnp.maximum(i * hb - 1, 0), 0)),
                  pl.BlockSpec((1, CONV_HALO, 2 * c), lambda b, i: (b, jnp.minimum((i + 1) * hb, last), 0)),
                  pl.BlockSpec((1, 2 * c), lambda b, i: (0, 0)),
                  pl.BlockSpec((32, c), lambda b, i: (0, 0)),
                  pl.BlockSpec((1, c), lambda b, i: (0, 0)),
                  pl.BlockSpec((1, c), lambda b, i: (0, 0)),
                  pl.BlockSpec((1, c), lambda b, i: (0, 0))],
        out_specs=pl.BlockSpec((1, ts, c), lambda b, i: (b, i, 0)),
        out_shape=jax.ShapeDtypeStruct((bsz, s, c), BF16),
        scratch_shapes=[pltpu.VMEM((ts + 2 * CONV_HALO, c), F32)],
        compiler_params=_params(("arbitrary", "arbitrary")),
        name="mixer_conv",
    )(pl_all, pl_all, pl_all, b_in.reshape(1, 2 * c), dw_pad, dw_b.reshape(1, c), ln_g.reshape(1, c), ln_b.reshape(1, c))


def _gmlp_kernel(z_ref, g_ref, b_ref, ws_ref, bs_ref, o_ref):
    c = BRANCH_DIM
    gw = c // GMLP_GROUPS
    ts = z_ref.shape[1]
    for n in range(ts // GMLP_CHUNK):
        rows = slice(n * GMLP_CHUNK, (n + 1) * GMLP_CHUNK)
        z = _gelu_tanh(z_ref[0, rows, :])
        u = z[:, :c]
        v = _ln(z[:, c:], g_ref[...], b_ref[...]).astype(BF16)
        for g in range(GMLP_GROUPS):
            cols = slice(g * gw, (g + 1) * gw)
            sg = _dot(ws_ref[g], v[:, cols]) + bs_ref[:, g:g + 1]
            o_ref[0, rows, cols] = (u[:, cols] * sg).astype(o_ref.dtype)


def _gmlp_call(pl_all, ln_g, ln_b, ws, bs):
    bsz, s, _ = pl_all.shape
    ts = min(s, 512)
    c = BRANCH_DIM
    return pl.pallas_call(
        _gmlp_kernel,
        grid=(bsz, s // ts),
        in_specs=[pl.BlockSpec((1, ts, 2 * c), lambda b, i: (b, i, COL_B // (2 * c))),
                  pl.BlockSpec((1, c), lambda b, i: (0, 0)),
                  pl.BlockSpec((1, c), lambda b, i: (0, 0)),
                  pl.BlockSpec((GMLP_GROUPS, GMLP_CHUNK, GMLP_CHUNK), lambda b, i: (0, 0, 0)),
                  pl.BlockSpec((GMLP_CHUNK, GMLP_GROUPS), lambda b, i: (0, 0))],
        out_specs=pl.BlockSpec((1, ts, c), lambda b, i: (b, i, 0)),
        out_shape=jax.ShapeDtypeStruct((bsz, s, c), BF16),
        compiler_params=_params(("arbitrary", "arbitrary")),
        name="mixer_gmlp",
    )(pl_all, ln_g.reshape(1, c), ln_b.reshape(1, c), ws.astype(BF16), bs.T)


def _rope_tables(s):
    half, quarter = HEAD_DIM // 2, HEAD_DIM // 4
    pos = np.arange(s)
    prow = (pos // GRID_W).astype(np.float32)
    pcol = (pos % GRID_W).astype(np.float32)
    inv = (ROPE_BASE ** (-np.arange(quarter, dtype=np.float32) / quarter)).astype(np.float32)
    lane = np.arange(128)
    in_head = lane % HEAD_DIM
    p = np.where((in_head < half)[None, :], prow[:, None], pcol[:, None]).astype(np.float32)
    ang = (p * inv[lane % quarter][None, :]).astype(np.float32).astype(np.float64)
    first = (in_head % half) < quarter
    cos = np.cos(ang)
    sin = np.where(first[None, :], -np.sin(ang), np.sin(ang))
    swap = np.zeros((128, 128), np.float32)
    for j in lane:
        swap[j + quarter if first[j] else j - quarter, j] = 1.0
    return cos.astype(np.float32), sin.astype(np.float32), swap


def _rope_kernel(q_ref, k_ref, cos_ref, sin_ref, swap_ref, qo_ref, ko_ref):
    cos, sin, swap = cos_ref[...], sin_ref[...], swap_ref[...]

    def rot(x):
        parts = []
        for t in range(x.shape[1] // 128):
            xt = x[:, t * 128:(t + 1) * 128]
            parts.append(xt * cos + _dot(xt.astype(BF16), swap) * sin)
        return parts[0] if len(parts) == 1 else jnp.concatenate(parts, axis=1)

    qo_ref[0] = rot(q_ref[0])
    ko_ref[0] = rot(k_ref[0])


def _rope_call(pl_all):
    bsz, s, _ = pl_all.shape
    tm = 512
    cos, sin, swap = _rope_tables(s)
    qw, kw = WIN_HEADS * HEAD_DIM, WIN_KV_HEADS * HEAD_DIM
    return pl.pallas_call(
        _rope_kernel,
        grid=(bsz, s // tm),
        in_specs=[pl.BlockSpec((1, tm, qw), lambda b, i: (b, i, COL_CQ // qw)),
                  pl.BlockSpec((1, tm, kw), lambda b, i: (b, i, COL_CK // kw)),
                  pl.BlockSpec((tm, 128), lambda b, i: (i, 0)),
                  pl.BlockSpec((tm, 128), lambda b, i: (i, 0)),
                  pl.BlockSpec((128, 128), lambda b, i: (0, 0))],
        out_specs=[pl.BlockSpec((1, tm, qw), lambda b, i: (b, i, 0)),
                   pl.BlockSpec((1, tm, kw), lambda b, i: (b, i, 0))],
        out_shape=[jax.ShapeDtypeStruct((bsz, s, qw), F32), jax.ShapeDtypeStruct((bsz, s, kw), F32)],
        compiler_params=_params(("arbitrary", "arbitrary")),
        name="rope",
    )(pl_all, pl_all, jnp.asarray(cos), jnp.asarray(sin), jnp.asarray(swap, BF16))


WIN_STEP = 4


def _win_kernel(sink_ref, q_ref, kp_ref, kc_ref, kn_ref, vp_ref, vc_ref, vn_ref, kx_ref, vx_ref, o_ref, *, seq):
    g = pl.program_id(1)
    j = pl.program_id(2)
    grp = q_ref.shape[1]
    wb = WIN_BLOCK
    m_rows = grp * wb
    k_ext = jnp.concatenate([kp_ref[0, 0], kc_ref[0, 0], kn_ref[0, 0]], axis=0).astype(BF16)
    v_ext = _with_ones(jnp.concatenate([vp_ref[0, 0], vc_ref[0, 0], vn_ref[0, 0]], axis=0).astype(BF16))
    k_ctx = kx_ref[0, 0].astype(BF16)
    v_ctx = _with_ones(vx_ref[0, 0].astype(BF16))
    sink = jnp.concatenate([jnp.full((wb, 1), sink_ref[g * grp + h], F32) for h in range(grp)], axis=0)
    row = lax.broadcasted_iota(jnp.int32, (m_rows, 3 * wb), 0) & (wb - 1)
    col = lax.broadcasted_iota(jnp.int32, (m_rows, 3 * wb), 1)
    in_band = jnp.where(jnp.abs(col - wb - row) <= WINDOW, 1.0, 0.0)
    col1 = lax.broadcasted_iota(jnp.int32, (1, 3 * wb), 1)
    for t in range(WIN_STEP):
        n = j * WIN_STEP + t
        kpos = (n - 1) * wb + col1
        in_seq = jnp.where((kpos >= 0) & (kpos < seq), 1.0, 0.0)
        q = jnp.concatenate([q_ref[0, h, t * wb:(t + 1) * wb, :] for h in range(grp)], axis=0)
        q = (q * (HEAD_DIM ** -0.5)).astype(BF16)
        s_loc = jnp.where(in_band * in_seq > 0.5, _dot_nt(q, k_ext[t * wb:(t + 3) * wb, :]), NEG)
        s_ctx = _dot_nt(q, k_ctx)
        m = jnp.maximum(jnp.max(jnp.concatenate([s_loc, s_ctx], axis=1), axis=-1, keepdims=True), sink)
        p_loc = jnp.exp(s_loc - m).astype(BF16)
        p_ctx = jnp.exp(s_ctx - m).astype(BF16)
        ov = _dot(p_loc, v_ext[t * wb:(t + 3) * wb, :]) + _dot(p_ctx, v_ctx)
        den = ov[:, HEAD_DIM:HEAD_DIM + 1] + jnp.exp(sink - m)
        o = (ov[:, :HEAD_DIM] / den).astype(o_ref.dtype)
        for h in range(grp):
            o_ref[0, h, t * wb:(t + 1) * wb, :] = o[h * wb:(h + 1) * wb, :]


def _win_call(q, k, v, kx, vx, sink):
    bsz, hq, s, dh = q.shape
    hkv = k.shape[1]
    grp = hq // hkv
    cx = kx.shape[2]
    nb = s // WIN_BLOCK
    step = WIN_STEP * WIN_BLOCK
    blk = (1, 1, WIN_BLOCK, dh)
    prev = lambda b, g, j, sk: (b, g, jnp.maximum(j * WIN_STEP - 1, 0), 0)
    cur = lambda b, g, j, sk: (b, g, j, 0)
    nxt = lambda b, g, j, sk: (b, g, jnp.minimum((j + 1) * WIN_STEP, nb - 1), 0)
    ctx = lambda b, g, j, sk: (b, g, 0, 0)
    return pl.pallas_call(
        functools.partial(_win_kernel, seq=s),
        grid_spec=pltpu.PrefetchScalarGridSpec(
            num_scalar_prefetch=1,
            grid=(bsz, hkv, nb // WIN_STEP),
            in_specs=[pl.BlockSpec((1, grp, step, dh), cur),
                      pl.BlockSpec(blk, prev), pl.BlockSpec((1, 1, step, dh), cur), pl.BlockSpec(blk, nxt),
                      pl.BlockSpec(blk, prev), pl.BlockSpec((1, 1, step, dh), cur), pl.BlockSpec(blk, nxt),
                      pl.BlockSpec((1, 1, cx, dh), ctx), pl.BlockSpec((1, 1, cx, dh), ctx)],
            out_specs=pl.BlockSpec((1, grp, step, dh), cur)),
        out_shape=jax.ShapeDtypeStruct((bsz, hq, s, dh), BF16),
        compiler_params=_params(("arbitrary", "arbitrary", "arbitrary")),
        name="mixer_window_attn",
    )(sink, q, k, k, k, v, v, v, kx, vx)


def _na_bias_table(rpb):
    qc = np.arange(GRID_W)
    kc = np.arange(GRID_W)
    cs = np.clip(qc - NA_COLS // 2, 0, GRID_W - NA_COLS)
    cmask = (kc[None, :] >= cs[:, None]) & (kc[None, :] < cs[:, None] + NA_COLS)
    ci = np.clip(kc[None, :] - qc[:, None] + NA_COLS - 1, 0, 2 * NA_COLS - 2)
    pick = (ci[None] == np.arange(2 * NA_COLS - 1)[:, None, None]).astype(np.float32)
    t15 = jnp.einsum('hrc,cqk->hrqk', rpb, jnp.asarray(pick), precision=lax.Precision.HIGHEST)
    t15 = jnp.where(cmask[None, None], t15, NEG)
    tab = jnp.stack([t15[:, d0:d0 + NA_ROWS] for d0 in range(NA_ROWS)], axis=1)
    return tab.transpose(0, 1, 3, 2, 4).reshape(rpb.shape[0], NA_ROWS, GRID_W, NA_ROWS * GRID_W)


def _na_kernel(q_ref, k_ref, v_ref, kx_ref, vx_ref, bias_ref, o_ref):
    rows = q_ref.shape[2] // GRID_W
    k_ctx = kx_ref[0, 0].astype(BF16)
    v_ctx = _with_ones(vx_ref[0, 0].astype(BF16))
    k_all = k_ref[0, 0].astype(BF16)
    v_all = _with_ones(v_ref[0, 0].astype(BF16))
    span = NA_ROWS * GRID_W
    chunk = 8
    for r0 in range(0, rows, chunk):
        q = (q_ref[0, 0, r0 * GRID_W:(r0 + chunk) * GRID_W, :] * (HEAD_DIM ** -0.5)).astype(BF16)
        s_parts = []
        for r in range(r0, r0 + chunk):
            rs = min(max(r - NA_ROWS // 2, 0), rows - NA_ROWS)
            k_loc = k_all[rs * GRID_W:rs * GRID_W + span, :]
            qr = q[(r - r0) * GRID_W:(r - r0 + 1) * GRID_W, :]
            s_parts.append(_dot_nt(qr, k_loc) + bias_ref[0, rs - r + NA_ROWS - 1])
        s_loc = jnp.concatenate(s_parts, axis=0)
        s_ctx = _dot_nt(q, k_ctx)
        m = jnp.max(jnp.concatenate([s_loc, s_ctx], axis=1), axis=-1, keepdims=True)
        p_loc = jnp.exp(s_loc - m).astype(BF16)
        p_ctx = jnp.exp(s_ctx - m).astype(BF16)
        o_parts = []
        for r in range(r0, r0 + chunk):
            rs = min(max(r - NA_ROWS // 2, 0), rows - NA_ROWS)
            v_loc = v_all[rs * GRID_W:rs * GRID_W + span, :]
            o_parts.append(_dot(p_loc[(r - r0) * GRID_W:(r - r0 + 1) * GRID_W, :], v_loc))
        ov = jnp.concatenate(o_parts, axis=0) + _dot(p_ctx, v_ctx)
        o = ov[:, :HEAD_DIM] / ov[:, HEAD_DIM:HEAD_DIM + 1]
        o_ref[0, 0, r0 * GRID_W:(r0 + chunk) * GRID_W, :] = o.astype(o_ref.dtype)


def _na_call(q, k, v, kx, vx, bias_tab):
    bsz, h, s, dh = q.shape
    cx = kx.shape[2]
    full = pl.BlockSpec((1, 1, s, dh), lambda b, hh: (b, hh, 0, 0))
    ctx = pl.BlockSpec((1, 1, cx, dh), lambda b, hh: (b, hh, 0, 0))
    return pl.pallas_call(
        _na_kernel,
        grid=(bsz, h),
        in_specs=[full, full, full, ctx, ctx,
                  pl.BlockSpec((1, NA_ROWS, GRID_W, NA_ROWS * GRID_W), lambda b, hh: (hh, 0, 0, 0))],
        out_specs=full,
        out_shape=jax.ShapeDtypeStruct((bsz, h, s, dh), BF16),
        compiler_params=_params(("arbitrary", "arbitrary")),
        name="mixer_neighbourhood_attn",
    )(q, k, v, kx, vx, bias_tab)


def _ctx_attn_kernel(sink_ref, q_ref, k_ref, v_ref, o_ref, *, use_sink):
    h = pl.program_id(1)
    q = (q_ref[0, 0] * (HEAD_DIM ** -0.5)).astype(BF16)
    s = _dot_nt(q, k_ref[0, 0].astype(BF16))
    m = jnp.max(s, axis=-1, keepdims=True)
    if use_sink:
        sink = sink_ref[h]
        m = jnp.maximum(m, sink)
    p = jnp.exp(s - m)
    den = jnp.sum(p, axis=-1, keepdims=True)
    if use_sink:
        den = den + jnp.exp(sink - m)
    o_ref[0, 0] = (_dot(p.astype(BF16), v_ref[0, 0].astype(BF16)) / den).astype(o_ref.dtype)


def _ctx_attn_call(q, k, v, sink):
    bsz, hq, cx, dh = q.shape
    grp = hq // k.shape[1]
    use_sink = sink is not None
    if sink is None:
        sink = jnp.zeros((hq,), F32)
    qs = pl.BlockSpec((1, 1, cx, dh), lambda b, h, sk: (b, h, 0, 0))
    ks = pl.BlockSpec((1, 1, cx, dh), lambda b, h, sk: (b, h // grp, 0, 0))
    return pl.pallas_call(
        functools.partial(_ctx_attn_kernel, use_sink=use_sink),
        grid_spec=pltpu.PrefetchScalarGridSpec(
            num_scalar_prefetch=1, grid=(bsz, hq), in_specs=[qs, ks, ks], out_specs=qs),
        out_shape=jax.ShapeDtypeStruct((bsz, hq, cx, dh), BF16),
        compiler_params=_params(("arbitrary", "arbitrary")),
        name="context_attn",
    )(sink, q, k, v)


def _merge_kernel(x_ref, ya_ref, yb_ref, yc_ref, yd_ref, wg0_ref, wg1_ref, wg2_ref, wg3_ref, wbr_ref, wout_ref,
                  ng_ref, sh_ref, sc_ref, gt_ref, o_ref, h_ref, acc_ref):
    n = pl.program_id(2)
    nt = acc_ref.shape[0]
    d = o_ref.shape[2]

    @pl.when(n == 0)
    def _():
        h = _rms(x_ref[0], ng_ref[0:1, :]) * (1.0 + sc_ref[0]) + sh_ref[0]
        h_ref[...] = h.astype(BF16)

    h = h_ref[...]
    acc = None
    for i, (y_ref, wg_ref) in enumerate(((ya_ref, wg0_ref), (yb_ref, wg1_ref), (yc_ref, wg2_ref), (yd_ref, wg3_ref))):
        t = _sigmoid(_dot(h, wg_ref[...])) * _dot(y_ref[0], wbr_ref[i])
        acc = t if acc is None else acc + t
    acc_ref[n] = acc.astype(BF16)

    @pl.when(n == nt - 1)
    def _():
        a = jnp.concatenate([acc_ref[j] for j in range(nt)], axis=1)
        oc = d // 4
        for c in range(4):
            o_ref[0, :, c * oc:(c + 1) * oc] = _dot(a, wout_ref[:, c * oc:(c + 1) * oc])
        o_ref[0] = x_ref[0] + gt_ref[0] * _rms(o_ref[0], ng_ref[1:2, :])


def _merge_call(x2, bsz, s, row_off, ys, w_gate, w_branch, w_out, norm_g, mods, shared_row):
    d = x2.shape[1]
    c = BRANCH_DIM
    tm = min(s, 512)
    tn = 256
    nt = d // tn
    x = x2.reshape(-1, tm, d)
    base, per = row_off // tm, s // tm
    row = (lambda b: b) if shared_row is None else (lambda b: shared_row)
    ysp = pl.BlockSpec((1, tm, c), lambda b, i, n: (b, i, 0))
    wg = [pl.BlockSpec((d, tn), functools.partial(lambda b, i, n, k: (0, k * nt + n), k=k)) for k in range(4)]
    mod = lambda col: pl.BlockSpec((1, 1, d), lambda b, i, n: (row(b), 0, col))
    return pl.pallas_call(
        _merge_kernel,
        grid=(bsz, s // tm, nt),
        in_specs=[pl.BlockSpec((1, tm, d), lambda b, i, n: (base + b * per + i, 0, 0)), ysp, ysp, ysp, ysp, *wg,
                  pl.BlockSpec((4, c, tn), lambda b, i, n: (0, 0, n)),
                  pl.BlockSpec((d, d), lambda b, i, n: (0, 0), pipeline_mode=pl.Buffered(1)),
                  pl.BlockSpec((4, d), lambda b, i, n: (0, 0)),
                  mod(0), mod(1), mod(2)],
        out_specs=pl.BlockSpec((1, tm, d), lambda b, i, n: (b * per + i, 0, 0)),
        out_shape=jax.ShapeDtypeStruct((bsz * per, tm, d), F32),
        scratch_shapes=[pltpu.VMEM((tm, d), BF16), pltpu.VMEM((nt, tm, tn), BF16)],
        compiler_params=_params(("arbitrary", "arbitrary", "arbitrary")),
        name="mixer_merge",
    )(x, *ys, w_gate, w_gate, w_gate, w_gate, w_branch, w_out, norm_g, mods, mods, mods).reshape(bsz * s, d)


def _expert_row_ids():
    r = np.arange(N_EXPERTS)
    return ((r % N_GROUPS) * GROUP_SIZE + r // N_GROUPS).astype(np.int32)


def _router_kernel(rows_ref, xa_ref, xb_ref, ng_ref, sh_ref, sc_ref, rw_ref, rb_ref, eid_ref, tri_e_ref, tri_t_ref,
                   sw1_ref, sw3_ref, sw2_ref, h_ref, so_ref, e_ref, r_ref, w_ref, cnt_ref, *, na):
    i = pl.program_id(0)
    tm = xa_ref.shape[0]

    @pl.when(i == 0)
    def _():
        cnt_ref[...] = jnp.zeros(cnt_ref.shape, F32)

    x = jnp.where(i < na, xa_ref[...], xb_ref[...])
    h = _rms(x, ng_ref[2:3, :]) * (1.0 + sc_ref[0]) + sh_ref[0]
    h_ref[:, 0, :] = _pack_halves(h)
    hb = h.astype(BF16)
    so_ref[...] = _dot((_silu(_dot(hb, sw1_ref[...])) * _dot(hb, sw3_ref[...])).astype(BF16), sw2_ref[...])

    h_tail = (h - hb.astype(F32)).astype(BF16)
    head = _dot_nt(rw_ref[...], hb)
    logits = head[:N_EXPERTS] + head[N_EXPERTS:] + _dot_nt(rw_ref[:N_EXPERTS, :], h_tail)
    scores = _sigmoid(logits)
    biased = scores + rb_ref[...]
    m1 = jnp.full((N_GROUPS, tm), -jnp.inf, F32)
    m2 = m1
    for j in range(GROUP_SIZE):
        v = biased[j * N_GROUPS:(j + 1) * N_GROUPS, :]
        m2 = jnp.maximum(m2, jnp.minimum(m1, v))
        m1 = jnp.maximum(m1, v)
    gs = m1 + m2
    gid = lax.broadcasted_iota(jnp.int32, (N_GROUPS, tm), 0)
    beat = jnp.zeros((N_GROUPS, tm), jnp.int32)
    for g in range(N_GROUPS):
        o = gs[g:g + 1, :]
        beat = beat + jnp.where((o > gs) | ((o == gs) & (g < gid)), 1, 0)
    keep = beat < TOPK_GROUPS
    masked = jnp.concatenate(
        [jnp.where(keep, biased[j * N_GROUPS:(j + 1) * N_GROUPS, :], NEG) for j in range(GROUP_SIZE)], axis=0)
    eid = eid_ref[...]
    row_ids = _expert_row_ids()
    beat = jnp.zeros((N_EXPERTS, tm), jnp.int32)
    for r in range(N_EXPERTS):
        o = masked[r:r + 1, :]
        beat = beat + jnp.where((o > masked) | ((o == masked) & (int(row_ids[r]) < eid)), 1, 0)
    sel = beat < TOP_K
    self32 = jnp.where(sel, 1.0, 0.0)
    selb = self32.astype(BF16)
    wsel = jnp.where(sel, scores, 0.0)
    wsel = wsel / jnp.sum(wsel, axis=0, keepdims=True) * ROUTE_SCALE
    rank = cnt_ref[:, 0:1] + _dot(selb, tri_t_ref[...])
    cnt_ref[...] = cnt_ref[...] + jnp.sum(self32, axis=1, keepdims=True)
    slot = _dot(tri_e_ref[...], selb)
    eidf = eid.astype(F32)
    for k in range(TOP_K):
        mk = sel & (slot == float(k))
        e_ref[k:k + 1, :] = jnp.sum(jnp.where(mk, eidf, 0.0), axis=0, keepdims=True).astype(jnp.int32)
        r_ref[k:k + 1, :] = jnp.sum(jnp.where(mk, rank, 0.0), axis=0, keepdims=True).astype(jnp.int32)
        w_ref[k:k + 1, :] = jnp.sum(jnp.where(mk, wsel, 0.0), axis=0, keepdims=True)


def _two_sources(xa, xb, tm):
    na = xa.shape[0] // tm
    d = xa.shape[1]
    if xb is None:
        xb = xa
    nb_last = xb.shape[0] // tm - 1
    spec_a = pl.BlockSpec((tm, d), lambda i, rows: (jnp.minimum(i, na - 1), 0))
    spec_b = pl.BlockSpec((tm, d), lambda i, rows: (jnp.clip(i - na, 0, nb_last), 0))
    return na, xb, spec_a, spec_b


def _router_call(xa, xb, tile_rows, norm_g, mods, router_w, router_b, sw1, sw3, sw2):
    d = xa.shape[1]
    t = xa.shape[0] + (0 if xb is None else xb.shape[0])
    tm = 256
    na, xb, spec_a, spec_b = _two_sources(xa, xb, tm)
    f = sw1.shape[1]
    ids = _expert_row_ids()
    rw = router_w.T[ids]
    rw_head = rw.astype(BF16)
    rw = jnp.concatenate([rw_head, (rw - rw_head.astype(F32)).astype(BF16)], axis=0)
    rb = router_b[ids].reshape(N_EXPERTS, 1)
    tri_e = jnp.asarray(np.tril(np.ones((N_EXPERTS, N_EXPERTS), np.float32), -1), BF16)
    tri_t = jnp.asarray(np.triu(np.ones((tm, tm), np.float32), 1), BF16)
    const = lambda shape: pl.BlockSpec(shape, lambda i, rows: (0,) * len(shape))
    tok = pl.BlockSpec((tm, d), lambda i, rows: (i, 0))
    slots = pl.BlockSpec((TOP_K, tm), lambda i, rows: (0, i))
    return pl.pallas_call(
        functools.partial(_router_kernel, na=na),
        grid_spec=pltpu.PrefetchScalarGridSpec(
            num_scalar_prefetch=1,
            grid=(t // tm,),
            in_specs=[spec_a, spec_b, const((4, d)),
                      pl.BlockSpec((1, 1, d), lambda i, rows: (rows[i], 0, 3)),
                      pl.BlockSpec((1, 1, d), lambda i, rows: (rows[i], 0, 4)),
                      const((2 * N_EXPERTS, d)), const((N_EXPERTS, 1)), const((N_EXPERTS, 1)),
                      const((N_EXPERTS, N_EXPERTS)), const((tm, tm)),
                      const((d, f)), const((d, f)), const((f, d))],
            out_specs=[pl.BlockSpec((tm, 1, d // 2), lambda i, rows: (i, 0, 0)), tok, slots, slots, slots,
                       const((N_EXPERTS, 128))]),
        out_shape=[jax.ShapeDtypeStruct((t, 1, d // 2), jnp.uint32), jax.ShapeDtypeStruct((t, d), F32),
                   jax.ShapeDtypeStruct((TOP_K, t), jnp.int32), jax.ShapeDtypeStruct((TOP_K, t), jnp.int32),
                   jax.ShapeDtypeStruct((TOP_K, t), F32), jax.ShapeDtypeStruct((N_EXPERTS, 128), F32)],
        compiler_params=_params(("arbitrary",)),
        name="moe_router",
    )(tile_rows, xa, xb, norm_g, mods, mods, rw, rb, jnp.asarray(ids).reshape(N_EXPERTS, 1), tri_e, tri_t,
      sw1, sw3, sw2)


def _gmm_kernel(blk_ref, exp_ref, lo_ref, hi_ref, first_ref, newexp_ref, tok_cur_ref, tok_nxt_ref, h_hbm,
                w1_ref, w3_ref, w2_ref, o_ref, xa_ref, xb_ref, w1b_ref, w3b_ref, w2b_ref, sem):
    i = pl.program_id(0)
    last = pl.num_programs(0) - 1
    nrows = xa_ref.shape[0]
    xbufs = (xa_ref, xb_ref)

    def start_gather(tok_ref, s):
        for r in range(nrows):
            pltpu.make_async_copy(h_hbm.at[tok_ref[0, 0, r]], xbufs[s].at[pl.ds(r, 1)],
                                  sem.at[s]).start(priority=r % 2)

    def wait_gather(s):
        for r in range(nrows):
            pltpu.make_async_copy(h_hbm.at[0], xbufs[s].at[pl.ds(r, 1)], sem.at[s]).wait()

    @pl.when(i == 0)
    def _():
        start_gather(tok_cur_ref, 0)

    @pl.when(first_ref[i] == 1)
    def _():
        o_ref[...] = jnp.zeros(o_ref.shape, o_ref.dtype)

    @pl.when(newexp_ref[i] == 1)
    def _():
        w1b_ref[...] = w1_ref[0].astype(BF16)
        w3b_ref[...] = w3_ref[0].astype(BF16)
        w2b_ref[...] = w2_ref[0].astype(BF16)

    def step(s):
        wait_gather(s)

        @pl.when(lo_ref[i] >= 0)
        def _():
            start_gather(tok_nxt_ref, 1 - s)

        x_lo, x_hi = _unpack_halves(xbufs[s][...])
        x = jnp.concatenate([x_lo.astype(BF16), x_hi.astype(BF16)], axis=1)
        a = (_silu(_dot(x, w1b_ref[...])) * _dot(x, w3b_ref[...])).astype(BF16)
        y = _pack_halves(_dot(a, w2b_ref[...]))
        rows = lax.broadcasted_iota(jnp.int32, (nrows, 1), 0)
        keep = (rows >= lo_ref[i]) & (rows < hi_ref[i])
        o_ref[:, 0, :] = jnp.where(keep, y, o_ref[:, 0, :])

        @pl.when(i == last)
        def _():
            wait_gather(1 - s)

    for s in range(2):
        pl.when(i % 2 == s)(functools.partial(step, s))


def _gmm_call(items, slot_tok, hp, w1, w3, w2, layer):
    a = slot_tok.shape[0]
    d, f = w1.shape[2], w1.shape[3]
    n_items = items[0].shape[0]
    nb = a // MOE_BLOCK
    tok3 = slot_tok.reshape(nb, 1, MOE_BLOCK)
    rows = pl.BlockSpec((MOE_BLOCK,) + hp.shape[1:], lambda i, blk, ex, *_: (blk[i], 0, 0))
    toks = lambda step: pl.BlockSpec((1, 1, MOE_BLOCK), lambda i, blk, ex, *_: (blk[step(i)], 0, 0),
                                     memory_space=pltpu.SMEM)
    return pl.pallas_call(
        _gmm_kernel,
        grid_spec=pltpu.PrefetchScalarGridSpec(
            num_scalar_prefetch=len(items),
            grid=(n_items,),
            in_specs=[toks(lambda i: i), toks(lambda i: jnp.minimum(i + 1, n_items - 1)),
                      pl.BlockSpec(memory_space=pl.ANY),
                      pl.BlockSpec((None, 1, d, f), lambda i, blk, ex, *_: (layer, ex[i], 0, 0)),
                      pl.BlockSpec((None, 1, d, f), lambda i, blk, ex, *_: (layer, ex[i], 0, 0)),
                      pl.BlockSpec((None, 1, f, d), lambda i, blk, ex, *_: (layer, ex[i], 0, 0))],
            out_specs=rows,
            scratch_shapes=[pltpu.VMEM((MOE_BLOCK, hp.shape[2]), jnp.uint32),
                            pltpu.VMEM((MOE_BLOCK, hp.shape[2]), jnp.uint32),
                            pltpu.VMEM((d, f), BF16), pltpu.VMEM((d, f), BF16), pltpu.VMEM((f, d), BF16),
                            pltpu.SemaphoreType.DMA((2,))]),
        out_shape=jax.ShapeDtypeStruct((a,) + hp.shape[1:], jnp.uint32),
        compiler_params=_params(("arbitrary",)),
        name="moe_experts",
    )(*items, tok3, tok3, hp, w1, w3, w2)


def _moe_items(counts, n_rows):
    nb = n_rows // MOE_BLOCK
    n_items = nb + N_EXPERTS
    ends = jnp.cumsum(counts).astype(jnp.int32)
    bstart = jnp.arange(nb, dtype=jnp.int32) * MOE_BLOCK
    pos_b = jnp.arange(nb, dtype=jnp.int32) + jnp.sum(ends[None, :] < bstart[:, None], axis=1)
    idx = jnp.arange(n_items, dtype=jnp.int32)
    n_b = jnp.sum(pos_b[None, :] <= idx[:, None], axis=1).astype(jnp.int32)
    n_e = idx + 1 - n_b
    ends0 = jnp.concatenate([jnp.zeros((1,), jnp.int32), ends])
    last_end = jnp.sum(jnp.where(n_e[:, None] == jnp.arange(N_EXPERTS + 1)[None, :], ends0[None, :], 0), axis=1)
    start = jnp.maximum((n_b - 1) * MOE_BLOCK, last_end).astype(jnp.int32)
    stop = jnp.concatenate([start[1:], jnp.full((1,), n_rows, jnp.int32)])
    blk = jnp.minimum(start // MOE_BLOCK, nb - 1)
    ex = jnp.minimum(jnp.sum(ends[None, :] <= start[:, None], axis=1), N_EXPERTS - 1).astype(jnp.int32)
    lo = start - blk * MOE_BLOCK
    hi = stop - blk * MOE_BLOCK
    one = jnp.ones((1,), jnp.int32)
    first = jnp.concatenate([one, (blk[1:] != blk[:-1]).astype(jnp.int32)])
    newexp = jnp.concatenate([one, (ex[1:] != ex[:-1]).astype(jnp.int32)])
    return blk.astype(jnp.int32), ex, lo.astype(jnp.int32), hi.astype(jnp.int32), first, newexp


def _combine_kernel(rows_ref, dest_ref, dest_nxt_ref, ys_hbm, w_ref, so_ref, xa_ref, xb_ref, ng_ref, gt_ref, o_ref,
                    buf_ref, sem, *, na):
    i = pl.program_id(0)
    tc = xa_ref.shape[0]
    slot = i % 2

    def request(idx_ref, s, j, k):
        pltpu.make_async_copy(ys_hbm.at[idx_ref[0, 0, k * tc + j]], buf_ref.at[s, k, pl.ds(j, 1)],
                              sem.at[s]).start(priority=k % 2)

    @pl.when(i == 0)
    def _():
        def body(j, c):
            for k in range(TOP_K):
                request(dest_ref, 0, j, k)
            return c
        lax.fori_loop(0, tc, body, 0)

    for s in range(2):
        @pl.when((i < pl.num_programs(0) - 1) & (slot == 1 - s))
        def _(s=s):
            for j in range(tc):
                for k in range(TOP_K):
                    request(dest_nxt_ref, s, j, k)

    pltpu.make_async_copy(buf_ref.at[slot], buf_ref.at[slot], sem.at[slot]).wait()
    half = xa_ref.shape[1] // 2
    acc_lo = so_ref[:, :half]
    acc_hi = so_ref[:, half:]
    for k in range(TOP_K):
        y_lo, y_hi = _unpack_halves(buf_ref[slot, k])
        wk = w_ref[:, k:k + 1]
        acc_lo = acc_lo + y_lo * wk
        acc_hi = acc_hi + y_hi * wk
    fl = jnp.concatenate([acc_lo, acc_hi], axis=1)
    x = jnp.where(i < na, xa_ref[...], xb_ref[...])
    o_ref[...] = x + gt_ref[0] * _rms(fl, ng_ref[3:4, :])


def _combine_call(tile_rows, dest, ys, wsel, shared_out, xa, xb, norm_g, mods):
    t, d = shared_out.shape
    tc = 128
    dest3 = dest.reshape(TOP_K, t // tc, tc).transpose(1, 0, 2).reshape(t // tc, 1, TOP_K * tc)
    tok = pl.BlockSpec((tc, d), lambda i, rows: (i, 0))
    nt = t // tc
    na, xb, spec_a, spec_b = _two_sources(xa, xb, tc)
    return pl.pallas_call(
        functools.partial(_combine_kernel, na=na),
        grid_spec=pltpu.PrefetchScalarGridSpec(
            num_scalar_prefetch=1,
            grid=(nt,),
            in_specs=[pl.BlockSpec((1, 1, TOP_K * tc), lambda i, rows: (i, 0, 0), memory_space=pltpu.SMEM),
                      pl.BlockSpec((1, 1, TOP_K * tc), lambda i, rows: (jnp.minimum(i + 1, nt - 1), 0, 0),
                                   memory_space=pltpu.SMEM),
                      pl.BlockSpec(memory_space=pl.ANY),
                      pl.BlockSpec((tc, TOP_K), lambda i, rows: (i, 0)),
                      tok, spec_a, spec_b,
                      pl.BlockSpec((4, d), lambda i, rows: (0, 0)),
                      pl.BlockSpec((1, 1, d), lambda i, rows: (rows[i], 0, 5))],
            out_specs=tok,
            scratch_shapes=[pltpu.VMEM((2, TOP_K, tc, d // 2), jnp.uint32), pltpu.SemaphoreType.DMA((2,))]),
        out_shape=jax.ShapeDtypeStruct((t, d), F32),
        compiler_params=_params(("arbitrary",)),
        name="moe_combine",
    )(tile_rows, dest3, dest3, ys, wsel.T, shared_out, xa, xb, norm_g, mods)


def _channel_mixer(xa, xb, mod_rows, norm_g, mods, router_w, router_b, w1, w3, w2, layer, sw1, sw3, sw2):
    t = xa.shape[0] + (0 if xb is None else xb.shape[0])
    h, shared_out, eidx, rank, wsel, cnt = _router_call(xa, xb, mod_rows[::2], norm_g, mods, router_w, router_b,
                                                         sw1, sw3, sw2)
    counts = jnp.zeros((N_EXPERTS,), jnp.int32).at[_expert_row_ids()].set(cnt[:, 0].astype(jnp.int32))
    starts = jnp.cumsum(counts) - counts
    sel = eidx[:, :, None] == jnp.arange(N_EXPERTS, dtype=jnp.int32)[None, None, :]
    dest = jnp.sum(jnp.where(sel, starts[None, None, :], 0), axis=-1) + rank
    slot_tok = (jnp.argsort(dest.reshape(-1)) % t).astype(jnp.int32)
    ys = _gmm_call(_moe_items(counts, t * TOP_K), slot_tok, h, w1, w3, w2, layer)
    return _combine_call(mod_rows, dest, ys, wsel, shared_out, xa, xb, norm_g, mods)


def _heads(t, n):
    b, s, _ = t.shape
    return t.reshape(b, s, n, HEAD_DIM).transpose(0, 2, 1, 3)


def _unheads(t):
    b, n, s, dh = t.shape
    return t.transpose(0, 2, 1, 3).reshape(b, s, n * dh)


def kernel(x, c, ctx, c_ctx, w_ada, b_ada, norm_g, w_in, conv_b_in, conv_dw, conv_dw_b, conv_ln_g, conv_ln_b, gmlp_ln_g, gmlp_ln_b, gmlp_ws, gmlp_bs, win_sink, na_rpb, w_branch, w_out, router_w, router_b, exp_w1, exp_w3, exp_w2, sh_w1, sh_w3, sh_w2):
    bsz, s, d = x.shape
    cx = ctx.shape[1]
    depth = w_ada.shape[0]
    ctx_row = bsz
    c_all = jnp.zeros((MODS_ROWS, d), F32).at[:bsz].set(c).at[ctx_row].set(c_ctx)
    mods_all = _ada_call(c_all, w_ada, b_ada)
    lat_rows = jnp.repeat(jnp.arange(bsz, dtype=jnp.int32), s // 128)
    ctx_rows = jnp.full((bsz * cx // 128,), ctx_row, jnp.int32)
    xl, xl_off = x.reshape(bsz * s, d), 0
    xc, xc_off = ctx.reshape(bsz * cx, d), 0
    for l in range(depth):
        last = l == depth - 1
        mods = mods_all[l].reshape(MODS_ROWS, 1, N_MODS * d)
        g_off = w_in.shape[2] - 4 * d
        kv_off = g_off - KV_COLS
        w_small = jnp.concatenate([w_in[l, :, :kv_off], w_in[l, :, kv_off + 256:g_off],
                                   w_in[l, :, kv_off:kv_off + 256]], axis=1).astype(BF16)
        w_gate = w_in[l, :, g_off:].astype(BF16)
        w_br = w_branch[l].astype(BF16)
        w_o = w_out[l].astype(BF16)
        pl_l = _proj_call(xl, bsz, s, xl_off, norm_g[l], mods, w_small, None, SMALL_COLS // 2)
        if last:
            pkv_c = _proj_call(xc, bsz, cx, xc_off, norm_g[l], mods, w_small[:, COL_DK:], ctx_row, KV_COLS)
            kv_base = 0
        else:
            pc = _proj_call(xc, bsz, cx, xc_off, norm_g[l], mods, w_small, ctx_row, SMALL_COLS // 2)
            pkv_c, kv_base = pc, COL_DK
        dkc = _heads(pkv_c[..., kv_base:kv_base + 512], NA_HEADS)
        dvc = _heads(pkv_c[..., kv_base + 512:kv_base + 1024], NA_HEADS)
        ckc = _heads(pkv_c[..., kv_base + 1024:kv_base + 1152], WIN_KV_HEADS)
        cvc = _heads(pkv_c[..., kv_base + 1152:kv_base + 1280], WIN_KV_HEADS)
        conv_args = (conv_b_in[l], conv_dw[l], conv_dw_b[l], conv_ln_g[l], conv_ln_b[l])
        gmlp_args = (gmlp_ln_g[l], gmlp_ln_b[l], gmlp_ws[l], gmlp_bs[l])
        ya = _conv_call(pl_l, *conv_args)
        yb = _gmlp_call(pl_l, *gmlp_args)
        q_rope, k_rope = _rope_call(pl_l)
        yc = _unheads(_win_call(_heads(q_rope, WIN_HEADS), _heads(k_rope, WIN_KV_HEADS),
                                _heads(pl_l[..., COL_CV:COL_CV + 128], WIN_KV_HEADS), ckc, cvc, win_sink[l]))
        yd = _unheads(_na_call(_heads(pl_l[..., COL_DQ:COL_DQ + 512], NA_HEADS),
                               _heads(pl_l[..., COL_DK:COL_DK + 512], NA_HEADS),
                               _heads(pl_l[..., COL_DV:COL_DV + 512], NA_HEADS), dkc, dvc,
                               _na_bias_table(na_rpb[l])))
        xl_mid = _merge_call(xl, bsz, s, xl_off, (ya, yb, yc, yd), w_gate, w_br, w_o, norm_g[l], mods, None)
        moe_w = (router_w[l], router_b[l], exp_w1, exp_w3, exp_w2, l,
                 sh_w1[l].astype(BF16), sh_w3[l].astype(BF16), sh_w2[l].astype(BF16))
        if last:
            xl = _channel_mixer(xl_mid, None, lat_rows, norm_g[l], mods, *moe_w)
        else:
            yca = _conv_call(pc, *conv_args)
            ycb = _gmlp_call(pc, *gmlp_args)
            ycc = _unheads(_ctx_attn_call(_heads(pc[..., COL_CQ:COL_CQ + 512], WIN_HEADS), ckc, cvc, win_sink[l]))
            ycd = _unheads(_ctx_attn_call(_heads(pc[..., COL_DQ:COL_DQ + 512], NA_HEADS), dkc, dvc, None))
            xc_mid = _merge_call(xc, bsz, cx, xc_off, (yca, ycb, ycc, ycd), w_gate, w_br, w_o, norm_g[l], mods,
                                 ctx_row)
            out = _channel_mixer(xl_mid, xc_mid, jnp.concatenate([lat_rows, ctx_rows]), norm_g[l], mods, *moe_w)
            xl, xl_off = out, 0
            xc, xc_off = out, bsz * s
    return xl.reshape(bsz, s, d)
```

```python
import functools

import numpy as np
import jax
import jax.numpy as jnp
from jax import lax
from jax.experimental import pallas as pl
from jax.experimental.pallas import tpu as pltpu

F32 = jnp.float32
BF16 = jnp.bfloat16

GRID_W = 64
HEAD_DIM = 64
EPS = 1e-6
NEG = -1e30
BRANCH_DIM = 512
CONV_K = 31
CONV_HALO = 16
GMLP_CHUNK = 128
GMLP_GROUPS = 4
WIN_HEADS = 8
WIN_KV_HEADS = 2
WINDOW = 128
WIN_BLOCK = 128
ROPE_BASE = 10000.0
NA_HEADS = 8
NA_ROWS = 8
NA_COLS = 16
N_EXPERTS = 64
TOP_K = 8
N_GROUPS = 8
GROUP_SIZE = N_EXPERTS // N_GROUPS
TOPK_GROUPS = 4
ROUTE_SCALE = 2.5
MOE_BLOCK = 512
N_MODS = 6
MODS_ROWS = 16
VMEM_LIMIT = 52 * 1024 * 1024

COL_A, COL_B, COL_CQ, COL_DQ, COL_DK, COL_DV, COL_CK, COL_CV = 0, 1024, 2048, 2560, 3072, 3584, 4096, 4224
SMALL_COLS = 4352
KV_COLS = 1280


def _sigmoid(x):
    return 1.0 / (1.0 + jnp.exp(-x))


def _silu(x):
    return x * _sigmoid(x)


def _gelu_tanh(x):
    return 0.5 * x * (1.0 + jnp.tanh(np.sqrt(2.0 / np.pi).astype(np.float32) * (x + 0.044715 * (x * x * x))))


def _rms(x, g):
    return x * lax.rsqrt(jnp.mean(x * x, axis=-1, keepdims=True) + EPS) * g


def _ln(x, g, b):
    mu = jnp.mean(x, axis=-1, keepdims=True)
    xc = x - mu
    var = jnp.mean(xc * xc, axis=-1, keepdims=True)
    return xc * lax.rsqrt(var + EPS) * g + b


def _dot(a, b):
    return jnp.dot(a, b, preferred_element_type=F32)


def _dot_nt(a, b):
    return lax.dot_general(a, b, (((1,), (1,)), ((), ())), preferred_element_type=F32)


def _pack_halves(x):
    n = x.shape[1] // 2
    lo = lax.bitcast_convert_type(x[:, :n].astype(BF16).astype(F32), jnp.uint32)
    hi = lax.bitcast_convert_type(x[:, n:].astype(BF16).astype(F32), jnp.uint32)
    return (hi & jnp.uint32(0xFFFF0000)) | (lo >> 16)


def _unpack_halves(w):
    lo = lax.bitcast_convert_type(w << 16, F32)
    hi = lax.bitcast_convert_type(w & jnp.uint32(0xFFFF0000), F32)
    return lo, hi


def _with_ones(v):
    return jnp.concatenate([v, jnp.ones_like(v)], axis=1)


def _params(sem):
    return pltpu.CompilerParams(dimension_semantics=sem, vmem_limit_bytes=VMEM_LIMIT)


def _ada_kernel(c_ref, w_ref, b_ref, o_ref):
    s = _silu(c_ref[...])
    o_ref[0] = _dot(s.astype(BF16), w_ref[0].astype(BF16)) + b_ref[0]


def _ada_call(c_all, w_ada, b_ada):
    depth, d, n = w_ada.shape
    tn = 512
    return pl.pallas_call(
        _ada_kernel,
        grid=(depth, n // tn),
        in_specs=[pl.BlockSpec((MODS_ROWS, d), lambda l, j: (0, 0)),
                  pl.BlockSpec((1, d, tn), lambda l, j: (l, 0, j)),
                  pl.BlockSpec((1, 1, tn), lambda l, j: (l, 0, j))],
        out_specs=pl.BlockSpec((1, MODS_ROWS, tn), lambda l, j: (l, 0, j)),
        out_shape=jax.ShapeDtypeStruct((depth, MODS_ROWS, n), F32),
        compiler_params=_params(("arbitrary", "arbitrary")),
        name="ada_mods",
    )(c_all, w_ada, b_ada.reshape(depth, 1, n))


def _proj_kernel(x_ref, g_ref, sh_ref, sc_ref, w_ref, o_ref):
    h = _rms(x_ref[0], g_ref[0:1, :]) * (1.0 + sc_ref[0]) + sh_ref[0]
    o_ref[0] = _dot(h.astype(BF16), w_ref[...])


def _proj_call(x2, bsz, s, row_off, norm_g, mods, w, shared_row, tn):
    d = x2.shape[1]
    n = w.shape[1]
    tm = min(s, 512)
    x = x2.reshape(-1, tm, d)
    base, per = row_off // tm, s // tm
    row = (lambda b: b) if shared_row is None else (lambda b: shared_row)
    return pl.pallas_call(
        _proj_kernel,
        grid=(n // tn, bsz, s // tm),
        in_specs=[pl.BlockSpec((1, tm, d), lambda j, b, i: (base + b * per + i, 0, 0)),
                  pl.BlockSpec((4, d), lambda j, b, i: (0, 0)),
                  pl.BlockSpec((1, 1, d), lambda j, b, i: (row(b), 0, 0)),
                  pl.BlockSpec((1, 1, d), lambda j, b, i: (row(b), 0, 1)),
                  pl.BlockSpec((d, tn), lambda j, b, i: (0, j))],
        out_specs=pl.BlockSpec((1, tm, tn), lambda j, b, i: (b, i, j)),
        out_shape=jax.ShapeDtypeStruct((bsz, s, n), F32),
        compiler_params=_params(("arbitrary", "arbitrary", "arbitrary")),
        name="proj_in",
    )(x, norm_g, mods, mods, w)


def _conv_kernel(a_ref, ap_ref, an_ref, bin_ref, dw_ref, dwb_ref, g_ref, b_ref, o_ref, ypad_ref):
    i = pl.program_id(1)
    nblk = pl.num_programs(1)
    ts = a_ref.shape[1]
    c = BRANCH_DIM

    def glu(a):
        a = a + bin_ref[...]
        return a[:, :c] * _sigmoid(a[:, c:])

    ypad_ref[0:CONV_HALO, :] = jnp.where(i > 0, glu(ap_ref[0]), 0.0)
    ypad_ref[CONV_HALO:CONV_HALO + ts, :] = glu(a_ref[0])
    ypad_ref[CONV_HALO + ts:2 * CONV_HALO + ts, :] = jnp.where(i < nblk - 1, glu(an_ref[0]), 0.0)
    rc = 64
    off = CONV_HALO - CONV_K // 2
    sub = 8
    for r0 in range(0, ts, rc):
        acc = None
        for r in range(sub):
            z = None
            for base in range(0, off + CONV_K, sub):
                j = base + r - off
                if 0 <= j < CONV_K:
                    term = ypad_ref[r0 + base:r0 + base + rc + sub, :] * dw_ref[j:j + 1, :]
                    z = term if z is None else z + term
            if z is not None:
                z = z[r:r + rc, :]
                acc = z if acc is None else acc + z
        y = _ln(acc + dwb_ref[...], g_ref[...], b_ref[...])
        o_ref[0, r0:r0 + rc, :] = _silu(y).astype(o_ref.dtype)


def _conv_call(pl_all, b_in, dw, dw_b, ln_g, ln_b):
    bsz, s, _ = pl_all.shape
    ts = min(s, 512)
    c = BRANCH_DIM
    hb = ts // CONV_HALO
    last = s // CONV_HALO - 1
    dw_pad = jnp.zeros((32, c), F32).at[:CONV_K].set(dw)
    return pl.pallas_call(
        _conv_kernel,
        grid=(bsz, s // ts),
        in_specs=[pl.BlockSpec((1, ts, 2 * c), lambda b, i: (b, i, 0)),
                  pl.BlockSpec((1, CONV_HALO, 2 * c), lambda b, i: (b, jnp.maximum(i * hb - 1, 0), 0)),
                  pl.BlockSpec((1, CONV_HALO, 2 * c), lambda b, i: (b, jnp.minimum((i + 1) * hb, last), 0)),
                  pl.BlockSpec((1, 2 * c), lambda b, i: (0, 0)),
                  pl.BlockSpec((32, c), lambda b, i: (0, 0)),
                  pl.BlockSpec((1, c), lambda b, i: (0, 0)),
                  pl.BlockSpec((1, c), lambda b, i: (0, 0)),
                  pl.BlockSpec((1, c), lambda b, i: (0, 0))],
        out_specs=pl.BlockSpec((1, ts, c), lambda b, i: (b, i, 0)),
        out_shape=jax.ShapeDtypeStruct((bsz, s, c), BF16),
        scratch_shapes=[pltpu.VMEM((ts + 2 * CONV_HALO, c), F32)],
        compiler_params=_params(("arbitrary", "arbitrary")),
        name="mixer_conv",
    )(pl_all, pl_all, pl_all, b_in.reshape(1, 2 * c), dw_pad, dw_b.reshape(1, c), ln_g.reshape(1, c), ln_b.reshape(1, c))


def _gmlp_kernel(z_ref, g_ref, b_ref, ws_ref, bs_ref, o_ref):
    c = BRANCH_DIM
    gw = c // GMLP_GROUPS
    ts = z_ref.shape[1]
    for n in range(ts // GMLP_CHUNK):
        rows = slice(n * GMLP_CHUNK, (n + 1) * GMLP_CHUNK)
        z = _gelu_tanh(z_ref[0, rows, :])
        u = z[:, :c]
        v = _ln(z[:, c:], g_ref[...], b_ref[...]).astype(BF16)
        for g in range(GMLP_GROUPS):
            cols = slice(g * gw, (g + 1) * gw)
            sg = _dot(ws_ref[g], v[:, cols]) + bs_ref[:, g:g + 1]
            o_ref[0, rows, cols] = (u[:, cols] * sg).astype(o_ref.dtype)


def _gmlp_call(pl_all, ln_g, ln_b, ws, bs):
    bsz, s, _ = pl_all.shape
    ts = min(s, 512)
    c = BRANCH_DIM
    return pl.pallas_call(
        _gmlp_kernel,
        grid=(bsz, s // ts),
        in_specs=[pl.BlockSpec((1, ts, 2 * c), lambda b, i: (b, i, COL_B // (2 * c))),
                  pl.BlockSpec((1, c), lambda b, i: (0, 0)),
                  pl.BlockSpec((1, c), lambda b, i: (0, 0)),
                  pl.BlockSpec((GMLP_GROUPS, GMLP_CHUNK, GMLP_CHUNK), lambda b, i: (0, 0, 0)),
                  pl.BlockSpec((GMLP_CHUNK, GMLP_GROUPS), lambda b, i: (0, 0))],
        out_specs=pl.BlockSpec((1, ts, c), lambda b, i: (b, i, 0)),
        out_shape=jax.ShapeDtypeStruct((bsz, s, c), BF16),
        compiler_params=_params(("arbitrary", "arbitrary")),
        name="mixer_gmlp",
    )(pl_all, ln_g.reshape(1, c), ln_b.reshape(1, c), ws.astype(BF16), bs.T)


def _rope_tables(s):
    half, quarter = HEAD_DIM // 2, HEAD_DIM // 4
    pos = np.arange(s)
    prow = (pos // GRID_W).astype(np.float32)
    pcol = (pos % GRID_W).astype(np.float32)
    inv = (ROPE_BASE ** (-np.arange(quarter, dtype=np.float32) / quarter)).astype(np.float32)
    lane = np.arange(128)
    in_head = lane % HEAD_DIM
    p = np.where((in_head < half)[None, :], prow[:, None], pcol[:, None]).astype(np.float32)
    ang = (p * inv[lane % quarter][None, :]).astype(np.float32).astype(np.float64)
    first = (in_head % half) < quarter
    cos = np.cos(ang)
    sin = np.where(first[None, :], -np.sin(ang), np.sin(ang))
    swap = np.zeros((128, 128), np.float32)
    for j in lane:
        swap[j + quarter if first[j] else j - quarter, j] = 1.0
    return cos.astype(np.float32), sin.astype(np.float32), swap


def _rope_kernel(q_ref, k_ref, cos_ref, sin_ref, swap_ref, qo_ref, ko_ref):
    cos, sin, swap = cos_ref[...], sin_ref[...], swap_ref[...]

    def rot(x):
        parts = []
        for t in range(x.shape[1] // 128):
            xt = x[:, t * 128:(t + 1) * 128]
            parts.append(xt * cos + _dot(xt.astype(BF16), swap) * sin)
        return parts[0] if len(parts) == 1 else jnp.concatenate(parts, axis=1)

    qo_ref[0] = rot(q_ref[0])
    ko_ref[0] = rot(k_ref[0])


def _rope_call(pl_all):
    bsz, s, _ = pl_all.shape
    tm = 512
    cos, sin, swap = _rope_tables(s)
    qw, kw = WIN_HEADS * HEAD_DIM, WIN_KV_HEADS * HEAD_DIM
    return pl.pallas_call(
        _rope_kernel,
        grid=(bsz, s // tm),
        in_specs=[pl.BlockSpec((1, tm, qw), lambda b, i: (b, i, COL_CQ // qw)),
                  pl.BlockSpec((1, tm, kw), lambda b, i: (b, i, COL_CK // kw)),
                  pl.BlockSpec((tm, 128), lambda b, i: (i, 0)),
                  pl.BlockSpec((tm, 128), lambda b, i: (i, 0)),
                  pl.BlockSpec((128, 128), lambda b, i: (0, 0))],
        out_specs=[pl.BlockSpec((1, tm, qw), lambda b, i: (b, i, 0)),
                   pl.BlockSpec((1, tm, kw), lambda b, i: (b, i, 0))],
        out_shape=[jax.ShapeDtypeStruct((bsz, s, qw), F32), jax.ShapeDtypeStruct((bsz, s, kw), F32)],
        compiler_params=_params(("arbitrary", "arbitrary")),
        name="rope",
    )(pl_all, pl_all, jnp.asarray(cos), jnp.asarray(sin), jnp.asarray(swap, BF16))


WIN_STEP = 4


def _win_kernel(sink_ref, q_ref, kp_ref, kc_ref, kn_ref, vp_ref, vc_ref, vn_ref, kx_ref, vx_ref, o_ref, *, seq):
    g = pl.program_id(1)
    j = pl.program_id(2)
    grp = q_ref.shape[1]
    wb = WIN_BLOCK
    m_rows = grp * wb
    k_ext = jnp.concatenate([kp_ref[0, 0], kc_ref[0, 0], kn_ref[0, 0]], axis=0).astype(BF16)
    v_ext = _with_ones(jnp.concatenate([vp_ref[0, 0], vc_ref[0, 0], vn_ref[0, 0]], axis=0).astype(BF16))
    k_ctx = kx_ref[0, 0].astype(BF16)
    v_ctx = _with_ones(vx_ref[0, 0].astype(BF16))
    sink = jnp.concatenate([jnp.full((wb, 1), sink_ref[g * grp + h], F32) for h in range(grp)], axis=0)
    row = lax.broadcasted_iota(jnp.int32, (m_rows, 3 * wb), 0) & (wb - 1)
    col = lax.broadcasted_iota(jnp.int32, (m_rows, 3 * wb), 1)
    in_band = jnp.where(jnp.abs(col - wb - row) <= WINDOW, 1.0, 0.0)
    col1 = lax.broadcasted_iota(jnp.int32, (1, 3 * wb), 1)
    for t in range(WIN_STEP):
        n = j * WIN_STEP + t
        kpos = (n - 1) * wb + col1
        in_seq = jnp.where((kpos >= 0) & (kpos < seq), 1.0, 0.0)
        q = jnp.concatenate([q_ref[0, h, t * wb:(t + 1) * wb, :] for h in range(grp)], axis=0)
        q = (q * (HEAD_DIM ** -0.5)).astype(BF16)
        s_loc = jnp.where(in_band * in_seq > 0.5, _dot_nt(q, k_ext[t * wb:(t + 3) * wb, :]), NEG)
        s_ctx = _dot_nt(q, k_ctx)
        m = jnp.maximum(jnp.max(jnp.concatenate([s_loc, s_ctx], axis=1), axis=-1, keepdims=True), sink)
        p_loc = jnp.exp(s_loc - m).astype(BF16)
        p_ctx = jnp.exp(s_ctx - m).astype(BF16)
        ov = _dot(p_loc, v_ext[t * wb:(t + 3) * wb, :]) + _dot(p_ctx, v_ctx)
        den = ov[:, HEAD_DIM:HEAD_DIM + 1] + jnp.exp(sink - m)
        o = (ov[:, :HEAD_DIM] / den).astype(o_ref.dtype)
        for h in range(grp):
            o_ref[0, h, t * wb:(t + 1) * wb, :] = o[h * wb:(h + 1) * wb, :]


def _win_call(q, k, v, kx, vx, sink):
    bsz, hq, s, dh = q.shape
    hkv = k.shape[1]
    grp = hq // hkv
    cx = kx.shape[2]
    nb = s // WIN_BLOCK
    step = WIN_STEP * WIN_BLOCK
    blk = (1, 1, WIN_BLOCK, dh)
    prev = lambda b, g, j, sk: (b, g, jnp.maximum(j * WIN_STEP - 1, 0), 0)
    cur = lambda b, g, j, sk: (b, g, j, 0)
    nxt = lambda b, g, j, sk: (b, g, jnp.minimum((j + 1) * WIN_STEP, nb - 1), 0)
    ctx = lambda b, g, j, sk: (b, g, 0, 0)
    return pl.pallas_call(
        functools.partial(_win_kernel, seq=s),
        grid_spec=pltpu.PrefetchScalarGridSpec(
            num_scalar_prefetch=1,
            grid=(bsz, hkv, nb // WIN_STEP),
            in_specs=[pl.BlockSpec((1, grp, step, dh), cur),
                      pl.BlockSpec(blk, prev), pl.BlockSpec((1, 1, step, dh), cur), pl.BlockSpec(blk, nxt),
                      pl.BlockSpec(blk, prev), pl.BlockSpec((1, 1, step, dh), cur), pl.BlockSpec(blk, nxt),
                      pl.BlockSpec((1, 1, cx, dh), ctx), pl.BlockSpec((1, 1, cx, dh), ctx)],
            out_specs=pl.BlockSpec((1, grp, step, dh), cur)),
        out_shape=jax.ShapeDtypeStruct((bsz, hq, s, dh), BF16),
        compiler_params=_params(("arbitrary", "arbitrary", "arbitrary")),
        name="mixer_window_attn",
    )(sink, q, k, k, k, v, v, v, kx, vx)


def _na_bias_table(rpb):
    qc = np.arange(GRID_W)
    kc = np.arange(GRID_W)
    cs = np.clip(qc - NA_COLS // 2, 0, GRID_W - NA_COLS)
    cmask = (kc[None, :] >= cs[:, None]) & (kc[None, :] < cs[:, None] + NA_COLS)
    ci = np.clip(kc[None, :] - qc[:, None] + NA_COLS - 1, 0, 2 * NA_COLS - 2)
    pick = (ci[None] == np.arange(2 * NA_COLS - 1)[:, None, None]).astype(np.float32)
    t15 = jnp.einsum('hrc,cqk->hrqk', rpb, jnp.asarray(pick), precision=lax.Precision.HIGHEST)
    t15 = jnp.where(cmask[None, None], t15, NEG)
    tab = jnp.stack([t15[:, d0:d0 + NA_ROWS] for d0 in range(NA_ROWS)], axis=1)
    return tab.transpose(0, 1, 3, 2, 4).reshape(rpb.shape[0], NA_ROWS, GRID_W, NA_ROWS * GRID_W)


def _na_kernel(q_ref, k_ref, v_ref, kx_ref, vx_ref, bias_ref, o_ref):
    rows = q_ref.shape[2] // GRID_W
    k_ctx = kx_ref[0, 0].astype(BF16)
    v_ctx = _with_ones(vx_ref[0, 0].astype(BF16))
    k_all = k_ref[0, 0].astype(BF16)
    v_all = _with_ones(v_ref[0, 0].astype(BF16))
    span = NA_ROWS * GRID_W
    chunk = 8
    for r0 in range(0, rows, chunk):
        q = (q_ref[0, 0, r0 * GRID_W:(r0 + chunk) * GRID_W, :] * (HEAD_DIM ** -0.5)).astype(BF16)
        s_parts = []
        for r in range(r0, r0 + chunk):
            rs = min(max(r - NA_ROWS // 2, 0), rows - NA_ROWS)
            k_loc = k_all[rs * GRID_W:rs * GRID_W + span, :]
            qr = q[(r - r0) * GRID_W:(r - r0 + 1) * GRID_W, :]
            s_parts.append(_dot_nt(qr, k_loc) + bias_ref[0, rs - r + NA_ROWS - 1])
        s_loc = jnp.concatenate(s_parts, axis=0)
        s_ctx = _dot_nt(q, k_ctx)
        m = jnp.max(jnp.concatenate([s_loc, s_ctx], axis=1), axis=-1, keepdims=True)
        p_loc = jnp.exp(s_loc - m).astype(BF16)
        p_ctx = jnp.exp(s_ctx - m).astype(BF16)
        o_parts = []
        for r in range(r0, r0 + chunk):
            rs = min(max(r - NA_ROWS // 2, 0), rows - NA_ROWS)
            v_loc = v_all[rs * GRID_W:rs * GRID_W + span, :]
            o_parts.append(_dot(p_loc[(r - r0) * GRID_W:(r - r0 + 1) * GRID_W, :], v_loc))
        ov = jnp.concatenate(o_parts, axis=0) + _dot(p_ctx, v_ctx)
        o = ov[:, :HEAD_DIM] / ov[:, HEAD_DIM:HEAD_DIM + 1]
        o_ref[0, 0, r0 * GRID_W:(r0 + chunk) * GRID_W, :] = o.astype(o_ref.dtype)


def _na_call(q, k, v, kx, vx, bias_tab):
    bsz, h, s, dh = q.shape
    cx = kx.shape[2]
    full = pl.BlockSpec((1, 1, s, dh), lambda b, hh: (b, hh, 0, 0))
    ctx = pl.BlockSpec((1, 1, cx, dh), lambda b, hh: (b, hh, 0, 0))
    return pl.pallas_call(
        _na_kernel,
        grid=(bsz, h),
        in_specs=[full, full, full, ctx, ctx,
                  pl.BlockSpec((1, NA_ROWS, GRID_W, NA_ROWS * GRID_W), lambda b, hh: (hh, 0, 0, 0))],
        out_specs=full,
        out_shape=jax.ShapeDtypeStruct((bsz, h, s, dh), BF16),
        compiler_params=_params(("arbitrary", "arbitrary")),
        name="mixer_neighbourhood_attn",
    )(q, k, v, kx, vx, bias_tab)


def _ctx_attn_kernel(sink_ref, q_ref, k_ref, v_ref, o_ref, *, use_sink):
    h = pl.program_id(1)
    q = (q_ref[0, 0] * (HEAD_DIM ** -0.5)).astype(BF16)
    s = _dot_nt(q, k_ref[0, 0].astype(BF16))
    m = jnp.max(s, axis=-1, keepdims=True)
    if use_sink:
        sink = sink_ref[h]
        m = jnp.maximum(m, sink)
    p = jnp.exp(s - m)
    den = jnp.sum(p, axis=-1, keepdims=True)
    if use_sink:
        den = den + jnp.exp(sink - m)
    o_ref[0, 0] = (_dot(p.astype(BF16), v_ref[0, 0].astype(BF16)) / den).astype(o_ref.dtype)


def _ctx_attn_call(q, k, v, sink):
    bsz, hq, cx, dh = q.shape
    grp = hq // k.shape[1]
    use_sink = sink is not None
    if sink is None:
        sink = jnp.zeros((hq,), F32)
    qs = pl.BlockSpec((1, 1, cx, dh), lambda b, h, sk: (b, h, 0, 0))
    ks = pl.BlockSpec((1, 1, cx, dh), lambda b, h, sk: (b, h // grp, 0, 0))
    return pl.pallas_call(
        functools.partial(_ctx_attn_kernel, use_sink=use_sink),
        grid_spec=pltpu.PrefetchScalarGridSpec(
            num_scalar_prefetch=1, grid=(bsz, hq), in_specs=[qs, ks, ks], out_specs=qs),
        out_shape=jax.ShapeDtypeStruct((bsz, hq, cx, dh), BF16),
        compiler_params=_params(("arbitrary", "arbitrary")),
        name="context_attn",
    )(sink, q, k, v)


def _merge_kernel(x_ref, ya_ref, yb_ref, yc_ref, yd_ref, wg0_ref, wg1_ref, wg2_ref, wg3_ref, wbr_ref, wout_ref,
                  ng_ref, sh_ref, sc_ref, gt_ref, o_ref, h_ref, acc_ref):
    n = pl.program_id(2)
    nt = acc_ref.shape[0]
    d = o_ref.shape[2]

    @pl.when(n == 0)
    def _():
        h = _rms(x_ref[0], ng_ref[0:1, :]) * (1.0 + sc_ref[0]) + sh_ref[0]
        h_ref[...] = h.astype(BF16)

    h = h_ref[...]
    acc = None
    for i, (y_ref, wg_ref) in enumerate(((ya_ref, wg0_ref), (yb_ref, wg1_ref), (yc_ref, wg2_ref), (yd_ref, wg3_ref))):
        t = _sigmoid(_dot(h, wg_ref[...])) * _dot(y_ref[0], wbr_ref[i])
        acc = t if acc is None else acc + t
    acc_ref[n] = acc.astype(BF16)

    @pl.when(n == nt - 1)
    def _():
        a = jnp.concatenate([acc_ref[j] for j in range(nt)], axis=1)
        oc = d // 4
        for c in range(4):
            o_ref[0, :, c * oc:(c + 1) * oc] = _dot(a, wout_ref[:, c * oc:(c + 1) * oc])
        o_ref[0] = x_ref[0] + gt_ref[0] * _rms(o_ref[0], ng_ref[1:2, :])


def _merge_call(x2, bsz, s, row_off, ys, w_gate, w_branch, w_out, norm_g, mods, shared_row):
    d = x2.shape[1]
    c = BRANCH_DIM
    tm = min(s, 512)
    tn = 256
    nt = d // tn
    x = x2.reshape(-1, tm, d)
    base, per = row_off // tm, s // tm
    row = (lambda b: b) if shared_row is None else (lambda b: shared_row)
    ysp = pl.BlockSpec((1, tm, c), lambda b, i, n: (b, i, 0))
    wg = [pl.BlockSpec((d, tn), functools.partial(lambda b, i, n, k: (0, k * nt + n), k=k)) for k in range(4)]
    mod = lambda col: pl.BlockSpec((1, 1, d), lambda b, i, n: (row(b), 0, col))
    return pl.pallas_call(
        _merge_kernel,
        grid=(bsz, s // tm, nt),
        in_specs=[pl.BlockSpec((1, tm, d), lambda b, i, n: (base + b * per + i, 0, 0)), ysp, ysp, ysp, ysp, *wg,
                  pl.BlockSpec((4, c, tn), lambda b, i, n: (0, 0, n)),
                  pl.BlockSpec((d, d), lambda b, i, n: (0, 0), pipeline_mode=pl.Buffered(1)),
                  pl.BlockSpec((4, d), lambda b, i, n: (0, 0)),
                  mod(0), mod(1), mod(2)],
        out_specs=pl.BlockSpec((1, tm, d), lambda b, i, n: (b * per + i, 0, 0)),
        out_shape=jax.ShapeDtypeStruct((bsz * per, tm, d), F32),
        scratch_shapes=[pltpu.VMEM((tm, d), BF16), pltpu.VMEM((nt, tm, tn), BF16)],
        compiler_params=_params(("arbitrary", "arbitrary", "arbitrary")),
        name="mixer_merge",
    )(x, *ys, w_gate, w_gate, w_gate, w_gate, w_branch, w_out, norm_g, mods, mods, mods).reshape(bsz * s, d)


def _expert_row_ids():
    r = np.arange(N_EXPERTS)
    return ((r % N_GROUPS) * GROUP_SIZE + r // N_GROUPS).astype(np.int32)


def _router_kernel(rows_ref, xa_ref, xb_ref, ng_ref, sh_ref, sc_ref, rw_ref, rb_ref, eid_ref, tri_e_ref, tri_t_ref,
                   sw1_ref, sw3_ref, sw2_ref, h_ref, so_ref, e_ref, r_ref, w_ref, cnt_ref, *, na):
    i = pl.program_id(0)
    tm = xa_ref.shape[0]

    @pl.when(i == 0)
    def _():
        cnt_ref[...] = jnp.zeros(cnt_ref.shape, F32)

    x = jnp.where(i < na, xa_ref[...], xb_ref[...])
    h = _rms(x, ng_ref[2:3, :]) * (1.0 + sc_ref[0]) + sh_ref[0]
    h_ref[:, 0, :] = _pack_halves(h)
    hb = h.astype(BF16)
    so_ref[...] = _dot((_silu(_dot(hb, sw1_ref[...])) * _dot(hb, sw3_ref[...])).astype(BF16), sw2_ref[...])

    h_tail = (h - hb.astype(F32)).astype(BF16)
    head = _dot_nt(rw_ref[...], hb)
    logits = head[:N_EXPERTS] + head[N_EXPERTS:] + _dot_nt(rw_ref[:N_EXPERTS, :], h_tail)
    scores = _sigmoid(logits)
    biased = scores + rb_ref[...]
    m1 = jnp.full((N_GROUPS, tm), -jnp.inf, F32)
    m2 = m1
    for j in range(GROUP_SIZE):
        v = biased[j * N_GROUPS:(j + 1) * N_GROUPS, :]
        m2 = jnp.maximum(m2, jnp.minimum(m1, v))
        m1 = jnp.maximum(m1, v)
    gs = m1 + m2
    gid = lax.broadcasted_iota(jnp.int32, (N_GROUPS, tm), 0)
    beat = jnp.zeros((N_GROUPS, tm), jnp.int32)
    for g in range(N_GROUPS):
        o = gs[g:g + 1, :]
        beat = beat + jnp.where((o > gs) | ((o == gs) & (g < gid)), 1, 0)
    keep = beat < TOPK_GROUPS
    masked = jnp.concatenate(
        [jnp.where(keep, biased[j * N_GROUPS:(j + 1) * N_GROUPS, :], NEG) for j in range(GROUP_SIZE)], axis=0)
    eid = eid_ref[...]
    row_ids = _expert_row_ids()
    beat = jnp.zeros((N_EXPERTS, tm), jnp.int32)
    for r in range(N_EXPERTS):
        o = masked[r:r + 1, :]
        beat = beat + jnp.where((o > masked) | ((o == masked) & (int(row_ids[r]) < eid)), 1, 0)
    sel = beat < TOP_K
    self32 = jnp.where(sel, 1.0, 0.0)
    selb = self32.astype(BF16)
    wsel = jnp.where(sel, scores, 0.0)
    wsel = wsel / jnp.sum(wsel, axis=0, keepdims=True) * ROUTE_SCALE
    rank = cnt_ref[:, 0:1] + _dot(selb, tri_t_ref[...])
    cnt_ref[...] = cnt_ref[...] + jnp.sum(self32, axis=1, keepdims=True)
    slot = _dot(tri_e_ref[...], selb)
    eidf = eid.astype(F32)
    for k in range(TOP_K):
        mk = sel & (slot == float(k))
        e_ref[k:k + 1, :] = jnp.sum(jnp.where(mk, eidf, 0.0), axis=0, keepdims=True).astype(jnp.int32)
        r_ref[k:k + 1, :] = jnp.sum(jnp.where(mk, rank, 0.0), axis=0, keepdims=True).astype(jnp.int32)
        w_ref[k:k + 1, :] = jnp.sum(jnp.where(mk, wsel, 0.0), axis=0, keepdims=True)


def _two_sources(xa, xb, tm):
    na = xa.shape[0] // tm
    d = xa.shape[1]
    if xb is None:
        xb = xa
    nb_last = xb.shape[0] // tm - 1
    spec_a = pl.BlockSpec((tm, d), lambda i, rows: (jnp.minimum(i, na - 1), 0))
    spec_b = pl.BlockSpec((tm, d), lambda i, rows: (jnp.clip(i - na, 0, nb_last), 0))
    return na, xb, spec_a, spec_b


def _router_call(xa, xb, tile_rows, norm_g, mods, router_w, router_b, sw1, sw3, sw2):
    d = xa.shape[1]
    t = xa.shape[0] + (0 if xb is None else xb.shape[0])
    tm = 256
    na, xb, spec_a, spec_b = _two_sources(xa, xb, tm)
    f = sw1.shape[1]
    ids = _expert_row_ids()
    rw = router_w.T[ids]
    rw_head = rw.astype(BF16)
    rw = jnp.concatenate([rw_head, (rw - rw_head.astype(F32)).astype(BF16)], axis=0)
    rb = router_b[ids].reshape(N_EXPERTS, 1)
    tri_e = jnp.asarray(np.tril(np.ones((N_EXPERTS, N_EXPERTS), np.float32), -1), BF16)
    tri_t = jnp.asarray(np.triu(np.ones((tm, tm), np.float32), 1), BF16)
    const = lambda shape: pl.BlockSpec(shape, lambda i, rows: (0,) * len(shape))
    tok = pl.BlockSpec((tm, d), lambda i, rows: (i, 0))
    slots = pl.BlockSpec((TOP_K, tm), lambda i, rows: (0, i))
    return pl.pallas_call(
        functools.partial(_router_kernel, na=na),
        grid_spec=pltpu.PrefetchScalarGridSpec(
            num_scalar_prefetch=1,
            grid=(t // tm,),
            in_specs=[spec_a, spec_b, const((4, d)),
                      pl.BlockSpec((1, 1, d), lambda i, rows: (rows[i], 0, 3)),
                      pl.BlockSpec((1, 1, d), lambda i, rows: (rows[i], 0, 4)),
                      const((2 * N_EXPERTS, d)), const((N_EXPERTS, 1)), const((N_EXPERTS, 1)),
                      const((N_EXPERTS, N_EXPERTS)), const((tm, tm)),
                      const((d, f)), const((d, f)), const((f, d))],
            out_specs=[pl.BlockSpec((tm, 1, d // 2), lambda i, rows: (i, 0, 0)), tok, slots, slots, slots,
                       const((N_EXPERTS, 128))]),
        out_shape=[jax.ShapeDtypeStruct((t, 1, d // 2), jnp.uint32), jax.ShapeDtypeStruct((t, d), F32),
                   jax.ShapeDtypeStruct((TOP_K, t), jnp.int32), jax.ShapeDtypeStruct((TOP_K, t), jnp.int32),
                   jax.ShapeDtypeStruct((TOP_K, t), F32), jax.ShapeDtypeStruct((N_EXPERTS, 128), F32)],
        compiler_params=_params(("arbitrary",)),
        name="moe_router",
    )(tile_rows, xa, xb, norm_g, mods, mods, rw, rb, jnp.asarray(ids).reshape(N_EXPERTS, 1), tri_e, tri_t,
      sw1, sw3, sw2)


def _gmm_kernel(blk_ref, exp_ref, lo_ref, hi_ref, first_ref, newexp_ref, tok_cur_ref, tok_nxt_ref, h_hbm,
                w1_ref, w3_ref, w2_ref, o_ref, xa_ref, xb_ref, w1b_ref, w3b_ref, w2b_ref, sem):
    i = pl.program_id(0)
    last = pl.num_programs(0) - 1
    nrows = xa_ref.shape[0]
    xbufs = (xa_ref, xb_ref)

    def start_gather(tok_ref, s):
        for r in range(nrows):
            pltpu.make_async_copy(h_hbm.at[tok_ref[0, 0, r]], xbufs[s].at[pl.ds(r, 1)],
                                  sem.at[s]).start(priority=r % 2)

    def wait_gather(s):
        for r in range(nrows):
            pltpu.make_async_copy(h_hbm.at[0], xbufs[s].at[pl.ds(r, 1)], sem.at[s]).wait()

    @pl.when(i == 0)
    def _():
        start_gather(tok_cur_ref, 0)

    @pl.when(first_ref[i] == 1)
    def _():
        o_ref[...] = jnp.zeros(o_ref.shape, o_ref.dtype)

    @pl.when(newexp_ref[i] == 1)
    def _():
        w1b_ref[...] = w1_ref[0].astype(BF16)
        w3b_ref[...] = w3_ref[0].astype(BF16)
        w2b_ref[...] = w2_ref[0].astype(BF16)

    def step(s):
        wait_gather(s)

        @pl.when(lo_ref[i] >= 0)
        def _():
            start_gather(tok_nxt_ref, 1 - s)

        x_lo, x_hi = _unpack_halves(xbufs[s][...])
        x = jnp.concatenate([x_lo.astype(BF16), x_hi.astype(BF16)], axis=1)
        a = (_silu(_dot(x, w1b_ref[...])) * _dot(x, w3b_ref[...])).astype(BF16)
        y = _pack_halves(_dot(a, w2b_ref[...]))
        rows = lax.broadcasted_iota(jnp.int32, (nrows, 1), 0)
        keep = (rows >= lo_ref[i]) & (rows < hi_ref[i])
        o_ref[:, 0, :] = jnp.where(keep, y, o_ref[:, 0, :])

        @pl.when(i == last)
        def _():
            wait_gather(1 - s)

    for s in range(2):
        pl.when(i % 2 == s)(functools.partial(step, s))


def _gmm_call(items, slot_tok, hp, w1, w3, w2, layer):
    a = slot_tok.shape[0]
    d, f = w1.shape[2], w1.shape[3]
    n_items = items[0].shape[0]
    nb = a // MOE_BLOCK
    tok3 = slot_tok.reshape(nb, 1, MOE_BLOCK)
    rows = pl.BlockSpec((MOE_BLOCK,) + hp.shape[1:], lambda i, blk, ex, *_: (blk[i], 0, 0))
    toks = lambda step: pl.BlockSpec((1, 1, MOE_BLOCK), lambda i, blk, ex, *_: (blk[step(i)], 0, 0),
                                     memory_space=pltpu.SMEM)
    return pl.pallas_call(
        _gmm_kernel,
        grid_spec=pltpu.PrefetchScalarGridSpec(
            num_scalar_prefetch=len(items),
            grid=(n_items,),
            in_specs=[toks(lambda i: i), toks(lambda i: jnp.minimum(i + 1, n_items - 1)),
                      pl.BlockSpec(memory_space=pl.ANY),
                      pl.BlockSpec((None, 1, d, f), lambda i, blk, ex, *_: (layer, ex[i], 0, 0)),
                      pl.BlockSpec((None, 1, d, f), lambda i, blk, ex, *_: (layer, ex[i], 0, 0)),
                      pl.BlockSpec((None, 1, f, d), lambda i, blk, ex, *_: (layer, ex[i], 0, 0))],
            out_specs=rows,
            scratch_shapes=[pltpu.VMEM((MOE_BLOCK, hp.shape[2]), jnp.uint32),
                            pltpu.VMEM((MOE_BLOCK, hp.shape[2]), jnp.uint32),
                            pltpu.VMEM((d, f), BF16), pltpu.VMEM((d, f), BF16), pltpu.VMEM((f, d), BF16),
                            pltpu.SemaphoreType.DMA((2,))]),
        out_shape=jax.ShapeDtypeStruct((a,) + hp.shape[1:], jnp.uint32),
        compiler_params=_params(("arbitrary",)),
        name="moe_experts",
    )(*items, tok3, tok3, hp, w1, w3, w2)


def _moe_items(counts, n_rows):
    nb = n_rows // MOE_BLOCK
    n_items = nb + N_EXPERTS
    ends = jnp.cumsum(counts).astype(jnp.int32)
    bstart = jnp.arange(nb, dtype=jnp.int32) * MOE_BLOCK
    pos_b = jnp.arange(nb, dtype=jnp.int32) + jnp.sum(ends[None, :] < bstart[:, None], axis=1)
    idx = jnp.arange(n_items, dtype=jnp.int32)
    n_b = jnp.sum(pos_b[None, :] <= idx[:, None], axis=1).astype(jnp.int32)
    n_e = idx + 1 - n_b
    ends0 = jnp.concatenate([jnp.zeros((1,), jnp.int32), ends])
    last_end = jnp.sum(jnp.where(n_e[:, None] == jnp.arange(N_EXPERTS + 1)[None, :], ends0[None, :], 0), axis=1)
    start = jnp.maximum((n_b - 1) * MOE_BLOCK, last_end).astype(jnp.int32)
    stop = jnp.concatenate([start[1:], jnp.full((1,), n_rows, jnp.int32)])
    blk = jnp.minimum(start // MOE_BLOCK, nb - 1)
    ex = jnp.minimum(jnp.sum(ends[None, :] <= start[:, None], axis=1), N_EXPERTS - 1).astype(jnp.int32)
    lo = start - blk * MOE_BLOCK
    hi = stop - blk * MOE_BLOCK
    one = jnp.ones((1,), jnp.int32)
    first = jnp.concatenate([one, (blk[1:] != blk[:-1]).astype(jnp.int32)])
    newexp = jnp.concatenate([one, (ex[1:] != ex[:-1]).astype(jnp.int32)])
    return blk.astype(jnp.int32), ex, lo.astype(jnp.int32), hi.astype(jnp.int32), first, newexp


def _combine_kernel(rows_ref, dest_ref, dest_nxt_ref, ys_hbm, w_ref, so_ref, xa_ref, xb_ref, ng_ref, gt_ref, o_ref,
                    buf_ref, sem, *, na):
    i = pl.program_id(0)
    tc = xa_ref.shape[0]
    slot = i % 2

    def request(idx_ref, s, j, k):
        pltpu.make_async_copy(ys_hbm.at[idx_ref[0, 0, k * tc + j]], buf_ref.at[s, k, pl.ds(j, 1)],
                              sem.at[s]).start(priority=k % 2)

    @pl.when(i == 0)
    def _():
        def body(j, c):
            for k in range(TOP_K):
                request(dest_ref, 0, j, k)
            return c
        lax.fori_loop(0, tc, body, 0)

    for s in range(2):
        @pl.when((i < pl.num_programs(0) - 1) & (slot == 1 - s))
        def _(s=s):
            for j in range(tc):
                for k in range(TOP_K):
                    request(dest_nxt_ref, s, j, k)

    pltpu.make_async_copy(buf_ref.at[slot], buf_ref.at[slot], sem.at[slot]).wait()
    half = xa_ref.shape[1] // 2
    acc_lo = so_ref[:, :half]
    acc_hi = so_ref[:, half:]
    for k in range(TOP_K):
        y_lo, y_hi = _unpack_halves(buf_ref[slot, k])
        wk = w_ref[:, k:k + 1]
        acc_lo = acc_lo + y_lo * wk
        acc_hi = acc_hi + y_hi * wk
    fl = jnp.concatenate([acc_lo, acc_hi], axis=1)
    x = jnp.where(i < na, xa_ref[...], xb_ref[...])
    o_ref[...] = x + gt_ref[0] * _rms(fl, ng_ref[3:4, :])


def _combine_call(tile_rows, dest, ys, wsel, shared_out, xa, xb, norm_g, mods):
    t, d = shared_out.shape
    tc = 128
    dest3 = dest.reshape(TOP_K, t // tc, tc).transpose(1, 0, 2).reshape(t // tc, 1, TOP_K * tc)
    tok = pl.BlockSpec((tc, d), lambda i, rows: (i, 0))
    nt = t // tc
    na, xb, spec_a, spec_b = _two_sources(xa, xb, tc)
    return pl.pallas_call(
        functools.partial(_combine_kernel, na=na),
        grid_spec=pltpu.PrefetchScalarGridSpec(
            num_scalar_prefetch=1,
            grid=(nt,),
            in_specs=[pl.BlockSpec((1, 1, TOP_K * tc), lambda i, rows: (i, 0, 0), memory_space=pltpu.SMEM),
                      pl.BlockSpec((1, 1, TOP_K * tc), lambda i, rows: (jnp.minimum(i + 1, nt - 1), 0, 0),
                                   memory_space=pltpu.SMEM),
                      pl.BlockSpec(memory_space=pl.ANY),
                      pl.BlockSpec((tc, TOP_K), lambda i, rows: (i, 0)),
                      tok, spec_a, spec_b,
                      pl.BlockSpec((4, d), lambda i, rows: (0, 0)),
                      pl.BlockSpec((1, 1, d), lambda i, rows: (rows[i], 0, 5))],
            out_specs=tok,
            scratch_shapes=[pltpu.VMEM((2, TOP_K, tc, d // 2), jnp.uint32), pltpu.SemaphoreType.DMA((2,))]),
        out_shape=jax.ShapeDtypeStruct((t, d), F32),
        compiler_params=_params(("arbitrary",)),
        name="moe_combine",
    )(tile_rows, dest3, dest3, ys, wsel.T, shared_out, xa, xb, norm_g, mods)


def _channel_mixer(xa, xb, mod_rows, norm_g, mods, router_w, router_b, w1, w3, w2, layer, sw1, sw3, sw2):
    t = xa.shape[0] + (0 if xb is None else xb.shape[0])
    h, shared_out, eidx, rank, wsel, cnt = _router_call(xa, xb, mod_rows[::2], norm_g, mods, router_w, router_b,
                                                         sw1, sw3, sw2)
    counts = jnp.zeros((N_EXPERTS,), jnp.int32).at[_expert_row_ids()].set(cnt[:, 0].astype(jnp.int32))
    starts = jnp.cumsum(counts) - counts
    sel = eidx[:, :, None] == jnp.arange(N_EXPERTS, dtype=jnp.int32)[None, None, :]
    dest = jnp.sum(jnp.where(sel, starts[None, None, :], 0), axis=-1) + rank
    slot_tok = (jnp.argsort(dest.reshape(-1)) % t).astype(jnp.int32)
    ys = _gmm_call(_moe_items(counts, t * TOP_K), slot_tok, h, w1, w3, w2, layer)
    return _combine_call(mod_rows, dest, ys, wsel, shared_out, xa, xb, norm_g, mods)


def _heads(t, n):
    b, s, _ = t.shape
    return t.reshape(b, s, n, HEAD_DIM).transpose(0, 2, 1, 3)


def _unheads(t):
    b, n, s, dh = t.shape
    return t.transpose(0, 2, 1, 3).reshape(b, s, n * dh)


def kernel(x, c, ctx, c_ctx, w_ada, b_ada, norm_g, w_in, conv_b_in, conv_dw, conv_dw_b, conv_ln_g, conv_ln_b, gmlp_ln_g, gmlp_ln_b, gmlp_ws, gmlp_bs, win_sink, na_rpb, w_branch, w_out, router_w, router_b, exp_w1, exp_w3, exp_w2, sh_w1, sh_w3, sh_w2):
    bsz, s, d = x.shape
    cx = ctx.shape[1]
    depth = w_ada.shape[0]
    ctx_row = bsz
    c_all = jnp.zeros((MODS_ROWS, d), F32).at[:bsz].set(c).at[ctx_row].set(c_ctx)
    mods_all = _ada_call(c_all, w_ada, b_ada)
    lat_rows = jnp.repeat(jnp.arange(bsz, dtype=jnp.int32), s // 128)
    ctx_rows = jnp.full((bsz * cx // 128,), ctx_row, jnp.int32)
    xl, xl_off = x.reshape(bsz * s, d), 0
    xc, xc_off = ctx.reshape(bsz * cx, d), 0
    for l in range(depth):
        last = l == depth - 1
        mods = mods_all[l].reshape(MODS_ROWS, 1, N_MODS * d)
        g_off = w_in.shape[2] - 4 * d
        kv_off = g_off - KV_COLS
        w_small = jnp.concatenate([w_in[l, :, :kv_off], w_in[l, :, kv_off + 256:g_off],
                                   w_in[l, :, kv_off:kv_off + 256]], axis=1).astype(BF16)
        w_gate = w_in[l, :, g_off:].astype(BF16)
        w_br = w_branch[l].astype(BF16)
        w_o = w_out[l].astype(BF16)
        pl_l = _proj_call(xl, bsz, s, xl_off, norm_g[l], mods, w_small, None, SMALL_COLS // 2)
        if last:
            pkv_c = _proj_call(xc, bsz, cx, xc_off, norm_g[l], mods, w_small[:, COL_DK:], ctx_row, KV_COLS)
            kv_base = 0
        else:
            pc = _proj_call(xc, bsz, cx, xc_off, norm_g[l], mods, w_small, ctx_row, SMALL_COLS // 2)
            pkv_c, kv_base = pc, COL_DK
        dkc = _heads(pkv_c[..., kv_base:kv_base + 512], NA_HEADS)
        dvc = _heads(pkv_c[..., kv_base + 512:kv_base + 1024], NA_HEADS)
        ckc = _heads(pkv_c[..., kv_base + 1024:kv_base + 1152], WIN_KV_HEADS)
        cvc = _heads(pkv_c[..., kv_base + 1152:kv_base + 1280], WIN_KV_HEADS)
        conv_args = (conv_b_in[l], conv_dw[l], conv_dw_b[l], conv_ln_g[l], conv_ln_b[l])
        gmlp_args = (gmlp_ln_g[l], gmlp_ln_b[l], gmlp_ws[l], gmlp_bs[l])
        ya = _conv_call(pl_l, *conv_args)
        yb = _gmlp_call(pl_l, *gmlp_args)
        q_rope, k_rope = _rope_call(pl_l)
        yc = _unheads(_win_call(_heads(q_rope, WIN_HEADS), _heads(k_rope, WIN_KV_HEADS),
                                _heads(pl_l[..., COL_CV:COL_CV + 128], WIN_KV_HEADS), ckc, cvc, win_sink[l]))
        yd = _unheads(_na_call(_heads(pl_l[..., COL_DQ:COL_DQ + 512], NA_HEADS),
                               _heads(pl_l[..., COL_DK:COL_DK + 512], NA_HEADS),
                               _heads(pl_l[..., COL_DV:COL_DV + 512], NA_HEADS), dkc, dvc,
                               _na_bias_table(na_rpb[l])))
        xl_mid = _merge_call(xl, bsz, s, xl_off, (ya, yb, yc, yd), w_gate, w_br, w_o, norm_g[l], mods, None)
        moe_w = (router_w[l], router_b[l], exp_w1, exp_w3, exp_w2, l,
                 sh_w1[l].astype(BF16), sh_w3[l].astype(BF16), sh_w2[l].astype(BF16))
        if last:
            xl = _channel_mixer(xl_mid, None, lat_rows, norm_g[l], mods, *moe_w)
        else:
            yca = _conv_call(pc, *conv_args)
            ycb = _gmlp_call(pc, *gmlp_args)
            ycc = _unheads(_ctx_attn_call(_heads(pc[..., COL_CQ:COL_CQ + 512], WIN_HEADS), ckc, cvc, win_sink[l]))
            ycd = _unheads(_ctx_attn_call(_heads(pc[..., COL_DQ:COL_DQ + 512], NA_HEADS), dkc, dvc, None))
            xc_mid = _merge_call(xc, bsz, cx, xc_off, (yca, ycb, ycc, ycd), w_gate, w_br, w_o, norm_g[l], mods,
                                 ctx_row)
            out = _channel_mixer(xl_mid, xc_mid, jnp.concatenate([lat_rows, ctx_rows]), norm_g[l], mods, *moe_w)
            xl, xl_off = out, 0
            xc, xc_off = out, bsz * s
    return xl.reshape(bsz, s, d)
```

```python
import functools

import numpy as np
import jax
import jax.numpy as jnp
from jax import lax
from jax.experimental import pallas as pl
from jax.experimental.pallas import tpu as pltpu

F32 = jnp.float32
BF16 = jnp.bfloat16

GRID_W = 64
HEAD_DIM = 64
EPS = 1e-6
NEG = -1e30
BRANCH_DIM = 512
CONV_K = 31
CONV_HALO = 16
GMLP_CHUNK = 128
GMLP_GROUPS = 4
WIN_HEADS = 8
WIN_KV_HEADS = 2
WINDOW = 128
WIN_BLOCK = 128
ROPE_BASE = 10000.0
NA_HEADS = 8
NA_ROWS = 8
NA_COLS = 16
N_EXPERTS = 64
TOP_K = 8
N_GROUPS = 8
GROUP_SIZE = N_EXPERTS // N_GROUPS
TOPK_GROUPS = 4
ROUTE_SCALE = 2.5
MOE_BLOCK = 256
N_MODS = 6
MODS_ROWS = 16
VMEM_LIMIT = 52 * 1024 * 1024

COL_A, COL_B, COL_CQ, COL_DQ, COL_DK, COL_DV, COL_CK, COL_CV = 0, 1024, 2048, 2560, 3072, 3584, 4096, 4224
SMALL_COLS = 4352
KV_COLS = 1280


def _sigmoid(x):
    return 1.0 / (1.0 + jnp.exp(-x))


def _silu(x):
    return x * _sigmoid(x)


def _gelu_tanh(x):
    return 0.5 * x * (1.0 + jnp.tanh(np.sqrt(2.0 / np.pi).astype(np.float32) * (x + 0.044715 * (x * x * x))))


def _rms(x, g):
    return x * lax.rsqrt(jnp.mean(x * x, axis=-1, keepdims=True) + EPS) * g


def _ln(x, g, b):
    mu = jnp.mean(x, axis=-1, keepdims=True)
    xc = x - mu
    var = jnp.mean(xc * xc, axis=-1, keepdims=True)
    return xc * lax.rsqrt(var + EPS) * g + b


def _dot(a, b):
    return jnp.dot(a, b, preferred_element_type=F32)


def _dot_nt(a, b):
    return lax.dot_general(a, b, (((1,), (1,)), ((), ())), preferred_element_type=F32)


def _pack_halves(x):
    n = x.shape[1] // 2
    lo = lax.bitcast_convert_type(x[:, :n].astype(BF16).astype(F32), jnp.uint32)
    hi = lax.bitcast_convert_type(x[:, n:].astype(BF16).astype(F32), jnp.uint32)
    return (hi & jnp.uint32(0xFFFF0000)) | (lo >> 16)


def _unpack_halves(w):
    lo = lax.bitcast_convert_type(w << 16, F32)
    hi = lax.bitcast_convert_type(w & jnp.uint32(0xFFFF0000), F32)
    return lo, hi


def _with_ones(v):
    return jnp.concatenate([v, jnp.ones_like(v)], axis=1)


def _params(sem):
    return pltpu.CompilerParams(dimension_semantics=sem, vmem_limit_bytes=VMEM_LIMIT)


def _ada_kernel(c_ref, w_ref, b_ref, o_ref):
    s = _silu(c_ref[...])
    o_ref[0] = _dot(s.astype(BF16), w_ref[0].astype(BF16)) + b_ref[0]


def _ada_call(c_all, w_ada, b_ada):
    depth, d, n = w_ada.shape
    tn = 512
    return pl.pallas_call(
        _ada_kernel,
        grid=(depth, n // tn),
        in_specs=[pl.BlockSpec((MODS_ROWS, d), lambda l, j: (0, 0)),
                  pl.BlockSpec((1, d, tn), lambda l, j: (l, 0, j)),
                  pl.BlockSpec((1, 1, tn), lambda l, j: (l, 0, j))],
        out_specs=pl.BlockSpec((1, MODS_ROWS, tn), lambda l, j: (l, 0, j)),
        out_shape=jax.ShapeDtypeStruct((depth, MODS_ROWS, n), F32),
        compiler_params=_params(("arbitrary", "arbitrary")),
        name="ada_mods",
    )(c_all, w_ada, b_ada.reshape(depth, 1, n))


def _proj_kernel(x_ref, g_ref, sh_ref, sc_ref, w_ref, o_ref):
    h = _rms(x_ref[0], g_ref[0:1, :]) * (1.0 + sc_ref[0]) + sh_ref[0]
    o_ref[0] = _dot(h.astype(BF16), w_ref[...])


def _proj_call(x2, bsz, s, row_off, norm_g, mods, w, shared_row, tn):
    d = x2.shape[1]
    n = w.shape[1]
    tm = min(s, 512)
    x = x2.reshape(-1, tm, d)
    base, per = row_off // tm, s // tm
    row = (lambda b: b) if shared_row is None else (lambda b: shared_row)
    return pl.pallas_call(
        _proj_kernel,
        grid=(n // tn, bsz, s // tm),
        in_specs=[pl.BlockSpec((1, tm, d), lambda j, b, i: (base + b * per + i, 0, 0)),
                  pl.BlockSpec((4, d), lambda j, b, i: (0, 0)),
                  pl.BlockSpec((1, 1, d), lambda j, b, i: (row(b), 0, 0)),
                  pl.BlockSpec((1, 1, d), lambda j, b, i: (row(b), 0, 1)),
                  pl.BlockSpec((d, tn), lambda j, b, i: (0, j))],
        out_specs=pl.BlockSpec((1, tm, tn), lambda j, b, i: (b, i, j)),
        out_shape=jax.ShapeDtypeStruct((bsz, s, n), F32),
        compiler_params=_params(("arbitrary", "arbitrary", "arbitrary")),
        name="proj_in",
    )(x, norm_g, mods, mods, w)


def _conv_kernel(a_ref, ap_ref, an_ref, bin_ref, dw_ref, dwb_ref, g_ref, b_ref, o_ref, ypad_ref):
    i = pl.program_id(1)
    nblk = pl.num_programs(1)
    ts = a_ref.shape[1]
    c = BRANCH_DIM

    def glu(a):
        a = a + bin_ref[...]
        return a[:, :c] * _sigmoid(a[:, c:])

    ypad_ref[0:CONV_HALO, :] = jnp.where(i > 0, glu(ap_ref[0]), 0.0)
    ypad_ref[CONV_HALO:CONV_HALO + ts, :] = glu(a_ref[0])
    ypad_ref[CONV_HALO + ts:2 * CONV_HALO + ts, :] = jnp.where(i < nblk - 1, glu(an_ref[0]), 0.0)
    rc = 64
    off = CONV_HALO - CONV_K // 2
    sub = 8
    for r0 in range(0, ts, rc):
        acc = None
        for r in range(sub):
            z = None
            for base in range(0, off + CONV_K, sub):
                j = base + r - off
                if 0 <= j < CONV_K:
                    term = ypad_ref[r0 + base:r0 + base + rc + sub, :] * dw_ref[j:j + 1, :]
                    z = term if z is None else z + term
            if z is not None:
                z = z[r:r + rc, :]
                acc = z if acc is None else acc + z
        y = _ln(acc + dwb_ref[...], g_ref[...], b_ref[...])
        o_ref[0, r0:r0 + rc, :] = _silu(y).astype(o_ref.dtype)


def _conv_call(pl_all, b_in, dw, dw_b, ln_g, ln_b):
    bsz, s, _ = pl_all.shape
    ts = min(s, 512)
    c = BRANCH_DIM
    hb = ts // CONV_HALO
    last = s // CONV_HALO - 1
    dw_pad = jnp.zeros((32, c), F32).at[:CONV_K].set(dw)
    return pl.pallas_call(
        _conv_kernel,
        grid=(bsz, s // ts),
        in_specs=[pl.BlockSpec((1, ts, 2 * c), lambda b, i: (b, i, 0)),
                  pl.BlockSpec((1, CONV_HALO, 2 * c), lambda b, i: (b, jnp.maximum(i * hb - 1, 0), 0)),
                  pl.BlockSpec((1, CONV_HALO, 2 * c), lambda b, i: (b, jnp.minimum((i + 1) * hb, last), 0)),
                  pl.BlockSpec((1, 2 * c), lambda b, i: (0, 0)),
                  pl.BlockSpec((32, c), lambda b, i: (0, 0)),
                  pl.BlockSpec((1, c), lambda b, i: (0, 0)),
                  pl.BlockSpec((1, c), lambda b, i: (0, 0)),
                  pl.BlockSpec((1, c), lambda b, i: (0, 0))],
        out_specs=pl.BlockSpec((1, ts, c), lambda b, i: (b, i, 0)),
        out_shape=jax.ShapeDtypeStruct((bsz, s, c), BF16),
        scratch_shapes=[pltpu.VMEM((ts + 2 * CONV_HALO, c), F32)],
        compiler_params=_params(("arbitrary", "arbitrary")),
        name="mixer_conv",
    )(pl_all, pl_all, pl_all, b_in.reshape(1, 2 * c), dw_pad, dw_b.reshape(1, c), ln_g.reshape(1, c), ln_b.reshape(1, c))


def _gmlp_kernel(z_ref, g_ref, b_ref, ws_ref, bs_ref, o_ref):
    c = BRANCH_DIM
    gw = c // GMLP_GROUPS
    ts = z_ref.shape[1]
    for n in range(ts // GMLP_CHUNK):
        rows = slice(n * GMLP_CHUNK, (n + 1) * GMLP_CHUNK)
        z = _gelu_tanh(z_ref[0, rows, :])
        u = z[:, :c]
        v = _ln(z[:, c:], g_ref[...], b_ref[...]).astype(BF16)
        for g in range(GMLP_GROUPS):
            cols = slice(g * gw, (g + 1) * gw)
            sg = _dot(ws_ref[g], v[:, cols]) + bs_ref[:, g:g + 1]
            o_ref[0, rows, cols] = (u[:, cols] * sg).astype(o_ref.dtype)


def _gmlp_call(pl_all, ln_g, ln_b, ws, bs):
    bsz, s, _ = pl_all.shape
    ts = min(s, 512)
    c = BRANCH_DIM
    return pl.pallas_call(
        _gmlp_kernel,
        grid=(bsz, s // ts),
        in_specs=[pl.BlockSpec((1, ts, 2 * c), lambda b, i: (b, i, COL_B // (2 * c))),
                  pl.BlockSpec((1, c), lambda b, i: (0, 0)),
                  pl.BlockSpec((1, c), lambda b, i: (0, 0)),
                  pl.BlockSpec((GMLP_GROUPS, GMLP_CHUNK, GMLP_CHUNK), lambda b, i: (0, 0, 0)),
                  pl.BlockSpec((GMLP_CHUNK, GMLP_GROUPS), lambda b, i: (0, 0))],
        out_specs=pl.BlockSpec((1, ts, c), lambda b, i: (b, i, 0)),
        out_shape=jax.ShapeDtypeStruct((bsz, s, c), BF16),
        compiler_params=_params(("arbitrary", "arbitrary")),
        name="mixer_gmlp",
    )(pl_all, ln_g.reshape(1, c), ln_b.reshape(1, c), ws.astype(BF16), bs.T)


def _rope_tables(s):
    half, quarter = HEAD_DIM // 2, HEAD_DIM // 4
    pos = np.arange(s)
    prow = (pos // GRID_W).astype(np.float32)
    pcol = (pos % GRID_W).astype(np.float32)
    inv = (ROPE_BASE ** (-np.arange(quarter, dtype=np.float32) / quarter)).astype(np.float32)
    lane = np.arange(128)
    in_head = lane % HEAD_DIM
    p = np.where((in_head < half)[None, :], prow[:, None], pcol[:, None]).astype(np.float32)
    ang = (p * inv[lane % quarter][None, :]).astype(np.float32).astype(np.float64)
    first = (in_head % half) < quarter
    cos = np.cos(ang)
    sin = np.where(first[None, :], -np.sin(ang), np.sin(ang))
    swap = np.zeros((128, 128), np.float32)
    for j in lane:
        swap[j + quarter if first[j] else j - quarter, j] = 1.0
    return cos.astype(np.float32), sin.astype(np.float32), swap


def _rope_kernel(q_ref, k_ref, cos_ref, sin_ref, swap_ref, qo_ref, ko_ref):
    cos, sin, swap = cos_ref[...], sin_ref[...], swap_ref[...]

    def rot(x):
        parts = []
        for t in range(x.shape[1] // 128):
            xt = x[:, t * 128:(t + 1) * 128]
            parts.append(xt * cos + _dot(xt.astype(BF16), swap) * sin)
        return parts[0] if len(parts) == 1 else jnp.concatenate(parts, axis=1)

    qo_ref[0] = rot(q_ref[0])
    ko_ref[0] = rot(k_ref[0])


def _rope_call(pl_all):
    bsz, s, _ = pl_all.shape
    tm = 512
    cos, sin, swap = _rope_tables(s)
    qw, kw = WIN_HEADS * HEAD_DIM, WIN_KV_HEADS * HEAD_DIM
    return pl.pallas_call(
        _rope_kernel,
        grid=(bsz, s // tm),
        in_specs=[pl.BlockSpec((1, tm, qw), lambda b, i: (b, i, COL_CQ // qw)),
                  pl.BlockSpec((1, tm, kw), lambda b, i: (b, i, COL_CK // kw)),
                  pl.BlockSpec((tm, 128), lambda b, i: (i, 0)),
                  pl.BlockSpec((tm, 128), lambda b, i: (i, 0)),
                  pl.BlockSpec((128, 128), lambda b, i: (0, 0))],
        out_specs=[pl.BlockSpec((1, tm, qw), lambda b, i: (b, i, 0)),
                   pl.BlockSpec((1, tm, kw), lambda b, i: (b, i, 0))],
        out_shape=[jax.ShapeDtypeStruct((bsz, s, qw), F32), jax.ShapeDtypeStruct((bsz, s, kw), F32)],
        compiler_params=_params(("arbitrary", "arbitrary")),
        name="rope",
    )(pl_all, pl_all, jnp.asarray(cos), jnp.asarray(sin), jnp.asarray(swap, BF16))


WIN_STEP = 4


def _win_kernel(sink_ref, q_ref, kp_ref, kc_ref, kn_ref, vp_ref, vc_ref, vn_ref, kx_ref, vx_ref, o_ref, *, seq):
    g = pl.program_id(1)
    j = pl.program_id(2)
    grp = q_ref.shape[1]
    wb = WIN_BLOCK
    m_rows = grp * wb
    k_ext = jnp.concatenate([kp_ref[0, 0], kc_ref[0, 0], kn_ref[0, 0]], axis=0).astype(BF16)
    v_ext = _with_ones(jnp.concatenate([vp_ref[0, 0], vc_ref[0, 0], vn_ref[0, 0]], axis=0).astype(BF16))
    k_ctx = kx_ref[0, 0].astype(BF16)
    v_ctx = _with_ones(vx_ref[0, 0].astype(BF16))
    sink = jnp.concatenate([jnp.full((wb, 1), sink_ref[g * grp + h], F32) for h in range(grp)], axis=0)
    row = lax.broadcasted_iota(jnp.int32, (m_rows, 3 * wb), 0) & (wb - 1)
    col = lax.broadcasted_iota(jnp.int32, (m_rows, 3 * wb), 1)
    in_band = jnp.where(jnp.abs(col - wb - row) <= WINDOW, 1.0, 0.0)
    col1 = lax.broadcasted_iota(jnp.int32, (1, 3 * wb), 1)
    for t in range(WIN_STEP):
        n = j * WIN_STEP + t
        kpos = (n - 1) * wb + col1
        in_seq = jnp.where((kpos >= 0) & (kpos < seq), 1.0, 0.0)
        q = jnp.concatenate([q_ref[0, h, t * wb:(t + 1) * wb, :] for h in range(grp)], axis=0)
        q = (q * (HEAD_DIM ** -0.5)).astype(BF16)
        s_loc = jnp.where(in_band * in_seq > 0.5, _dot_nt(q, k_ext[t * wb:(t + 3) * wb, :]), NEG)
        s_ctx = _dot_nt(q, k_ctx)
        m = jnp.maximum(jnp.max(jnp.concatenate([s_loc, s_ctx], axis=1), axis=-1, keepdims=True), sink)
        p_loc = jnp.exp(s_loc - m).astype(BF16)
        p_ctx = jnp.exp(s_ctx - m).astype(BF16)
        ov = _dot(p_loc, v_ext[t * wb:(t + 3) * wb, :]) + _dot(p_ctx, v_ctx)
        den = ov[:, HEAD_DIM:HEAD_DIM + 1] + jnp.exp(sink - m)
        o = (ov[:, :HEAD_DIM] / den).astype(o_ref.dtype)
        for h in range(grp):
            o_ref[0, h, t * wb:(t + 1) * wb, :] = o[h * wb:(h + 1) * wb, :]


def _win_call(q, k, v, kx, vx, sink):
    bsz, hq, s, dh = q.shape
    hkv = k.shape[1]
    grp = hq // hkv
    cx = kx.shape[2]
    nb = s // WIN_BLOCK
    step = WIN_STEP * WIN_BLOCK
    blk = (1, 1, WIN_BLOCK, dh)
    prev = lambda b, g, j, sk: (b, g, jnp.maximum(j * WIN_STEP - 1, 0), 0)
    cur = lambda b, g, j, sk: (b, g, j, 0)
    nxt = lambda b, g, j, sk: (b, g, jnp.minimum((j + 1) * WIN_STEP, nb - 1), 0)
    ctx = lambda b, g, j, sk: (b, g, 0, 0)
    return pl.pallas_call(
        functools.partial(_win_kernel, seq=s),
        grid_spec=pltpu.PrefetchScalarGridSpec(
            num_scalar_prefetch=1,
            grid=(bsz, hkv, nb // WIN_STEP),
            in_specs=[pl.BlockSpec((1, grp, step, dh), cur),
                      pl.BlockSpec(blk, prev), pl.BlockSpec((1, 1, step, dh), cur), pl.BlockSpec(blk, nxt),
                      pl.BlockSpec(blk, prev), pl.BlockSpec((1, 1, step, dh), cur), pl.BlockSpec(blk, nxt),
                      pl.BlockSpec((1, 1, cx, dh), ctx), pl.BlockSpec((1, 1, cx, dh), ctx)],
            out_specs=pl.BlockSpec((1, grp, step, dh), cur)),
        out_shape=jax.ShapeDtypeStruct((bsz, hq, s, dh), BF16),
        compiler_params=_params(("arbitrary", "arbitrary", "arbitrary")),
        name="mixer_window_attn",
    )(sink, q, k, k, k, v, v, v, kx, vx)


def _na_bias_table(rpb):
    qc = np.arange(GRID_W)
    kc = np.arange(GRID_W)
    cs = np.clip(qc - NA_COLS // 2, 0, GRID_W - NA_COLS)
    cmask = (kc[None, :] >= cs[:, None]) & (kc[None, :] < cs[:, None] + NA_COLS)
    ci = np.clip(kc[None, :] - qc[:, None] + NA_COLS - 1, 0, 2 * NA_COLS - 2)
    pick = (ci[None] == np.arange(2 * NA_COLS - 1)[:, None, None]).astype(np.float32)
    t15 = jnp.einsum('hrc,cqk->hrqk', rpb, jnp.asarray(pick), precision=lax.Precision.HIGHEST)
    t15 = jnp.where(cmask[None, None], t15, NEG)
    tab = jnp.stack([t15[:, d0:d0 + NA_ROWS] for d0 in range(NA_ROWS)], axis=1)
    return tab.transpose(0, 1, 3, 2, 4).reshape(rpb.shape[0], NA_ROWS, GRID_W, NA_ROWS * GRID_W)


def _na_kernel(q_ref, k_ref, v_ref, kx_ref, vx_ref, bias_ref, o_ref):
    rows = q_ref.shape[2] // GRID_W
    k_ctx = kx_ref[0, 0].astype(BF16)
    v_ctx = _with_ones(vx_ref[0, 0].astype(BF16))
    k_all = k_ref[0, 0].astype(BF16)
    v_all = _with_ones(v_ref[0, 0].astype(BF16))
    span = NA_ROWS * GRID_W
    chunk = 8
    for r0 in range(0, rows, chunk):
        q = (q_ref[0, 0, r0 * GRID_W:(r0 + chunk) * GRID_W, :] * (HEAD_DIM ** -0.5)).astype(BF16)
        s_parts = []
        for r in range(r0, r0 + chunk):
            rs = min(max(r - NA_ROWS // 2, 0), rows - NA_ROWS)
            k_loc = k_all[rs * GRID_W:rs * GRID_W + span, :]
            qr = q[(r - r0) * GRID_W:(r - r0 + 1) * GRID_W, :]
            s_parts.append(_dot_nt(qr, k_loc) + bias_ref[0, rs - r + NA_ROWS - 1])
        s_loc = jnp.concatenate(s_parts, axis=0)
        s_ctx = _dot_nt(q, k_ctx)
        m = jnp.max(jnp.concatenate([s_loc, s_ctx], axis=1), axis=-1, keepdims=True)
        p_loc = jnp.exp(s_loc - m).astype(BF16)
        p_ctx = jnp.exp(s_ctx - m).astype(BF16)
        o_parts = []
        for r in range(r0, r0 + chunk):
            rs = min(max(r - NA_ROWS // 2, 0), rows - NA_ROWS)
            v_loc = v_all[rs * GRID_W:rs * GRID_W + span, :]
            o_parts.append(_dot(p_loc[(r - r0) * GRID_W:(r - r0 + 1) * GRID_W, :], v_loc))
        ov = jnp.concatenate(o_parts, axis=0) + _dot(p_ctx, v_ctx)
        o = ov[:, :HEAD_DIM] / ov[:, HEAD_DIM:HEAD_DIM + 1]
        o_ref[0, 0, r0 * GRID_W:(r0 + chunk) * GRID_W, :] = o.astype(o_ref.dtype)


def _na_call(q, k, v, kx, vx, bias_tab):
    bsz, h, s, dh = q.shape
    cx = kx.shape[2]
    full = pl.BlockSpec((1, 1, s, dh), lambda b, hh: (b, hh, 0, 0))
    ctx = pl.BlockSpec((1, 1, cx, dh), lambda b, hh: (b, hh, 0, 0))
    return pl.pallas_call(
        _na_kernel,
        grid=(bsz, h),
        in_specs=[full, full, full, ctx, ctx,
                  pl.BlockSpec((1, NA_ROWS, GRID_W, NA_ROWS * GRID_W), lambda b, hh: (hh, 0, 0, 0))],
        out_specs=full,
        out_shape=jax.ShapeDtypeStruct((bsz, h, s, dh), BF16),
        compiler_params=_params(("arbitrary", "arbitrary")),
        name="mixer_neighbourhood_attn",
    )(q, k, v, kx, vx, bias_tab)


def _ctx_attn_kernel(sink_ref, q_ref, k_ref, v_ref, o_ref, *, use_sink):
    h = pl.program_id(1)
    q = (q_ref[0, 0] * (HEAD_DIM ** -0.5)).astype(BF16)
    s = _dot_nt(q, k_ref[0, 0].astype(BF16))
    m = jnp.max(s, axis=-1, keepdims=True)
    if use_sink:
        sink = sink_ref[h]
        m = jnp.maximum(m, sink)
    p = jnp.exp(s - m)
    den = jnp.sum(p, axis=-1, keepdims=True)
    if use_sink:
        den = den + jnp.exp(sink - m)
    o_ref[0, 0] = (_dot(p.astype(BF16), v_ref[0, 0].astype(BF16)) / den).astype(o_ref.dtype)


def _ctx_attn_call(q, k, v, sink):
    bsz, hq, cx, dh = q.shape
    grp = hq // k.shape[1]
    use_sink = sink is not None
    if sink is None:
        sink = jnp.zeros((hq,), F32)
    qs = pl.BlockSpec((1, 1, cx, dh), lambda b, h, sk: (b, h, 0, 0))
    ks = pl.BlockSpec((1, 1, cx, dh), lambda b, h, sk: (b, h // grp, 0, 0))
    return pl.pallas_call(
        functools.partial(_ctx_attn_kernel, use_sink=use_sink),
        grid_spec=pltpu.PrefetchScalarGridSpec(
            num_scalar_prefetch=1, grid=(bsz, hq), in_specs=[qs, ks, ks], out_specs=qs),
        out_shape=jax.ShapeDtypeStruct((bsz, hq, cx, dh), BF16),
        compiler_params=_params(("arbitrary", "arbitrary")),
        name="context_attn",
    )(sink, q, k, v)


def _merge_kernel(x_ref, ya_ref, yb_ref, yc_ref, yd_ref, wg0_ref, wg1_ref, wg2_ref, wg3_ref, wbr_ref, wout_ref,
                  ng_ref, sh_ref, sc_ref, gt_ref, o_ref, h_ref, acc_ref):
    n = pl.program_id(2)
    nt = acc_ref.shape[0]
    d = o_ref.shape[2]

    @pl.when(n == 0)
    def _():
        h = _rms(x_ref[0], ng_ref[0:1, :]) * (1.0 + sc_ref[0]) + sh_ref[0]
        h_ref[...] = h.astype(BF16)

    h = h_ref[...]
    acc = None
    for i, (y_ref, wg_ref) in enumerate(((ya_ref, wg0_ref), (yb_ref, wg1_ref), (yc_ref, wg2_ref), (yd_ref, wg3_ref))):
        t = _sigmoid(_dot(h, wg_ref[...])) * _dot(y_ref[0], wbr_ref[i])
        acc = t if acc is None else acc + t
    acc_ref[n] = acc.astype(BF16)

    @pl.when(n == nt - 1)
    def _():
        a = jnp.concatenate([acc_ref[j] for j in range(nt)], axis=1)
        oc = d // 4
        for c in range(4):
            o_ref[0, :, c * oc:(c + 1) * oc] = _dot(a, wout_ref[:, c * oc:(c + 1) * oc])
        o_ref[0] = x_ref[0] + gt_ref[0] * _rms(o_ref[0], ng_ref[1:2, :])


def _merge_call(x2, bsz, s, row_off, ys, w_gate, w_branch, w_out, norm_g, mods, shared_row):
    d = x2.shape[1]
    c = BRANCH_DIM
    tm = min(s, 512)
    tn = 256
    nt = d // tn
    x = x2.reshape(-1, tm, d)
    base, per = row_off // tm, s // tm
    row = (lambda b: b) if shared_row is None else (lambda b: shared_row)
    ysp = pl.BlockSpec((1, tm, c), lambda b, i, n: (b, i, 0))
    wg = [pl.BlockSpec((d, tn), functools.partial(lambda b, i, n, k: (0, k * nt + n), k=k)) for k in range(4)]
    mod = lambda col: pl.BlockSpec((1, 1, d), lambda b, i, n: (row(b), 0, col))
    return pl.pallas_call(
        _merge_kernel,
        grid=(bsz, s // tm, nt),
        in_specs=[pl.BlockSpec((1, tm, d), lambda b, i, n: (base + b * per + i, 0, 0)), ysp, ysp, ysp, ysp, *wg,
                  pl.BlockSpec((4, c, tn), lambda b, i, n: (0, 0, n)),
                  pl.BlockSpec((d, d), lambda b, i, n: (0, 0), pipeline_mode=pl.Buffered(1)),
                  pl.BlockSpec((4, d), lambda b, i, n: (0, 0)),
                  mod(0), mod(1), mod(2)],
        out_specs=pl.BlockSpec((1, tm, d), lambda b, i, n: (b * per + i, 0, 0)),
        out_shape=jax.ShapeDtypeStruct((bsz * per, tm, d), F32),
        scratch_shapes=[pltpu.VMEM((tm, d), BF16), pltpu.VMEM((nt, tm, tn), BF16)],
        compiler_params=_params(("arbitrary", "arbitrary", "arbitrary")),
        name="mixer_merge",
    )(x, *ys, w_gate, w_gate, w_gate, w_gate, w_branch, w_out, norm_g, mods, mods, mods).reshape(bsz * s, d)


def _expert_row_ids():
    r = np.arange(N_EXPERTS)
    return ((r % N_GROUPS) * GROUP_SIZE + r // N_GROUPS).astype(np.int32)


def _router_kernel(rows_ref, xa_ref, xb_ref, ng_ref, sh_ref, sc_ref, rw_ref, rb_ref, eid_ref, tri_t_ref,
                   sw1_ref, sw3_ref, sw2_ref, h_ref, so_ref, e_ref, r_ref, w_ref, cnt_ref, *, na):
    i = pl.program_id(0)
    tm = xa_ref.shape[0]

    @pl.when(i == 0)
    def _():
        cnt_ref[...] = jnp.zeros(cnt_ref.shape, F32)

    x = jnp.where(i < na, xa_ref[...], xb_ref[...])
    h = _rms(x, ng_ref[2:3, :]) * (1.0 + sc_ref[0]) + sh_ref[0]
    h_ref[:, 0, :] = _pack_halves(h)
    hb = h.astype(BF16)
    so_ref[...] = _dot((_silu(_dot(hb, sw1_ref[...])) * _dot(hb, sw3_ref[...])).astype(BF16), sw2_ref[...])

    h_tail = (h - hb.astype(F32)).astype(BF16)
    head = _dot_nt(rw_ref[...], hb)
    logits = head[:N_EXPERTS] + head[N_EXPERTS:] + _dot_nt(rw_ref[:N_EXPERTS, :], h_tail)
    scores = _sigmoid(logits)
    biased = scores + rb_ref[...]
    m1 = jnp.full((N_GROUPS, tm), -jnp.inf, F32)
    m2 = m1
    for j in range(GROUP_SIZE):
        v = biased[j * N_GROUPS:(j + 1) * N_GROUPS, :]
        m2 = jnp.maximum(m2, jnp.minimum(m1, v))
        m1 = jnp.maximum(m1, v)
    gs = m1 + m2
    gid = lax.broadcasted_iota(jnp.int32, (N_GROUPS, tm), 0)
    beat = jnp.zeros((N_GROUPS, tm), jnp.int32)
    for g in range(N_GROUPS):
        o = gs[g:g + 1, :]
        beat = beat + jnp.where((o > gs) | ((o == gs) & (g < gid)), 1, 0)
    keep = beat < TOPK_GROUPS
    masked = jnp.concatenate(
        [jnp.where(keep, biased[j * N_GROUPS:(j + 1) * N_GROUPS, :], NEG) for j in range(GROUP_SIZE)], axis=0)
    eidf = eid_ref[...].astype(F32)
    work = masked
    picks = []
    self32 = jnp.zeros((N_EXPERTS, tm), F32)
    for k in range(TOP_K):
        top = jnp.max(work, axis=0, keepdims=True)
        pick = jnp.min(jnp.where(work == top, eidf, float(N_EXPERTS)), axis=0, keepdims=True)
        hit = eidf == pick
        work = jnp.where(hit, -jnp.inf, work)
        self32 = jnp.where(hit, 1.0, self32)
        picks.append(pick)
    selb = self32.astype(BF16)
    wsel = self32 * scores
    wsel = wsel / jnp.sum(wsel, axis=0, keepdims=True) * ROUTE_SCALE
    rank = cnt_ref[:, 0:1] + _dot(selb, tri_t_ref[...])
    cnt_ref[...] = cnt_ref[...] + jnp.sum(self32, axis=1, keepdims=True)
    for k in range(TOP_K):
        hit = eidf == picks[k]
        e_ref[k:k + 1, :] = picks[k].astype(jnp.int32)
        r_ref[k:k + 1, :] = jnp.sum(jnp.where(hit, rank, 0.0), axis=0, keepdims=True).astype(jnp.int32)
        w_ref[k:k + 1, :] = jnp.sum(jnp.where(hit, wsel, 0.0), axis=0, keepdims=True)


def _two_sources(xa, xb, tm):
    na = xa.shape[0] // tm
    d = xa.shape[1]
    if xb is None:
        xb = xa
    nb_last = xb.shape[0] // tm - 1
    spec_a = pl.BlockSpec((tm, d), lambda i, rows: (jnp.minimum(i, na - 1), 0))
    spec_b = pl.BlockSpec((tm, d), lambda i, rows: (jnp.clip(i - na, 0, nb_last), 0))
    return na, xb, spec_a, spec_b


def _router_call(xa, xb, tile_rows, norm_g, mods, router_w, router_b, sw1, sw3, sw2):
    d = xa.shape[1]
    t = xa.shape[0] + (0 if xb is None else xb.shape[0])
    tm = 256
    na, xb, spec_a, spec_b = _two_sources(xa, xb, tm)
    f = sw1.shape[1]
    ids = _expert_row_ids()
    rw = router_w.T[ids]
    rw_head = rw.astype(BF16)
    rw = jnp.concatenate([rw_head, (rw - rw_head.astype(F32)).astype(BF16)], axis=0)
    rb = router_b[ids].reshape(N_EXPERTS, 1)
    tri_t = jnp.asarray(np.triu(np.ones((tm, tm), np.float32), 1), BF16)
    const = lambda shape: pl.BlockSpec(shape, lambda i, rows: (0,) * len(shape))
    tok = pl.BlockSpec((tm, d), lambda i, rows: (i, 0))
    slots = pl.BlockSpec((TOP_K, tm), lambda i, rows: (0, i))
    return pl.pallas_call(
        functools.partial(_router_kernel, na=na),
        grid_spec=pltpu.PrefetchScalarGridSpec(
            num_scalar_prefetch=1,
            grid=(t // tm,),
            in_specs=[spec_a, spec_b, const((4, d)),
                      pl.BlockSpec((1, 1, d), lambda i, rows: (rows[i], 0, 3)),
                      pl.BlockSpec((1, 1, d), lambda i, rows: (rows[i], 0, 4)),
                      const((2 * N_EXPERTS, d)), const((N_EXPERTS, 1)), const((N_EXPERTS, 1)),
                      const((tm, tm)),
                      const((d, f)), const((d, f)), const((f, d))],
            out_specs=[pl.BlockSpec((tm, 1, d // 2), lambda i, rows: (i, 0, 0)), tok, slots, slots, slots,
                       const((N_EXPERTS, 128))]),
        out_shape=[jax.ShapeDtypeStruct((t, 1, d // 2), jnp.uint32), jax.ShapeDtypeStruct((t, d), F32),
                   jax.ShapeDtypeStruct((TOP_K, t), jnp.int32), jax.ShapeDtypeStruct((TOP_K, t), jnp.int32),
                   jax.ShapeDtypeStruct((TOP_K, t), F32), jax.ShapeDtypeStruct((N_EXPERTS, 128), F32)],
        compiler_params=_params(("arbitrary",)),
        name="moe_router",
    )(tile_rows, xa, xb, norm_g, mods, mods, rw, rb, jnp.asarray(ids).reshape(N_EXPERTS, 1), tri_t,
      sw1, sw3, sw2)


def _gmm_kernel(blk_ref, exp_ref, lo_ref, hi_ref, first_ref, newexp_ref, tok_cur_ref, tok_nxt_ref, h_hbm,
                w1_ref, w3_ref, w2_ref, o_ref, xa_ref, xb_ref, w1b_ref, w3b_ref, w2b_ref, sem):
    i = pl.program_id(0)
    last = pl.num_programs(0) - 1
    nrows = xa_ref.shape[0]
    xbufs = (xa_ref, xb_ref)

    def start_gather(tok_ref, s):
        for r in range(nrows):
            pltpu.make_async_copy(h_hbm.at[tok_ref[0, 0, r]], xbufs[s].at[pl.ds(r, 1)],
                                  sem.at[s]).start(priority=r % 2)

    def wait_gather(s):
        for r in range(nrows):
            pltpu.make_async_copy(h_hbm.at[0], xbufs[s].at[pl.ds(r, 1)], sem.at[s]).wait()

    @pl.when(i == 0)
    def _():
        start_gather(tok_cur_ref, 0)

    @pl.when(first_ref[i] == 1)
    def _():
        o_ref[...] = jnp.zeros(o_ref.shape, o_ref.dtype)

    @pl.when(newexp_ref[i] == 1)
    def _():
        w1b_ref[...] = w1_ref[0].astype(BF16)
        w3b_ref[...] = w3_ref[0].astype(BF16)
        w2b_ref[...] = w2_ref[0].astype(BF16)

    def step(s):
        wait_gather(s)

        @pl.when(lo_ref[i] >= 0)
        def _():
            start_gather(tok_nxt_ref, 1 - s)

        x_lo, x_hi = _unpack_halves(xbufs[s][...])
        x = jnp.concatenate([x_lo.astype(BF16), x_hi.astype(BF16)], axis=1)
        a = (_silu(_dot(x, w1b_ref[...])) * _dot(x, w3b_ref[...])).astype(BF16)
        y = _pack_halves(_dot(a, w2b_ref[...]))
        rows = lax.broadcasted_iota(jnp.int32, (nrows, 1), 0)
        keep = (rows >= lo_ref[i]) & (rows < hi_ref[i])
        o_ref[:, 0, :] = jnp.where(keep, y, o_ref[:, 0, :])

        @pl.when(i == last)
        def _():
            wait_gather(1 - s)

    for s in range(2):
        pl.when(i % 2 == s)(functools.partial(step, s))


def _gmm_call(items, slot_tok, hp, w1, w3, w2, layer):
    a = slot_tok.shape[0]
    d, f = w1.shape[2], w1.shape[3]
    n_items = items[0].shape[0]
    nb = a // MOE_BLOCK
    tok3 = slot_tok.reshape(nb, 1, MOE_BLOCK)
    rows = pl.BlockSpec((MOE_BLOCK,) + hp.shape[1:], lambda i, blk, ex, *_: (blk[i], 0, 0))
    toks = lambda step: pl.BlockSpec((1, 1, MOE_BLOCK), lambda i, blk, ex, *_: (blk[step(i)], 0, 0),
                                     memory_space=pltpu.SMEM)
    return pl.pallas_call(
        _gmm_kernel,
        grid_spec=pltpu.PrefetchScalarGridSpec(
            num_scalar_prefetch=len(items),
            grid=(n_items,),
            in_specs=[toks(lambda i: i), toks(lambda i: jnp.minimum(i + 1, n_items - 1)),
                      pl.BlockSpec(memory_space=pl.ANY),
                      pl.BlockSpec((None, 1, d, f), lambda i, blk, ex, *_: (layer, ex[i], 0, 0)),
                      pl.BlockSpec((None, 1, d, f), lambda i, blk, ex, *_: (layer, ex[i], 0, 0)),
                      pl.BlockSpec((None, 1, f, d), lambda i, blk, ex, *_: (layer, ex[i], 0, 0))],
            out_specs=rows,
            scratch_shapes=[pltpu.VMEM((MOE_BLOCK, hp.shape[2]), jnp.uint32),
                            pltpu.VMEM((MOE_BLOCK, hp.shape[2]), jnp.uint32),
                            pltpu.VMEM((d, f), BF16), pltpu.VMEM((d, f), BF16), pltpu.VMEM((f, d), BF16),
                            pltpu.SemaphoreType.DMA((2,))]),
        out_shape=jax.ShapeDtypeStruct((a,) + hp.shape[1:], jnp.uint32),
        compiler_params=_params(("arbitrary",)),
        name="moe_experts",
    )(*items, tok3, tok3, hp, w1, w3, w2)


def _moe_items(counts, n_rows):
    nb = n_rows // MOE_BLOCK
    n_items = nb + N_EXPERTS
    ends = jnp.cumsum(counts).astype(jnp.int32)
    bstart = jnp.arange(nb, dtype=jnp.int32) * MOE_BLOCK
    pos_b = jnp.arange(nb, dtype=jnp.int32) + jnp.sum(ends[None, :] < bstart[:, None], axis=1)
    idx = jnp.arange(n_items, dtype=jnp.int32)
    n_b = jnp.sum(pos_b[None, :] <= idx[:, None], axis=1).astype(jnp.int32)
    n_e = idx + 1 - n_b
    ends0 = jnp.concatenate([jnp.zeros((1,), jnp.int32), ends])
    last_end = jnp.sum(jnp.where(n_e[:, None] == jnp.arange(N_EXPERTS + 1)[None, :], ends0[None, :], 0), axis=1)
    start = jnp.maximum((n_b - 1) * MOE_BLOCK, last_end).astype(jnp.int32)
    stop = jnp.concatenate([start[1:], jnp.full((1,), n_rows, jnp.int32)])
    blk = jnp.minimum(start // MOE_BLOCK, nb - 1)
    ex = jnp.minimum(jnp.sum(ends[None, :] <= start[:, None], axis=1), N_EXPERTS - 1).astype(jnp.int32)
    lo = start - blk * MOE_BLOCK
    hi = stop - blk * MOE_BLOCK
    one = jnp.ones((1,), jnp.int32)
    first = jnp.concatenate([one, (blk[1:] != blk[:-1]).astype(jnp.int32)])
    newexp = jnp.concatenate([one, (ex[1:] != ex[:-1]).astype(jnp.int32)])
    return blk.astype(jnp.int32), ex, lo.astype(jnp.int32), hi.astype(jnp.int32), first, newexp


def _combine_kernel(rows_ref, dest_ref, dest_nxt_ref, ys_hbm, w_ref, so_ref, xa_ref, xb_ref, ng_ref, gt_ref, o_ref,
                    buf_ref, sem, *, na):
    i = pl.program_id(0)
    tc = xa_ref.shape[0]
    slot = i % 2

    def request(idx_ref, s, j, k):
        pltpu.make_async_copy(ys_hbm.at[idx_ref[0, 0, k * tc + j]], buf_ref.at[s, k, pl.ds(j, 1)],
                              sem.at[s]).start(priority=k % 2)

    @pl.when(i == 0)
    def _():
        def body(j, c):
            for k in range(TOP_K):
                request(dest_ref, 0, j, k)
            return c
        lax.fori_loop(0, tc, body, 0)

    for s in range(2):
        @pl.when((i < pl.num_programs(0) - 1) & (slot == 1 - s))
        def _(s=s):
            for j in range(tc):
                for k in range(TOP_K):
                    request(dest_nxt_ref, s, j, k)

    pltpu.make_async_copy(buf_ref.at[slot], buf_ref.at[slot], sem.at[slot]).wait()
    half = xa_ref.shape[1] // 2
    acc_lo = so_ref[:, :half]
    acc_hi = so_ref[:, half:]
    for k in range(TOP_K):
        y_lo, y_hi = _unpack_halves(buf_ref[slot, k])
        wk = w_ref[:, k:k + 1]
        acc_lo = acc_lo + y_lo * wk
        acc_hi = acc_hi + y_hi * wk
    fl = jnp.concatenate([acc_lo, acc_hi], axis=1)
    x = jnp.where(i < na, xa_ref[...], xb_ref[...])
    o_ref[...] = x + gt_ref[0] * _rms(fl, ng_ref[3:4, :])


def _combine_call(tile_rows, dest, ys, wsel, shared_out, xa, xb, norm_g, mods):
    t, d = shared_out.shape
    tc = 128
    dest3 = dest.reshape(TOP_K, t // tc, tc).transpose(1, 0, 2).reshape(t // tc, 1, TOP_K * tc)
    tok = pl.BlockSpec((tc, d), lambda i, rows: (i, 0))
    nt = t // tc
    na, xb, spec_a, spec_b = _two_sources(xa, xb, tc)
    return pl.pallas_call(
        functools.partial(_combine_kernel, na=na),
        grid_spec=pltpu.PrefetchScalarGridSpec(
            num_scalar_prefetch=1,
            grid=(nt,),
            in_specs=[pl.BlockSpec((1, 1, TOP_K * tc), lambda i, rows: (i, 0, 0), memory_space=pltpu.SMEM),
                      pl.BlockSpec((1, 1, TOP_K * tc), lambda i, rows: (jnp.minimum(i + 1, nt - 1), 0, 0),
                                   memory_space=pltpu.SMEM),
                      pl.BlockSpec(memory_space=pl.ANY),
                      pl.BlockSpec((tc, TOP_K), lambda i, rows: (i, 0)),
                      tok, spec_a, spec_b,
                      pl.BlockSpec((4, d), lambda i, rows: (0, 0)),
                      pl.BlockSpec((1, 1, d), lambda i, rows: (rows[i], 0, 5))],
            out_specs=tok,
            scratch_shapes=[pltpu.VMEM((2, TOP_K, tc, d // 2), jnp.uint32), pltpu.SemaphoreType.DMA((2,))]),
        out_shape=jax.ShapeDtypeStruct((t, d), F32),
        compiler_params=_params(("arbitrary",)),
        name="moe_combine",
    )(tile_rows, dest3, dest3, ys, wsel.T, shared_out, xa, xb, norm_g, mods)


def _channel_mixer(xa, xb, mod_rows, norm_g, mods, router_w, router_b, w1, w3, w2, layer, sw1, sw3, sw2):
    t = xa.shape[0] + (0 if xb is None else xb.shape[0])
    h, shared_out, eidx, rank, wsel, cnt = _router_call(xa, xb, mod_rows[::2], norm_g, mods, router_w, router_b,
                                                         sw1, sw3, sw2)
    counts = jnp.zeros((N_EXPERTS,), jnp.int32).at[_expert_row_ids()].set(cnt[:, 0].astype(jnp.int32))
    starts = jnp.cumsum(counts) - counts
    sel = eidx[:, :, None] == jnp.arange(N_EXPERTS, dtype=jnp.int32)[None, None, :]
    dest = jnp.sum(jnp.where(sel, starts[None, None, :], 0), axis=-1) + rank
    slot_tok = (jnp.argsort(dest.reshape(-1)) % t).astype(jnp.int32)
    ys = _gmm_call(_moe_items(counts, t * TOP_K), slot_tok, h, w1, w3, w2, layer)
    return _combine_call(mod_rows, dest, ys, wsel, shared_out, xa, xb, norm_g, mods)


def _heads(t, n):
    b, s, _ = t.shape
    return t.reshape(b, s, n, HEAD_DIM).transpose(0, 2, 1, 3)


def _unheads(t):
    b, n, s, dh = t.shape
    return t.transpose(0, 2, 1, 3).reshape(b, s, n * dh)


def kernel(x, c, ctx, c_ctx, w_ada, b_ada, norm_g, w_in, conv_b_in, conv_dw, conv_dw_b, conv_ln_g, conv_ln_b, gmlp_ln_g, gmlp_ln_b, gmlp_ws, gmlp_bs, win_sink, na_rpb, w_branch, w_out, router_w, router_b, exp_w1, exp_w3, exp_w2, sh_w1, sh_w3, sh_w2):
    bsz, s, d = x.shape
    cx = ctx.shape[1]
    depth = w_ada.shape[0]
    ctx_row = bsz
    c_all = jnp.zeros((MODS_ROWS, d), F32).at[:bsz].set(c).at[ctx_row].set(c_ctx)
    mods_all = _ada_call(c_all, w_ada, b_ada)
    lat_rows = jnp.repeat(jnp.arange(bsz, dtype=jnp.int32), s // 128)
    ctx_rows = jnp.full((bsz * cx // 128,), ctx_row, jnp.int32)
    xl, xl_off = x.reshape(bsz * s, d), 0
    xc, xc_off = ctx.reshape(bsz * cx, d), 0
    for l in range(depth):
        last = l == depth - 1
        mods = mods_all[l].reshape(MODS_ROWS, 1, N_MODS * d)
        g_off = w_in.shape[2] - 4 * d
        kv_off = g_off - KV_COLS
        w_small = jnp.concatenate([w_in[l, :, :kv_off], w_in[l, :, kv_off + 256:g_off],
                                   w_in[l, :, kv_off:kv_off + 256]], axis=1).astype(BF16)
        w_gate = w_in[l, :, g_off:].astype(BF16)
        w_br = w_branch[l].astype(BF16)
        w_o = w_out[l].astype(BF16)
        pl_l = _proj_call(xl, bsz, s, xl_off, norm_g[l], mods, w_small, None, SMALL_COLS // 2)
        if last:
            pkv_c = _proj_call(xc, bsz, cx, xc_off, norm_g[l], mods, w_small[:, COL_DK:], ctx_row, KV_COLS)
            kv_base = 0
        else:
            pc = _proj_call(xc, bsz, cx, xc_off, norm_g[l], mods, w_small, ctx_row, SMALL_COLS // 2)
            pkv_c, kv_base = pc, COL_DK
        dkc = _heads(pkv_c[..., kv_base:kv_base + 512], NA_HEADS)
        dvc = _heads(pkv_c[..., kv_base + 512:kv_base + 1024], NA_HEADS)
        ckc = _heads(pkv_c[..., kv_base + 1024:kv_base + 1152], WIN_KV_HEADS)
        cvc = _heads(pkv_c[..., kv_base + 1152:kv_base + 1280], WIN_KV_HEADS)
        conv_args = (conv_b_in[l], conv_dw[l], conv_dw_b[l], conv_ln_g[l], conv_ln_b[l])
        gmlp_args = (gmlp_ln_g[l], gmlp_ln_b[l], gmlp_ws[l], gmlp_bs[l])
        ya = _conv_call(pl_l, *conv_args)
        yb = _gmlp_call(pl_l, *gmlp_args)
        q_rope, k_rope = _rope_call(pl_l)
        yc = _unheads(_win_call(_heads(q_rope, WIN_HEADS), _heads(k_rope, WIN_KV_HEADS),
                                _heads(pl_l[..., COL_CV:COL_CV + 128], WIN_KV_HEADS), ckc, cvc, win_sink[l]))
        yd = _unheads(_na_call(_heads(pl_l[..., COL_DQ:COL_DQ + 512], NA_HEADS),
                               _heads(pl_l[..., COL_DK:COL_DK + 512], NA_HEADS),
                               _heads(pl_l[..., COL_DV:COL_DV + 512], NA_HEADS), dkc, dvc,
                               _na_bias_table(na_rpb[l])))
        xl_mid = _merge_call(xl, bsz, s, xl_off, (ya, yb, yc, yd), w_gate, w_br, w_o, norm_g[l], mods, None)
        moe_w = (router_w[l], router_b[l], exp_w1, exp_w3, exp_w2, l,
                 sh_w1[l].astype(BF16), sh_w3[l].astype(BF16), sh_w2[l].astype(BF16))
        if last:
            xl = _channel_mixer(xl_mid, None, lat_rows, norm_g[l], mods, *moe_w)
        else:
            yca = _conv_call(pc, *conv_args)
            ycb = _gmlp_call(pc, *gmlp_args)
            ycc = _unheads(_ctx_attn_call(_heads(pc[..., COL_CQ:COL_CQ + 512], WIN_HEADS), ckc, cvc, win_sink[l]))
            ycd = _unheads(_ctx_attn_call(_heads(pc[..., COL_DQ:COL_DQ + 512], NA_HEADS), dkc, dvc, None))
            xc_mid = _merge_call(xc, bsz, cx, xc_off, (yca, ycb, ycc, ycd), w_gate, w_br, w_o, norm_g[l], mods,
                                 ctx_row)
            out = _channel_mixer(xl_mid, xc_mid, jnp.concatenate([lat_rows, ctx_rows]), norm_g[l], mods, *moe_w)
            xl, xl_off = out, 0
            xc, xc_off = out, bsz * s
    return xl.reshape(bsz, s, d)
```

```python
import functools

import numpy as np
import jax
import jax.numpy as jnp
from jax import lax
from jax.experimental import pallas as pl
from jax.experimental.pallas import tpu as pltpu

F32 = jnp.float32
BF16 = jnp.bfloat16

GRID_W = 64
HEAD_DIM = 64
EPS = 1e-6
NEG = -1e30
BRANCH_DIM = 512
CONV_K = 31
CONV_HALO = 16
GMLP_CHUNK = 128
GMLP_GROUPS = 4
WIN_HEADS = 8
WIN_KV_HEADS = 2
WINDOW = 128
WIN_BLOCK = 128
ROPE_BASE = 10000.0
NA_HEADS = 8
NA_ROWS = 8
NA_COLS = 16
N_EXPERTS = 64
TOP_K = 8
N_GROUPS = 8
GROUP_SIZE = N_EXPERTS // N_GROUPS
TOPK_GROUPS = 4
ROUTE_SCALE = 2.5
MOE_BLOCK = 256
N_MODS = 6
MODS_ROWS = 16
VMEM_LIMIT = 52 * 1024 * 1024

COL_A, COL_B, COL_CQ, COL_DQ, COL_DK, COL_DV, COL_CK, COL_CV = 0, 1024, 2048, 2560, 3072, 3584, 4096, 4224
SMALL_COLS = 4352
KV_COLS = 1280


def _sigmoid(x):
    return 1.0 / (1.0 + jnp.exp(-x))


def _silu(x):
    return x * _sigmoid(x)


def _gelu_tanh(x):
    return 0.5 * x * (1.0 + jnp.tanh(np.sqrt(2.0 / np.pi).astype(np.float32) * (x + 0.044715 * (x * x * x))))


def _rms(x, g):
    return x * lax.rsqrt(jnp.mean(x * x, axis=-1, keepdims=True) + EPS) * g


def _ln(x, g, b):
    mu = jnp.mean(x, axis=-1, keepdims=True)
    xc = x - mu
    var = jnp.mean(xc * xc, axis=-1, keepdims=True)
    return xc * lax.rsqrt(var + EPS) * g + b


def _dot(a, b):
    return jnp.dot(a, b, preferred_element_type=F32)


def _dot_nt(a, b):
    return lax.dot_general(a, b, (((1,), (1,)), ((), ())), preferred_element_type=F32)


def _pack_halves(x):
    n = x.shape[1] // 2
    lo = lax.bitcast_convert_type(x[:, :n].astype(BF16).astype(F32), jnp.uint32)
    hi = lax.bitcast_convert_type(x[:, n:].astype(BF16).astype(F32), jnp.uint32)
    return (hi & jnp.uint32(0xFFFF0000)) | (lo >> 16)


def _unpack_halves(w):
    lo = lax.bitcast_convert_type(w << 16, F32)
    hi = lax.bitcast_convert_type(w & jnp.uint32(0xFFFF0000), F32)
    return lo, hi


def _with_ones(v):
    return jnp.concatenate([v, jnp.ones_like(v)], axis=1)


def _params(sem):
    return pltpu.CompilerParams(dimension_semantics=sem, vmem_limit_bytes=VMEM_LIMIT)


def _ada_kernel(c_ref, w_ref, b_ref, o_ref):
    s = _silu(c_ref[...])
    o_ref[0] = _dot(s.astype(BF16), w_ref[0].astype(BF16)) + b_ref[0]


def _ada_call(c_all, w_ada, b_ada):
    depth, d, n = w_ada.shape
    tn = 512
    return pl.pallas_call(
        _ada_kernel,
        grid=(depth, n // tn),
        in_specs=[pl.BlockSpec((MODS_ROWS, d), lambda l, j: (0, 0)),
                  pl.BlockSpec((1, d, tn), lambda l, j: (l, 0, j)),
                  pl.BlockSpec((1, 1, tn), lambda l, j: (l, 0, j))],
        out_specs=pl.BlockSpec((1, MODS_ROWS, tn), lambda l, j: (l, 0, j)),
        out_shape=jax.ShapeDtypeStruct((depth, MODS_ROWS, n), F32),
        compiler_params=_params(("arbitrary", "arbitrary")),
        name="ada_mods",
    )(c_all, w_ada, b_ada.reshape(depth, 1, n))


def _proj_kernel(x_ref, g_ref, sh_ref, sc_ref, w_ref, o_ref):
    h = _rms(x_ref[0], g_ref[0:1, :]) * (1.0 + sc_ref[0]) + sh_ref[0]
    o_ref[0] = _dot(h.astype(BF16), w_ref[...])


def _proj_call(x2, bsz, s, row_off, norm_g, mods, w, shared_row, tn):
    d = x2.shape[1]
    n = w.shape[1]
    tm = min(s, 512)
    x = x2.reshape(-1, tm, d)
    base, per = row_off // tm, s // tm
    row = (lambda b: b) if shared_row is None else (lambda b: shared_row)
    return pl.pallas_call(
        _proj_kernel,
        grid=(n // tn, bsz, s // tm),
        in_specs=[pl.BlockSpec((1, tm, d), lambda j, b, i: (base + b * per + i, 0, 0)),
                  pl.BlockSpec((4, d), lambda j, b, i: (0, 0)),
                  pl.BlockSpec((1, 1, d), lambda j, b, i: (row(b), 0, 0)),
                  pl.BlockSpec((1, 1, d), lambda j, b, i: (row(b), 0, 1)),
                  pl.BlockSpec((d, tn), lambda j, b, i: (0, j))],
        out_specs=pl.BlockSpec((1, tm, tn), lambda j, b, i: (b, i, j)),
        out_shape=jax.ShapeDtypeStruct((bsz, s, n), F32),
        compiler_params=_params(("arbitrary", "arbitrary", "arbitrary")),
        name="proj_in",
    )(x, norm_g, mods, mods, w)


def _conv_kernel(a_ref, ap_ref, an_ref, bin_ref, dw_ref, dwb_ref, g_ref, b_ref, o_ref, ypad_ref):
    i = pl.program_id(1)
    nblk = pl.num_programs(1)
    ts = a_ref.shape[1]
    c = BRANCH_DIM

    def glu(a):
        a = a + bin_ref[...]
        return a[:, :c] * _sigmoid(a[:, c:])

    ypad_ref[0:CONV_HALO, :] = jnp.where(i > 0, glu(ap_ref[0]), 0.0)
    ypad_ref[CONV_HALO:CONV_HALO + ts, :] = glu(a_ref[0])
    ypad_ref[CONV_HALO + ts:2 * CONV_HALO + ts, :] = jnp.where(i < nblk - 1, glu(an_ref[0]), 0.0)
    rc = 64
    off = CONV_HALO - CONV_K // 2
    sub = 8
    for r0 in range(0, ts, rc):
        acc = None
        for r in range(sub):
            z = None
            for base in range(0, off + CONV_K, sub):
                j = base + r - off
                if 0 <= j < CONV_K:
                    term = ypad_ref[r0 + base:r0 + base + rc + sub, :] * dw_ref[j:j + 1, :]
                    z = term if z is None else z + term
            if z is not None:
                z = z[r:r + rc, :]
                acc = z if acc is None else acc + z
        y = _ln(acc + dwb_ref[...], g_ref[...], b_ref[...])
        o_ref[0, r0:r0 + rc, :] = _silu(y).astype(o_ref.dtype)


def _conv_call(pl_all, b_in, dw, dw_b, ln_g, ln_b):
    bsz, s, _ = pl_all.shape
    ts = min(s, 512)
    c = BRANCH_DIM
    hb = ts // CONV_HALO
    last = s // CONV_HALO - 1
    dw_pad = jnp.zeros((32, c), F32).at[:CONV_K].set(dw)
    return pl.pallas_call(
        _conv_kernel,
        grid=(bsz, s // ts),
        in_specs=[pl.BlockSpec((1, ts, 2 * c), lambda b, i: (b, i, 0)),
                  pl.BlockSpec((1, CONV_HALO, 2 * c), lambda b, i: (b, jnp.maximum(i * hb - 1, 0), 0)),
                  pl.BlockSpec((1, CONV_HALO, 2 * c), lambda b, i: (b, jnp.minimum((i + 1) * hb, last), 0)),
                  pl.BlockSpec((1, 2 * c), lambda b, i: (0, 0)),
                  pl.BlockSpec((32, c), lambda b, i: (0, 0)),
                  pl.BlockSpec((1, c), lambda b, i: (0, 0)),
                  pl.BlockSpec((1, c), lambda b, i: (0, 0)),
                  pl.BlockSpec((1, c), lambda b, i: (0, 0))],
        out_specs=pl.BlockSpec((1, ts, c), lambda b, i: (b, i, 0)),
        out_shape=jax.ShapeDtypeStruct((bsz, s, c), BF16),
        scratch_shapes=[pltpu.VMEM((ts + 2 * CONV_HALO, c), F32)],
        compiler_params=_params(("arbitrary", "arbitrary")),
        name="mixer_conv",
    )(pl_all, pl_all, pl_all, b_in.reshape(1, 2 * c), dw_pad, dw_b.reshape(1, c), ln_g.reshape(1, c), ln_b.reshape(1, c))


def _gmlp_kernel(z_ref, g_ref, b_ref, ws_ref, bs_ref, o_ref):
    c = BRANCH_DIM
    gw = c // GMLP_GROUPS
    ts = z_ref.shape[1]
    for n in range(ts // GMLP_CHUNK):
        rows = slice(n * GMLP_CHUNK, (n + 1) * GMLP_CHUNK)
        z = _gelu_tanh(z_ref[0, rows, :])
        u = z[:, :c]
        v = _ln(z[:, c:], g_ref[...], b_ref[...]).astype(BF16)
        for g in range(GMLP_GROUPS):
            cols = slice(g * gw, (g + 1) * gw)
            sg = _dot(ws_ref[g], v[:, cols]) + bs_ref[:, g:g + 1]
            o_ref[0, rows, cols] = (u[:, cols] * sg).astype(o_ref.dtype)


def _gmlp_call(pl_all, ln_g, ln_b, ws, bs):
    bsz, s, _ = pl_all.shape
    ts = min(s, 512)
    c = BRANCH_DIM
    return pl.pallas_call(
        _gmlp_kernel,
        grid=(bsz, s // ts),
        in_specs=[pl.BlockSpec((1, ts, 2 * c), lambda b, i: (b, i, COL_B // (2 * c))),
                  pl.BlockSpec((1, c), lambda b, i: (0, 0)),
                  pl.BlockSpec((1, c), lambda b, i: (0, 0)),
                  pl.BlockSpec((GMLP_GROUPS, GMLP_CHUNK, GMLP_CHUNK), lambda b, i: (0, 0, 0)),
                  pl.BlockSpec((GMLP_CHUNK, GMLP_GROUPS), lambda b, i: (0, 0))],
        out_specs=pl.BlockSpec((1, ts, c), lambda b, i: (b, i, 0)),
        out_shape=jax.ShapeDtypeStruct((bsz, s, c), BF16),
        compiler_params=_params(("arbitrary", "arbitrary")),
        name="mixer_gmlp",
    )(pl_all, ln_g.reshape(1, c), ln_b.reshape(1, c), ws.astype(BF16), bs.T)


def _rope_tables(s):
    half, quarter = HEAD_DIM // 2, HEAD_DIM // 4
    pos = np.arange(s)
    prow = (pos // GRID_W).astype(np.float32)
    pcol = (pos % GRID_W).astype(np.float32)
    inv = (ROPE_BASE ** (-np.arange(quarter, dtype=np.float32) / quarter)).astype(np.float32)
    lane = np.arange(128)
    in_head = lane % HEAD_DIM
    p = np.where((in_head < half)[None, :], prow[:, None], pcol[:, None]).astype(np.float32)
    ang = (p * inv[lane % quarter][None, :]).astype(np.float32).astype(np.float64)
    first = (in_head % half) < quarter
    cos = np.cos(ang)
    sin = np.where(first[None, :], -np.sin(ang), np.sin(ang))
    swap = np.zeros((128, 128), np.float32)
    for j in lane:
        swap[j + quarter if first[j] else j - quarter, j] = 1.0
    return cos.astype(np.float32), sin.astype(np.float32), swap


def _rope_kernel(q_ref, k_ref, cos_ref, sin_ref, swap_ref, qo_ref, ko_ref):
    cos, sin, swap = cos_ref[...], sin_ref[...], swap_ref[...]

    def rot(x):
        parts = []
        for t in range(x.shape[1] // 128):
            xt = x[:, t * 128:(t + 1) * 128]
            parts.append(xt * cos + _dot(xt.astype(BF16), swap) * sin)
        return parts[0] if len(parts) == 1 else jnp.concatenate(parts, axis=1)

    qo_ref[0] = rot(q_ref[0])
    ko_ref[0] = rot(k_ref[0])


def _rope_call(pl_all):
    bsz, s, _ = pl_all.shape
    tm = 512
    cos, sin, swap = _rope_tables(s)
    qw, kw = WIN_HEADS * HEAD_DIM, WIN_KV_HEADS * HEAD_DIM
    return pl.pallas_call(
        _rope_kernel,
        grid=(bsz, s // tm),
        in_specs=[pl.BlockSpec((1, tm, qw), lambda b, i: (b, i, COL_CQ // qw)),
                  pl.BlockSpec((1, tm, kw), lambda b, i: (b, i, COL_CK // kw)),
                  pl.BlockSpec((tm, 128), lambda b, i: (i, 0)),
                  pl.BlockSpec((tm, 128), lambda b, i: (i, 0)),
                  pl.BlockSpec((128, 128), lambda b, i: (0, 0))],
        out_specs=[pl.BlockSpec((1, tm, qw), lambda b, i: (b, i, 0)),
                   pl.BlockSpec((1, tm, kw), lambda b, i: (b, i, 0))],
        out_shape=[jax.ShapeDtypeStruct((bsz, s, qw), F32), jax.ShapeDtypeStruct((bsz, s, kw), F32)],
        compiler_params=_params(("arbitrary", "arbitrary")),
        name="rope",
    )(pl_all, pl_all, jnp.asarray(cos), jnp.asarray(sin), jnp.asarray(swap, BF16))


WIN_STEP = 4


def _win_kernel(sink_ref, q_ref, kp_ref, kc_ref, kn_ref, vp_ref, vc_ref, vn_ref, kx_ref, vx_ref, o_ref, *, seq):
    g = pl.program_id(1)
    j = pl.program_id(2)
    grp = q_ref.shape[1]
    wb = WIN_BLOCK
    m_rows = grp * wb
    k_ext = jnp.concatenate([kp_ref[0, 0], kc_ref[0, 0], kn_ref[0, 0]], axis=0).astype(BF16)
    v_ext = _with_ones(jnp.concatenate([vp_ref[0, 0], vc_ref[0, 0], vn_ref[0, 0]], axis=0).astype(BF16))
    k_ctx = kx_ref[0, 0].astype(BF16)
    v_ctx = _with_ones(vx_ref[0, 0].astype(BF16))
    sink = jnp.concatenate([jnp.full((wb, 1), sink_ref[g * grp + h], F32) for h in range(grp)], axis=0)
    row = lax.broadcasted_iota(jnp.int32, (m_rows, 3 * wb), 0) & (wb - 1)
    col = lax.broadcasted_iota(jnp.int32, (m_rows, 3 * wb), 1)
    in_band = jnp.where(jnp.abs(col - wb - row) <= WINDOW, 1.0, 0.0)
    col1 = lax.broadcasted_iota(jnp.int32, (1, 3 * wb), 1)
    for t in range(WIN_STEP):
        n = j * WIN_STEP + t
        kpos = (n - 1) * wb + col1
        in_seq = jnp.where((kpos >= 0) & (kpos < seq), 1.0, 0.0)
        q = jnp.concatenate([q_ref[0, h, t * wb:(t + 1) * wb, :] for h in range(grp)], axis=0)
        q = (q * (HEAD_DIM ** -0.5)).astype(BF16)
        s_loc = jnp.where(in_band * in_seq > 0.5, _dot_nt(q, k_ext[t * wb:(t + 3) * wb, :]), NEG)
        s_ctx = _dot_nt(q, k_ctx)
        m = jnp.maximum(jnp.max(jnp.concatenate([s_loc, s_ctx], axis=1), axis=-1, keepdims=True), sink)
        p_loc = jnp.exp(s_loc - m).astype(BF16)
        p_ctx = jnp.exp(s_ctx - m).astype(BF16)
        ov = _dot(p_loc, v_ext[t * wb:(t + 3) * wb, :]) + _dot(p_ctx, v_ctx)
        den = ov[:, HEAD_DIM:HEAD_DIM + 1] + jnp.exp(sink - m)
        o = (ov[:, :HEAD_DIM] / den).astype(o_ref.dtype)
        for h in range(grp):
            o_ref[0, h, t * wb:(t + 1) * wb, :] = o[h * wb:(h + 1) * wb, :]


def _win_call(q, k, v, kx, vx, sink):
    bsz, hq, s, dh = q.shape
    hkv = k.shape[1]
    grp = hq // hkv
    cx = kx.shape[2]
    nb = s // WIN_BLOCK
    step = WIN_STEP * WIN_BLOCK
    blk = (1, 1, WIN_BLOCK, dh)
    prev = lambda b, g, j, sk: (b, g, jnp.maximum(j * WIN_STEP - 1, 0), 0)
    cur = lambda b, g, j, sk: (b, g, j, 0)
    nxt = lambda b, g, j, sk: (b, g, jnp.minimum((j + 1) * WIN_STEP, nb - 1), 0)
    ctx = lambda b, g, j, sk: (b, g, 0, 0)
    return pl.pallas_call(
        functools.partial(_win_kernel, seq=s),
        grid_spec=pltpu.PrefetchScalarGridSpec(
            num_scalar_prefetch=1,
            grid=(bsz, hkv, nb // WIN_STEP),
            in_specs=[pl.BlockSpec((1, grp, step, dh), cur),
                      pl.BlockSpec(blk, prev), pl.BlockSpec((1, 1, step, dh), cur), pl.BlockSpec(blk, nxt),
                      pl.BlockSpec(blk, prev), pl.BlockSpec((1, 1, step, dh), cur), pl.BlockSpec(blk, nxt),
                      pl.BlockSpec((1, 1, cx, dh), ctx), pl.BlockSpec((1, 1, cx, dh), ctx)],
            out_specs=pl.BlockSpec((1, grp, step, dh), cur)),
        out_shape=jax.ShapeDtypeStruct((bsz, hq, s, dh), BF16),
        compiler_params=_params(("arbitrary", "arbitrary", "arbitrary")),
        name="mixer_window_attn",
    )(sink, q, k, k, k, v, v, v, kx, vx)


def _na_bias_table(rpb):
    qc = np.arange(GRID_W)
    kc = np.arange(GRID_W)
    cs = np.clip(qc - NA_COLS // 2, 0, GRID_W - NA_COLS)
    cmask = (kc[None, :] >= cs[:, None]) & (kc[None, :] < cs[:, None] + NA_COLS)
    ci = np.clip(kc[None, :] - qc[:, None] + NA_COLS - 1, 0, 2 * NA_COLS - 2)
    pick = (ci[None] == np.arange(2 * NA_COLS - 1)[:, None, None]).astype(np.float32)
    t15 = jnp.einsum('hrc,cqk->hrqk', rpb, jnp.asarray(pick), precision=lax.Precision.HIGHEST)
    t15 = jnp.where(cmask[None, None], t15, NEG)
    tab = jnp.stack([t15[:, d0:d0 + NA_ROWS] for d0 in range(NA_ROWS)], axis=1)
    return tab.transpose(0, 1, 3, 2, 4).reshape(rpb.shape[0], NA_ROWS, GRID_W, NA_ROWS * GRID_W)


def _na_kernel(q_ref, k_ref, v_ref, kx_ref, vx_ref, bias_ref, o_ref):
    rows = q_ref.shape[2] // GRID_W
    k_ctx = kx_ref[0, 0].astype(BF16)
    v_ctx = _with_ones(vx_ref[0, 0].astype(BF16))
    k_all = k_ref[0, 0].astype(BF16)
    v_all = _with_ones(v_ref[0, 0].astype(BF16))
    span = NA_ROWS * GRID_W
    chunk = 8
    for r0 in range(0, rows, chunk):
        q = (q_ref[0, 0, r0 * GRID_W:(r0 + chunk) * GRID_W, :] * (HEAD_DIM ** -0.5)).astype(BF16)
        s_parts = []
        for r in range(r0, r0 + chunk):
            rs = min(max(r - NA_ROWS // 2, 0), rows - NA_ROWS)
            k_loc = k_all[rs * GRID_W:rs * GRID_W + span, :]
            qr = q[(r - r0) * GRID_W:(r - r0 + 1) * GRID_W, :]
            s_parts.append(_dot_nt(qr, k_loc) + bias_ref[0, rs - r + NA_ROWS - 1])
        s_loc = jnp.concatenate(s_parts, axis=0)
        s_ctx = _dot_nt(q, k_ctx)
        m = jnp.max(jnp.concatenate([s_loc, s_ctx], axis=1), axis=-1, keepdims=True)
        p_loc = jnp.exp(s_loc - m).astype(BF16)
        p_ctx = jnp.exp(s_ctx - m).astype(BF16)
        o_parts = []
        for r in range(r0, r0 + chunk):
            rs = min(max(r - NA_ROWS // 2, 0), rows - NA_ROWS)
            v_loc = v_all[rs * GRID_W:rs * GRID_W + span, :]
            o_parts.append(_dot(p_loc[(r - r0) * GRID_W:(r - r0 + 1) * GRID_W, :], v_loc))
        ov = jnp.concatenate(o_parts, axis=0) + _dot(p_ctx, v_ctx)
        o = ov[:, :HEAD_DIM] / ov[:, HEAD_DIM:HEAD_DIM + 1]
        o_ref[0, 0, r0 * GRID_W:(r0 + chunk) * GRID_W, :] = o.astype(o_ref.dtype)


def _na_call(q, k, v, kx, vx, bias_tab):
    bsz, h, s, dh = q.shape
    cx = kx.shape[2]
    full = pl.BlockSpec((1, 1, s, dh), lambda b, hh: (b, hh, 0, 0))
    ctx = pl.BlockSpec((1, 1, cx, dh), lambda b, hh: (b, hh, 0, 0))
    return pl.pallas_call(
        _na_kernel,
        grid=(bsz, h),
        in_specs=[full, full, full, ctx, ctx,
                  pl.BlockSpec((1, NA_ROWS, GRID_W, NA_ROWS * GRID_W), lambda b, hh: (hh, 0, 0, 0))],
        out_specs=full,
        out_shape=jax.ShapeDtypeStruct((bsz, h, s, dh), BF16),
        compiler_params=_params(("arbitrary", "arbitrary")),
        name="mixer_neighbourhood_attn",
    )(q, k, v, kx, vx, bias_tab)


def _ctx_attn_kernel(sink_ref, q_ref, k_ref, v_ref, o_ref, *, use_sink):
    h = pl.program_id(1)
    q = (q_ref[0, 0] * (HEAD_DIM ** -0.5)).astype(BF16)
    s = _dot_nt(q, k_ref[0, 0].astype(BF16))
    m = jnp.max(s, axis=-1, keepdims=True)
    if use_sink:
        sink = sink_ref[h]
        m = jnp.maximum(m, sink)
    p = jnp.exp(s - m)
    den = jnp.sum(p, axis=-1, keepdims=True)
    if use_sink:
        den = den + jnp.exp(sink - m)
    o_ref[0, 0] = (_dot(p.astype(BF16), v_ref[0, 0].astype(BF16)) / den).astype(o_ref.dtype)


def _ctx_attn_call(q, k, v, sink):
    bsz, hq, cx, dh = q.shape
    grp = hq // k.shape[1]
    use_sink = sink is not None
    if sink is None:
        sink = jnp.zeros((hq,), F32)
    qs = pl.BlockSpec((1, 1, cx, dh), lambda b, h, sk: (b, h, 0, 0))
    ks = pl.BlockSpec((1, 1, cx, dh), lambda b, h, sk: (b, h // grp, 0, 0))
    return pl.pallas_call(
        functools.partial(_ctx_attn_kernel, use_sink=use_sink),
        grid_spec=pltpu.PrefetchScalarGridSpec(
            num_scalar_prefetch=1, grid=(bsz, hq), in_specs=[qs, ks, ks], out_specs=qs),
        out_shape=jax.ShapeDtypeStruct((bsz, hq, cx, dh), BF16),
        compiler_params=_params(("arbitrary", "arbitrary")),
        name="context_attn",
    )(sink, q, k, v)


def _merge_kernel(x_ref, ya_ref, yb_ref, yc_ref, yd_ref, wg0_ref, wg1_ref, wg2_ref, wg3_ref, wbr_ref, wout_ref,
                  ng_ref, sh_ref, sc_ref, gt_ref, o_ref, h_ref, acc_ref):
    n = pl.program_id(2)
    nt = acc_ref.shape[0]
    d = o_ref.shape[2]

    @pl.when(n == 0)
    def _():
        h = _rms(x_ref[0], ng_ref[0:1, :]) * (1.0 + sc_ref[0]) + sh_ref[0]
        h_ref[...] = h.astype(BF16)

    h = h_ref[...]
    acc = None
    for i, (y_ref, wg_ref) in enumerate(((ya_ref, wg0_ref), (yb_ref, wg1_ref), (yc_ref, wg2_ref), (yd_ref, wg3_ref))):
        t = _sigmoid(_dot(h, wg_ref[...])) * _dot(y_ref[0], wbr_ref[i])
        acc = t if acc is None else acc + t
    acc_ref[n] = acc.astype(BF16)

    @pl.when(n == nt - 1)
    def _():
        a = jnp.concatenate([acc_ref[j] for j in range(nt)], axis=1)
        oc = d // 4
        for c in range(4):
            o_ref[0, :, c * oc:(c + 1) * oc] = _dot(a, wout_ref[:, c * oc:(c + 1) * oc])
        o_ref[0] = x_ref[0] + gt_ref[0] * _rms(o_ref[0], ng_ref[1:2, :])


def _merge_call(x2, bsz, s, row_off, ys, w_gate, w_branch, w_out, norm_g, mods, shared_row):
    d = x2.shape[1]
    c = BRANCH_DIM
    tm = min(s, 512)
    tn = 256
    nt = d // tn
    x = x2.reshape(-1, tm, d)
    base, per = row_off // tm, s // tm
    row = (lambda b: b) if shared_row is None else (lambda b: shared_row)
    ysp = pl.BlockSpec((1, tm, c), lambda b, i, n: (b, i, 0))
    wg = [pl.BlockSpec((d, tn), functools.partial(lambda b, i, n, k: (0, k * nt + n), k=k)) for k in range(4)]
    mod = lambda col: pl.BlockSpec((1, 1, d), lambda b, i, n: (row(b), 0, col))
    return pl.pallas_call(
        _merge_kernel,
        grid=(bsz, s // tm, nt),
        in_specs=[pl.BlockSpec((1, tm, d), lambda b, i, n: (base + b * per + i, 0, 0)), ysp, ysp, ysp, ysp, *wg,
                  pl.BlockSpec((4, c, tn), lambda b, i, n: (0, 0, n)),
                  pl.BlockSpec((d, d), lambda b, i, n: (0, 0), pipeline_mode=pl.Buffered(1)),
                  pl.BlockSpec((4, d), lambda b, i, n: (0, 0)),
                  mod(0), mod(1), mod(2)],
        out_specs=pl.BlockSpec((1, tm, d), lambda b, i, n: (b * per + i, 0, 0)),
        out_shape=jax.ShapeDtypeStruct((bsz * per, tm, d), F32),
        scratch_shapes=[pltpu.VMEM((tm, d), BF16), pltpu.VMEM((nt, tm, tn), BF16)],
        compiler_params=_params(("arbitrary", "arbitrary", "arbitrary")),
        name="mixer_merge",
    )(x, *ys, w_gate, w_gate, w_gate, w_gate, w_branch, w_out, norm_g, mods, mods, mods).reshape(bsz * s, d)


def _expert_row_ids():
    r = np.arange(N_EXPERTS)
    return ((r % N_GROUPS) * GROUP_SIZE + r // N_GROUPS).astype(np.int32)


def _router_kernel(rows_ref, xa_ref, xb_ref, ng_ref, sh_ref, sc_ref, rw_ref, rb_ref, eid_ref, tri_t_ref,
                   sw1_ref, sw3_ref, sw2_ref, h_ref, so_ref, e_ref, r_ref, w_ref, cnt_ref, *, na):
    i = pl.program_id(0)
    tm = xa_ref.shape[0]

    @pl.when(i == 0)
    def _():
        cnt_ref[...] = jnp.zeros(cnt_ref.shape, F32)

    x = jnp.where(i < na, xa_ref[...], xb_ref[...])
    h = _rms(x, ng_ref[2:3, :]) * (1.0 + sc_ref[0]) + sh_ref[0]
    h_ref[:, 0, :] = _pack_halves(h)
    hb = h.astype(BF16)
    so_ref[...] = _dot((_silu(_dot(hb, sw1_ref[...])) * _dot(hb, sw3_ref[...])).astype(BF16), sw2_ref[...])

    h_tail = (h - hb.astype(F32)).astype(BF16)
    head = _dot_nt(rw_ref[...], hb)
    logits = head[:N_EXPERTS] + head[N_EXPERTS:] + _dot_nt(rw_ref[:N_EXPERTS, :], h_tail)
    scores = _sigmoid(logits)
    biased = scores + rb_ref[...]
    m1 = jnp.full((N_GROUPS, tm), -jnp.inf, F32)
    m2 = m1
    for j in range(GROUP_SIZE):
        v = biased[j * N_GROUPS:(j + 1) * N_GROUPS, :]
        m2 = jnp.maximum(m2, jnp.minimum(m1, v))
        m1 = jnp.maximum(m1, v)
    gs = m1 + m2
    gid = lax.broadcasted_iota(jnp.int32, (N_GROUPS, tm), 0)
    beat = jnp.zeros((N_GROUPS, tm), jnp.int32)
    for g in range(N_GROUPS):
        o = gs[g:g + 1, :]
        beat = beat + jnp.where((o > gs) | ((o == gs) & (g < gid)), 1, 0)
    keep = beat < TOPK_GROUPS
    masked = jnp.concatenate(
        [jnp.where(keep, biased[j * N_GROUPS:(j + 1) * N_GROUPS, :], NEG) for j in range(GROUP_SIZE)], axis=0)
    eidf = eid_ref[...].astype(F32)
    work = masked
    picks = []
    self32 = jnp.zeros((N_EXPERTS, tm), F32)
    for k in range(TOP_K):
        top = jnp.max(work, axis=0, keepdims=True)
        pick = jnp.min(jnp.where(work == top, eidf, float(N_EXPERTS)), axis=0, keepdims=True)
        hit = eidf == pick
        work = jnp.where(hit, -jnp.inf, work)
        self32 = jnp.where(hit, 1.0, self32)
        picks.append(pick)
    selb = self32.astype(BF16)
    wsel = self32 * scores
    wsel = wsel / jnp.sum(wsel, axis=0, keepdims=True) * ROUTE_SCALE
    rank = cnt_ref[:, 0:1] + _dot(selb, tri_t_ref[...])
    cnt_ref[...] = cnt_ref[...] + jnp.sum(self32, axis=1, keepdims=True)
    for k in range(TOP_K):
        hit = eidf == picks[k]
        e_ref[k:k + 1, :] = picks[k].astype(jnp.int32)
        r_ref[k:k + 1, :] = jnp.sum(jnp.where(hit, rank, 0.0), axis=0, keepdims=True).astype(jnp.int32)
        w_ref[k:k + 1, :] = jnp.sum(jnp.where(hit, wsel, 0.0), axis=0, keepdims=True)


def _two_sources(xa, xb, tm):
    na = xa.shape[0] // tm
    d = xa.shape[1]
    if xb is None:
        xb = xa
    nb_last = xb.shape[0] // tm - 1
    spec_a = pl.BlockSpec((tm, d), lambda i, rows: (jnp.minimum(i, na - 1), 0))
    spec_b = pl.BlockSpec((tm, d), lambda i, rows: (jnp.clip(i - na, 0, nb_last), 0))
    return na, xb, spec_a, spec_b


def _router_call(xa, xb, tile_rows, norm_g, mods, router_w, router_b, sw1, sw3, sw2):
    d = xa.shape[1]
    t = xa.shape[0] + (0 if xb is None else xb.shape[0])
    tm = 256
    na, xb, spec_a, spec_b = _two_sources(xa, xb, tm)
    f = sw1.shape[1]
    ids = _expert_row_ids()
    rw = router_w.T[ids]
    rw_head = rw.astype(BF16)
    rw = jnp.concatenate([rw_head, (rw - rw_head.astype(F32)).astype(BF16)], axis=0)
    rb = router_b[ids].reshape(N_EXPERTS, 1)
    tri_t = jnp.asarray(np.triu(np.ones((tm, tm), np.float32), 1), BF16)
    const = lambda shape: pl.BlockSpec(shape, lambda i, rows: (0,) * len(shape))
    tok = pl.BlockSpec((tm, d), lambda i, rows: (i, 0))
    slots = pl.BlockSpec((TOP_K, tm), lambda i, rows: (0, i))
    return pl.pallas_call(
        functools.partial(_router_kernel, na=na),
        grid_spec=pltpu.PrefetchScalarGridSpec(
            num_scalar_prefetch=1,
            grid=(t // tm,),
            in_specs=[spec_a, spec_b, const((4, d)),
                      pl.BlockSpec((1, 1, d), lambda i, rows: (rows[i], 0, 3)),
                      pl.BlockSpec((1, 1, d), lambda i, rows: (rows[i], 0, 4)),
                      const((2 * N_EXPERTS, d)), const((N_EXPERTS, 1)), const((N_EXPERTS, 1)),
                      const((tm, tm)),
                      const((d, f)), const((d, f)), const((f, d))],
            out_specs=[pl.BlockSpec((tm, 1, d // 2), lambda i, rows: (i, 0, 0)), tok, slots, slots, slots,
                       const((N_EXPERTS, 128))]),
        out_shape=[jax.ShapeDtypeStruct((t, 1, d // 2), jnp.uint32), jax.ShapeDtypeStruct((t, d), F32),
                   jax.ShapeDtypeStruct((TOP_K, t), jnp.int32), jax.ShapeDtypeStruct((TOP_K, t), jnp.int32),
                   jax.ShapeDtypeStruct((TOP_K, t), F32), jax.ShapeDtypeStruct((N_EXPERTS, 128), F32)],
        compiler_params=_params(("arbitrary",)),
        name="moe_router",
    )(tile_rows, xa, xb, norm_g, mods, mods, rw, rb, jnp.asarray(ids).reshape(N_EXPERTS, 1), tri_t,
      sw1, sw3, sw2)


def _gmm_kernel(blk_ref, exp_ref, lo_ref, hi_ref, first_ref, newexp_ref, tok_cur_ref, tok_nxt_ref, h_hbm,
                w1_ref, w3_ref, w2_ref, o_ref, xa_ref, xb_ref, w1b_ref, w3b_ref, w2b_ref, sem):
    i = pl.program_id(0)
    last = pl.num_programs(0) - 1
    nrows = xa_ref.shape[0]
    xbufs = (xa_ref, xb_ref)

    def start_gather(tok_ref, s):
        for r in range(nrows):
            pltpu.make_async_copy(h_hbm.at[tok_ref[0, 0, r]], xbufs[s].at[pl.ds(r, 1)],
                                  sem.at[s]).start(priority=r % 2)

    def wait_gather(s):
        for r in range(nrows):
            pltpu.make_async_copy(h_hbm.at[0], xbufs[s].at[pl.ds(r, 1)], sem.at[s]).wait()

    @pl.when(i == 0)
    def _():
        start_gather(tok_cur_ref, 0)

    @pl.when(first_ref[i] == 1)
    def _():
        o_ref[...] = jnp.zeros(o_ref.shape, o_ref.dtype)

    @pl.when(newexp_ref[i] == 1)
    def _():
        w1b_ref[...] = w1_ref[0].astype(BF16)
        w3b_ref[...] = w3_ref[0].astype(BF16)
        w2b_ref[...] = w2_ref[0].astype(BF16)

    def step(s):
        wait_gather(s)

        @pl.when(lo_ref[i] >= 0)
        def _():
            start_gather(tok_nxt_ref, 1 - s)

        x_lo, x_hi = _unpack_halves(xbufs[s][...])
        x = jnp.concatenate([x_lo.astype(BF16), x_hi.astype(BF16)], axis=1)
        a = (_silu(_dot(x, w1b_ref[...])) * _dot(x, w3b_ref[...])).astype(BF16)
        y = _pack_halves(_dot(a, w2b_ref[...]))
        rows = lax.broadcasted_iota(jnp.int32, (nrows, 1), 0)
        keep = (rows >= lo_ref[i]) & (rows < hi_ref[i])
        o_ref[:, 0, :] = jnp.where(keep, y, o_ref[:, 0, :])

        @pl.when(i == last)
        def _():
            wait_gather(1 - s)

    for s in range(2):
        pl.when(i % 2 == s)(functools.partial(step, s))


def _gmm_call(items, slot_tok, hp, w1, w3, w2, layer):
    a = slot_tok.shape[0]
    d, f = w1.shape[2], w1.shape[3]
    n_items = items[0].shape[0]
    nb = a // MOE_BLOCK
    tok3 = slot_tok.reshape(nb, 1, MOE_BLOCK)
    rows = pl.BlockSpec((MOE_BLOCK,) + hp.shape[1:], lambda i, blk, ex, *_: (blk[i], 0, 0))
    toks = lambda step: pl.BlockSpec((1, 1, MOE_BLOCK), lambda i, blk, ex, *_: (blk[step(i)], 0, 0),
                                     memory_space=pltpu.SMEM)
    return pl.pallas_call(
        _gmm_kernel,
        grid_spec=pltpu.PrefetchScalarGridSpec(
            num_scalar_prefetch=len(items),
            grid=(n_items,),
            in_specs=[toks(lambda i: i), toks(lambda i: jnp.minimum(i + 1, n_items - 1)),
                      pl.BlockSpec(memory_space=pl.ANY),
                      pl.BlockSpec((None, 1, d, f), lambda i, blk, ex, *_: (layer, ex[i], 0, 0)),
                      pl.BlockSpec((None, 1, d, f), lambda i, blk, ex, *_: (layer, ex[i], 0, 0)),
                      pl.BlockSpec((None, 1, f, d), lambda i, blk, ex, *_: (layer, ex[i], 0, 0))],
            out_specs=rows,
            scratch_shapes=[pltpu.VMEM((MOE_BLOCK, hp.shape[2]), jnp.uint32),
                            pltpu.VMEM((MOE_BLOCK, hp.shape[2]), jnp.uint32),
                            pltpu.VMEM((d, f), BF16), pltpu.VMEM((d, f), BF16), pltpu.VMEM((f, d), BF16),
                            pltpu.SemaphoreType.DMA((2,))]),
        out_shape=jax.ShapeDtypeStruct((a,) + hp.shape[1:], jnp.uint32),
        compiler_params=_params(("arbitrary",)),
        name="moe_experts",
    )(*items, tok3, tok3, hp, w1, w3, w2)


def _moe_items(counts, n_rows):
    nb = n_rows // MOE_BLOCK
    n_items = nb + N_EXPERTS
    ends = jnp.cumsum(counts).astype(jnp.int32)
    bstart = jnp.arange(nb, dtype=jnp.int32) * MOE_BLOCK
    pos_b = jnp.arange(nb, dtype=jnp.int32) + jnp.sum(ends[None, :] < bstart[:, None], axis=1)
    idx = jnp.arange(n_items, dtype=jnp.int32)
    n_b = jnp.sum(pos_b[None, :] <= idx[:, None], axis=1).astype(jnp.int32)
    n_e = idx + 1 - n_b
    ends0 = jnp.concatenate([jnp.zeros((1,), jnp.int32), ends])
    last_end = jnp.sum(jnp.where(n_e[:, None] == jnp.arange(N_EXPERTS + 1)[None, :], ends0[None, :], 0), axis=1)
    start = jnp.maximum((n_b - 1) * MOE_BLOCK, last_end).astype(jnp.int32)
    stop = jnp.concatenate([start[1:], jnp.full((1,), n_rows, jnp.int32)])
    blk = jnp.minimum(start // MOE_BLOCK, nb - 1)
    ex = jnp.minimum(jnp.sum(ends[None, :] <= start[:, None], axis=1), N_EXPERTS - 1).astype(jnp.int32)
    lo = start - blk * MOE_BLOCK
    hi = stop - blk * MOE_BLOCK
    one = jnp.ones((1,), jnp.int32)
    first = jnp.concatenate([one, (blk[1:] != blk[:-1]).astype(jnp.int32)])
    newexp = jnp.concatenate([one, (ex[1:] != ex[:-1]).astype(jnp.int32)])
    return blk.astype(jnp.int32), ex, lo.astype(jnp.int32), hi.astype(jnp.int32), first, newexp


def _combine_kernel(rows_ref, dest_ref, dest_nxt_ref, ys_hbm, w_ref, so_ref, xa_ref, xb_ref, ng_ref, gt_ref, o_ref,
                    buf_ref, sem, *, na):
    i = pl.program_id(0)
    tc = xa_ref.shape[0]
    slot = i % 2

    def request(idx_ref, s, j, k):
        pltpu.make_async_copy(ys_hbm.at[idx_ref[0, 0, k * tc + j]], buf_ref.at[s, k, pl.ds(j, 1)],
                              sem.at[s]).start(priority=k % 2)

    @pl.when(i == 0)
    def _():
        def body(j, c):
            for k in range(TOP_K):
                request(dest_ref, 0, j, k)
            return c
        lax.fori_loop(0, tc, body, 0)

    for s in range(2):
        @pl.when((i < pl.num_programs(0) - 1) & (slot == 1 - s))
        def _(s=s):
            for j in range(tc):
                for k in range(TOP_K):
                    request(dest_nxt_ref, s, j, k)

    pltpu.make_async_copy(buf_ref.at[slot], buf_ref.at[slot], sem.at[slot]).wait()
    half = xa_ref.shape[1] // 2
    acc_lo = so_ref[:, :half]
    acc_hi = so_ref[:, half:]
    for k in range(TOP_K):
        y_lo, y_hi = _unpack_halves(buf_ref[slot, k])
        wk = w_ref[:, k:k + 1]
        acc_lo = acc_lo + y_lo * wk
        acc_hi = acc_hi + y_hi * wk
    fl = jnp.concatenate([acc_lo, acc_hi], axis=1)
    x = jnp.where(i < na, xa_ref[...], xb_ref[...])
    o_ref[...] = x + gt_ref[0] * _rms(fl, ng_ref[3:4, :])


def _combine_call(tile_rows, dest, ys, wsel, shared_out, xa, xb, norm_g, mods):
    t, d = shared_out.shape
    tc = 128
    dest3 = dest.reshape(TOP_K, t // tc, tc).transpose(1, 0, 2).reshape(t // tc, 1, TOP_K * tc)
    tok = pl.BlockSpec((tc, d), lambda i, rows: (i, 0))
    nt = t // tc
    na, xb, spec_a, spec_b = _two_sources(xa, xb, tc)
    return pl.pallas_call(
        functools.partial(_combine_kernel, na=na),
        grid_spec=pltpu.PrefetchScalarGridSpec(
            num_scalar_prefetch=1,
            grid=(nt,),
            in_specs=[pl.BlockSpec((1, 1, TOP_K * tc), lambda i, rows: (i, 0, 0), memory_space=pltpu.SMEM),
                      pl.BlockSpec((1, 1, TOP_K * tc), lambda i, rows: (jnp.minimum(i + 1, nt - 1), 0, 0),
                                   memory_space=pltpu.SMEM),
                      pl.BlockSpec(memory_space=pl.ANY),
                      pl.BlockSpec((tc, TOP_K), lambda i, rows: (i, 0)),
                      tok, spec_a, spec_b,
                      pl.BlockSpec((4, d), lambda i, rows: (0, 0)),
                      pl.BlockSpec((1, 1, d), lambda i, rows: (rows[i], 0, 5))],
            out_specs=tok,
            scratch_shapes=[pltpu.VMEM((2, TOP_K, tc, d // 2), jnp.uint32), pltpu.SemaphoreType.DMA((2,))]),
        out_shape=jax.ShapeDtypeStruct((t, d), F32),
        compiler_params=_params(("arbitrary",)),
        name="moe_combine",
    )(tile_rows, dest3, dest3, ys, wsel.T, shared_out, xa, xb, norm_g, mods)


def _channel_mixer(xa, xb, mod_rows, norm_g, mods, router_w, router_b, w1, w3, w2, layer, sw1, sw3, sw2):
    t = xa.shape[0] + (0 if xb is None else xb.shape[0])
    h, shared_out, eidx, rank, wsel, cnt = _router_call(xa, xb, mod_rows[::2], norm_g, mods, router_w, router_b,
                                                         sw1, sw3, sw2)
    counts = jnp.zeros((N_EXPERTS,), jnp.int32).at[_expert_row_ids()].set(cnt[:, 0].astype(jnp.int32))
    starts = jnp.cumsum(counts) - counts
    sel = eidx[:, :, None] == jnp.arange(N_EXPERTS, dtype=jnp.int32)[None, None, :]
    dest = jnp.sum(jnp.where(sel, starts[None, None, :], 0), axis=-1) + rank
    slot_tok = (jnp.argsort(dest.reshape(-1)) % t).astype(jnp.int32)
    ys = _gmm_call(_moe_items(counts, t * TOP_K), slot_tok, h, w1, w3, w2, layer)
    return _combine_call(mod_rows, dest, ys, wsel, shared_out, xa, xb, norm_g, mods)


def _heads(t, n):
    b, s, _ = t.shape
    return t.reshape(b, s, n, HEAD_DIM).transpose(0, 2, 1, 3)


def _unheads(t):
    b, n, s, dh = t.shape
    return t.transpose(0, 2, 1, 3).reshape(b, s, n * dh)


def kernel(x, c, ctx, c_ctx, w_ada, b_ada, norm_g, w_in, conv_b_in, conv_dw, conv_dw_b, conv_ln_g, conv_ln_b, gmlp_ln_g, gmlp_ln_b, gmlp_ws, gmlp_bs, win_sink, na_rpb, w_branch, w_out, router_w, router_b, exp_w1, exp_w3, exp_w2, sh_w1, sh_w3, sh_w2):
    bsz, s, d = x.shape
    cx = ctx.shape[1]
    depth = w_ada.shape[0]
    ctx_row = bsz
    c_all = jnp.zeros((MODS_ROWS, d), F32).at[:bsz].set(c).at[ctx_row].set(c_ctx)
    mods_all = _ada_call(c_all, w_ada, b_ada)
    lat_rows = jnp.repeat(jnp.arange(bsz, dtype=jnp.int32), s // 128)
    ctx_rows = jnp.full((bsz * cx // 128,), ctx_row, jnp.int32)
    xl, xl_off = x.reshape(bsz * s, d), 0
    xc, xc_off = ctx.reshape(bsz * cx, d), 0
    for l in range(depth):
        last = l == depth - 1
        mods = mods_all[l].reshape(MODS_ROWS, 1, N_MODS * d)
        g_off = w_in.shape[2] - 4 * d
        kv_off = g_off - KV_COLS
        w_small = jnp.concatenate([w_in[l, :, :kv_off], w_in[l, :, kv_off + 256:g_off],
                                   w_in[l, :, kv_off:kv_off + 256]], axis=1).astype(BF16)
        w_gate = w_in[l, :, g_off:].astype(BF16)
        w_br = w_branch[l].astype(BF16)
        w_o = w_out[l].astype(BF16)
        pl_l = _proj_call(xl, bsz, s, xl_off, norm_g[l], mods, w_small, None, SMALL_COLS // 2)
        if last:
            pkv_c = _proj_call(xc, bsz, cx, xc_off, norm_g[l], mods, w_small[:, COL_DK:], ctx_row, KV_COLS)
            kv_base = 0
        else:
            pc = _proj_call(xc, bsz, cx, xc_off, norm_g[l], mods, w_small, ctx_row, SMALL_COLS // 2)
            pkv_c, kv_base = pc, COL_DK
        dkc = _heads(pkv_c[..., kv_base:kv_base + 512], NA_HEADS)
        dvc = _heads(pkv_c[..., kv_base + 512:kv_base + 1024], NA_HEADS)
        ckc = _heads(pkv_c[..., kv_base + 1024:kv_base + 1152], WIN_KV_HEADS)
        cvc = _heads(pkv_c[..., kv_base + 1152:kv_base + 1280], WIN_KV_HEADS)
        conv_args = (conv_b_in[l], conv_dw[l], conv_dw_b[l], conv_ln_g[l], conv_ln_b[l])
        gmlp_args = (gmlp_ln_g[l], gmlp_ln_b[l], gmlp_ws[l], gmlp_bs[l])
        ya = _conv_call(pl_l, *conv_args)
        yb = _gmlp_call(pl_l, *gmlp_args)
        q_rope, k_rope = _rope_call(pl_l)
        yc = _unheads(_win_call(_heads(q_rope, WIN_HEADS), _heads(k_rope, WIN_KV_HEADS),
                                _heads(pl_l[..., COL_CV:COL_CV + 128], WIN_KV_HEADS), ckc, cvc, win_sink[l]))
        yd = _unheads(_na_call(_heads(pl_l[..., COL_DQ:COL_DQ + 512], NA_HEADS),
                               _heads(pl_l[..., COL_DK:COL_DK + 512], NA_HEADS),
                               _heads(pl_l[..., COL_DV:COL_DV + 512], NA_HEADS), dkc, dvc,
                               _na_bias_table(na_rpb[l])))
        xl_mid = _merge_call(xl, bsz, s, xl_off, (ya, yb, yc, yd), w_gate, w_br, w_o, norm_g[l], mods, None)
        moe_w = (router_w[l], router_b[l], exp_w1, exp_w3, exp_w2, l,
                 sh_w1[l].astype(BF16), sh_w3[l].astype(BF16), sh_w2[l].astype(BF16))
        if last:
            xl = _channel_mixer(xl_mid, None, lat_rows, norm_g[l], mods, *moe_w)
        else:
            yca = _conv_call(pc, *conv_args)
            ycb = _gmlp_call(pc, *gmlp_args)
            ycc = _unheads(_ctx_attn_call(_heads(pc[..., COL_CQ:COL_CQ + 512], WIN_HEADS), ckc, cvc, win_sink[l]))
            ycd = _unheads(_ctx_attn_call(_heads(pc[..., COL_DQ:COL_DQ + 512], NA_HEADS), dkc, dvc, None))
            pair = 2 if bsz % 2 == 0 else 1
            ys_c = tuple(y.reshape(bsz // pair, pair * cx, BRANCH_DIM) for y in (yca, ycb, ycc, ycd))
            xc_mid = _merge_call(xc, bsz // pair, pair * cx, xc_off, ys_c, w_gate, w_br, w_o, norm_g[l], mods,
                                 ctx_row)
            out = _channel_mixer(xl_mid, xc_mid, jnp.concatenate([lat_rows, ctx_rows]), norm_g[l], mods, *moe_w)
            xl, xl_off = out, 0
            xc, xc_off = out, bsz * s
    return xl.reshape(bsz, s, d)
```

```python
import functools

import numpy as np
import jax
import jax.numpy as jnp
from jax import lax
from jax.experimental import pallas as pl
from jax.experimental.pallas import tpu as pltpu

F32 = jnp.float32
BF16 = jnp.bfloat16

GRID_W = 64
HEAD_DIM = 64
EPS = 1e-6
NEG = -1e30
BRANCH_DIM = 512
CONV_K = 31
CONV_HALO = 16
GMLP_CHUNK = 128
GMLP_GROUPS = 4
WIN_HEADS = 8
WIN_KV_HEADS = 2
WINDOW = 128
WIN_BLOCK = 128
ROPE_BASE = 10000.0
NA_HEADS = 8
NA_ROWS = 8
NA_COLS = 16
N_EXPERTS = 64
TOP_K = 8
N_GROUPS = 8
GROUP_SIZE = N_EXPERTS // N_GROUPS
TOPK_GROUPS = 4
ROUTE_SCALE = 2.5
MOE_BLOCK = 256
N_MODS = 6
MODS_ROWS = 16
VMEM_LIMIT = 52 * 1024 * 1024

COL_A, COL_B, COL_CQ, COL_DQ, COL_DK, COL_DV, COL_CK, COL_CV = 0, 1024, 2048, 2560, 3072, 3584, 4096, 4224
SMALL_COLS = 4352
KV_COLS = 1280


def _sigmoid(x):
    return 1.0 / (1.0 + jnp.exp(-x))


def _silu(x):
    return x * _sigmoid(x)


def _gelu_tanh(x):
    return 0.5 * x * (1.0 + jnp.tanh(np.sqrt(2.0 / np.pi).astype(np.float32) * (x + 0.044715 * (x * x * x))))


def _rms(x, g):
    return x * lax.rsqrt(jnp.mean(x * x, axis=-1, keepdims=True) + EPS) * g


def _ln(x, g, b):
    mu = jnp.mean(x, axis=-1, keepdims=True)
    xc = x - mu
    var = jnp.mean(xc * xc, axis=-1, keepdims=True)
    return xc * lax.rsqrt(var + EPS) * g + b


def _dot(a, b):
    return jnp.dot(a, b, preferred_element_type=F32)


def _dot_nt(a, b):
    return lax.dot_general(a, b, (((1,), (1,)), ((), ())), preferred_element_type=F32)


def _pack_halves(x):
    n = x.shape[1] // 2
    lo = lax.bitcast_convert_type(x[:, :n].astype(BF16).astype(F32), jnp.uint32)
    hi = lax.bitcast_convert_type(x[:, n:].astype(BF16).astype(F32), jnp.uint32)
    return (hi & jnp.uint32(0xFFFF0000)) | (lo >> 16)


def _unpack_halves(w):
    lo = lax.bitcast_convert_type(w << 16, F32)
    hi = lax.bitcast_convert_type(w & jnp.uint32(0xFFFF0000), F32)
    return lo, hi


def _with_ones(v):
    return jnp.concatenate([v, jnp.ones_like(v)], axis=1)


def _params(sem):
    return pltpu.CompilerParams(dimension_semantics=sem, vmem_limit_bytes=VMEM_LIMIT)


def _ada_kernel(c_ref, w_ref, b_ref, o_ref):
    s = _silu(c_ref[...])
    o_ref[0] = _dot(s.astype(BF16), w_ref[0].astype(BF16)) + b_ref[0]


def _ada_call(c_all, w_ada, b_ada):
    depth, d, n = w_ada.shape
    tn = 512
    return pl.pallas_call(
        _ada_kernel,
        grid=(depth, n // tn),
        in_specs=[pl.BlockSpec((MODS_ROWS, d), lambda l, j: (0, 0)),
                  pl.BlockSpec((1, d, tn), lambda l, j: (l, 0, j)),
                  pl.BlockSpec((1, 1, tn), lambda l, j: (l, 0, j))],
        out_specs=pl.BlockSpec((1, MODS_ROWS, tn), lambda l, j: (l, 0, j)),
        out_shape=jax.ShapeDtypeStruct((depth, MODS_ROWS, n), F32),
        compiler_params=_params(("arbitrary", "arbitrary")),
        name="ada_mods",
    )(c_all, w_ada, b_ada.reshape(depth, 1, n))


def _proj_kernel(x_ref, g_ref, sh_ref, sc_ref, w_ref, o_ref):
    h = _rms(x_ref[0], g_ref[0:1, :]) * (1.0 + sc_ref[0]) + sh_ref[0]
    o_ref[0] = _dot(h.astype(BF16), w_ref[...])


def _proj_call(x2, bsz, s, row_off, norm_g, mods, w, shared_row, tn):
    d = x2.shape[1]
    n = w.shape[1]
    tm = min(s, 512)
    x = x2.reshape(-1, tm, d)
    base, per = row_off // tm, s // tm
    row = (lambda b: b) if shared_row is None else (lambda b: shared_row)
    return pl.pallas_call(
        _proj_kernel,
        grid=(n // tn, bsz, s // tm),
        in_specs=[pl.BlockSpec((1, tm, d), lambda j, b, i: (base + b * per + i, 0, 0)),
                  pl.BlockSpec((4, d), lambda j, b, i: (0, 0)),
                  pl.BlockSpec((1, 1, d), lambda j, b, i: (row(b), 0, 0)),
                  pl.BlockSpec((1, 1, d), lambda j, b, i: (row(b), 0, 1)),
                  pl.BlockSpec((d, tn), lambda j, b, i: (0, j))],
        out_specs=pl.BlockSpec((1, tm, tn), lambda j, b, i: (b, i, j)),
        out_shape=jax.ShapeDtypeStruct((bsz, s, n), F32),
        compiler_params=_params(("arbitrary", "arbitrary", "arbitrary")),
        name="proj_in",
    )(x, norm_g, mods, mods, w)


def _conv_kernel(a_ref, ap_ref, an_ref, bin_ref, dw_ref, dwb_ref, g_ref, b_ref, o_ref, ypad_ref):
    i = pl.program_id(1)
    nblk = pl.num_programs(1)
    ts = a_ref.shape[1]
    c = BRANCH_DIM

    def glu(a):
        a = a + bin_ref[...]
        return a[:, :c] * _sigmoid(a[:, c:])

    ypad_ref[0:CONV_HALO, :] = jnp.where(i > 0, glu(ap_ref[0]), 0.0)
    ypad_ref[CONV_HALO:CONV_HALO + ts, :] = glu(a_ref[0])
    ypad_ref[CONV_HALO + ts:2 * CONV_HALO + ts, :] = jnp.where(i < nblk - 1, glu(an_ref[0]), 0.0)
    rc = 64
    off = CONV_HALO - CONV_K // 2
    sub = 8
    for r0 in range(0, ts, rc):
        acc = None
        for r in range(sub):
            z = None
            for base in range(0, off + CONV_K, sub):
                j = base + r - off
                if 0 <= j < CONV_K:
                    term = ypad_ref[r0 + base:r0 + base + rc + sub, :] * dw_ref[j:j + 1, :]
                    z = term if z is None else z + term
            if z is not None:
                z = z[r:r + rc, :]
                acc = z if acc is None else acc + z
        y = _ln(acc + dwb_ref[...], g_ref[...], b_ref[...])
        o_ref[0, r0:r0 + rc, :] = _silu(y).astype(o_ref.dtype)


def _conv_call(pl_all, b_in, dw, dw_b, ln_g, ln_b):
    bsz, s, _ = pl_all.shape
    ts = min(s, 512)
    c = BRANCH_DIM
    hb = ts // CONV_HALO
    last = s // CONV_HALO - 1
    dw_pad = jnp.zeros((32, c), F32).at[:CONV_K].set(dw)
    return pl.pallas_call(
        _conv_kernel,
        grid=(bsz, s // ts),
        in_specs=[pl.BlockSpec((1, ts, 2 * c), lambda b, i: (b, i, 0)),
                  pl.BlockSpec((1, CONV_HALO, 2 * c), lambda b, i: (b, jnp.maximum(i * hb - 1, 0), 0)),
                  pl.BlockSpec((1, CONV_HALO, 2 * c), lambda b, i: (b, jnp.minimum((i + 1) * hb, last), 0)),
                  pl.BlockSpec((1, 2 * c), lambda b, i: (0, 0)),
                  pl.BlockSpec((32, c), lambda b, i: (0, 0)),
                  pl.BlockSpec((1, c), lambda b, i: (0, 0)),
                  pl.BlockSpec((1, c), lambda b, i: (0, 0)),
                  pl.BlockSpec((1, c), lambda b, i: (0, 0))],
        out_specs=pl.BlockSpec((1, ts, c), lambda b, i: (b, i, 0)),
        out_shape=jax.ShapeDtypeStruct((bsz, s, c), BF16),
        scratch_shapes=[pltpu.VMEM((ts + 2 * CONV_HALO, c), F32)],
        compiler_params=_params(("arbitrary", "arbitrary")),
        name="mixer_conv",
    )(pl_all, pl_all, pl_all, b_in.reshape(1, 2 * c), dw_pad, dw_b.reshape(1, c), ln_g.reshape(1, c), ln_b.reshape(1, c))


def _gmlp_kernel(z_ref, g_ref, b_ref, ws_ref, bs_ref, o_ref):
    c = BRANCH_DIM
    gw = c // GMLP_GROUPS
    ts = z_ref.shape[1]
    for n in range(ts // GMLP_CHUNK):
        rows = slice(n * GMLP_CHUNK, (n + 1) * GMLP_CHUNK)
        z = _gelu_tanh(z_ref[0, rows, :])
        u = z[:, :c]
        v = _ln(z[:, c:], g_ref[...], b_ref[...]).astype(BF16)
        for g in range(GMLP_GROUPS):
            cols = slice(g * gw, (g + 1) * gw)
            sg = _dot(ws_ref[g], v[:, cols]) + bs_ref[:, g:g + 1]
            o_ref[0, rows, cols] = (u[:, cols] * sg).astype(o_ref.dtype)


def _gmlp_call(pl_all, ln_g, ln_b, ws, bs):
    bsz, s, _ = pl_all.shape
    ts = min(s, 512)
    c = BRANCH_DIM
    return pl.pallas_call(
        _gmlp_kernel,
        grid=(bsz, s // ts),
        in_specs=[pl.BlockSpec((1, ts, 2 * c), lambda b, i: (b, i, COL_B // (2 * c))),
                  pl.BlockSpec((1, c), lambda b, i: (0, 0)),
                  pl.BlockSpec((1, c), lambda b, i: (0, 0)),
                  pl.BlockSpec((GMLP_GROUPS, GMLP_CHUNK, GMLP_CHUNK), lambda b, i: (0, 0, 0)),
                  pl.BlockSpec((GMLP_CHUNK, GMLP_GROUPS), lambda b, i: (0, 0))],
        out_specs=pl.BlockSpec((1, ts, c), lambda b, i: (b, i, 0)),
        out_shape=jax.ShapeDtypeStruct((bsz, s, c), BF16),
        compiler_params=_params(("arbitrary", "arbitrary")),
        name="mixer_gmlp",
    )(pl_all, ln_g.reshape(1, c), ln_b.reshape(1, c), ws.astype(BF16), bs.T)


def _rope_tables(s):
    half, quarter = HEAD_DIM // 2, HEAD_DIM // 4
    pos = np.arange(s)
    prow = (pos // GRID_W).astype(np.float32)
    pcol = (pos % GRID_W).astype(np.float32)
    inv = (ROPE_BASE ** (-np.arange(quarter, dtype=np.float32) / quarter)).astype(np.float32)
    lane = np.arange(128)
    in_head = lane % HEAD_DIM
    p = np.where((in_head < half)[None, :], prow[:, None], pcol[:, None]).astype(np.float32)
    ang = (p * inv[lane % quarter][None, :]).astype(np.float32).astype(np.float64)
    first = (in_head % half) < quarter
    cos = np.cos(ang)
    sin = np.where(first[None, :], -np.sin(ang), np.sin(ang))
    swap = np.zeros((128, 128), np.float32)
    for j in lane:
        swap[j + quarter if first[j] else j - quarter, j] = 1.0
    return cos.astype(np.float32), sin.astype(np.float32), swap


def _rope_kernel(q_ref, k_ref, cos_ref, sin_ref, swap_ref, qo_ref, ko_ref):
    cos, sin, swap = cos_ref[...], sin_ref[...], swap_ref[...]

    def rot(x):
        parts = []
        for t in range(x.shape[1] // 128):
            xt = x[:, t * 128:(t + 1) * 128]
            parts.append(xt * cos + _dot(xt.astype(BF16), swap) * sin)
        return parts[0] if len(parts) == 1 else jnp.concatenate(parts, axis=1)

    qo_ref[0] = rot(q_ref[0])
    ko_ref[0] = rot(k_ref[0])


def _rope_call(pl_all):
    bsz, s, _ = pl_all.shape
    tm = 512
    cos, sin, swap = _rope_tables(s)
    qw, kw = WIN_HEADS * HEAD_DIM, WIN_KV_HEADS * HEAD_DIM
    return pl.pallas_call(
        _rope_kernel,
        grid=(bsz, s // tm),
        in_specs=[pl.BlockSpec((1, tm, qw), lambda b, i: (b, i, COL_CQ // qw)),
                  pl.BlockSpec((1, tm, kw), lambda b, i: (b, i, COL_CK // kw)),
                  pl.BlockSpec((tm, 128), lambda b, i: (i, 0)),
                  pl.BlockSpec((tm, 128), lambda b, i: (i, 0)),
                  pl.BlockSpec((128, 128), lambda b, i: (0, 0))],
        out_specs=[pl.BlockSpec((1, tm, qw), lambda b, i: (b, i, 0)),
                   pl.BlockSpec((1, tm, kw), lambda b, i: (b, i, 0))],
        out_shape=[jax.ShapeDtypeStruct((bsz, s, qw), F32), jax.ShapeDtypeStruct((bsz, s, kw), F32)],
        compiler_params=_params(("arbitrary", "arbitrary")),
        name="rope",
    )(pl_all, pl_all, jnp.asarray(cos), jnp.asarray(sin), jnp.asarray(swap, BF16))


WIN_STEP = 4


def _win_kernel(sink_ref, q_ref, kp_ref, kc_ref, kn_ref, vp_ref, vc_ref, vn_ref, kx_ref, vx_ref, o_ref, *, seq):
    g = pl.program_id(1)
    j = pl.program_id(2)
    grp = q_ref.shape[1]
    wb = WIN_BLOCK
    m_rows = grp * wb
    k_ext = jnp.concatenate([kp_ref[0, 0], kc_ref[0, 0], kn_ref[0, 0]], axis=0).astype(BF16)
    v_ext = _with_ones(jnp.concatenate([vp_ref[0, 0], vc_ref[0, 0], vn_ref[0, 0]], axis=0).astype(BF16))
    k_ctx = kx_ref[0, 0].astype(BF16)
    v_ctx = _with_ones(vx_ref[0, 0].astype(BF16))
    sink = jnp.concatenate([jnp.full((wb, 1), sink_ref[g * grp + h], F32) for h in range(grp)], axis=0)
    row = lax.broadcasted_iota(jnp.int32, (m_rows, 3 * wb), 0) & (wb - 1)
    col = lax.broadcasted_iota(jnp.int32, (m_rows, 3 * wb), 1)
    in_band = jnp.where(jnp.abs(col - wb - row) <= WINDOW, 1.0, 0.0)
    col1 = lax.broadcasted_iota(jnp.int32, (1, 3 * wb), 1)
    for t in range(WIN_STEP):
        n = j * WIN_STEP + t
        kpos = (n - 1) * wb + col1
        in_seq = jnp.where((kpos >= 0) & (kpos < seq), 1.0, 0.0)
        q = jnp.concatenate([q_ref[0, h, t * wb:(t + 1) * wb, :] for h in range(grp)], axis=0)
        q = (q * (HEAD_DIM ** -0.5)).astype(BF16)
        s_loc = jnp.where(in_band * in_seq > 0.5, _dot_nt(q, k_ext[t * wb:(t + 3) * wb, :]), NEG)
        s_ctx = _dot_nt(q, k_ctx)
        m = jnp.maximum(jnp.max(jnp.concatenate([s_loc, s_ctx], axis=1), axis=-1, keepdims=True), sink)
        p_loc = jnp.exp(s_loc - m).astype(BF16)
        p_ctx = jnp.exp(s_ctx - m).astype(BF16)
        ov = _dot(p_loc, v_ext[t * wb:(t + 3) * wb, :]) + _dot(p_ctx, v_ctx)
        den = ov[:, HEAD_DIM:HEAD_DIM + 1] + jnp.exp(sink - m)
        o = (ov[:, :HEAD_DIM] / den).astype(o_ref.dtype)
        for h in range(grp):
            o_ref[0, h, t * wb:(t + 1) * wb, :] = o[h * wb:(h + 1) * wb, :]


def _win_call(q, k, v, kx, vx, sink):
    bsz, hq, s, dh = q.shape
    hkv = k.shape[1]
    grp = hq // hkv
    cx = kx.shape[2]
    nb = s // WIN_BLOCK
    step = WIN_STEP * WIN_BLOCK
    blk = (1, 1, WIN_BLOCK, dh)
    prev = lambda b, g, j, sk: (b, g, jnp.maximum(j * WIN_STEP - 1, 0), 0)
    cur = lambda b, g, j, sk: (b, g, j, 0)
    nxt = lambda b, g, j, sk: (b, g, jnp.minimum((j + 1) * WIN_STEP, nb - 1), 0)
    ctx = lambda b, g, j, sk: (b, g, 0, 0)
    return pl.pallas_call(
        functools.partial(_win_kernel, seq=s),
        grid_spec=pltpu.PrefetchScalarGridSpec(
            num_scalar_prefetch=1,
            grid=(bsz, hkv, nb // WIN_STEP),
            in_specs=[pl.BlockSpec((1, grp, step, dh), cur),
                      pl.BlockSpec(blk, prev), pl.BlockSpec((1, 1, step, dh), cur), pl.BlockSpec(blk, nxt),
                      pl.BlockSpec(blk, prev), pl.BlockSpec((1, 1, step, dh), cur), pl.BlockSpec(blk, nxt),
                      pl.BlockSpec((1, 1, cx, dh), ctx), pl.BlockSpec((1, 1, cx, dh), ctx)],
            out_specs=pl.BlockSpec((1, grp, step, dh), cur)),
        out_shape=jax.ShapeDtypeStruct((bsz, hq, s, dh), BF16),
        compiler_params=_params(("arbitrary", "arbitrary", "arbitrary")),
        name="mixer_window_attn",
    )(sink, q, k, k, k, v, v, v, kx, vx)


def _na_bias_table(rpb):
    qc = np.arange(GRID_W)
    kc = np.arange(GRID_W)
    cs = np.clip(qc - NA_COLS // 2, 0, GRID_W - NA_COLS)
    cmask = (kc[None, :] >= cs[:, None]) & (kc[None, :] < cs[:, None] + NA_COLS)
    ci = np.clip(kc[None, :] - qc[:, None] + NA_COLS - 1, 0, 2 * NA_COLS - 2)
    pick = (ci[None] == np.arange(2 * NA_COLS - 1)[:, None, None]).astype(np.float32)
    t15 = jnp.einsum('hrc,cqk->hrqk', rpb, jnp.asarray(pick), precision=lax.Precision.HIGHEST)
    t15 = jnp.where(cmask[None, None], t15, NEG)
    tab = jnp.stack([t15[:, d0:d0 + NA_ROWS] for d0 in range(NA_ROWS)], axis=1)
    return tab.transpose(0, 1, 3, 2, 4).reshape(rpb.shape[0], NA_ROWS, GRID_W, NA_ROWS * GRID_W)


def _na_kernel(q_ref, k_ref, v_ref, kx_ref, vx_ref, bias_ref, o_ref):
    rows = q_ref.shape[2] // GRID_W
    k_ctx = kx_ref[0, 0].astype(BF16)
    v_ctx = _with_ones(vx_ref[0, 0].astype(BF16))
    k_all = k_ref[0, 0].astype(BF16)
    v_all = _with_ones(v_ref[0, 0].astype(BF16))
    span = NA_ROWS * GRID_W
    chunk = 8
    for r0 in range(0, rows, chunk):
        q = (q_ref[0, 0, r0 * GRID_W:(r0 + chunk) * GRID_W, :] * (HEAD_DIM ** -0.5)).astype(BF16)
        s_parts = []
        for r in range(r0, r0 + chunk):
            rs = min(max(r - NA_ROWS // 2, 0), rows - NA_ROWS)
            k_loc = k_all[rs * GRID_W:rs * GRID_W + span, :]
            qr = q[(r - r0) * GRID_W:(r - r0 + 1) * GRID_W, :]
            s_parts.append(_dot_nt(qr, k_loc) + bias_ref[0, rs - r + NA_ROWS - 1])
        s_loc = jnp.concatenate(s_parts, axis=0)
        s_ctx = _dot_nt(q, k_ctx)
        m = jnp.max(jnp.concatenate([s_loc, s_ctx], axis=1), axis=-1, keepdims=True)
        p_loc = jnp.exp(s_loc - m).astype(BF16)
        p_ctx = jnp.exp(s_ctx - m).astype(BF16)
        o_parts = []
        for r in range(r0, r0 + chunk):
            rs = min(max(r - NA_ROWS // 2, 0), rows - NA_ROWS)
            v_loc = v_all[rs * GRID_W:rs * GRID_W + span, :]
            o_parts.append(_dot(p_loc[(r - r0) * GRID_W:(r - r0 + 1) * GRID_W, :], v_loc))
        ov = jnp.concatenate(o_parts, axis=0) + _dot(p_ctx, v_ctx)
        o = ov[:, :HEAD_DIM] / ov[:, HEAD_DIM:HEAD_DIM + 1]
        o_ref[0, 0, r0 * GRID_W:(r0 + chunk) * GRID_W, :] = o.astype(o_ref.dtype)


def _na_call(q, k, v, kx, vx, bias_tab):
    bsz, h, s, dh = q.shape
    cx = kx.shape[2]
    full = pl.BlockSpec((1, 1, s, dh), lambda b, hh: (b, hh, 0, 0))
    ctx = pl.BlockSpec((1, 1, cx, dh), lambda b, hh: (b, hh, 0, 0))
    return pl.pallas_call(
        _na_kernel,
        grid=(bsz, h),
        in_specs=[full, full, full, ctx, ctx,
                  pl.BlockSpec((1, NA_ROWS, GRID_W, NA_ROWS * GRID_W), lambda b, hh: (hh, 0, 0, 0))],
        out_specs=full,
        out_shape=jax.ShapeDtypeStruct((bsz, h, s, dh), BF16),
        compiler_params=_params(("arbitrary", "arbitrary")),
        name="mixer_neighbourhood_attn",
    )(q, k, v, kx, vx, bias_tab)


def _ctx_attn_kernel(sink_ref, q_ref, k_ref, v_ref, o_ref, *, use_sink):
    h = pl.program_id(1)
    q = (q_ref[0, 0] * (HEAD_DIM ** -0.5)).astype(BF16)
    s = _dot_nt(q, k_ref[0, 0].astype(BF16))
    m = jnp.max(s, axis=-1, keepdims=True)
    if use_sink:
        sink = sink_ref[h]
        m = jnp.maximum(m, sink)
    p = jnp.exp(s - m)
    den = jnp.sum(p, axis=-1, keepdims=True)
    if use_sink:
        den = den + jnp.exp(sink - m)
    o_ref[0, 0] = (_dot(p.astype(BF16), v_ref[0, 0].astype(BF16)) / den).astype(o_ref.dtype)


def _ctx_attn_call(q, k, v, sink):
    bsz, hq, cx, dh = q.shape
    grp = hq // k.shape[1]
    use_sink = sink is not None
    if sink is None:
        sink = jnp.zeros((hq,), F32)
    qs = pl.BlockSpec((1, 1, cx, dh), lambda b, h, sk: (b, h, 0, 0))
    ks = pl.BlockSpec((1, 1, cx, dh), lambda b, h, sk: (b, h // grp, 0, 0))
    return pl.pallas_call(
        functools.partial(_ctx_attn_kernel, use_sink=use_sink),
        grid_spec=pltpu.PrefetchScalarGridSpec(
            num_scalar_prefetch=1, grid=(bsz, hq), in_specs=[qs, ks, ks], out_specs=qs),
        out_shape=jax.ShapeDtypeStruct((bsz, hq, cx, dh), BF16),
        compiler_params=_params(("arbitrary", "arbitrary")),
        name="context_attn",
    )(sink, q, k, v)


def _merge_kernel(x_ref, ya_ref, yb_ref, yc_ref, yd_ref, wg0_ref, wg1_ref, wg2_ref, wg3_ref, wbr_ref, wout_ref,
                  ng_ref, sh_ref, sc_ref, gt_ref, o_ref, h_ref, acc_ref):
    n = pl.program_id(2)
    nt = acc_ref.shape[0]
    d = o_ref.shape[2]

    @pl.when(n == 0)
    def _():
        h = _rms(x_ref[0], ng_ref[0:1, :]) * (1.0 + sc_ref[0]) + sh_ref[0]
        h_ref[...] = h.astype(BF16)

    h = h_ref[...]
    acc = None
    for i, (y_ref, wg_ref) in enumerate(((ya_ref, wg0_ref), (yb_ref, wg1_ref), (yc_ref, wg2_ref), (yd_ref, wg3_ref))):
        t = _sigmoid(_dot(h, wg_ref[...])) * _dot(y_ref[0], wbr_ref[i])
        acc = t if acc is None else acc + t
    acc_ref[n] = acc.astype(BF16)

    @pl.when(n == nt - 1)
    def _():
        a = jnp.concatenate([acc_ref[j] for j in range(nt)], axis=1)
        oc = d // 4
        for c in range(4):
            o_ref[0, :, c * oc:(c + 1) * oc] = _dot(a, wout_ref[:, c * oc:(c + 1) * oc])
        o_ref[0] = x_ref[0] + gt_ref[0] * _rms(o_ref[0], ng_ref[1:2, :])


def _merge_call(x2, bsz, s, row_off, ys, w_gate, w_branch, w_out, norm_g, mods, shared_row):
    d = x2.shape[1]
    c = BRANCH_DIM
    tm = min(s, 512)
    tn = 256
    nt = d // tn
    x = x2.reshape(-1, tm, d)
    base, per = row_off // tm, s // tm
    row = (lambda b: b) if shared_row is None else (lambda b: shared_row)
    ysp = pl.BlockSpec((1, tm, c), lambda b, i, n: (b, i, 0))
    wg = [pl.BlockSpec((d, tn), functools.partial(lambda b, i, n, k: (0, k * nt + n), k=k)) for k in range(4)]
    mod = lambda col: pl.BlockSpec((1, 1, d), lambda b, i, n: (row(b), 0, col))
    return pl.pallas_call(
        _merge_kernel,
        grid=(bsz, s // tm, nt),
        in_specs=[pl.BlockSpec((1, tm, d), lambda b, i, n: (base + b * per + i, 0, 0)), ysp, ysp, ysp, ysp, *wg,
                  pl.BlockSpec((4, c, tn), lambda b, i, n: (0, 0, n)),
                  pl.BlockSpec((d, d), lambda b, i, n: (0, 0), pipeline_mode=pl.Buffered(1)),
                  pl.BlockSpec((4, d), lambda b, i, n: (0, 0)),
                  mod(0), mod(1), mod(2)],
        out_specs=pl.BlockSpec((1, tm, d), lambda b, i, n: (b * per + i, 0, 0)),
        out_shape=jax.ShapeDtypeStruct((bsz * per, tm, d), F32),
        scratch_shapes=[pltpu.VMEM((tm, d), BF16), pltpu.VMEM((nt, tm, tn), BF16)],
        compiler_params=_params(("arbitrary", "arbitrary", "arbitrary")),
        name="mixer_merge",
    )(x, *ys, w_gate, w_gate, w_gate, w_gate, w_branch, w_out, norm_g, mods, mods, mods).reshape(bsz * s, d)


def _expert_row_ids():
    r = np.arange(N_EXPERTS)
    return ((r % N_GROUPS) * GROUP_SIZE + r // N_GROUPS).astype(np.int32)


def _router_kernel(rows_ref, xa_ref, xb_ref, ng_ref, sh_ref, sc_ref, rw_ref, rb_ref, eid_ref, tri_t_ref,
                   h_ref, e_ref, r_ref, w_ref, cnt_ref, *, na):
    i = pl.program_id(0)
    tm = xa_ref.shape[0]

    @pl.when(i == 0)
    def _():
        cnt_ref[...] = jnp.zeros(cnt_ref.shape, F32)

    x = jnp.where(i < na, xa_ref[...], xb_ref[...])
    h = _rms(x, ng_ref[2:3, :]) * (1.0 + sc_ref[0]) + sh_ref[0]
    h_ref[:, 0, :] = _pack_halves(h)
    hb = h.astype(BF16)

    h_tail = (h - hb.astype(F32)).astype(BF16)
    head = _dot_nt(rw_ref[...], hb)
    logits = head[:N_EXPERTS] + head[N_EXPERTS:] + _dot_nt(rw_ref[:N_EXPERTS, :], h_tail)
    scores = _sigmoid(logits)
    biased = scores + rb_ref[...]
    m1 = jnp.full((N_GROUPS, tm), -jnp.inf, F32)
    m2 = m1
    for j in range(GROUP_SIZE):
        v = biased[j * N_GROUPS:(j + 1) * N_GROUPS, :]
        m2 = jnp.maximum(m2, jnp.minimum(m1, v))
        m1 = jnp.maximum(m1, v)
    gs = m1 + m2
    gid = lax.broadcasted_iota(jnp.int32, (N_GROUPS, tm), 0)
    beat = jnp.zeros((N_GROUPS, tm), jnp.int32)
    for g in range(N_GROUPS):
        o = gs[g:g + 1, :]
        beat = beat + jnp.where((o > gs) | ((o == gs) & (g < gid)), 1, 0)
    keep = beat < TOPK_GROUPS
    masked = jnp.concatenate(
        [jnp.where(keep, biased[j * N_GROUPS:(j + 1) * N_GROUPS, :], NEG) for j in range(GROUP_SIZE)], axis=0)
    eidf = eid_ref[...].astype(F32)
    work = masked
    picks = []
    self32 = jnp.zeros((N_EXPERTS, tm), F32)
    for k in range(TOP_K):
        top = jnp.max(work, axis=0, keepdims=True)
        pick = jnp.min(jnp.where(work == top, eidf, float(N_EXPERTS)), axis=0, keepdims=True)
        hit = eidf == pick
        work = jnp.where(hit, -jnp.inf, work)
        self32 = jnp.where(hit, 1.0, self32)
        picks.append(pick)
    selb = self32.astype(BF16)
    wsel = self32 * scores
    wsel = wsel / jnp.sum(wsel, axis=0, keepdims=True) * ROUTE_SCALE
    rank = cnt_ref[:, 0:1] + _dot(selb, tri_t_ref[...])
    cnt_ref[...] = cnt_ref[...] + jnp.sum(self32, axis=1, keepdims=True)
    for k in range(TOP_K):
        hit = eidf == picks[k]
        e_ref[k:k + 1, :] = picks[k].astype(jnp.int32)
        r_ref[k:k + 1, :] = jnp.sum(jnp.where(hit, rank, 0.0), axis=0, keepdims=True).astype(jnp.int32)
        w_ref[k:k + 1, :] = jnp.sum(jnp.where(hit, wsel, 0.0), axis=0, keepdims=True)


def _two_sources(xa, xb, tm):
    na = xa.shape[0] // tm
    d = xa.shape[1]
    if xb is None:
        xb = xa
    nb_last = xb.shape[0] // tm - 1
    spec_a = pl.BlockSpec((tm, d), lambda i, rows: (jnp.minimum(i, na - 1), 0))
    spec_b = pl.BlockSpec((tm, d), lambda i, rows: (jnp.clip(i - na, 0, nb_last), 0))
    return na, xb, spec_a, spec_b


def _router_call(xa, xb, tile_rows, norm_g, mods, router_w, router_b):
    d = xa.shape[1]
    t = xa.shape[0] + (0 if xb is None else xb.shape[0])
    tm = 256
    na, xb, spec_a, spec_b = _two_sources(xa, xb, tm)
    ids = _expert_row_ids()
    rw = router_w.T[ids]
    rw_head = rw.astype(BF16)
    rw = jnp.concatenate([rw_head, (rw - rw_head.astype(F32)).astype(BF16)], axis=0)
    rb = router_b[ids].reshape(N_EXPERTS, 1)
    tri_t = jnp.asarray(np.triu(np.ones((tm, tm), np.float32), 1), BF16)
    const = lambda shape: pl.BlockSpec(shape, lambda i, rows: (0,) * len(shape))
    tok = pl.BlockSpec((tm, d), lambda i, rows: (i, 0))
    slots = pl.BlockSpec((TOP_K, tm), lambda i, rows: (0, i))
    return pl.pallas_call(
        functools.partial(_router_kernel, na=na),
        grid_spec=pltpu.PrefetchScalarGridSpec(
            num_scalar_prefetch=1,
            grid=(t // tm,),
            in_specs=[spec_a, spec_b, const((4, d)),
                      pl.BlockSpec((1, 1, d), lambda i, rows: (rows[i], 0, 3)),
                      pl.BlockSpec((1, 1, d), lambda i, rows: (rows[i], 0, 4)),
                      const((2 * N_EXPERTS, d)), const((N_EXPERTS, 1)), const((N_EXPERTS, 1)),
                      const((tm, tm))],
            out_specs=[pl.BlockSpec((tm, 1, d // 2), lambda i, rows: (i, 0, 0)), slots, slots, slots,
                       const((N_EXPERTS, 128))]),
        out_shape=[jax.ShapeDtypeStruct((t, 1, d // 2), jnp.uint32),
                   jax.ShapeDtypeStruct((TOP_K, t), jnp.int32), jax.ShapeDtypeStruct((TOP_K, t), jnp.int32),
                   jax.ShapeDtypeStruct((TOP_K, t), F32), jax.ShapeDtypeStruct((N_EXPERTS, 128), F32)],
        compiler_params=_params(("arbitrary",)),
        name="moe_router",
    )(tile_rows, xa, xb, norm_g, mods, mods, rw, rb, jnp.asarray(ids).reshape(N_EXPERTS, 1), tri_t)


def _gmm_kernel(blk_ref, exp_ref, lo_ref, hi_ref, first_ref, newexp_ref, tok_cur_ref, tok_nxt_ref, h_hbm,
                w1_ref, w3_ref, w2_ref, o_ref, xa_ref, xb_ref, w1b_ref, w3b_ref, w2b_ref, sem):
    i = pl.program_id(0)
    last = pl.num_programs(0) - 1
    nrows = xa_ref.shape[0]
    xbufs = (xa_ref, xb_ref)

    def start_gather(tok_ref, s):
        for r in range(nrows):
            pltpu.make_async_copy(h_hbm.at[tok_ref[0, 0, r]], xbufs[s].at[pl.ds(r, 1)],
                                  sem.at[s]).start(priority=r % 2)

    def wait_gather(s):
        for r in range(nrows):
            pltpu.make_async_copy(h_hbm.at[0], xbufs[s].at[pl.ds(r, 1)], sem.at[s]).wait()

    @pl.when(i == 0)
    def _():
        start_gather(tok_cur_ref, 0)

    @pl.when(first_ref[i] == 1)
    def _():
        o_ref[...] = jnp.zeros(o_ref.shape, o_ref.dtype)

    @pl.when(newexp_ref[i] == 1)
    def _():
        w1b_ref[...] = w1_ref[0].astype(BF16)
        w3b_ref[...] = w3_ref[0].astype(BF16)
        w2b_ref[...] = w2_ref[0].astype(BF16)

    def step(s):
        wait_gather(s)

        @pl.when(lo_ref[i] >= 0)
        def _():
            start_gather(tok_nxt_ref, 1 - s)

        x_lo, x_hi = _unpack_halves(xbufs[s][...])
        x = jnp.concatenate([x_lo.astype(BF16), x_hi.astype(BF16)], axis=1)
        a = (_silu(_dot(x, w1b_ref[...])) * _dot(x, w3b_ref[...])).astype(BF16)
        y = _pack_halves(_dot(a, w2b_ref[...]))
        rows = lax.broadcasted_iota(jnp.int32, (nrows, 1), 0)
        keep = (rows >= lo_ref[i]) & (rows < hi_ref[i])
        o_ref[:, 0, :] = jnp.where(keep, y, o_ref[:, 0, :])

        @pl.when(i == last)
        def _():
            wait_gather(1 - s)

    for s in range(2):
        pl.when(i % 2 == s)(functools.partial(step, s))


def _gmm_call(items, slot_tok, hp, w1, w3, w2, layer):
    a = slot_tok.shape[0]
    d, f = w1.shape[2], w1.shape[3]
    n_items = items[0].shape[0]
    nb = a // MOE_BLOCK
    tok3 = slot_tok.reshape(nb, 1, MOE_BLOCK)
    rows = pl.BlockSpec((MOE_BLOCK,) + hp.shape[1:], lambda i, blk, ex, *_: (blk[i], 0, 0))
    toks = lambda step: pl.BlockSpec((1, 1, MOE_BLOCK), lambda i, blk, ex, *_: (blk[step(i)], 0, 0),
                                     memory_space=pltpu.SMEM)
    return pl.pallas_call(
        _gmm_kernel,
        grid_spec=pltpu.PrefetchScalarGridSpec(
            num_scalar_prefetch=len(items),
            grid=(n_items,),
            in_specs=[toks(lambda i: i), toks(lambda i: jnp.minimum(i + 1, n_items - 1)),
                      pl.BlockSpec(memory_space=pl.ANY),
                      pl.BlockSpec((None, 1, d, f), lambda i, blk, ex, *_: (layer, ex[i], 0, 0)),
                      pl.BlockSpec((None, 1, d, f), lambda i, blk, ex, *_: (layer, ex[i], 0, 0)),
                      pl.BlockSpec((None, 1, f, d), lambda i, blk, ex, *_: (layer, ex[i], 0, 0))],
            out_specs=rows,
            scratch_shapes=[pltpu.VMEM((MOE_BLOCK, hp.shape[2]), jnp.uint32),
                            pltpu.VMEM((MOE_BLOCK, hp.shape[2]), jnp.uint32),
                            pltpu.VMEM((d, f), BF16), pltpu.VMEM((d, f), BF16), pltpu.VMEM((f, d), BF16),
                            pltpu.SemaphoreType.DMA((2,))]),
        out_shape=jax.ShapeDtypeStruct((a,) + hp.shape[1:], jnp.uint32),
        compiler_params=_params(("arbitrary",)),
        name="moe_experts",
    )(*items, tok3, tok3, hp, w1, w3, w2)


def _moe_items(counts, n_rows):
    nb = n_rows // MOE_BLOCK
    n_items = nb + N_EXPERTS
    ends = jnp.cumsum(counts).astype(jnp.int32)
    bstart = jnp.arange(nb, dtype=jnp.int32) * MOE_BLOCK
    pos_b = jnp.arange(nb, dtype=jnp.int32) + jnp.sum(ends[None, :] < bstart[:, None], axis=1)
    idx = jnp.arange(n_items, dtype=jnp.int32)
    n_b = jnp.sum(pos_b[None, :] <= idx[:, None], axis=1).astype(jnp.int32)
    n_e = idx + 1 - n_b
    ends0 = jnp.concatenate([jnp.zeros((1,), jnp.int32), ends])
    last_end = jnp.sum(jnp.where(n_e[:, None] == jnp.arange(N_EXPERTS + 1)[None, :], ends0[None, :], 0), axis=1)
    start = jnp.maximum((n_b - 1) * MOE_BLOCK, last_end).astype(jnp.int32)
    stop = jnp.concatenate([start[1:], jnp.full((1,), n_rows, jnp.int32)])
    blk = jnp.minimum(start // MOE_BLOCK, nb - 1)
    ex = jnp.minimum(jnp.sum(ends[None, :] <= start[:, None], axis=1), N_EXPERTS - 1).astype(jnp.int32)
    lo = start - blk * MOE_BLOCK
    hi = stop - blk * MOE_BLOCK
    one = jnp.ones((1,), jnp.int32)
    first = jnp.concatenate([one, (blk[1:] != blk[:-1]).astype(jnp.int32)])
    newexp = jnp.concatenate([one, (ex[1:] != ex[:-1]).astype(jnp.int32)])
    return blk.astype(jnp.int32), ex, lo.astype(jnp.int32), hi.astype(jnp.int32), first, newexp


def _combine_kernel(rows_ref, dest_ref, dest_nxt_ref, ys_hbm, w_ref, xa_ref, xb_ref, ng_ref, sh_ref, sc_ref, gt_ref,
                    sw1_ref, sw3_ref, sw2_ref, o_ref, buf_ref, sem, *, na):
    i = pl.program_id(0)
    tc = xa_ref.shape[0]
    slot = i % 2

    def request(idx_ref, s, j, k):
        pltpu.make_async_copy(ys_hbm.at[idx_ref[0, 0, k * tc + j]], buf_ref.at[s, k, pl.ds(j, 1)],
                              sem.at[s]).start(priority=k % 2)

    @pl.when(i == 0)
    def _():
        def body(j, c):
            for k in range(TOP_K):
                request(dest_ref, 0, j, k)
            return c
        lax.fori_loop(0, tc, body, 0)

    for s in range(2):
        @pl.when((i < pl.num_programs(0) - 1) & (slot == 1 - s))
        def _(s=s):
            for j in range(tc):
                for k in range(TOP_K):
                    request(dest_nxt_ref, s, j, k)

    x = jnp.where(i < na, xa_ref[...], xb_ref[...])
    hb = (_rms(x, ng_ref[2:3, :]) * (1.0 + sc_ref[0]) + sh_ref[0]).astype(BF16)
    shared = _dot((_silu(_dot(hb, sw1_ref[...])) * _dot(hb, sw3_ref[...])).astype(BF16), sw2_ref[...])
    pltpu.make_async_copy(buf_ref.at[slot], buf_ref.at[slot], sem.at[slot]).wait()
    half = xa_ref.shape[1] // 2
    acc_lo = shared[:, :half]
    acc_hi = shared[:, half:]
    for k in range(TOP_K):
        y_lo, y_hi = _unpack_halves(buf_ref[slot, k])
        wk = w_ref[:, k:k + 1]
        acc_lo = acc_lo + y_lo * wk
        acc_hi = acc_hi + y_hi * wk
    fl = jnp.concatenate([acc_lo, acc_hi], axis=1)
    o_ref[...] = x + gt_ref[0] * _rms(fl, ng_ref[3:4, :])


def _combine_call(tile_rows, dest, ys, wsel, sw1, sw3, sw2, xa, xb, norm_g, mods):
    d = xa.shape[1]
    t = dest.shape[1]
    f = sw1.shape[1]
    tc = 128
    const = lambda shape: pl.BlockSpec(shape, lambda i, rows: (0,) * len(shape))
    mod = lambda col: pl.BlockSpec((1, 1, d), lambda i, rows: (rows[i], 0, col))
    dest3 = dest.reshape(TOP_K, t // tc, tc).transpose(1, 0, 2).reshape(t // tc, 1, TOP_K * tc)
    tok = pl.BlockSpec((tc, d), lambda i, rows: (i, 0))
    nt = t // tc
    na, xb, spec_a, spec_b = _two_sources(xa, xb, tc)
    return pl.pallas_call(
        functools.partial(_combine_kernel, na=na),
        grid_spec=pltpu.PrefetchScalarGridSpec(
            num_scalar_prefetch=1,
            grid=(nt,),
            in_specs=[pl.BlockSpec((1, 1, TOP_K * tc), lambda i, rows: (i, 0, 0), memory_space=pltpu.SMEM),
                      pl.BlockSpec((1, 1, TOP_K * tc), lambda i, rows: (jnp.minimum(i + 1, nt - 1), 0, 0),
                                   memory_space=pltpu.SMEM),
                      pl.BlockSpec(memory_space=pl.ANY),
                      pl.BlockSpec((tc, TOP_K), lambda i, rows: (i, 0)),
                      spec_a, spec_b,
                      pl.BlockSpec((4, d), lambda i, rows: (0, 0)),
                      mod(3), mod(4), mod(5),
                      const((d, f)), const((d, f)), const((f, d))],
            out_specs=tok,
            scratch_shapes=[pltpu.VMEM((2, TOP_K, tc, d // 2), jnp.uint32), pltpu.SemaphoreType.DMA((2,))]),
        out_shape=jax.ShapeDtypeStruct((t, d), F32),
        compiler_params=_params(("arbitrary",)),
        name="moe_combine",
    )(tile_rows, dest3, dest3, ys, wsel.T, xa, xb, norm_g, mods, mods, mods, sw1, sw3, sw2)


def _channel_mixer(xa, xb, mod_rows, norm_g, mods, router_w, router_b, w1, w3, w2, layer, sw1, sw3, sw2):
    t = xa.shape[0] + (0 if xb is None else xb.shape[0])
    h, eidx, rank, wsel, cnt = _router_call(xa, xb, mod_rows[::2], norm_g, mods, router_w, router_b)
    counts = jnp.zeros((N_EXPERTS,), jnp.int32).at[_expert_row_ids()].set(cnt[:, 0].astype(jnp.int32))
    starts = jnp.cumsum(counts) - counts
    sel = eidx[:, :, None] == jnp.arange(N_EXPERTS, dtype=jnp.int32)[None, None, :]
    dest = jnp.sum(jnp.where(sel, starts[None, None, :], 0), axis=-1) + rank
    slot_tok = (jnp.argsort(dest.reshape(-1)) % t).astype(jnp.int32)
    ys = _gmm_call(_moe_items(counts, t * TOP_K), slot_tok, h, w1, w3, w2, layer)
    return _combine_call(mod_rows, dest, ys, wsel, sw1, sw3, sw2, xa, xb, norm_g, mods)


def _heads(t, n):
    b, s, _ = t.shape
    return t.reshape(b, s, n, HEAD_DIM).transpose(0, 2, 1, 3)


def _unheads(t):
    b, n, s, dh = t.shape
    return t.transpose(0, 2, 1, 3).reshape(b, s, n * dh)


def kernel(x, c, ctx, c_ctx, w_ada, b_ada, norm_g, w_in, conv_b_in, conv_dw, conv_dw_b, conv_ln_g, conv_ln_b, gmlp_ln_g, gmlp_ln_b, gmlp_ws, gmlp_bs, win_sink, na_rpb, w_branch, w_out, router_w, router_b, exp_w1, exp_w3, exp_w2, sh_w1, sh_w3, sh_w2):
    bsz, s, d = x.shape
    cx = ctx.shape[1]
    depth = w_ada.shape[0]
    ctx_row = bsz
    c_all = jnp.zeros((MODS_ROWS, d), F32).at[:bsz].set(c).at[ctx_row].set(c_ctx)
    mods_all = _ada_call(c_all, w_ada, b_ada)
    lat_rows = jnp.repeat(jnp.arange(bsz, dtype=jnp.int32), s // 128)
    ctx_rows = jnp.full((bsz * cx // 128,), ctx_row, jnp.int32)
    xl, xl_off = x.reshape(bsz * s, d), 0
    xc, xc_off = ctx.reshape(bsz * cx, d), 0
    for l in range(depth):
        last = l == depth - 1
        mods = mods_all[l].reshape(MODS_ROWS, 1, N_MODS * d)
        g_off = w_in.shape[2] - 4 * d
        kv_off = g_off - KV_COLS
        w_small = jnp.concatenate([w_in[l, :, :kv_off], w_in[l, :, kv_off + 256:g_off],
                                   w_in[l, :, kv_off:kv_off + 256]], axis=1).astype(BF16)
        w_gate = w_in[l, :, g_off:].astype(BF16)
        w_br = w_branch[l].astype(BF16)
        w_o = w_out[l].astype(BF16)
        pl_l = _proj_call(xl, bsz, s, xl_off, norm_g[l], mods, w_small, None, SMALL_COLS // 2)
        if last:
            pkv_c = _proj_call(xc, bsz, cx, xc_off, norm_g[l], mods, w_small[:, COL_DK:], ctx_row, KV_COLS)
            kv_base = 0
        else:
            pc = _proj_call(xc, bsz, cx, xc_off, norm_g[l], mods, w_small, ctx_row, SMALL_COLS // 2)
            pkv_c, kv_base = pc, COL_DK
        dkc = _heads(pkv_c[..., kv_base:kv_base + 512], NA_HEADS)
        dvc = _heads(pkv_c[..., kv_base + 512:kv_base + 1024], NA_HEADS)
        ckc = _heads(pkv_c[..., kv_base + 1024:kv_base + 1152], WIN_KV_HEADS)
        cvc = _heads(pkv_c[..., kv_base + 1152:kv_base + 1280], WIN_KV_HEADS)
        conv_args = (conv_b_in[l], conv_dw[l], conv_dw_b[l], conv_ln_g[l], conv_ln_b[l])
        gmlp_args = (gmlp_ln_g[l], gmlp_ln_b[l], gmlp_ws[l], gmlp_bs[l])
        ya = _conv_call(pl_l, *conv_args)
        yb = _gmlp_call(pl_l, *gmlp_args)
        q_rope, k_rope = _rope_call(pl_l)
        yc = _unheads(_win_call(_heads(q_rope, WIN_HEADS), _heads(k_rope, WIN_KV_HEADS),
                                _heads(pl_l[..., COL_CV:COL_CV + 128], WIN_KV_HEADS), ckc, cvc, win_sink[l]))
        yd = _unheads(_na_call(_heads(pl_l[..., COL_DQ:COL_DQ + 512], NA_HEADS),
                               _heads(pl_l[..., COL_DK:COL_DK + 512], NA_HEADS),
                               _heads(pl_l[..., COL_DV:COL_DV + 512], NA_HEADS), dkc, dvc,
                               _na_bias_table(na_rpb[l])))
        xl_mid = _merge_call(xl, bsz, s, xl_off, (ya, yb, yc, yd), w_gate, w_br, w_o, norm_g[l], mods, None)
        moe_w = (router_w[l], router_b[l], exp_w1, exp_w3, exp_w2, l,
                 sh_w1[l].astype(BF16), sh_w3[l].astype(BF16), sh_w2[l].astype(BF16))
        if last:
            xl = _channel_mixer(xl_mid, None, lat_rows, norm_g[l], mods, *moe_w)
        else:
            yca = _conv_call(pc, *conv_args)
            ycb = _gmlp_call(pc, *gmlp_args)
            ycc = _unheads(_ctx_attn_call(_heads(pc[..., COL_CQ:COL_CQ + 512], WIN_HEADS), ckc, cvc, win_sink[l]))
            ycd = _unheads(_ctx_attn_call(_heads(pc[..., COL_DQ:COL_DQ + 512], NA_HEADS), dkc, dvc, None))
            pair = 2 if bsz % 2 == 0 else 1
            ys_c = tuple(y.reshape(bsz // pair, pair * cx, BRANCH_DIM) for y in (yca, ycb, ycc, ycd))
            xc_mid = _merge_call(xc, bsz // pair, pair * cx, xc_off, ys_c, w_gate, w_br, w_o, norm_g[l], mods,
                                 ctx_row)
            out = _channel_mixer(xl_mid, xc_mid, jnp.concatenate([lat_rows, ctx_rows]), norm_g[l], mods, *moe_w)
            xl, xl_off = out, 0
            xc, xc_off = out, bsz * s
    return xl.reshape(bsz, s, d)
```
